```python
import jax, jax.numpy as jnp
from jax import lax
import numpy as np

D_MODEL = 1024
BATCH = 2
SEQ = 8192
DEPTH = 2
DEC_BATCH = 128
DEC_SEQ = 4
PAST_LEN = 16384
PAGE_SIZE = 128

HEAD_DIM = 64
RWKV_HEADS = 8
RWKV_DIM = RWKV_HEADS * HEAD_DIM
D_DECAY_LORA = 64
D_AAA_LORA = 64
D_GATE_LORA = 128
RWKV_PROJ = 3 * RWKV_DIM + D_DECAY_LORA + D_AAA_LORA + D_GATE_LORA
ATT_Q_HEADS = 8
ATT_KV_HEADS = 2
ATT_GROUP = ATT_Q_HEADS // ATT_KV_HEADS
ATT_DIM = ATT_Q_HEADS * HEAD_DIM
ATT_KV_DIM = ATT_KV_HEADS * HEAD_DIM
WINDOW = 128
BLOCK = 128
IN_PROJ = RWKV_PROJ + ATT_DIM + 2 * ATT_KV_DIM + 2 * D_MODEL
D_FF = 2816
CONV_W = 3
ROPE_THETA = 10000.0
RMS_EPS = 1e-6
GN_EPS = 64e-5

kernel_name = 'rwkv7_swa_sink_gated_hybrid_step'


def rmsnorm(x, g):
    xf = x.astype(jnp.float32)
    y = xf * lax.rsqrt(jnp.mean(xf * xf, axis=-1, keepdims=True) + RMS_EPS)
    return (y * g.astype(jnp.float32)).astype(x.dtype)


def rope(x, pos):
    inv = ROPE_THETA ** (-jnp.arange(0, HEAD_DIM, 2, dtype=jnp.float32) / HEAD_DIM)
    ang = pos.astype(jnp.float32)[:, None] * inv[None, :]
    cos = jnp.cos(ang)[None, :, None, :]
    sin = jnp.sin(ang)[None, :, None, :]
    xf = x.astype(jnp.float32)
    x1, x2 = xf[..., :HEAD_DIM // 2], xf[..., HEAD_DIM // 2:]
    return jnp.concatenate([x1 * cos - x2 * sin, x2 * cos + x1 * sin], axis=-1).astype(x.dtype)


def rwkv_branch(z, shift_prev, wkv0, mu, w0, w2, a0, a2, g2, k_k, k_a, r_k, ln_g, ln_b):
    B, T, _ = z.shape
    f32 = jnp.float32
    z_prev = jnp.concatenate([shift_prev[:, None].astype(z.dtype), z[:, :-1]], axis=1)
    zs = z + mu * (z_prev - z)
    cuts = [RWKV_DIM, 2 * RWKV_DIM, 3 * RWKV_DIM, 3 * RWKV_DIM + D_DECAY_LORA,
            3 * RWKV_DIM + D_DECAY_LORA + D_AAA_LORA]
    r, k, v, zw, za, zg = jnp.split(zs, cuts, axis=-1)
    w_log = -jax.nn.softplus(-(w0 + jnp.tanh(zw) @ w2).astype(f32)) - 0.5
    decay = jnp.exp(-jnp.exp(w_log))
    a = jax.nn.sigmoid((a0 + za @ a2).astype(f32))
    g = jax.nn.sigmoid(zg) @ g2
    hs = lambda t: t.reshape(B, T, RWKV_HEADS, HEAD_DIM)
    kk = hs(k.astype(f32) * k_k)
    kk = kk / jnp.maximum(jnp.linalg.norm(kk, axis=-1, keepdims=True), 1e-12)
    kf = hs(k.astype(f32) * (1.0 + (a - 1.0) * k_a))
    rf, vf, a_h, w_h = hs(r.astype(f32)), hs(v.astype(f32)), hs(a), hs(decay)

    def step(S, inp):
        r_t, w_t, k_t, v_t, kk_t, a_t = inp
        sk = jnp.einsum('bhvk,bhk->bhv', S, kk_t)
        S = (S * w_t[:, :, None, :]
             - sk[..., None] * (kk_t * a_t)[:, :, None, :]
             + v_t[..., None] * k_t[:, :, None, :])
        return S, jnp.einsum('bhvk,bhk->bhv', S, r_t)

    tm = lambda t: jnp.swapaxes(t, 0, 1)
    S_T, y = lax.scan(step, wkv0.astype(f32), (tm(rf), tm(w_h), tm(kf), tm(vf), tm(kk), tm(a_h)))
    y = tm(y)
    y_mu = jnp.mean(y, axis=-1, keepdims=True)
    y_var = jnp.mean(jnp.square(y - y_mu), axis=-1, keepdims=True)
    yn = ((y - y_mu) * lax.rsqrt(y_var + GN_EPS)).reshape(B, T, RWKV_DIM) * ln_g + ln_b
    bonus = (jnp.sum(rf * kf * r_k, axis=-1, keepdims=True) * vf).reshape(B, T, RWKV_DIM)
    out = ((yn + bonus) * g.astype(f32)).astype(z.dtype)
    return out, z[:, -1], S_T.astype(z.dtype)


def sink_attention(q, k, v, mask, sinks):
    s = jnp.einsum('bnqkgd,bnskd->bnkgqs', q, k, preferred_element_type=jnp.float32) * (HEAD_DIM ** -0.5)
    s = jnp.where(mask[None, :, None, None], s, -jnp.inf)
    sink = sinks.astype(jnp.float32).reshape(1, 1, ATT_KV_HEADS, ATT_GROUP, 1, 1)
    m = jnp.maximum(jnp.max(s, axis=-1, keepdims=True), sink)
    e = jnp.exp(s - m)
    p = e / (jnp.sum(e, axis=-1, keepdims=True) + jnp.exp(sink - m))
    return jnp.einsum('bnkgqs,bnskd->bnqkgd', p.astype(v.dtype), v)


def swa_prompt(q, k, v, sinks):
    B, T = q.shape[:2]
    nb = T // BLOCK
    qb = q.reshape(B, nb, BLOCK, ATT_KV_HEADS, ATT_GROUP, HEAD_DIM)

    def band(t):
        tb = t.reshape(B, nb, BLOCK, ATT_KV_HEADS, HEAD_DIM)
        prev = jnp.concatenate([jnp.zeros_like(tb[:, :1]), tb[:, :-1]], axis=1)
        return jnp.concatenate([prev, tb], axis=2)

    i = jnp.arange(BLOCK)[:, None]
    j = jnp.arange(2 * BLOCK)[None, :]
    diff = BLOCK + i - j
    band_ok = (diff >= 0) & (diff < WINDOW)
    kpos = (jnp.arange(nb)[:, None, None] - 1) * BLOCK + j[None]
    mask = band_ok[None] & (kpos >= 0)
    o = sink_attention(qb, band(k), band(v), mask, sinks)
    return o.reshape(B, T, ATT_DIM)


def swa_sample(q, k, v, k_cache, v_cache, sinks):
    B, T = q.shape[:2]
    k_all = jnp.concatenate([k_cache.astype(k.dtype), k], axis=1)
    v_all = jnp.concatenate([v_cache.astype(v.dtype), v], axis=1)
    i = jnp.arange(T)[:, None]
    j = jnp.arange(WINDOW + T)[None, :]
    diff = WINDOW + i - j
    mask = ((diff >= 0) & (diff < WINDOW))[None]
    o = sink_attention(q.reshape(B, 1, T, ATT_KV_HEADS, ATT_GROUP, HEAD_DIM),
                       k_all[:, None], v_all[:, None], mask, sinks)
    return o.reshape(B, T, ATT_DIM), k_all[:, -WINDOW:], v_all[:, -WINDOW:]


def conv_ffn(h, conv_prev, w_in, conv_w, conv_b, w_down):
    T = h.shape[1]
    c, up = jnp.split(h @ w_in, 2, axis=-1)
    c_ext = jnp.concatenate([conv_prev.astype(c.dtype), c], axis=1)
    conv = conv_b
    for j in range(CONV_W):
        conv = conv + c_ext[:, j:j + T] * conv_w[j]
    a = jax.nn.gelu(conv, approximate=False) * up
    return a @ w_down, c_ext[:, -(CONV_W - 1):]


def layer(x, pos, shift_prev, wkv0, k_cache, v_cache, conv_prev, P):
    B, T, _ = x.shape
    h = rmsnorm(x, P['norm_mix_g'])
    zin = h @ P['w_in']
    cuts = [RWKV_PROJ, RWKV_PROJ + ATT_DIM, RWKV_PROJ + ATT_DIM + ATT_KV_DIM,
            RWKV_PROJ + ATT_DIM + 2 * ATT_KV_DIM, RWKV_PROJ + ATT_DIM + 2 * ATT_KV_DIM + D_MODEL]
    z_rwkv, q, k, v, g_r, g_a = jnp.split(zin, cuts, axis=-1)
    o_r, shift_new, wkv_new = rwkv_branch(
        z_rwkv, shift_prev, wkv0, P['rwkv_mu'], P['rwkv_w0'], P['rwkv_w2'], P['rwkv_a0'], P['rwkv_a2'],
        P['rwkv_g2'], P['rwkv_k_k'], P['rwkv_k_a'], P['rwkv_r_k'], P['rwkv_ln_g'], P['rwkv_ln_b'])
    q = rope(q.reshape(B, T, ATT_Q_HEADS, HEAD_DIM), pos)
    k = rope(k.reshape(B, T, ATT_KV_HEADS, HEAD_DIM), pos)
    v = v.reshape(B, T, ATT_KV_HEADS, HEAD_DIM)
    if k_cache is None:
        o_a = swa_prompt(q, k, v, P['attn_sinks'])
        k_new, v_new = k[:, -WINDOW:], v[:, -WINDOW:]
    else:
        o_a, k_new, v_new = swa_sample(q, k, v, k_cache, v_cache, P['attn_sinks'])
    merged = (jax.nn.sigmoid(g_r) * (o_r @ P['w_br_rwkv'])
              + jax.nn.sigmoid(g_a) * (o_a @ P['w_br_attn']))
    x = x + merged @ P['w_out']
    f, conv_new = conv_ffn(rmsnorm(x, P['norm_ffn_g']), conv_prev, P['ffn_w_in'],
                           P['ffn_conv_w'], P['ffn_conv_b'], P['ffn_w_down'])
    x = x + f
    return x, (shift_new, wkv_new, k_new, v_new, conv_new)


def setup_inputs(seed: int = 0) -> dict:
    key = jax.random.key(seed)
    ks = jax.random.split(key, 32)
    f32 = jnp.float32
    nrm = lambda i, shape, scale: jax.random.normal(ks[i], shape, f32) * scale
    L = DEPTH
    return {
        'x_prompt': nrm(0, (BATCH, SEQ, D_MODEL), 1.0),
        'x_sample': nrm(1, (DEC_BATCH, DEC_SEQ, D_MODEL), 1.0),
        'state_rwkv_shift': nrm(2, (L, DEC_BATCH, RWKV_PROJ), 1.0),
        'state_rwkv_wkv': nrm(3, (L, DEC_BATCH, RWKV_HEADS, HEAD_DIM, HEAD_DIM), 0.5),
        'cache_swa_k': nrm(4, (L, DEC_BATCH, WINDOW, ATT_KV_HEADS, HEAD_DIM), 1.0),
        'cache_swa_v': nrm(5, (L, DEC_BATCH, WINDOW, ATT_KV_HEADS, HEAD_DIM), 1.0),
        'state_ffn_conv': nrm(6, (L, DEC_BATCH, CONV_W - 1, D_FF), 1.0),
        'norm_mix_g': 1.0 + nrm(7, (L, D_MODEL), 0.02),
        'w_in': nrm(8, (L, D_MODEL, IN_PROJ), D_MODEL ** -0.5),
        'rwkv_mu': jax.random.uniform(ks[9], (L, RWKV_PROJ), f32),
        'rwkv_w0': jax.random.uniform(ks[10], (L, RWKV_DIM), f32, minval=-6.0, maxval=-1.0),
        'rwkv_w2': nrm(11, (L, D_DECAY_LORA, RWKV_DIM), 0.5 * D_DECAY_LORA ** -0.5),
        'rwkv_a0': nrm(12, (L, RWKV_DIM), 0.1),
        'rwkv_a2': nrm(13, (L, D_AAA_LORA, RWKV_DIM), 0.5 * D_AAA_LORA ** -0.5),
        'rwkv_g2': nrm(14, (L, D_GATE_LORA, RWKV_DIM), D_GATE_LORA ** -0.5),
        'rwkv_k_k': 0.85 + nrm(15, (L, RWKV_DIM), 0.02),
        'rwkv_k_a': 1.0 + nrm(16, (L, RWKV_DIM), 0.02),
        'rwkv_r_k': nrm(17, (L, RWKV_HEADS, HEAD_DIM), 0.1),
        'rwkv_ln_g': 1.0 + nrm(18, (L, RWKV_DIM), 0.02),
        'rwkv_ln_b': nrm(19, (L, RWKV_DIM), 0.01),
        'attn_sinks': nrm(20, (L, ATT_Q_HEADS), 0.5),
        'w_br_rwkv': nrm(21, (L, RWKV_DIM, D_MODEL), RWKV_DIM ** -0.5),
        'w_br_attn': nrm(22, (L, ATT_DIM, D_MODEL), ATT_DIM ** -0.5),
        'w_out': nrm(23, (L, D_MODEL, D_MODEL), D_MODEL ** -0.5),
        'norm_ffn_g': 1.0 + nrm(24, (L, D_MODEL), 0.02),
        'ffn_w_in': nrm(25, (L, D_MODEL, 2 * D_FF), D_MODEL ** -0.5),
        'ffn_conv_w': nrm(26, (L, CONV_W, D_FF), CONV_W ** -0.5),
        'ffn_conv_b': nrm(27, (L, D_FF), 0.01),
        'ffn_w_down': nrm(28, (L, D_FF, D_MODEL), D_FF ** -0.5),
        'norm_final_g': 1.0 + nrm(29, (D_MODEL,), 0.02),
    }


def reference(x_prompt, x_sample, state_rwkv_shift, state_rwkv_wkv, cache_swa_k, cache_swa_v,
              state_ffn_conv, norm_mix_g, w_in, rwkv_mu, rwkv_w0, rwkv_w2, rwkv_a0, rwkv_a2, rwkv_g2,
              rwkv_k_k, rwkv_k_a, rwkv_r_k, rwkv_ln_g, rwkv_ln_b, attn_sinks, w_br_rwkv, w_br_attn,
              w_out, norm_ffn_g, ffn_w_in, ffn_conv_w, ffn_conv_b, ffn_w_down, norm_final_g):
    Bp, Tp, _ = x_prompt.shape
    pos_p = jnp.arange(Tp, dtype=jnp.int32)
    pos_s = PAST_LEN + jnp.arange(x_sample.shape[1], dtype=jnp.int32)
    xp, xs = x_prompt, x_sample
    outs_p, outs_s = [], []
    for l in range(DEPTH):
        P = {
            'norm_mix_g': norm_mix_g[l], 'w_in': w_in[l], 'rwkv_mu': rwkv_mu[l],
            'rwkv_w0': rwkv_w0[l], 'rwkv_w2': rwkv_w2[l], 'rwkv_a0': rwkv_a0[l], 'rwkv_a2': rwkv_a2[l],
            'rwkv_g2': rwkv_g2[l], 'rwkv_k_k': rwkv_k_k[l], 'rwkv_k_a': rwkv_k_a[l],
            'rwkv_r_k': rwkv_r_k[l], 'rwkv_ln_g': rwkv_ln_g[l], 'rwkv_ln_b': rwkv_ln_b[l],
            'attn_sinks': attn_sinks[l], 'w_br_rwkv': w_br_rwkv[l], 'w_br_attn': w_br_attn[l],
            'w_out': w_out[l], 'norm_ffn_g': norm_ffn_g[l], 'ffn_w_in': ffn_w_in[l],
            'ffn_conv_w': ffn_conv_w[l], 'ffn_conv_b': ffn_conv_b[l], 'ffn_w_down': ffn_w_down[l],
        }
        xp, sp = layer(xp, pos_p,
                       jnp.zeros((Bp, RWKV_PROJ), xp.dtype),
                       jnp.zeros((Bp, RWKV_HEADS, HEAD_DIM, HEAD_DIM), jnp.float32),
                       None, None,
                       jnp.zeros((Bp, CONV_W - 1, D_FF), xp.dtype), P)
        xs, ss = layer(xs, pos_s, state_rwkv_shift[l], state_rwkv_wkv[l], cache_swa_k[l],
                       cache_swa_v[l], state_ffn_conv[l], P)
        outs_p.append(sp)
        outs_s.append(ss)
    y_prompt = rmsnorm(xp, norm_final_g)
    y_sample = rmsnorm(xs, norm_final_g)
    p_shift = jnp.stack([o[0] for o in outs_p])
    p_wkv = jnp.stack([o[1] for o in outs_p])
    p_k = jnp.stack([o[2] for o in outs_p])
    p_v = jnp.stack([o[3] for o in outs_p])
    p_conv = jnp.stack([o[4] for o in outs_p])
    s_shift = jnp.stack([o[0] for o in outs_s])
    s_wkv = jnp.stack([o[1] for o in outs_s])
    s_k = jnp.stack([o[2] for o in outs_s])
    s_v = jnp.stack([o[3] for o in outs_s])
    s_conv = jnp.stack([o[4] for o in outs_s])
    return (y_prompt, y_sample, p_shift, p_wkv, p_k, p_v, p_conv, s_shift, s_wkv, s_k, s_v, s_conv)
```

```python
import functools

import jax
import jax.numpy as jnp
from jax import lax
from jax.experimental import pallas as pl
from jax.experimental.pallas import tpu as pltpu

F32 = jnp.float32
BF16 = jnp.bfloat16

D_MODEL = 1024
HEAD_DIM = 64
RWKV_HEADS = 8
RWKV_DIM = RWKV_HEADS * HEAD_DIM
D_DECAY_LORA = 64
D_AAA_LORA = 64
D_GATE_LORA = 128
RWKV_PROJ = 3 * RWKV_DIM + D_DECAY_LORA + D_AAA_LORA + D_GATE_LORA
ATT_Q_HEADS = 8
ATT_KV_HEADS = 2
ATT_GROUP = ATT_Q_HEADS // ATT_KV_HEADS
ATT_DIM = ATT_Q_HEADS * HEAD_DIM
ATT_KV_DIM = ATT_KV_HEADS * HEAD_DIM
WINDOW = 128
IN_PROJ = RWKV_PROJ + ATT_DIM + 2 * ATT_KV_DIM + 2 * D_MODEL
D_FF = 2816
CONV_W = 3
ROPE_THETA = 10000.0
RMS_EPS = 1e-6
GN_EPS = 64e-5
PAST_LEN = 16384

LANES = 128
SUBLANES = 8
MXU_DIM = 256
CHUNK = 64
HEADS_PER_GROUP = MXU_DIM // HEAD_DIM
VMEM_LIMIT = 56 * 1024 * 1024


def _params(sem):
    return pltpu.CompilerParams(dimension_semantics=sem, vmem_limit_bytes=VMEM_LIMIT)


def _const_spec(shape):
    zeros = (0,) * len(shape)
    return pl.BlockSpec(shape, lambda *_: zeros, pipeline_mode=pl.Buffered(1))


def _mm(a, b):
    return jnp.dot(a.astype(BF16), b.astype(BF16), preferred_element_type=F32)


def _mm_nt(a, b):
    return lax.dot_general(a.astype(BF16), b.astype(BF16), (((1,), (1,)), ((), ())),
                           preferred_element_type=F32)


def _mm_f32(a, b):
    return jnp.dot(a, b, precision=lax.Precision.HIGHEST, preferred_element_type=F32)


def _mm_nt_f32(a, b):
    return lax.dot_general(a, b, (((1,), (1,)), ((), ())), precision=lax.Precision.HIGHEST,
                           preferred_element_type=F32)


def _mm_tn_f32(a, b):
    return lax.dot_general(a, b, (((0,), (0,)), ((), ())), precision=lax.Precision.HIGHEST,
                           preferred_element_type=F32)


def _head_sum(x, ones_bd):
    hi = x.astype(BF16)
    lo = (x - hi.astype(F32)).astype(BF16)
    return (jnp.dot(hi, ones_bd, preferred_element_type=F32)
            + jnp.dot(lo, ones_bd, preferred_element_type=F32))


def _rmsnorm(x, g):
    return x * lax.rsqrt(jnp.mean(x * x, axis=-1, keepdims=True) + RMS_EPS) * g


def _rope(x, cos, sin_signed, first_half):
    partner = jnp.where(first_half, pltpu.roll(x, LANES - HEAD_DIM // 2, 1), pltpu.roll(x, HEAD_DIM // 2, 1))
    return x * cos + partner * sin_signed


def _proj_kernel(x_ref, g_ref, w_ref, cos_ref, sin_ref, z_ref, q_ref, k_ref, v_ref, gr_ref, ga_ref):
    h = _rmsnorm(x_ref[...], g_ref[...]).astype(BF16)
    c0, c1, c2, c3, c4 = (RWKV_PROJ, RWKV_PROJ + ATT_DIM, RWKV_PROJ + ATT_DIM + ATT_KV_DIM,
                          RWKV_PROJ + ATT_DIM + 2 * ATT_KV_DIM, RWKV_PROJ + ATT_DIM + 2 * ATT_KV_DIM + D_MODEL)
    z_ref[...] = jnp.dot(h, w_ref[:, 0:c0], preferred_element_type=F32)
    cos = cos_ref[...]
    sin = sin_ref[...]
    lane = lax.broadcasted_iota(jnp.int32, (1, LANES), 1)
    first_half = (lane % HEAD_DIM) < (HEAD_DIM // 2)
    q = jnp.dot(h, w_ref[:, c0:c1], preferred_element_type=F32)
    for grp in range(ATT_DIM // LANES):
        sl = slice(grp * LANES, (grp + 1) * LANES)
        q_ref[:, sl] = _rope(q[:, sl], cos, sin, first_half)
    k = jnp.dot(h, w_ref[:, c1:c2], preferred_element_type=F32)
    k_ref[...] = _rope(k, cos, sin, first_half)
    v_ref[...] = jnp.dot(h, w_ref[:, c2:c3], preferred_element_type=F32)
    gr_ref[...] = jax.nn.sigmoid(jnp.dot(h, w_ref[:, c3:c4], preferred_element_type=F32))
    ga_ref[...] = jax.nn.sigmoid(jnp.dot(h, w_ref[:, c4:IN_PROJ], preferred_element_type=F32))


def _proj(x2d, g, w_bf16, cos_t, sin_t, tm):
    n = x2d.shape[0]
    npos = cos_t.shape[0] // tm
    row = lambda i: (i, 0)
    out_w = (RWKV_PROJ, ATT_DIM, ATT_KV_DIM, ATT_KV_DIM, D_MODEL, D_MODEL)
    return pl.pallas_call(
        _proj_kernel,
        grid=(n // tm,),
        in_specs=[pl.BlockSpec((tm, D_MODEL), row), _const_spec((1, D_MODEL)),
                  _const_spec((D_MODEL, IN_PROJ)),
                  pl.BlockSpec((tm, LANES), lambda i: (i % npos, 0)),
                  pl.BlockSpec((tm, LANES), lambda i: (i % npos, 0))],
        out_specs=[pl.BlockSpec((tm, w), row) for w in out_w],
        out_shape=[jax.ShapeDtypeStruct((n, w), F32) for w in out_w],
        compiler_params=_params(("arbitrary",)),
        name="proj",
    )(x2d, g, w_bf16, cos_t, sin_t)


def _prep_math(z, zp, mu, w0, w2p, a0, a2p, g2, k_k, k_a, r_k, ones_bd, outs):
    r_ref, lw_ref, kk_ref, b_ref, kf_ref, v_ref, g_ref, bonus_ref = outs
    zs = z + mu * (zp - z)
    d = RWKV_DIM
    r, k, v = zs[:, 0:d], zs[:, d:2 * d], zs[:, 2 * d:3 * d]
    zwa = zs[:, 3 * d:3 * d + LANES]
    zg = zs[:, 3 * d + LANES:3 * d + 2 * LANES]
    w_pre = w0 + _mm(jnp.tanh(zwa), w2p)
    neg = -w_pre
    softplus = jnp.maximum(neg, 0.0) + jnp.log1p(jnp.exp(-jnp.abs(neg)))
    w_log = -softplus - 0.5
    lw_ref[...] = -jnp.exp(w_log)
    a = jax.nn.sigmoid(a0 + _mm(zwa, a2p))
    g_ref[...] = _mm(jax.nn.sigmoid(zg), g2)
    kk0 = k * k_k
    norm = jnp.sqrt(_head_sum(kk0 * kk0, ones_bd))
    kk = kk0 / jnp.maximum(norm, 1e-12)
    kf = k * (1.0 + (a - 1.0) * k_a)
    r_ref[...] = r
    kk_ref[...] = kk
    b_ref[...] = kk * a
    kf_ref[...] = kf
    v_ref[...] = v
    bonus_ref[...] = _head_sum(r * kf * r_k, ones_bd) * v


def _prep_prompt_kernel(z_ref, mu_ref, w0_ref, w2_ref, a0_ref, a2_ref, g2_ref, kk_ref_, ka_ref, rk_ref, ones_ref,
                        *rest):
    outs, carry = rest[:8], rest[8]
    t = pl.program_id(1)

    @pl.when(t == 0)
    def _():
        carry[...] = jnp.zeros_like(carry)

    z = z_ref[...]
    row = lax.broadcasted_iota(jnp.int32, (z.shape[0], 1), 0)
    zp = jnp.where(row == 0, carry[SUBLANES - 1:SUBLANES, :], pltpu.roll(z, 1, 0))
    carry[...] = z[z.shape[0] - SUBLANES:, :]
    _prep_math(z, zp, mu_ref[...], w0_ref[...], w2_ref[...], a0_ref[...], a2_ref[...], g2_ref[...],
               kk_ref_[...], ka_ref[...], rk_ref[...], ones_ref[...], outs)


def _prep_sample_kernel(z_ref, sp_ref, mu_ref, w0_ref, w2_ref, a0_ref, a2_ref, g2_ref, kk_ref_, ka_ref, rk_ref,
                        ones_ref, *outs):
    z = z_ref[...]
    nb = sp_ref.shape[0]
    zp = jnp.concatenate([sp_ref[...], z[:z.shape[0] - nb, :]], axis=0)
    _prep_math(z, zp, mu_ref[...], w0_ref[...], w2_ref[...], a0_ref[...], a2_ref[...], g2_ref[...],
               kk_ref_[...], ka_ref[...], rk_ref[...], ones_ref[...], outs)


def _prep_consts(P):
    return [P['mu'], P['w0'], P['w2p'], P['a0'], P['a2p'], P['g2'], P['k_k'], P['k_a'], P['r_k'], P['ones_bd']]


def _prep_const_specs():
    d = RWKV_DIM
    shapes = [(1, RWKV_PROJ), (1, d), (LANES, d), (1, d), (LANES, d), (D_GATE_LORA, d), (1, d), (1, d), (1, d),
              (d, d)]
    return [_const_spec(s) for s in shapes]


def _prep_prompt(z2d, P, batch, tm):
    n = z2d.shape[0]
    nt = n // batch // tm
    row = lambda b, t: (b * nt + t, 0)
    return pl.pallas_call(
        _prep_prompt_kernel,
        grid=(batch, nt),
        in_specs=[pl.BlockSpec((tm, RWKV_PROJ), row)] + _prep_const_specs(),
        out_specs=[pl.BlockSpec((tm, RWKV_DIM), row)] * 8,
        out_shape=[jax.ShapeDtypeStruct((n, RWKV_DIM), F32)] * 8,
        scratch_shapes=[pltpu.VMEM((SUBLANES, RWKV_PROJ), F32)],
        compiler_params=_params(("arbitrary", "arbitrary")),
        name="prep_prompt",
    )(z2d, *_prep_consts(P))


def _prep_sample(z2d, shift_prev, P):
    n = z2d.shape[0]
    nb = shift_prev.shape[0]
    return pl.pallas_call(
        _prep_sample_kernel,
        grid=(1,),
        in_specs=[_const_spec((n, RWKV_PROJ)), _const_spec((nb, RWKV_PROJ))] + _prep_const_specs(),
        out_specs=[_const_spec((n, RWKV_DIM))] * 8,
        out_shape=[jax.ShapeDtypeStruct((n, RWKV_DIM), F32)] * 8,
        compiler_params=_params(("arbitrary",)),
        name="prep_sample",
    )(z2d, shift_prev, *_prep_consts(P))


def _wkv_prompt_kernel(r_ref, lw_ref, kk_ref, b_ref, kf_ref, v_ref, y_ref, st_ref, s_scr, *, chunks):
    C, G, M = CHUNK, HEADS_PER_GROUP, MXU_DIM

    @pl.when(pl.program_id(1) == 0)
    def _():
        s_scr[...] = jnp.zeros_like(s_scr)

    ri = lax.broadcasted_iota(jnp.int32, (M, M), 0)
    ci = lax.broadcasted_iota(jnp.int32, (M, M), 1)
    strict = (ri % C) > (ci % C)
    incl = (ri % C) >= (ci % C)
    eye = (ri == ci).astype(F32)
    ti = lax.broadcasted_iota(jnp.int32, (C, C), 0)
    si = lax.broadcasted_iota(jnp.int32, (C, C), 1)
    tri = (ti >= si).astype(F32)
    lane_head = lax.broadcasted_iota(jnp.int32, (1, M), 1) // HEAD_DIM

    def stack(x):
        return jnp.concatenate([jnp.where(lane_head == h, x, 0.0) for h in range(G)], axis=0)

    def chunk_body(c, carry):
        rows = pl.ds(pl.multiple_of(c * C, C), C)
        lw = lw_ref[rows, :]
        cum = _mm_f32(tri, lw)
        w_end = jnp.exp(cum[C - 1:C, :])
        e_pos = jnp.exp(cum)
        e_neg = jnp.exp(-cum)
        a_hat = -kk_ref[rows, :] * jnp.exp(cum - lw)
        r_hat = r_ref[rows, :] * e_pos
        b_til = b_ref[rows, :] * e_neg
        k_til = kf_ref[rows, :] * e_neg
        v = v_ref[rows, :]
        for grp in range(RWKV_HEADS // G):
            ls = slice(grp * M, (grp + 1) * M)
            a_s, r_s, b_s, k_s, v_s = (stack(x[:, ls]) for x in (a_hat, r_hat, b_til, k_til, v))
            we = w_end[:, ls]
            l_ab = jnp.where(strict, _mm_nt_f32(a_s, b_s), 0.0)
            l_ak = jnp.where(strict, _mm_nt_f32(a_s, k_s), 0.0)
            p_rb = jnp.where(incl, _mm_nt_f32(r_s, b_s), 0.0)
            p_rk = jnp.where(incl, _mm_nt_f32(r_s, k_s), 0.0)
            t_inv = eye + l_ab
            pw = l_ab
            for _ in range(C.bit_length() - 2):
                pw = _mm_f32(pw, pw)
                t_inv = t_inv + _mm_f32(pw, t_inv)
            s0 = s_scr[grp]
            u = _mm_f32(t_inv, _mm_nt_f32(a_s, s0) + _mm_f32(l_ak, v_s))
            y_s = _mm_nt_f32(r_s, s0) + _mm_f32(p_rb, u) + _mm_f32(p_rk, v_s)
            y = y_s[0:C]
            for h in range(1, G):
                y = y + y_s[h * C:(h + 1) * C]
            y_ref[rows, ls] = y
            s_scr[grp] = s0 * we + _mm_tn_f32(u, b_s * we) + _mm_tn_f32(v_s, k_s * we)
        return carry

    lax.fori_loop(0, chunks, chunk_body, 0)

    @pl.when(pl.program_id(1) == pl.num_programs(1) - 1)
    def _():
        st_ref[0] = s_scr[...]


def _wkv_prompt(r, lw, kk, b, kf, v, batch, tl):
    n = r.shape[0]
    nt = n // batch // tl
    ngrp = RWKV_HEADS // HEADS_PER_GROUP
    row = lambda bi, t: (bi * nt + t, 0)
    y, st = pl.pallas_call(
        functools.partial(_wkv_prompt_kernel, chunks=tl // CHUNK),
        grid=(batch, nt),
        in_specs=[pl.BlockSpec((tl, RWKV_DIM), row)] * 6,
        out_specs=[pl.BlockSpec((tl, RWKV_DIM), row),
                   pl.BlockSpec((1, ngrp, MXU_DIM, MXU_DIM), lambda bi, t: (bi, 0, 0, 0))],
        out_shape=[jax.ShapeDtypeStruct((n, RWKV_DIM), F32),
                   jax.ShapeDtypeStruct((batch, ngrp, MXU_DIM, MXU_DIM), F32)],
        scratch_shapes=[pltpu.VMEM((ngrp, MXU_DIM, MXU_DIM), F32)],
        compiler_params=_params(("arbitrary", "arbitrary")),
        name="wkv_prompt",
    )(r, lw, kk, b, kf, v)
    st = st.reshape(batch, ngrp, HEADS_PER_GROUP, HEAD_DIM, HEADS_PER_GROUP, HEAD_DIM)
    idx = jnp.arange(HEADS_PER_GROUP)
    st = st[:, :, idx, :, idx, :]
    st = jnp.moveaxis(st, 0, 2).reshape(batch, RWKV_HEADS, HEAD_DIM, HEAD_DIM)
    return y, st


def _wkv_sample_kernel(r_ref, lw_ref, kk_ref, b_ref, kf_ref, v_ref, s_ref, y_ref, so_ref, *, steps):
    def value_row(vi, carry):
        s = s_ref[0, vi]
        for t in range(steps):
            sk = jnp.sum(s * kk_ref[t, 0], axis=0, keepdims=True)
            v_row = v_ref[t, 0, pl.ds(vi, 1), :]
            s = s * jnp.exp(lw_ref[t, 0]) - sk * b_ref[t, 0] + v_row * kf_ref[t, 0]
            y_ref[t, 0, pl.ds(vi, 1), :] = jnp.sum(s * r_ref[t, 0], axis=0, keepdims=True)
        so_ref[0, vi] = s
        return carry

    lax.fori_loop(0, HEAD_DIM, value_row, 0)


def _wkv_sample(r, lw, kk, b, kf, v, s0, steps):
    nb = r.shape[-1]
    vec = pl.BlockSpec((steps, 1, HEAD_DIM, nb), lambda h: (0, h, 0, 0))
    st = pl.BlockSpec((1, HEAD_DIM, HEAD_DIM, nb), lambda h: (h, 0, 0, 0))
    return pl.pallas_call(
        functools.partial(_wkv_sample_kernel, steps=steps),
        grid=(RWKV_HEADS,),
        in_specs=[vec] * 6 + [st],
        out_specs=[vec, st],
        out_shape=[jax.ShapeDtypeStruct(r.shape, F32), jax.ShapeDtypeStruct(s0.shape, F32)],
        compiler_params=_params(("arbitrary",)),
        name="wkv_sample",
    )(r, lw, kk, b, kf, v, s0)


def _swa_prompt_kernel(q_ref, kc_ref, kp_ref, vc_ref, vp_ref, sink_ref, o_ref):
    blk = q_ref.shape[0]
    first = pl.program_id(1) == 0
    k_all = jnp.concatenate([kp_ref[...], kc_ref[...]], axis=0)
    v_all = jnp.concatenate([vp_ref[...], vc_ref[...]], axis=0).astype(BF16)
    k_swapped = pltpu.roll(k_all, HEAD_DIM, 1)
    lane = lax.broadcasted_iota(jnp.int32, (1, LANES), 1)
    low = lane < HEAD_DIM
    k_var = {(0, 0): jnp.where(low, k_all, 0.0), (0, 1): jnp.where(low, 0.0, k_swapped),
             (1, 1): jnp.where(low, 0.0, k_all), (1, 0): jnp.where(low, k_swapped, 0.0)}
    k_var = {key: val.astype(BF16) for key, val in k_var.items()}
    qi = lax.broadcasted_iota(jnp.int32, (blk, 2 * blk), 0)
    kj = lax.broadcasted_iota(jnp.int32, (blk, 2 * blk), 1)
    diff = blk + qi - kj
    ok = (diff >= 0) & (diff < WINDOW) & jnp.logical_or(kj >= blk, jnp.logical_not(first))
    for grp in range(ATT_DIM // LANES):
        q_grp = q_ref[:, grp * LANES:(grp + 1) * LANES].astype(BF16)
        out = None
        for half in range(2):
            head = grp * 2 + half
            j = head // ATT_GROUP
            s = lax.dot_general(q_grp, k_var[(j, half)], (((1,), (1,)), ((), ())),
                                preferred_element_type=F32) * (HEAD_DIM ** -0.5)
            s = jnp.where(ok, s, -jnp.inf)
            sink = sink_ref[0:1, head:head + 1]
            m = jnp.maximum(jnp.max(s, axis=-1, keepdims=True), sink)
            e = jnp.exp(s - m)
            p = e / (jnp.sum(e, axis=-1, keepdims=True) + jnp.exp(sink - m))
            o = jnp.dot(p.astype(BF16), v_all, preferred_element_type=F32)
            if j != half:
                o = pltpu.roll(o, HEAD_DIM, 1)
            out = o if out is None else jnp.where(low, out, o)
        o_ref[:, grp * LANES:(grp + 1) * LANES] = out


def _swa_prompt(q, k, v, sinks, batch):
    n = q.shape[0]
    nb = n // batch // WINDOW
    cur = lambda b, i: (b * nb + i, 0)
    prev = lambda b, i: (b * nb + jnp.maximum(i - 1, 0), 0)
    return pl.pallas_call(
        _swa_prompt_kernel,
        grid=(batch, nb),
        in_specs=[pl.BlockSpec((WINDOW, ATT_DIM), cur),
                  pl.BlockSpec((WINDOW, ATT_KV_DIM), cur), pl.BlockSpec((WINDOW, ATT_KV_DIM), prev),
                  pl.BlockSpec((WINDOW, ATT_KV_DIM), cur), pl.BlockSpec((WINDOW, ATT_KV_DIM), prev),
                  _const_spec((1, ATT_Q_HEADS))],
        out_specs=pl.BlockSpec((WINDOW, ATT_DIM), cur),
        out_shape=jax.ShapeDtypeStruct((n, ATT_DIM), F32),
        compiler_params=_params(("arbitrary", "arbitrary")),
        name="swa_prompt",
    )(q, k, k, v, v, sinks)


def _swa_sample_kernel(q_ref, kc_ref, vc_ref, kn_ref, vn_ref, sink_ref, o_ref, *, steps):
    q = q_ref[...].astype(BF16)
    rows = q.shape[1]
    nk = kc_ref.shape[1]
    nn = kn_ref.shape[1]
    bdot = lambda a, b, dims: lax.dot_general(a, b, (dims, ((0,), (0,))), preferred_element_type=F32)
    s_c = bdot(q, kc_ref[...].astype(BF16), ((2,), (2,))) * (HEAD_DIM ** -0.5)
    s_n = bdot(q, kn_ref[...].astype(BF16), ((2,), (2,))) * (HEAD_DIM ** -0.5)
    t_c = lax.broadcasted_iota(jnp.int32, (1, rows, nk), 1) % steps
    w_c = lax.broadcasted_iota(jnp.int32, (1, rows, nk), 2)
    s_c = jnp.where(w_c > t_c, s_c, -jnp.inf)
    t_n = lax.broadcasted_iota(jnp.int32, (1, rows, nn), 1) % steps
    w_n = lax.broadcasted_iota(jnp.int32, (1, rows, nn), 2)
    s_n = jnp.where(w_n <= t_n, s_n, -jnp.inf)
    sink = sink_ref[...][None, :, 0:1]
    m = jnp.maximum(jnp.maximum(jnp.max(s_c, axis=-1, keepdims=True), jnp.max(s_n, axis=-1, keepdims=True)), sink)
    e_c = jnp.exp(s_c - m)
    e_n = jnp.exp(s_n - m)
    den = jnp.sum(e_c, axis=-1, keepdims=True) + jnp.sum(e_n, axis=-1, keepdims=True) + jnp.exp(sink - m)
    o = (bdot((e_c / den).astype(BF16), vc_ref[...].astype(BF16), ((2,), (1,)))
         + bdot((e_n / den).astype(BF16), vn_ref[...].astype(BF16), ((2,), (1,))))
    o_ref[...] = o


def _swa_sample(q_st, k_cache, v_cache, k_new, v_new, sink_rows, steps, bt):
    nb, rows, _ = q_st.shape
    nn = k_new.shape[1]
    blk = lambda r: pl.BlockSpec((bt, r, LANES), lambda i: (i, 0, 0))
    return pl.pallas_call(
        functools.partial(_swa_sample_kernel, steps=steps),
        grid=(nb // bt,),
        in_specs=[blk(rows), blk(WINDOW), blk(WINDOW), blk(nn), blk(nn), _const_spec((rows, LANES))],
        out_specs=blk(rows),
        out_shape=jax.ShapeDtypeStruct((nb, rows, LANES), F32),
        compiler_params=_params(("arbitrary",)),
        name="swa_sample",
    )(q_st, k_cache, v_cache, k_new, v_new, sink_rows)


def _merge_kernel(x_ref, y_ref, bonus_ref, g_ref, oa_ref, gr_ref, ga_ref, lng_ref, lnb_ref, ones_ref,
                  wa_ref, wb_ref, wo_ref, o_ref):
    y = y_ref[...]
    ones_bd = ones_ref[...]
    mean = _head_sum(y, ones_bd) * (1.0 / HEAD_DIM)
    d = y - mean
    var = _head_sum(d * d, ones_bd) * (1.0 / HEAD_DIM)
    yn = d * lax.rsqrt(var + GN_EPS) * lng_ref[...] + lnb_ref[...]
    o_r = (yn + bonus_ref[...]) * g_ref[...]
    merged = gr_ref[...] * _mm(o_r, wa_ref[...]) + ga_ref[...] * _mm(oa_ref[...], wb_ref[...])
    o_ref[...] = x_ref[...] + _mm(merged, wo_ref[...])


def _merge(x, y, bonus, g, o_a, gr, ga, P, tm):
    n = x.shape[0]
    row = lambda i: (i, 0)
    wide = pl.BlockSpec((tm, D_MODEL), row)
    half = pl.BlockSpec((tm, RWKV_DIM), row)
    return pl.pallas_call(
        _merge_kernel,
        grid=(n // tm,),
        in_specs=[wide, half, half, half, half, wide, wide,
                  _const_spec((1, RWKV_DIM)), _const_spec((1, RWKV_DIM)), _const_spec((RWKV_DIM, RWKV_DIM)),
                  _const_spec((RWKV_DIM, D_MODEL)), _const_spec((ATT_DIM, D_MODEL)),
                  _const_spec((D_MODEL, D_MODEL))],
        out_specs=wide,
        out_shape=jax.ShapeDtypeStruct((n, D_MODEL), F32),
        compiler_params=_params(("arbitrary",)),
        name="merge",
    )(x, y, bonus, g, o_a, gr, ga, P['ln_g'], P['ln_b'], P['ones_bd'], P['w_br_rwkv'], P['w_br_attn'], P['w_out'])


def _gelu(x):
    return 0.5 * x * (1.0 + lax.erf(x * (2.0 ** -0.5)))


def _ffn_tail(x, c, up, c1, c2, cw_ref, cb_ref, wd_ref, gf_ref, o_ref, final):
    conv = cb_ref[...] + c2 * cw_ref[0:1, :] + c1 * cw_ref[1:2, :] + c * cw_ref[2:3, :]
    out = x + _mm(_gelu(conv) * up, wd_ref[...])
    o_ref[...] = _rmsnorm(out, gf_ref[...]) if final else out


def _ffn_prompt_kernel(x_ref, gn_ref, wi_ref, cw_ref, cb_ref, wd_ref, gf_ref, o_ref, ct_ref, carry, *, final):
    @pl.when(pl.program_id(1) == 0)
    def _():
        carry[...] = jnp.zeros_like(carry)

    x = x_ref[...]
    h = _rmsnorm(x, gn_ref[...]).astype(BF16)
    c = jnp.dot(h, wi_ref[:, 0:D_FF], preferred_element_type=F32)
    up = jnp.dot(h, wi_ref[:, D_FF:2 * D_FF], preferred_element_type=F32)
    tm = c.shape[0]
    row = lax.broadcasted_iota(jnp.int32, (tm, 1), 0)
    last = carry[SUBLANES - 1:SUBLANES, :]
    c1 = jnp.where(row == 0, last, pltpu.roll(c, 1, 0))
    c2 = jnp.where(row == 0, carry[SUBLANES - 2:SUBLANES - 1, :],
                   jnp.where(row == 1, last, pltpu.roll(c, 2, 0)))
    tail = c[tm - SUBLANES:, :]
    carry[...] = tail
    ct_ref[0] = tail
    _ffn_tail(x, c, up, c1, c2, cw_ref, cb_ref, wd_ref, gf_ref, o_ref, final)


def _ffn_sample_kernel(x_ref, cp_ref, gn_ref, wi_ref, cw_ref, cb_ref, wd_ref, gf_ref, o_ref, ct_ref, *, final):
    x = x_ref[...]
    h = _rmsnorm(x, gn_ref[...]).astype(BF16)
    c = jnp.dot(h, wi_ref[:, 0:D_FF], preferred_element_type=F32)
    up = jnp.dot(h, wi_ref[:, D_FF:2 * D_FF], preferred_element_type=F32)
    n = c.shape[0]
    nb = cp_ref.shape[1]
    p0, p1 = cp_ref[0], cp_ref[1]
    c1 = jnp.concatenate([p1, c[:n - nb]], axis=0)
    c2 = jnp.concatenate([p0, p1, c[:n - 2 * nb]], axis=0)
    ct_ref[0] = c[n - 2 * nb:n - nb, :]
    ct_ref[1] = c[n - nb:, :]
    _ffn_tail(x, c, up, c1, c2, cw_ref, cb_ref, wd_ref, gf_ref, o_ref, final)


def _ffn_weight_specs():
    return [_const_spec((1, D_MODEL)), _const_spec((D_MODEL, 2 * D_FF)), _const_spec((CONV_W, D_FF)),
            _const_spec((1, D_FF)), _const_spec((D_FF, D_MODEL)), _const_spec((1, D_MODEL))]


def _ffn_weights(P, g_final):
    return [P['norm_ffn_g'], P['ffn_w_in'], P['ffn_conv_w'], P['ffn_conv_b'], P['ffn_w_down'], g_final]


def _ffn_prompt(x, P, g_final, final, batch, tm):
    n = x.shape[0]
    nt = n // batch // tm
    row = lambda b, t: (b * nt + t, 0)
    return pl.pallas_call(
        functools.partial(_ffn_prompt_kernel, final=final),
        grid=(batch, nt),
        in_specs=[pl.BlockSpec((tm, D_MODEL), row)] + _ffn_weight_specs(),
        out_specs=[pl.BlockSpec((tm, D_MODEL), row), pl.BlockSpec((1, SUBLANES, D_FF), lambda b, t: (b, 0, 0))],
        out_shape=[jax.ShapeDtypeStruct((n, D_MODEL), F32), jax.ShapeDtypeStruct((batch, SUBLANES, D_FF), F32)],
        scratch_shapes=[pltpu.VMEM((SUBLANES, D_FF), F32)],
        compiler_params=_params(("arbitrary", "arbitrary")),
        name="ffn_prompt",
    )(x, *_ffn_weights(P, g_final))


def _ffn_sample(x, conv_prev, P, g_final, final):
    n = x.shape[0]
    nb = conv_prev.shape[1]
    return pl.pallas_call(
        functools.partial(_ffn_sample_kernel, final=final),
        grid=(1,),
        in_specs=[_const_spec((n, D_MODEL)), _const_spec((CONV_W - 1, nb, D_FF))] + _ffn_weight_specs(),
        out_specs=[_const_spec((n, D_MODEL)), _const_spec((CONV_W - 1, nb, D_FF))],
        out_shape=[jax.ShapeDtypeStruct((n, D_MODEL), F32), jax.ShapeDtypeStruct((CONV_W - 1, nb, D_FF), F32)],
        compiler_params=_params(("arbitrary",)),
        name="ffn_sample",
    )(x, conv_prev, *_ffn_weights(P, g_final))


def _rope_tables(pos):
    inv = ROPE_THETA ** (-jnp.arange(0, HEAD_DIM, 2, dtype=F32) / HEAD_DIM)
    ang = pos.astype(F32)[:, None] * inv[None, :]
    cos, sin = jnp.cos(ang), jnp.sin(ang)
    reps = LANES // HEAD_DIM
    return jnp.tile(jnp.concatenate([cos, cos], axis=1), (1, reps)), jnp.tile(jnp.concatenate([-sin, sin], axis=1),
                                                                             (1, reps))


def _layer_params(l, norm_mix_g, w_in, rwkv_mu, rwkv_w0, rwkv_w2, rwkv_a0, rwkv_a2, rwkv_g2, rwkv_k_k, rwkv_k_a,
                  rwkv_r_k, rwkv_ln_g, rwkv_ln_b, attn_sinks, w_br_rwkv, w_br_attn, w_out, norm_ffn_g, ffn_w_in,
                  ffn_conv_w, ffn_conv_b, ffn_w_down):
    d = RWKV_DIM
    head = jnp.arange(d) // HEAD_DIM
    zeros_w = jnp.zeros((D_DECAY_LORA, d), F32)
    zeros_a = jnp.zeros((D_AAA_LORA, d), F32)
    return {
        'norm_mix_g': norm_mix_g[l][None], 'w_in': w_in[l].astype(BF16),
        'mu': rwkv_mu[l][None], 'w0': rwkv_w0[l][None],
        'w2p': jnp.concatenate([rwkv_w2[l], zeros_a], axis=0).astype(BF16),
        'a0': rwkv_a0[l][None],
        'a2p': jnp.concatenate([zeros_w, rwkv_a2[l]], axis=0).astype(BF16),
        'g2': rwkv_g2[l].astype(BF16), 'k_k': rwkv_k_k[l][None], 'k_a': rwkv_k_a[l][None],
        'r_k': rwkv_r_k[l].reshape(1, d), 'ln_g': rwkv_ln_g[l][None], 'ln_b': rwkv_ln_b[l][None],
        'ones_bd': (head[:, None] == head[None, :]).astype(BF16),
        'sinks': attn_sinks[l][None],
        'w_br_rwkv': w_br_rwkv[l].astype(BF16), 'w_br_attn': w_br_attn[l].astype(BF16),
        'w_out': w_out[l].astype(BF16), 'norm_ffn_g': norm_ffn_g[l][None],
        'ffn_w_in': ffn_w_in[l].astype(BF16), 'ffn_conv_w': ffn_conv_w[l], 'ffn_conv_b': ffn_conv_b[l][None],
        'ffn_w_down': ffn_w_down[l].astype(BF16),
    }


def _prompt_layer(x, P, tables, g_final, final, batch, seq):
    tm = min(256, seq)
    z, q, k, v, gr, ga = _proj(x, P['norm_mix_g'], P['w_in'], tables[0], tables[1], tm)
    r, lw, kk, b, kf, vv, g, bonus = _prep_prompt(z, P, batch, tm)
    y, s_t = _wkv_prompt(r, lw, kk, b, kf, vv, batch, min(512, seq))
    o_a = _swa_prompt(q, k, v, P['sinks'], batch)
    x1 = _merge(x, y, bonus, g, o_a, gr, ga, P, tm)
    x2, c_tail = _ffn_prompt(x1, P, g_final, final, batch, tm)
    last = lambda t: t.reshape(batch, seq, -1)[:, -WINDOW:].reshape(batch, WINDOW, ATT_KV_HEADS, HEAD_DIM)
    state = (z.reshape(batch, seq, RWKV_PROJ)[:, -1], s_t, last(k), last(v), c_tail[:, SUBLANES - (CONV_W - 1):])
    return x2, state


def _sample_layer(x, P, tables, g_final, final, shift_prev, wkv0, k_cache, v_cache, conv_prev, nb, steps):
    n = nb * steps
    z, q, k, v, gr, ga = _proj(x, P['norm_mix_g'], P['w_in'], tables[0], tables[1], n)
    r, lw, kk, b, kf, vv, g, bonus = _prep_sample(z, shift_prev, P)
    to_lanes = lambda t: t.reshape(steps, nb, RWKV_HEADS, HEAD_DIM).transpose(0, 2, 3, 1)
    y_l, s_l = _wkv_sample(*(to_lanes(t) for t in (r, lw, kk, b, kf, vv)), wkv0.transpose(1, 2, 3, 0), steps)
    y = y_l.transpose(0, 3, 1, 2).reshape(n, RWKV_DIM)
    s_new = s_l.transpose(3, 0, 1, 2)

    rows = ATT_Q_HEADS * steps
    q5 = q.reshape(steps, nb, ATT_KV_HEADS, ATT_GROUP, HEAD_DIM).transpose(1, 2, 3, 0, 4)
    sel = jnp.eye(ATT_KV_HEADS, dtype=F32)
    q_st = (q5[:, :, :, :, None, :] * sel[None, :, None, None, :, None]).reshape(nb, rows, ATT_KV_DIM)
    pad = ((0, 0), (0, SUBLANES - steps), (0, 0))
    k_new = k.reshape(steps, nb, ATT_KV_DIM).transpose(1, 0, 2)
    v_new = v.reshape(steps, nb, ATT_KV_DIM).transpose(1, 0, 2)
    sink_rows = jnp.broadcast_to(jnp.repeat(P['sinks'][0], steps)[:, None], (rows, LANES))
    kc = k_cache.reshape(nb, WINDOW, ATT_KV_DIM)
    vc = v_cache.reshape(nb, WINDOW, ATT_KV_DIM)
    o_st = _swa_sample(q_st, kc, vc, jnp.pad(k_new, pad), jnp.pad(v_new, pad), sink_rows, steps, 16)
    o5 = o_st.reshape(nb, ATT_KV_HEADS, ATT_GROUP, steps, ATT_KV_HEADS, HEAD_DIM)
    jdx = jnp.arange(ATT_KV_HEADS)
    o_a = o5[:, jdx, :, :, jdx, :]
    o_a = o_a.transpose(3, 1, 0, 2, 4).reshape(n, ATT_DIM)
    k_out = jnp.concatenate([kc, k_new], axis=1)[:, -WINDOW:].reshape(nb, WINDOW, ATT_KV_HEADS, HEAD_DIM)
    v_out = jnp.concatenate([vc, v_new], axis=1)[:, -WINDOW:].reshape(nb, WINDOW, ATT_KV_HEADS, HEAD_DIM)

    x1 = _merge(x, y, bonus, g, o_a, gr, ga, P, n)
    x2, c_tail = _ffn_sample(x1, conv_prev.transpose(1, 0, 2), P, g_final, final)
    state = (z[n - nb:], s_new, k_out, v_out, c_tail.transpose(1, 0, 2))
    return x2, state


def kernel(x_prompt, x_sample, state_rwkv_shift, state_rwkv_wkv, cache_swa_k, cache_swa_v, state_ffn_conv, norm_mix_g, w_in, rwkv_mu, rwkv_w0, rwkv_w2, rwkv_a0, rwkv_a2, rwkv_g2, rwkv_k_k, rwkv_k_a, rwkv_r_k, rwkv_ln_g, rwkv_ln_b, attn_sinks, w_br_rwkv, w_br_attn, w_out, norm_ffn_g, ffn_w_in, ffn_conv_w, ffn_conv_b, ffn_w_down, norm_final_g):
    bp, tp, _ = x_prompt.shape
    nb, steps, _ = x_sample.shape
    depth = w_in.shape[0]
    tab_p = _rope_tables(jnp.arange(tp, dtype=jnp.int32))
    tab_s = _rope_tables(PAST_LEN + jnp.repeat(jnp.arange(steps, dtype=jnp.int32), nb))
    g_final = norm_final_g[None]
    xp = x_prompt.reshape(bp * tp, D_MODEL)
    xs = x_sample.transpose(1, 0, 2).reshape(steps * nb, D_MODEL)
    outs_p, outs_s = [], []
    for l in range(depth):
        P = _layer_params(l, norm_mix_g, w_in, rwkv_mu, rwkv_w0, rwkv_w2, rwkv_a0, rwkv_a2, rwkv_g2, rwkv_k_k,
                          rwkv_k_a, rwkv_r_k, rwkv_ln_g, rwkv_ln_b, attn_sinks, w_br_rwkv, w_br_attn, w_out,
                          norm_ffn_g, ffn_w_in, ffn_conv_w, ffn_conv_b, ffn_w_down)
        final = l == depth - 1
        xp, sp = _prompt_layer(xp, P, tab_p, g_final, final, bp, tp)
        xs, ss = _sample_layer(xs, P, tab_s, g_final, final, state_rwkv_shift[l], state_rwkv_wkv[l],
                               cache_swa_k[l], cache_swa_v[l], state_ffn_conv[l], nb, steps)
        outs_p.append(sp)
        outs_s.append(ss)
    y_prompt = xp.reshape(bp, tp, D_MODEL)
    y_sample = xs.reshape(steps, nb, D_MODEL).transpose(1, 0, 2)
    stack = lambda outs, i: jnp.stack([o[i] for o in outs])
    return (y_prompt, y_sample, *(stack(outs_p, i) for i in range(5)), *(stack(outs_s, i) for i in range(5)))
```

```python
import functools

import jax
import jax.numpy as jnp
from jax import lax
from jax.experimental import pallas as pl
from jax.experimental.pallas import tpu as pltpu

F32 = jnp.float32
BF16 = jnp.bfloat16

D_MODEL = 1024
HEAD_DIM = 64
RWKV_HEADS = 8
RWKV_DIM = RWKV_HEADS * HEAD_DIM
D_DECAY_LORA = 64
D_AAA_LORA = 64
D_GATE_LORA = 128
RWKV_PROJ = 3 * RWKV_DIM + D_DECAY_LORA + D_AAA_LORA + D_GATE_LORA
ATT_Q_HEADS = 8
ATT_KV_HEADS = 2
ATT_GROUP = ATT_Q_HEADS // ATT_KV_HEADS
ATT_DIM = ATT_Q_HEADS * HEAD_DIM
ATT_KV_DIM = ATT_KV_HEADS * HEAD_DIM
WINDOW = 128
IN_PROJ = RWKV_PROJ + ATT_DIM + 2 * ATT_KV_DIM + 2 * D_MODEL
D_FF = 2816
CONV_W = 3
ROPE_THETA = 10000.0
RMS_EPS = 1e-6
GN_EPS = 64e-5
PAST_LEN = 16384

LANES = 128
SUBLANES = 8
MXU_DIM = 256
CHUNK = 64
HEADS_PER_GROUP = MXU_DIM // HEAD_DIM
VMEM_LIMIT = 56 * 1024 * 1024


def _params(sem):
    return pltpu.CompilerParams(dimension_semantics=sem, vmem_limit_bytes=VMEM_LIMIT)


def _const_spec(shape):
    zeros = (0,) * len(shape)
    return pl.BlockSpec(shape, lambda *_: zeros, pipeline_mode=pl.Buffered(1))


def _mm(a, b):
    return jnp.dot(a.astype(BF16), b.astype(BF16), preferred_element_type=F32)


def _mm_nt(a, b):
    return lax.dot_general(a.astype(BF16), b.astype(BF16), (((1,), (1,)), ((), ())),
                           preferred_element_type=F32)


def _mm_f32(a, b):
    return jnp.dot(a, b, precision=lax.Precision.HIGHEST, preferred_element_type=F32)


def _mm_tn(a, b):
    return lax.dot_general(a.astype(BF16), b.astype(BF16), (((0,), (0,)), ((), ())),
                           preferred_element_type=F32)


def _head_sum(x, ones_bd):
    hi = x.astype(BF16)
    lo = (x - hi.astype(F32)).astype(BF16)
    return (jnp.dot(hi, ones_bd, preferred_element_type=F32)
            + jnp.dot(lo, ones_bd, preferred_element_type=F32))


def _rmsnorm(x, g):
    return x * lax.rsqrt(jnp.mean(x * x, axis=-1, keepdims=True) + RMS_EPS) * g


def _rope(x, cos, sin_signed, first_half):
    partner = jnp.where(first_half, pltpu.roll(x, LANES - HEAD_DIM // 2, 1), pltpu.roll(x, HEAD_DIM // 2, 1))
    return x * cos + partner * sin_signed


def _proj_kernel(x_ref, g_ref, w_ref, cos_ref, sin_ref, z_ref, q_ref, k_ref, v_ref, gr_ref, ga_ref):
    h = _rmsnorm(x_ref[...], g_ref[...]).astype(BF16)
    c0, c1, c2, c3, c4 = (RWKV_PROJ, RWKV_PROJ + ATT_DIM, RWKV_PROJ + ATT_DIM + ATT_KV_DIM,
                          RWKV_PROJ + ATT_DIM + 2 * ATT_KV_DIM, RWKV_PROJ + ATT_DIM + 2 * ATT_KV_DIM + D_MODEL)
    z_ref[...] = jnp.dot(h, w_ref[:, 0:c0], preferred_element_type=F32)
    cos = cos_ref[...]
    sin = sin_ref[...]
    lane = lax.broadcasted_iota(jnp.int32, (1, LANES), 1)
    first_half = (lane % HEAD_DIM) < (HEAD_DIM // 2)
    q = jnp.dot(h, w_ref[:, c0:c1], preferred_element_type=F32)
    for grp in range(ATT_DIM // LANES):
        sl = slice(grp * LANES, (grp + 1) * LANES)
        q_ref[:, sl] = _rope(q[:, sl], cos, sin, first_half)
    k = jnp.dot(h, w_ref[:, c1:c2], preferred_element_type=F32)
    k_ref[...] = _rope(k, cos, sin, first_half)
    v_ref[...] = jnp.dot(h, w_ref[:, c2:c3], preferred_element_type=F32)
    gr_ref[...] = jax.nn.sigmoid(jnp.dot(h, w_ref[:, c3:c4], preferred_element_type=F32))
    ga_ref[...] = jax.nn.sigmoid(jnp.dot(h, w_ref[:, c4:IN_PROJ], preferred_element_type=F32))


def _proj(x2d, g, w_bf16, cos_t, sin_t, tm):
    n = x2d.shape[0]
    npos = cos_t.shape[0] // tm
    row = lambda i: (i, 0)
    out_w = (RWKV_PROJ, ATT_DIM, ATT_KV_DIM, ATT_KV_DIM, D_MODEL, D_MODEL)
    return pl.pallas_call(
        _proj_kernel,
        grid=(n // tm,),
        in_specs=[pl.BlockSpec((tm, D_MODEL), row), _const_spec((1, D_MODEL)),
                  _const_spec((D_MODEL, IN_PROJ)),
                  pl.BlockSpec((tm, LANES), lambda i: (i % npos, 0)),
                  pl.BlockSpec((tm, LANES), lambda i: (i % npos, 0))],
        out_specs=[pl.BlockSpec((tm, w), row) for w in out_w],
        out_shape=[jax.ShapeDtypeStruct((n, w), F32) for w in out_w],
        compiler_params=_params(("arbitrary",)),
        name="proj",
    )(x2d, g, w_bf16, cos_t, sin_t)


def _prep_math(z, zp, mu, w0, w2p, a0, a2p, g2, k_k, k_a, r_k, ones_bd, outs):
    r_ref, lw_ref, kk_ref, b_ref, kf_ref, v_ref, g_ref, bonus_ref = outs
    zs = z + mu * (zp - z)
    d = RWKV_DIM
    r, k, v = zs[:, 0:d], zs[:, d:2 * d], zs[:, 2 * d:3 * d]
    zwa = zs[:, 3 * d:3 * d + LANES]
    zg = zs[:, 3 * d + LANES:3 * d + 2 * LANES]
    w_pre = w0 + _mm(jnp.tanh(zwa), w2p)
    neg = -w_pre
    softplus = jnp.maximum(neg, 0.0) + jnp.log1p(jnp.exp(-jnp.abs(neg)))
    w_log = -softplus - 0.5
    lw_ref[...] = -jnp.exp(w_log)
    a = jax.nn.sigmoid(a0 + _mm(zwa, a2p))
    g_ref[...] = _mm(jax.nn.sigmoid(zg), g2)
    kk0 = k * k_k
    norm = jnp.sqrt(_head_sum(kk0 * kk0, ones_bd))
    kk = kk0 / jnp.maximum(norm, 1e-12)
    kf = k * (1.0 + (a - 1.0) * k_a)
    r_ref[...] = r
    kk_ref[...] = kk
    b_ref[...] = kk * a
    kf_ref[...] = kf
    v_ref[...] = v
    bonus_ref[...] = _head_sum(r * kf * r_k, ones_bd) * v


def _prep_prompt_kernel(z_ref, mu_ref, w0_ref, w2_ref, a0_ref, a2_ref, g2_ref, kk_ref_, ka_ref, rk_ref, ones_ref,
                        *rest):
    outs, carry = rest[:8], rest[8]
    t = pl.program_id(1)

    @pl.when(t == 0)
    def _():
        carry[...] = jnp.zeros_like(carry)

    z = z_ref[...]
    row = lax.broadcasted_iota(jnp.int32, (z.shape[0], 1), 0)
    zp = jnp.where(row == 0, carry[SUBLANES - 1:SUBLANES, :], pltpu.roll(z, 1, 0))
    carry[...] = z[z.shape[0] - SUBLANES:, :]
    _prep_math(z, zp, mu_ref[...], w0_ref[...], w2_ref[...], a0_ref[...], a2_ref[...], g2_ref[...],
               kk_ref_[...], ka_ref[...], rk_ref[...], ones_ref[...], outs)


def _prep_sample_kernel(z_ref, sp_ref, mu_ref, w0_ref, w2_ref, a0_ref, a2_ref, g2_ref, kk_ref_, ka_ref, rk_ref,
                        ones_ref, *outs):
    z = z_ref[...]
    nb = sp_ref.shape[0]
    zp = jnp.concatenate([sp_ref[...], z[:z.shape[0] - nb, :]], axis=0)
    _prep_math(z, zp, mu_ref[...], w0_ref[...], w2_ref[...], a0_ref[...], a2_ref[...], g2_ref[...],
               kk_ref_[...], ka_ref[...], rk_ref[...], ones_ref[...], outs)


def _prep_consts(P):
    return [P['mu'], P['w0'], P['w2p'], P['a0'], P['a2p'], P['g2'], P['k_k'], P['k_a'], P['r_k'], P['ones_bd']]


def _prep_const_specs():
    d = RWKV_DIM
    shapes = [(1, RWKV_PROJ), (1, d), (LANES, d), (1, d), (LANES, d), (D_GATE_LORA, d), (1, d), (1, d), (1, d),
              (d, d)]
    return [_const_spec(s) for s in shapes]


def _prep_prompt(z2d, P, batch, tm):
    n = z2d.shape[0]
    nt = n // batch // tm
    row = lambda b, t: (b * nt + t, 0)
    return pl.pallas_call(
        _prep_prompt_kernel,
        grid=(batch, nt),
        in_specs=[pl.BlockSpec((tm, RWKV_PROJ), row)] + _prep_const_specs(),
        out_specs=[pl.BlockSpec((tm, RWKV_DIM), row)] * 8,
        out_shape=[jax.ShapeDtypeStruct((n, RWKV_DIM), F32)] * 8,
        scratch_shapes=[pltpu.VMEM((SUBLANES, RWKV_PROJ), F32)],
        compiler_params=_params(("arbitrary", "arbitrary")),
        name="prep_prompt",
    )(z2d, *_prep_consts(P))


def _prep_sample(z2d, shift_prev, P):
    n = z2d.shape[0]
    nb = shift_prev.shape[0]
    return pl.pallas_call(
        _prep_sample_kernel,
        grid=(1,),
        in_specs=[_const_spec((n, RWKV_PROJ)), _const_spec((nb, RWKV_PROJ))] + _prep_const_specs(),
        out_specs=[_const_spec((n, RWKV_DIM))] * 8,
        out_shape=[jax.ShapeDtypeStruct((n, RWKV_DIM), F32)] * 8,
        compiler_params=_params(("arbitrary",)),
        name="prep_sample",
    )(z2d, shift_prev, *_prep_consts(P))


def _wkv_prompt_kernel(r_ref, lw_ref, kk_ref, b_ref, kf_ref, v_ref, y_ref, st_ref, s_scr, *, chunks):
    C, G, M = CHUNK, HEADS_PER_GROUP, MXU_DIM

    @pl.when(pl.program_id(1) == 0)
    def _():
        s_scr[...] = jnp.zeros_like(s_scr)

    ri = lax.broadcasted_iota(jnp.int32, (M, M), 0)
    ci = lax.broadcasted_iota(jnp.int32, (M, M), 1)
    strict = (ri % C) > (ci % C)
    incl = (ri % C) >= (ci % C)
    eye = (ri == ci).astype(F32)
    ti = lax.broadcasted_iota(jnp.int32, (C, C), 0)
    si = lax.broadcasted_iota(jnp.int32, (C, C), 1)
    tri = (ti >= si).astype(F32)
    lane_head = lax.broadcasted_iota(jnp.int32, (1, M), 1) // HEAD_DIM

    def stack(x):
        xb = x.astype(BF16)
        return jnp.concatenate([jnp.where(lane_head == h, xb, jnp.zeros_like(xb)) for h in range(G)], axis=0)

    def chunk_body(c, carry):
        rows = pl.ds(pl.multiple_of(c * C, C), C)
        lw = lw_ref[rows, :]
        cum = _mm_f32(tri, lw)
        w_end = jnp.exp(cum[C - 1:C, :])
        e_pos = jnp.exp(cum)
        e_neg = jnp.exp(-cum)
        a_hat = -kk_ref[rows, :] * jnp.exp(cum - lw)
        r_hat = r_ref[rows, :] * e_pos
        b_til = b_ref[rows, :] * e_neg
        k_til = kf_ref[rows, :] * e_neg
        b_bar = b_til * w_end
        k_bar = k_til * w_end
        v = v_ref[rows, :]
        for grp in range(RWKV_HEADS // G):
            ls = slice(grp * M, (grp + 1) * M)
            a_s, r_s, b_s, k_s, v_s, bb_s, kb_s = (stack(x[:, ls]) for x in
                                                   (a_hat, r_hat, b_til, k_til, v, b_bar, k_bar))
            l_ab = jnp.where(strict, _mm_nt(a_s, b_s), 0.0)
            l_ak = jnp.where(strict, _mm_nt(a_s, k_s), 0.0).astype(BF16)
            p_rb = jnp.where(incl, _mm_nt(r_s, b_s), 0.0).astype(BF16)
            p_rk = jnp.where(incl, _mm_nt(r_s, k_s), 0.0).astype(BF16)
            t_inv = eye + l_ab
            pw = l_ab.astype(BF16)
            for _ in range(C.bit_length() - 2):
                pw = _mm(pw, pw).astype(BF16)
                t_inv = t_inv + _mm(pw, t_inv)
            s0 = s_scr[grp]
            s0b = s0.astype(BF16)
            u = _mm(t_inv, _mm_nt(a_s, s0b) + _mm(l_ak, v_s)).astype(BF16)
            y_s = _mm_nt(r_s, s0b) + _mm(p_rb, u) + _mm(p_rk, v_s)
            y = y_s[0:C]
            for h in range(1, G):
                y = y + y_s[h * C:(h + 1) * C]
            y_ref[rows, ls] = y
            s_scr[grp] = s0 * w_end[:, ls] + _mm_tn(u, bb_s) + _mm_tn(v_s, kb_s)
        return carry

    lax.fori_loop(0, chunks, chunk_body, 0)

    @pl.when(pl.program_id(1) == pl.num_programs(1) - 1)
    def _():
        st_ref[0] = s_scr[...]


def _wkv_prompt(r, lw, kk, b, kf, v, batch, tl):
    n = r.shape[0]
    nt = n // batch // tl
    ngrp = RWKV_HEADS // HEADS_PER_GROUP
    row = lambda bi, t: (bi * nt + t, 0)
    y, st = pl.pallas_call(
        functools.partial(_wkv_prompt_kernel, chunks=tl // CHUNK),
        grid=(batch, nt),
        in_specs=[pl.BlockSpec((tl, RWKV_DIM), row)] * 6,
        out_specs=[pl.BlockSpec((tl, RWKV_DIM), row),
                   pl.BlockSpec((1, ngrp, MXU_DIM, MXU_DIM), lambda bi, t: (bi, 0, 0, 0))],
        out_shape=[jax.ShapeDtypeStruct((n, RWKV_DIM), F32),
                   jax.ShapeDtypeStruct((batch, ngrp, MXU_DIM, MXU_DIM), F32)],
        scratch_shapes=[pltpu.VMEM((ngrp, MXU_DIM, MXU_DIM), F32)],
        compiler_params=_params(("arbitrary", "arbitrary")),
        name="wkv_prompt",
    )(r, lw, kk, b, kf, v)
    st = st.reshape(batch, ngrp, HEADS_PER_GROUP, HEAD_DIM, HEADS_PER_GROUP, HEAD_DIM)
    st = jnp.stack([st[:, :, h, :, h, :] for h in range(HEADS_PER_GROUP)], axis=2)
    st = st.reshape(batch, RWKV_HEADS, HEAD_DIM, HEAD_DIM)
    return y, st


def _wkv_sample_kernel(r_ref, lw_ref, kk_ref, b_ref, kf_ref, v_ref, s_ref, y_ref, so_ref, *, steps):
    def value_row(vi, carry):
        s = s_ref[0, vi]
        for t in range(steps):
            sk = jnp.sum(s * kk_ref[t, 0], axis=0, keepdims=True)
            v_row = v_ref[t, 0, pl.ds(vi, 1), :]
            s = s * jnp.exp(lw_ref[t, 0]) - sk * b_ref[t, 0] + v_row * kf_ref[t, 0]
            y_ref[t, 0, pl.ds(vi, 1), :] = jnp.sum(s * r_ref[t, 0], axis=0, keepdims=True)
        so_ref[0, vi] = s
        return carry

    lax.fori_loop(0, HEAD_DIM, value_row, 0)


def _wkv_sample(r, lw, kk, b, kf, v, s0, steps):
    nb = r.shape[-1]
    vec = pl.BlockSpec((steps, 1, HEAD_DIM, nb), lambda h: (0, h, 0, 0))
    st = pl.BlockSpec((1, HEAD_DIM, HEAD_DIM, nb), lambda h: (h, 0, 0, 0))
    return pl.pallas_call(
        functools.partial(_wkv_sample_kernel, steps=steps),
        grid=(RWKV_HEADS,),
        in_specs=[vec] * 6 + [st],
        out_specs=[vec, st],
        out_shape=[jax.ShapeDtypeStruct(r.shape, F32), jax.ShapeDtypeStruct(s0.shape, F32)],
        compiler_params=_params(("arbitrary",)),
        name="wkv_sample",
    )(r, lw, kk, b, kf, v, s0)


def _swa_prompt_kernel(q_ref, kc_ref, kp_ref, vc_ref, vp_ref, sink_ref, o_ref, *, sub):
    blk = WINDOW
    first = pl.program_id(1) == 0
    k_all = jnp.concatenate([kp_ref[...], kc_ref[...]], axis=0)
    v_all = jnp.concatenate([vp_ref[...], vc_ref[...]], axis=0).astype(BF16)
    v_ext = jnp.concatenate([v_all, jnp.ones_like(v_all)], axis=1)
    k_swapped = pltpu.roll(k_all, HEAD_DIM, 1)
    lane = lax.broadcasted_iota(jnp.int32, (1, LANES), 1)
    low = lane < HEAD_DIM
    k_var = {(0, 0): jnp.where(low, k_all, 0.0), (0, 1): jnp.where(low, 0.0, k_swapped),
             (1, 1): jnp.where(low, 0.0, k_all), (1, 0): jnp.where(low, k_swapped, 0.0)}
    k_var = {key: val.astype(BF16) for key, val in k_var.items()}
    order = [j * ATT_GROUP + g for j in range(ATT_KV_HEADS) for half in range(2)
             for g in range(ATT_GROUP) if g % 2 == half]
    n_rows = ATT_Q_HEADS * blk
    qi = lax.broadcasted_iota(jnp.int32, (n_rows, 2 * blk), 0) % blk
    kj = lax.broadcasted_iota(jnp.int32, (n_rows, 2 * blk), 1)
    diff = blk + qi - kj
    band = (diff >= 0) & (diff < WINDOW)
    row_head = lax.broadcasted_iota(jnp.int32, (n_rows, 1), 0) // blk
    sink = jnp.zeros((n_rows, 1), F32)
    for idx, head in enumerate(order):
        sink = jnp.where(row_head == idx, sink_ref[0:1, head:head + 1], sink)
    scale = HEAD_DIM ** -0.5
    for sb in range(sub):
        qs = slice(sb * blk, (sb + 1) * blk)
        ks = slice(sb * blk, (sb + 2) * blk)
        parts = []
        for j in range(ATT_KV_HEADS):
            for half in range(2):
                heads = [j * ATT_GROUP + g for g in range(ATT_GROUP) if g % 2 == half]
                q_cat = jnp.concatenate([q_ref[qs, (h // 2) * LANES:(h // 2 + 1) * LANES] for h in heads], axis=0)
                parts.append(_mm_nt(q_cat * scale, k_var[(j, half)][ks]))
        s = jnp.concatenate(parts, axis=0)
        ok = band & jnp.logical_or(kj >= blk, jnp.logical_not(first)) if sb == 0 else band
        s = jnp.where(ok, s, -jnp.inf)
        m = jnp.maximum(jnp.max(s, axis=-1, keepdims=True), sink)
        e = jnp.exp(s - m).astype(BF16)
        ev = jnp.dot(e, v_ext[ks], preferred_element_type=F32)
        o = ev[:, 0:LANES] / (ev[:, LANES:LANES + 1] + jnp.exp(sink - m))
        by_head = {}
        for idx, head in enumerate(order):
            o_h = o[idx * blk:(idx + 1) * blk]
            by_head[head] = pltpu.roll(o_h, HEAD_DIM, 1) if head // ATT_GROUP != head % 2 else o_h
        for grp in range(ATT_DIM // LANES):
            o_ref[qs, grp * LANES:(grp + 1) * LANES] = jnp.where(low, by_head[2 * grp], by_head[2 * grp + 1])


def _swa_prompt(q, k, v, sinks, batch, sub):
    n = q.shape[0]
    rows = sub * WINDOW
    nb = n // batch // rows
    cur = lambda b, i: (b * nb + i, 0)
    prev = lambda b, i: ((b * nb + i) * sub - jnp.minimum(i, 1), 0)
    return pl.pallas_call(
        functools.partial(_swa_prompt_kernel, sub=sub),
        grid=(batch, nb),
        in_specs=[pl.BlockSpec((rows, ATT_DIM), cur),
                  pl.BlockSpec((rows, ATT_KV_DIM), cur), pl.BlockSpec((WINDOW, ATT_KV_DIM), prev),
                  pl.BlockSpec((rows, ATT_KV_DIM), cur), pl.BlockSpec((WINDOW, ATT_KV_DIM), prev),
                  _const_spec((1, ATT_Q_HEADS))],
        out_specs=pl.BlockSpec((rows, ATT_DIM), cur),
        out_shape=jax.ShapeDtypeStruct((n, ATT_DIM), F32),
        compiler_params=_params(("arbitrary", "arbitrary")),
        name="swa_prompt",
    )(q, k, k, v, v, sinks)


def _swa_sample_kernel(q_ref, kc_ref, vc_ref, kn_ref, vn_ref, sink_ref, o_ref, *, steps):
    q = q_ref[...].astype(BF16)
    rows = q.shape[1]
    nk = kc_ref.shape[1]
    nn = kn_ref.shape[1]
    bdot = lambda a, b, dims: lax.dot_general(a, b, (dims, ((0,), (0,))), preferred_element_type=F32)
    s_c = bdot(q, kc_ref[...].astype(BF16), ((2,), (2,))) * (HEAD_DIM ** -0.5)
    s_n = bdot(q, kn_ref[...].astype(BF16), ((2,), (2,))) * (HEAD_DIM ** -0.5)
    t_c = lax.broadcasted_iota(jnp.int32, (1, rows, nk), 1) % steps
    w_c = lax.broadcasted_iota(jnp.int32, (1, rows, nk), 2)
    s_c = jnp.where(w_c > t_c, s_c, -jnp.inf)
    t_n = lax.broadcasted_iota(jnp.int32, (1, rows, nn), 1) % steps
    w_n = lax.broadcasted_iota(jnp.int32, (1, rows, nn), 2)
    s_n = jnp.where(w_n <= t_n, s_n, -jnp.inf)
    sink = sink_ref[...][None, :, 0:1]
    m = jnp.maximum(jnp.maximum(jnp.max(s_c, axis=-1, keepdims=True), jnp.max(s_n, axis=-1, keepdims=True)), sink)
    e_c = jnp.exp(s_c - m)
    e_n = jnp.exp(s_n - m)
    den = jnp.sum(e_c, axis=-1, keepdims=True) + jnp.sum(e_n, axis=-1, keepdims=True) + jnp.exp(sink - m)
    o = (bdot((e_c / den).astype(BF16), vc_ref[...].astype(BF16), ((2,), (1,)))
         + bdot((e_n / den).astype(BF16), vn_ref[...].astype(BF16), ((2,), (1,))))
    o_ref[...] = o


def _swa_sample(q_st, k_cache, v_cache, k_new, v_new, sink_rows, steps, bt):
    nb, rows, _ = q_st.shape
    nn = k_new.shape[1]
    blk = lambda r: pl.BlockSpec((bt, r, LANES), lambda i: (i, 0, 0))
    return pl.pallas_call(
        functools.partial(_swa_sample_kernel, steps=steps),
        grid=(nb // bt,),
        in_specs=[blk(rows), blk(WINDOW), blk(WINDOW), blk(nn), blk(nn), _const_spec((rows, LANES))],
        out_specs=blk(rows),
        out_shape=jax.ShapeDtypeStruct((nb, rows, LANES), F32),
        compiler_params=_params(("arbitrary",)),
        name="swa_sample",
    )(q_st, k_cache, v_cache, k_new, v_new, sink_rows)


def _merge_kernel(x_ref, y_ref, bonus_ref, g_ref, oa_ref, gr_ref, ga_ref, lng_ref, lnb_ref, ones_ref,
                  wa_ref, wb_ref, wo_ref, o_ref):
    y = y_ref[...]
    ones_bd = ones_ref[...]
    mean = _head_sum(y, ones_bd) * (1.0 / HEAD_DIM)
    d = y - mean
    var = _head_sum(d * d, ones_bd) * (1.0 / HEAD_DIM)
    yn = d * lax.rsqrt(var + GN_EPS) * lng_ref[...] + lnb_ref[...]
    o_r = (yn + bonus_ref[...]) * g_ref[...]
    merged = gr_ref[...] * _mm(o_r, wa_ref[...]) + ga_ref[...] * _mm(oa_ref[...], wb_ref[...])
    o_ref[...] = x_ref[...] + _mm(merged, wo_ref[...])


def _merge(x, y, bonus, g, o_a, gr, ga, P, tm):
    n = x.shape[0]
    row = lambda i: (i, 0)
    wide = pl.BlockSpec((tm, D_MODEL), row)
    half = pl.BlockSpec((tm, RWKV_DIM), row)
    return pl.pallas_call(
        _merge_kernel,
        grid=(n // tm,),
        in_specs=[wide, half, half, half, half, wide, wide,
                  _const_spec((1, RWKV_DIM)), _const_spec((1, RWKV_DIM)), _const_spec((RWKV_DIM, RWKV_DIM)),
                  _const_spec((RWKV_DIM, D_MODEL)), _const_spec((ATT_DIM, D_MODEL)),
                  _const_spec((D_MODEL, D_MODEL))],
        out_specs=wide,
        out_shape=jax.ShapeDtypeStruct((n, D_MODEL), F32),
        compiler_params=_params(("arbitrary",)),
        name="merge",
    )(x, y, bonus, g, o_a, gr, ga, P['ln_g'], P['ln_b'], P['ones_bd'], P['w_br_rwkv'], P['w_br_attn'], P['w_out'])


def _gelu(x):
    return 0.5 * x * (1.0 + lax.erf(x * (2.0 ** -0.5)))


def _ffn_tail(x, c, up, c1, c2, cw_ref, cb_ref, wd_ref, gf_ref, o_ref, final):
    conv = cb_ref[...] + c2 * cw_ref[0:1, :] + c1 * cw_ref[1:2, :] + c * cw_ref[2:3, :]
    out = x + _mm(_gelu(conv) * up, wd_ref[...])
    o_ref[...] = _rmsnorm(out, gf_ref[...]) if final else out


def _ffn_prompt_kernel(x_ref, gn_ref, wi_ref, cw_ref, cb_ref, wd_ref, gf_ref, o_ref, ct_ref, carry, *, final):
    @pl.when(pl.program_id(1) == 0)
    def _():
        carry[...] = jnp.zeros_like(carry)

    x = x_ref[...]
    h = _rmsnorm(x, gn_ref[...]).astype(BF16)
    c = jnp.dot(h, wi_ref[:, 0:D_FF], preferred_element_type=F32)
    up = jnp.dot(h, wi_ref[:, D_FF:2 * D_FF], preferred_element_type=F32)
    tm = c.shape[0]
    row = lax.broadcasted_iota(jnp.int32, (tm, 1), 0)
    last = carry[SUBLANES - 1:SUBLANES, :]
    c1 = jnp.where(row == 0, last, pltpu.roll(c, 1, 0))
    c2 = jnp.where(row == 0, carry[SUBLANES - 2:SUBLANES - 1, :],
                   jnp.where(row == 1, last, pltpu.roll(c, 2, 0)))
    tail = c[tm - SUBLANES:, :]
    carry[...] = tail
    ct_ref[0] = tail
    _ffn_tail(x, c, up, c1, c2, cw_ref, cb_ref, wd_ref, gf_ref, o_ref, final)


def _ffn_sample_kernel(x_ref, cp_ref, gn_ref, wi_ref, cw_ref, cb_ref, wd_ref, gf_ref, o_ref, ct_ref, *, final):
    x = x_ref[...]
    h = _rmsnorm(x, gn_ref[...]).astype(BF16)
    c = jnp.dot(h, wi_ref[:, 0:D_FF], preferred_element_type=F32)
    up = jnp.dot(h, wi_ref[:, D_FF:2 * D_FF], preferred_element_type=F32)
    n = c.shape[0]
    nb = cp_ref.shape[1]
    p0, p1 = cp_ref[0], cp_ref[1]
    c1 = jnp.concatenate([p1, c[:n - nb]], axis=0)
    c2 = jnp.concatenate([p0, p1, c[:n - 2 * nb]], axis=0)
    ct_ref[0] = c[n - 2 * nb:n - nb, :]
    ct_ref[1] = c[n - nb:, :]
    _ffn_tail(x, c, up, c1, c2, cw_ref, cb_ref, wd_ref, gf_ref, o_ref, final)


def _ffn_weight_specs():
    return [_const_spec((1, D_MODEL)), _const_spec((D_MODEL, 2 * D_FF)), _const_spec((CONV_W, D_FF)),
            _const_spec((1, D_FF)), _const_spec((D_FF, D_MODEL)), _const_spec((1, D_MODEL))]


def _ffn_weights(P, g_final):
    return [P['norm_ffn_g'], P['ffn_w_in'], P['ffn_conv_w'], P['ffn_conv_b'], P['ffn_w_down'], g_final]


def _ffn_prompt(x, P, g_final, final, batch, tm):
    n = x.shape[0]
    nt = n // batch // tm
    row = lambda b, t: (b * nt + t, 0)
    return pl.pallas_call(
        functools.partial(_ffn_prompt_kernel, final=final),
        grid=(batch, nt),
        in_specs=[pl.BlockSpec((tm, D_MODEL), row)] + _ffn_weight_specs(),
        out_specs=[pl.BlockSpec((tm, D_MODEL), row), pl.BlockSpec((1, SUBLANES, D_FF), lambda b, t: (b, 0, 0))],
        out_shape=[jax.ShapeDtypeStruct((n, D_MODEL), F32), jax.ShapeDtypeStruct((batch, SUBLANES, D_FF), F32)],
        scratch_shapes=[pltpu.VMEM((SUBLANES, D_FF), F32)],
        compiler_params=_params(("arbitrary", "arbitrary")),
        name="ffn_prompt",
    )(x, *_ffn_weights(P, g_final))


def _ffn_sample(x, conv_prev, P, g_final, final):
    n = x.shape[0]
    nb = conv_prev.shape[1]
    return pl.pallas_call(
        functools.partial(_ffn_sample_kernel, final=final),
        grid=(1,),
        in_specs=[_const_spec((n, D_MODEL)), _const_spec((CONV_W - 1, nb, D_FF))] + _ffn_weight_specs(),
        out_specs=[_const_spec((n, D_MODEL)), _const_spec((CONV_W - 1, nb, D_FF))],
        out_shape=[jax.ShapeDtypeStruct((n, D_MODEL), F32), jax.ShapeDtypeStruct((CONV_W - 1, nb, D_FF), F32)],
        compiler_params=_params(("arbitrary",)),
        name="ffn_sample",
    )(x, conv_prev, *_ffn_weights(P, g_final))


def _rope_tables(pos):
    inv = ROPE_THETA ** (-jnp.arange(0, HEAD_DIM, 2, dtype=F32) / HEAD_DIM)
    ang = pos.astype(F32)[:, None] * inv[None, :]
    cos, sin = jnp.cos(ang), jnp.sin(ang)
    reps = LANES // HEAD_DIM
    return jnp.tile(jnp.concatenate([cos, cos], axis=1), (1, reps)), jnp.tile(jnp.concatenate([-sin, sin], axis=1),
                                                                             (1, reps))


def _layer_params(l, norm_mix_g, w_in, rwkv_mu, rwkv_w0, rwkv_w2, rwkv_a0, rwkv_a2, rwkv_g2, rwkv_k_k, rwkv_k_a,
                  rwkv_r_k, rwkv_ln_g, rwkv_ln_b, attn_sinks, w_br_rwkv, w_br_attn, w_out, norm_ffn_g, ffn_w_in,
                  ffn_conv_w, ffn_conv_b, ffn_w_down):
    d = RWKV_DIM
    head = jnp.arange(d) // HEAD_DIM
    zeros_w = jnp.zeros((D_DECAY_LORA, d), F32)
    zeros_a = jnp.zeros((D_AAA_LORA, d), F32)
    return {
        'norm_mix_g': norm_mix_g[l][None], 'w_in': w_in[l].astype(BF16),
        'mu': rwkv_mu[l][None], 'w0': rwkv_w0[l][None],
        'w2p': jnp.concatenate([rwkv_w2[l], zeros_a], axis=0).astype(BF16),
        'a0': rwkv_a0[l][None],
        'a2p': jnp.concatenate([zeros_w, rwkv_a2[l]], axis=0).astype(BF16),
        'g2': rwkv_g2[l].astype(BF16), 'k_k': rwkv_k_k[l][None], 'k_a': rwkv_k_a[l][None],
        'r_k': rwkv_r_k[l].reshape(1, d), 'ln_g': rwkv_ln_g[l][None], 'ln_b': rwkv_ln_b[l][None],
        'ones_bd': (head[:, None] == head[None, :]).astype(BF16),
        'sinks': attn_sinks[l][None],
        'w_br_rwkv': w_br_rwkv[l].astype(BF16), 'w_br_attn': w_br_attn[l].astype(BF16),
        'w_out': w_out[l].astype(BF16), 'norm_ffn_g': norm_ffn_g[l][None],
        'ffn_w_in': ffn_w_in[l].astype(BF16), 'ffn_conv_w': ffn_conv_w[l], 'ffn_conv_b': ffn_conv_b[l][None],
        'ffn_w_down': ffn_w_down[l].astype(BF16),
    }


def _prompt_layer(x, P, tables, g_final, final, batch, seq):
    tm = min(256, seq)
    z, q, k, v, gr, ga = _proj(x, P['norm_mix_g'], P['w_in'], tables[0], tables[1], tm)
    r, lw, kk, b, kf, vv, g, bonus = _prep_prompt(z, P, batch, tm)
    y, s_t = _wkv_prompt(r, lw, kk, b, kf, vv, batch, min(512, seq))
    o_a = _swa_prompt(q, k, v, P['sinks'], batch, min(2, seq // WINDOW))
    x1 = _merge(x, y, bonus, g, o_a, gr, ga, P, tm)
    x2, c_tail = _ffn_prompt(x1, P, g_final, final, batch, tm)
    last = lambda t: t.reshape(batch, seq, -1)[:, -WINDOW:].reshape(batch, WINDOW, ATT_KV_HEADS, HEAD_DIM)
    state = (z.reshape(batch, seq, RWKV_PROJ)[:, -1], s_t, last(k), last(v), c_tail[:, SUBLANES - (CONV_W - 1):])
    return x2, state


def _sample_layer(x, P, tables, g_final, final, shift_prev, wkv0, k_cache, v_cache, conv_prev, nb, steps):
    n = nb * steps
    z, q, k, v, gr, ga = _proj(x, P['norm_mix_g'], P['w_in'], tables[0], tables[1], n)
    r, lw, kk, b, kf, vv, g, bonus = _prep_sample(z, shift_prev, P)
    to_lanes = lambda t: t.reshape(steps, nb, RWKV_HEADS, HEAD_DIM).transpose(0, 2, 3, 1)
    y_l, s_l = _wkv_sample(*(to_lanes(t) for t in (r, lw, kk, b, kf, vv)), wkv0.transpose(1, 2, 3, 0), steps)
    y = y_l.transpose(0, 3, 1, 2).reshape(n, RWKV_DIM)
    s_new = s_l.transpose(3, 0, 1, 2)

    rows = ATT_Q_HEADS * steps
    q5 = q.reshape(steps, nb, ATT_KV_HEADS, ATT_GROUP, HEAD_DIM).transpose(1, 2, 3, 0, 4)
    sel = jnp.eye(ATT_KV_HEADS, dtype=F32)
    q_st = (q5[:, :, :, :, None, :] * sel[None, :, None, None, :, None]).reshape(nb, rows, ATT_KV_DIM)
    pad = ((0, 0), (0, SUBLANES - steps), (0, 0))
    k_new = k.reshape(steps, nb, ATT_KV_DIM).transpose(1, 0, 2)
    v_new = v.reshape(steps, nb, ATT_KV_DIM).transpose(1, 0, 2)
    sink_rows = jnp.broadcast_to(jnp.repeat(P['sinks'][0], steps)[:, None], (rows, LANES))
    kc = k_cache.reshape(nb, WINDOW, ATT_KV_DIM)
    vc = v_cache.reshape(nb, WINDOW, ATT_KV_DIM)
    o_st = _swa_sample(q_st, kc, vc, jnp.pad(k_new, pad), jnp.pad(v_new, pad), sink_rows, steps, 16)
    o5 = o_st.reshape(nb, ATT_KV_HEADS, ATT_GROUP, steps, ATT_KV_HEADS, HEAD_DIM)
    o_a = jnp.stack([o5[:, j, :, :, j, :] for j in range(ATT_KV_HEADS)], axis=0)
    o_a = o_a.transpose(3, 1, 0, 2, 4).reshape(n, ATT_DIM)
    k_out = jnp.concatenate([kc, k_new], axis=1)[:, -WINDOW:].reshape(nb, WINDOW, ATT_KV_HEADS, HEAD_DIM)
    v_out = jnp.concatenate([vc, v_new], axis=1)[:, -WINDOW:].reshape(nb, WINDOW, ATT_KV_HEADS, HEAD_DIM)

    x1 = _merge(x, y, bonus, g, o_a, gr, ga, P, n)
    x2, c_tail = _ffn_sample(x1, conv_prev.transpose(1, 0, 2), P, g_final, final)
    state = (z[n - nb:], s_new, k_out, v_out, c_tail.transpose(1, 0, 2))
    return x2, state


def kernel(x_prompt, x_sample, state_rwkv_shift, state_rwkv_wkv, cache_swa_k, cache_swa_v, state_ffn_conv, norm_mix_g, w_in, rwkv_mu, rwkv_w0, rwkv_w2, rwkv_a0, rwkv_a2, rwkv_g2, rwkv_k_k, rwkv_k_a, rwkv_r_k, rwkv_ln_g, rwkv_ln_b, attn_sinks, w_br_rwkv, w_br_attn, w_out, norm_ffn_g, ffn_w_in, ffn_conv_w, ffn_conv_b, ffn_w_down, norm_final_g):
    bp, tp, _ = x_prompt.shape
    nb, steps, _ = x_sample.shape
    depth = w_in.shape[0]
    tab_p = _rope_tables(jnp.arange(tp, dtype=jnp.int32))
    tab_s = _rope_tables(PAST_LEN + jnp.repeat(jnp.arange(steps, dtype=jnp.int32), nb))
    g_final = norm_final_g[None]
    xp = x_prompt.reshape(bp * tp, D_MODEL)
    xs = x_sample.transpose(1, 0, 2).reshape(steps * nb, D_MODEL)
    outs_p, outs_s = [], []
    for l in range(depth):
        P = _layer_params(l, norm_mix_g, w_in, rwkv_mu, rwkv_w0, rwkv_w2, rwkv_a0, rwkv_a2, rwkv_g2, rwkv_k_k,
                          rwkv_k_a, rwkv_r_k, rwkv_ln_g, rwkv_ln_b, attn_sinks, w_br_rwkv, w_br_attn, w_out,
                          norm_ffn_g, ffn_w_in, ffn_conv_w, ffn_conv_b, ffn_w_down)
        final = l == depth - 1
        xp, sp = _prompt_layer(xp, P, tab_p, g_final, final, bp, tp)
        xs, ss = _sample_layer(xs, P, tab_s, g_final, final, state_rwkv_shift[l], state_rwkv_wkv[l],
                               cache_swa_k[l], cache_swa_v[l], state_ffn_conv[l], nb, steps)
        outs_p.append(sp)
        outs_s.append(ss)
    y_prompt = xp.reshape(bp, tp, D_MODEL)
    y_sample = xs.reshape(steps, nb, D_MODEL).transpose(1, 0, 2)
    stack = lambda outs, i: jnp.stack([o[i] for o in outs])
    return (y_prompt, y_sample, *(stack(outs_p, i) for i in range(5)), *(stack(outs_s, i) for i in range(5)))
```

```python
import functools

import jax
import jax.numpy as jnp
from jax import lax
from jax.experimental import pallas as pl
from jax.experimental.pallas import tpu as pltpu

F32 = jnp.float32
BF16 = jnp.bfloat16

D_MODEL = 1024
HEAD_DIM = 64
RWKV_HEADS = 8
RWKV_DIM = RWKV_HEADS * HEAD_DIM
D_DECAY_LORA = 64
D_AAA_LORA = 64
D_GATE_LORA = 128
RWKV_PROJ = 3 * RWKV_DIM + D_DECAY_LORA + D_AAA_LORA + D_GATE_LORA
ATT_Q_HEADS = 8
ATT_KV_HEADS = 2
ATT_GROUP = ATT_Q_HEADS // ATT_KV_HEADS
ATT_DIM = ATT_Q_HEADS * HEAD_DIM
ATT_KV_DIM = ATT_KV_HEADS * HEAD_DIM
WINDOW = 128
IN_PROJ = RWKV_PROJ + ATT_DIM + 2 * ATT_KV_DIM + 2 * D_MODEL
D_FF = 2816
CONV_W = 3
ROPE_THETA = 10000.0
RMS_EPS = 1e-6
GN_EPS = 64e-5
PAST_LEN = 16384

LANES = 128
SUBLANES = 8
MXU_DIM = 256
CHUNK = 64
HEADS_PER_GROUP = MXU_DIM // HEAD_DIM
VMEM_LIMIT = 56 * 1024 * 1024


def _params(sem):
    return pltpu.CompilerParams(dimension_semantics=sem, vmem_limit_bytes=VMEM_LIMIT)


def _const_spec(shape):
    zeros = (0,) * len(shape)
    return pl.BlockSpec(shape, lambda *_: zeros, pipeline_mode=pl.Buffered(1))


def _layer_spec(shape, layer):
    index = (layer,) + (0,) * len(shape)
    return pl.BlockSpec((None,) + tuple(shape), lambda *_: index, pipeline_mode=pl.Buffered(1))


def _mm(a, b):
    return jnp.dot(a.astype(BF16), b.astype(BF16), preferred_element_type=F32)


def _mm_nt(a, b):
    return lax.dot_general(a.astype(BF16), b.astype(BF16), (((1,), (1,)), ((), ())),
                           preferred_element_type=F32)


def _mm_f32(a, b):
    return jnp.dot(a, b, precision=lax.Precision.HIGHEST, preferred_element_type=F32)


def _mm_tn(a, b):
    return lax.dot_general(a.astype(BF16), b.astype(BF16), (((0,), (0,)), ((), ())),
                           preferred_element_type=F32)


def _head_sum(x, ones_bd):
    hi = x.astype(BF16)
    lo = (x - hi.astype(F32)).astype(BF16)
    return (jnp.dot(hi, ones_bd, preferred_element_type=F32)
            + jnp.dot(lo, ones_bd, preferred_element_type=F32))


def _rmsnorm(x, g):
    return x * lax.rsqrt(jnp.mean(x * x, axis=-1, keepdims=True) + RMS_EPS) * g


def _rope(x, cos, sin_signed, first_half):
    partner = jnp.where(first_half, pltpu.roll(x, LANES - HEAD_DIM // 2, 1), pltpu.roll(x, HEAD_DIM // 2, 1))
    return x * cos + partner * sin_signed


def _proj_kernel(x_ref, g_ref, w_ref, cos_ref, sin_ref, z_ref, q_ref, k_ref, v_ref, gr_ref, ga_ref):
    h = _rmsnorm(x_ref[...], g_ref[...]).astype(BF16)
    c0, c1, c2, c3, c4 = (RWKV_PROJ, RWKV_PROJ + ATT_DIM, RWKV_PROJ + ATT_DIM + ATT_KV_DIM,
                          RWKV_PROJ + ATT_DIM + 2 * ATT_KV_DIM, RWKV_PROJ + ATT_DIM + 2 * ATT_KV_DIM + D_MODEL)
    z_ref[...] = jnp.dot(h, w_ref[:, 0:c0], preferred_element_type=F32)
    cos = cos_ref[...]
    sin = sin_ref[...]
    lane = lax.broadcasted_iota(jnp.int32, (1, LANES), 1)
    first_half = (lane % HEAD_DIM) < (HEAD_DIM // 2)
    q = jnp.dot(h, w_ref[:, c0:c1], preferred_element_type=F32)
    for grp in range(ATT_DIM // LANES):
        sl = slice(grp * LANES, (grp + 1) * LANES)
        q_ref[:, sl] = _rope(q[:, sl], cos, sin, first_half)
    k = jnp.dot(h, w_ref[:, c1:c2], preferred_element_type=F32)
    k_ref[...] = _rope(k, cos, sin, first_half)
    v_ref[...] = jnp.dot(h, w_ref[:, c2:c3], preferred_element_type=F32)
    gr_ref[...] = jax.nn.sigmoid(jnp.dot(h, w_ref[:, c3:c4], preferred_element_type=F32))
    ga_ref[...] = jax.nn.sigmoid(jnp.dot(h, w_ref[:, c4:IN_PROJ], preferred_element_type=F32))


def _proj(x2d, P, cos_t, sin_t, tm):
    n = x2d.shape[0]
    layer = P['layer']
    npos = cos_t.shape[0] // tm
    row = lambda i: (i, 0)
    out_w = (RWKV_PROJ, ATT_DIM, ATT_KV_DIM, ATT_KV_DIM, D_MODEL, D_MODEL)
    return pl.pallas_call(
        _proj_kernel,
        grid=(n // tm,),
        in_specs=[pl.BlockSpec((tm, D_MODEL), row), _layer_spec((1, D_MODEL), layer),
                  _layer_spec((D_MODEL, IN_PROJ), layer),
                  pl.BlockSpec((tm, LANES), lambda i: (i % npos, 0)),
                  pl.BlockSpec((tm, LANES), lambda i: (i % npos, 0))],
        out_specs=[pl.BlockSpec((tm, w), row) for w in out_w],
        out_shape=[jax.ShapeDtypeStruct((n, w), F32) for w in out_w],
        compiler_params=_params(("arbitrary",)),
        name="proj",
    )(x2d, P['norm_mix_g'], P['w_in'], cos_t, sin_t)


def _prep_math(z, zp, mu, w0, w2p, a0, a2p, g2, k_k, k_a, r_k, ones_bd, outs):
    r_ref, lw_ref, kk_ref, b_ref, kf_ref, v_ref, g_ref, bonus_ref = outs
    zs = z + mu * (zp - z)
    d = RWKV_DIM
    r, k, v = zs[:, 0:d], zs[:, d:2 * d], zs[:, 2 * d:3 * d]
    zwa = zs[:, 3 * d:3 * d + LANES]
    zg = zs[:, 3 * d + LANES:3 * d + 2 * LANES]
    w_pre = w0 + _mm(jnp.tanh(zwa), w2p)
    neg = -w_pre
    softplus = jnp.maximum(neg, 0.0) + jnp.log1p(jnp.exp(-jnp.abs(neg)))
    w_log = -softplus - 0.5
    lw_ref[...] = -jnp.exp(w_log)
    a = jax.nn.sigmoid(a0 + _mm(zwa, a2p))
    g_ref[...] = _mm(jax.nn.sigmoid(zg), g2)
    kk0 = k * k_k
    norm = jnp.sqrt(_head_sum(kk0 * kk0, ones_bd))
    kk = kk0 / jnp.maximum(norm, 1e-12)
    kf = k * (1.0 + (a - 1.0) * k_a)
    r_ref[...] = r
    kk_ref[...] = kk
    b_ref[...] = kk * a
    kf_ref[...] = kf
    v_ref[...] = v
    bonus_ref[...] = _head_sum(r * kf * r_k, ones_bd) * v


def _prep_prompt_kernel(z_ref, mu_ref, w0_ref, w2_ref, a0_ref, a2_ref, g2_ref, kk_ref_, ka_ref, rk_ref, ones_ref,
                        *rest):
    outs, carry = rest[:8], rest[8]
    t = pl.program_id(1)

    @pl.when(t == 0)
    def _():
        carry[...] = jnp.zeros_like(carry)

    z = z_ref[...]
    row = lax.broadcasted_iota(jnp.int32, (z.shape[0], 1), 0)
    zp = jnp.where(row == 0, carry[SUBLANES - 1:SUBLANES, :], pltpu.roll(z, 1, 0))
    carry[...] = z[z.shape[0] - SUBLANES:, :]
    _prep_math(z, zp, mu_ref[...], w0_ref[...], w2_ref[...], a0_ref[...], a2_ref[...], g2_ref[...],
               kk_ref_[...], ka_ref[...], rk_ref[...], ones_ref[...], outs)


def _prep_sample_kernel(z_ref, sp_ref, mu_ref, w0_ref, w2_ref, a0_ref, a2_ref, g2_ref, kk_ref_, ka_ref, rk_ref,
                        ones_ref, *outs):
    z = z_ref[...]
    nb = sp_ref.shape[0]
    zp = jnp.concatenate([sp_ref[...], z[:z.shape[0] - nb, :]], axis=0)
    _prep_math(z, zp, mu_ref[...], w0_ref[...], w2_ref[...], a0_ref[...], a2_ref[...], g2_ref[...],
               kk_ref_[...], ka_ref[...], rk_ref[...], ones_ref[...], outs)


def _prep_consts(P):
    return [P['mu'], P['w0'], P['w2p'], P['a0'], P['a2p'], P['g2'], P['k_k'], P['k_a'], P['r_k'], P['ones_bd']]


def _prep_const_specs(layer):
    d = RWKV_DIM
    shapes = [(1, RWKV_PROJ), (1, d), (LANES, d), (1, d), (LANES, d), (D_GATE_LORA, d), (1, d), (1, d), (1, d)]
    return [_layer_spec(s, layer) for s in shapes] + [_const_spec((d, d))]


def _prep_prompt(z2d, P, batch, tm):
    n = z2d.shape[0]
    nt = n // batch // tm
    row = lambda b, t: (b * nt + t, 0)
    return pl.pallas_call(
        _prep_prompt_kernel,
        grid=(batch, nt),
        in_specs=[pl.BlockSpec((tm, RWKV_PROJ), row)] + _prep_const_specs(P['layer']),
        out_specs=[pl.BlockSpec((tm, RWKV_DIM), row)] * 8,
        out_shape=[jax.ShapeDtypeStruct((n, RWKV_DIM), F32)] * 8,
        scratch_shapes=[pltpu.VMEM((SUBLANES, RWKV_PROJ), F32)],
        compiler_params=_params(("arbitrary", "arbitrary")),
        name="prep_prompt",
    )(z2d, *_prep_consts(P))


def _prep_sample(z2d, shift_prev, P):
    n = z2d.shape[0]
    nb = shift_prev.shape[1]
    return pl.pallas_call(
        _prep_sample_kernel,
        grid=(1,),
        in_specs=[_const_spec((n, RWKV_PROJ)), _layer_spec((nb, RWKV_PROJ), P['layer'])]
        + _prep_const_specs(P['layer']),
        out_specs=[_const_spec((n, RWKV_DIM))] * 8,
        out_shape=[jax.ShapeDtypeStruct((n, RWKV_DIM), F32)] * 8,
        compiler_params=_params(("arbitrary",)),
        name="prep_sample",
    )(z2d, shift_prev, *_prep_consts(P))


def _wkv_prompt_kernel(r_ref, lw_ref, kk_ref, b_ref, kf_ref, v_ref, y_ref, st_ref, s_scr, *, chunks):
    C, G, M = CHUNK, HEADS_PER_GROUP, MXU_DIM
    nbatch = r_ref.shape[0]
    ngrp = RWKV_HEADS // G

    @pl.when(pl.program_id(0) == 0)
    def _():
        s_scr[...] = jnp.zeros_like(s_scr)

    ri = lax.broadcasted_iota(jnp.int32, (M, M), 0)
    ci = lax.broadcasted_iota(jnp.int32, (M, M), 1)
    strict = (ri % C) > (ci % C)
    incl = (ri % C) >= (ci % C)
    eye = (ri == ci).astype(F32)
    ti = lax.broadcasted_iota(jnp.int32, (C, C), 0)
    si = lax.broadcasted_iota(jnp.int32, (C, C), 1)
    tri = (ti >= si).astype(F32)
    lane_head = lax.broadcasted_iota(jnp.int32, (1, M), 1) // HEAD_DIM

    def stack(x):
        xb = x.astype(BF16)
        return jnp.concatenate([jnp.where(lane_head == h, xb, jnp.zeros_like(xb)) for h in range(G)], axis=0)

    def chunk_body(c, carry):
        rows = pl.ds(pl.multiple_of(c * C, C), C)
        probs = []
        for bi in range(nbatch):
            lw = lw_ref[bi, rows, :]
            cum = _mm_f32(tri, lw)
            w_end = jnp.exp(cum[C - 1:C, :])
            e_neg = jnp.exp(-cum)
            a_hat = -kk_ref[bi, rows, :] * jnp.exp(cum - lw)
            r_hat = r_ref[bi, rows, :] * jnp.exp(cum)
            b_til = b_ref[bi, rows, :] * e_neg
            k_til = kf_ref[bi, rows, :] * e_neg
            v = v_ref[bi, rows, :]
            for grp in range(ngrp):
                ls = slice(grp * M, (grp + 1) * M)
                we = w_end[:, ls]
                probs.append(dict(
                    bi=bi, grp=grp, ls=ls, we=we, a=stack(a_hat[:, ls]), r=stack(r_hat[:, ls]),
                    b=stack(b_til[:, ls]), k=stack(k_til[:, ls]), v=stack(v[:, ls]),
                    bb=stack(b_til[:, ls] * we), kb=stack(k_til[:, ls] * we)))
        l_ab = [jnp.where(strict, _mm_nt(p['a'], p['b']), 0.0) for p in probs]
        l_ak = [jnp.where(strict, _mm_nt(p['a'], p['k']), 0.0).astype(BF16) for p in probs]
        p_rb = [jnp.where(incl, _mm_nt(p['r'], p['b']), 0.0).astype(BF16) for p in probs]
        p_rk = [jnp.where(incl, _mm_nt(p['r'], p['k']), 0.0).astype(BF16) for p in probs]
        t_inv = [eye + x for x in l_ab]
        pw = [x.astype(BF16) for x in l_ab]
        for _ in range(C.bit_length() - 2):
            pw = [_mm(x, x).astype(BF16) for x in pw]
            t_inv = [t + _mm(x, t) for x, t in zip(pw, t_inv)]
        lv = [_mm(x, p['v']) for x, p in zip(l_ak, probs)]
        s0 = [s_scr[p['bi'], p['grp']] for p in probs]
        s0b = [x.astype(BF16) for x in s0]
        rhs = [_mm_nt(p['a'], x) + y for p, x, y in zip(probs, s0b, lv)]
        u = [_mm(t, x).astype(BF16) for t, x in zip(t_inv, rhs)]
        y_s = [_mm_nt(p['r'], x) + _mm(q, w) + _mm(z, p['v'])
               for p, x, q, w, z in zip(probs, s0b, p_rb, u, p_rk)]
        for p, ys, x, w in zip(probs, y_s, s0, u):
            y = ys[0:C]
            for h in range(1, G):
                y = y + ys[h * C:(h + 1) * C]
            y_ref[p['bi'], rows, p['ls']] = y
            s_scr[p['bi'], p['grp']] = x * p['we'] + _mm_tn(w, p['bb']) + _mm_tn(p['v'], p['kb'])
        return carry

    lax.fori_loop(0, chunks, chunk_body, 0)

    @pl.when(pl.program_id(0) == pl.num_programs(0) - 1)
    def _():
        st_ref[...] = s_scr[...]


def _wkv_prompt(r, lw, kk, b, kf, v, batch, tl):
    n = r.shape[0]
    seq = n // batch
    ngrp = RWKV_HEADS // HEADS_PER_GROUP
    blk = pl.BlockSpec((batch, tl, RWKV_DIM), lambda t: (0, t, 0))
    y, st = pl.pallas_call(
        functools.partial(_wkv_prompt_kernel, chunks=tl // CHUNK),
        grid=(seq // tl,),
        in_specs=[blk] * 6,
        out_specs=[blk, _const_spec((batch, ngrp, MXU_DIM, MXU_DIM))],
        out_shape=[jax.ShapeDtypeStruct((batch, seq, RWKV_DIM), F32),
                   jax.ShapeDtypeStruct((batch, ngrp, MXU_DIM, MXU_DIM), F32)],
        scratch_shapes=[pltpu.VMEM((batch, ngrp, MXU_DIM, MXU_DIM), F32)],
        compiler_params=_params(("arbitrary",)),
        name="wkv_prompt",
    )(*(t.reshape(batch, seq, RWKV_DIM) for t in (r, lw, kk, b, kf, v)))
    st = st.reshape(batch, ngrp, HEADS_PER_GROUP, HEAD_DIM, HEADS_PER_GROUP, HEAD_DIM)
    st = jnp.stack([st[:, :, h, :, h, :] for h in range(HEADS_PER_GROUP)], axis=2)
    st = st.reshape(batch, RWKV_HEADS, HEAD_DIM, HEAD_DIM)
    return y.reshape(n, RWKV_DIM), st


def _wkv_sample_kernel(r_ref, lw_ref, kk_ref, b_ref, kf_ref, v_ref, s_ref, y_ref, so_ref, *, steps):
    def value_row(vi, carry):
        s = s_ref[0, vi]
        for t in range(steps):
            sk = jnp.sum(s * kk_ref[t, 0], axis=0, keepdims=True)
            v_row = v_ref[t, 0, pl.ds(vi, 1), :]
            s = s * jnp.exp(lw_ref[t, 0]) - sk * b_ref[t, 0] + v_row * kf_ref[t, 0]
            y_ref[t, 0, pl.ds(vi, 1), :] = jnp.sum(s * r_ref[t, 0], axis=0, keepdims=True)
        so_ref[0, vi] = s
        return carry

    lax.fori_loop(0, HEAD_DIM, value_row, 0)


def _wkv_sample(r, lw, kk, b, kf, v, s0, layer, steps):
    nb = r.shape[-1]
    vec = pl.BlockSpec((steps, 1, HEAD_DIM, nb), lambda h: (0, h, 0, 0))
    st = pl.BlockSpec((1, HEAD_DIM, HEAD_DIM, nb), lambda h: (h, 0, 0, 0))
    st_in = pl.BlockSpec((None, 1, HEAD_DIM, HEAD_DIM, nb), lambda h: (layer, h, 0, 0, 0))
    return pl.pallas_call(
        functools.partial(_wkv_sample_kernel, steps=steps),
        grid=(RWKV_HEADS,),
        in_specs=[vec] * 6 + [st_in],
        out_specs=[vec, st],
        out_shape=[jax.ShapeDtypeStruct(r.shape, F32), jax.ShapeDtypeStruct(s0.shape[1:], F32)],
        compiler_params=_params(("arbitrary",)),
        name="wkv_sample",
    )(r, lw, kk, b, kf, v, s0)


def _swa_prompt_kernel(q_ref, kc_ref, kp_ref, vc_ref, vp_ref, sink_ref, o_ref, *, sub):
    blk = WINDOW
    first = pl.program_id(1) == 0
    k_all = jnp.concatenate([kp_ref[...], kc_ref[...]], axis=0)
    v_all = jnp.concatenate([vp_ref[...], vc_ref[...]], axis=0).astype(BF16)
    v_ext = jnp.concatenate([v_all, jnp.ones_like(v_all)], axis=1)
    k_swapped = pltpu.roll(k_all, HEAD_DIM, 1)
    lane = lax.broadcasted_iota(jnp.int32, (1, LANES), 1)
    low = lane < HEAD_DIM
    k_var = {(0, 0): jnp.where(low, k_all, 0.0), (0, 1): jnp.where(low, 0.0, k_swapped),
             (1, 1): jnp.where(low, 0.0, k_all), (1, 0): jnp.where(low, k_swapped, 0.0)}
    k_var = {key: val.astype(BF16) for key, val in k_var.items()}
    order = [j * ATT_GROUP + g for j in range(ATT_KV_HEADS) for half in range(2)
             for g in range(ATT_GROUP) if g % 2 == half]
    n_rows = ATT_Q_HEADS * blk
    qi = lax.broadcasted_iota(jnp.int32, (n_rows, 2 * blk), 0) % blk
    kj = lax.broadcasted_iota(jnp.int32, (n_rows, 2 * blk), 1)
    diff = blk + qi - kj
    band = (diff >= 0) & (diff < WINDOW)
    row_head = lax.broadcasted_iota(jnp.int32, (n_rows, 1), 0) // blk
    sink = jnp.zeros((n_rows, 1), F32)
    for idx, head in enumerate(order):
        sink = jnp.where(row_head == idx, sink_ref[0:1, head:head + 1], sink)
    scale = HEAD_DIM ** -0.5
    for sb in range(sub):
        qs = slice(sb * blk, (sb + 1) * blk)
        ks = slice(sb * blk, (sb + 2) * blk)
        parts = []
        for j in range(ATT_KV_HEADS):
            for half in range(2):
                heads = [j * ATT_GROUP + g for g in range(ATT_GROUP) if g % 2 == half]
                q_cat = jnp.concatenate([q_ref[qs, (h // 2) * LANES:(h // 2 + 1) * LANES] for h in heads], axis=0)
                parts.append(_mm_nt(q_cat * scale, k_var[(j, half)][ks]))
        s = jnp.concatenate(parts, axis=0)
        ok = band & jnp.logical_or(kj >= blk, jnp.logical_not(first)) if sb == 0 else band
        s = jnp.where(ok, s, -jnp.inf)
        m = jnp.maximum(jnp.max(s, axis=-1, keepdims=True), sink)
        e = jnp.exp(s - m).astype(BF16)
        ev = jnp.dot(e, v_ext[ks], preferred_element_type=F32)
        o = ev[:, 0:LANES] / (ev[:, LANES:LANES + 1] + jnp.exp(sink - m))
        by_head = {}
        for idx, head in enumerate(order):
            o_h = o[idx * blk:(idx + 1) * blk]
            by_head[head] = pltpu.roll(o_h, HEAD_DIM, 1) if head // ATT_GROUP != head % 2 else o_h
        for grp in range(ATT_DIM // LANES):
            o_ref[qs, grp * LANES:(grp + 1) * LANES] = jnp.where(low, by_head[2 * grp], by_head[2 * grp + 1])


def _swa_prompt(q, k, v, P, batch, sub):
    n = q.shape[0]
    rows = sub * WINDOW
    nb = n // batch // rows
    cur = lambda b, i: (b * nb + i, 0)
    prev = lambda b, i: ((b * nb + i) * sub - jnp.minimum(i, 1), 0)
    return pl.pallas_call(
        functools.partial(_swa_prompt_kernel, sub=sub),
        grid=(batch, nb),
        in_specs=[pl.BlockSpec((rows, ATT_DIM), cur),
                  pl.BlockSpec((rows, ATT_KV_DIM), cur), pl.BlockSpec((WINDOW, ATT_KV_DIM), prev),
                  pl.BlockSpec((rows, ATT_KV_DIM), cur), pl.BlockSpec((WINDOW, ATT_KV_DIM), prev),
                  _layer_spec((1, ATT_Q_HEADS), P['layer'])],
        out_specs=pl.BlockSpec((rows, ATT_DIM), cur),
        out_shape=jax.ShapeDtypeStruct((n, ATT_DIM), F32),
        compiler_params=_params(("arbitrary", "arbitrary")),
        name="swa_prompt",
    )(q, k, k, v, v, P['sinks'])


def _swa_sample_kernel(q_ref, kc_ref, vc_ref, kn_ref, vn_ref, sink_ref, o_ref, *, steps):
    q = q_ref[...].astype(BF16)
    rows = q.shape[1]
    nk = kc_ref.shape[1]
    nn = kn_ref.shape[1]
    bdot = lambda a, b, dims: lax.dot_general(a, b, (dims, ((0,), (0,))), preferred_element_type=F32)
    s_c = bdot(q, kc_ref[...].astype(BF16), ((2,), (2,))) * (HEAD_DIM ** -0.5)
    s_n = bdot(q, kn_ref[...].astype(BF16), ((2,), (2,))) * (HEAD_DIM ** -0.5)
    t_c = lax.broadcasted_iota(jnp.int32, (1, rows, nk), 1) % steps
    w_c = lax.broadcasted_iota(jnp.int32, (1, rows, nk), 2)
    s_c = jnp.where(w_c > t_c, s_c, -jnp.inf)
    t_n = lax.broadcasted_iota(jnp.int32, (1, rows, nn), 1) % steps
    w_n = lax.broadcasted_iota(jnp.int32, (1, rows, nn), 2)
    s_n = jnp.where(w_n <= t_n, s_n, -jnp.inf)
    sink = sink_ref[...][None, :, 0:1]
    m = jnp.maximum(jnp.maximum(jnp.max(s_c, axis=-1, keepdims=True), jnp.max(s_n, axis=-1, keepdims=True)), sink)
    e_c = jnp.exp(s_c - m)
    e_n = jnp.exp(s_n - m)
    den = jnp.sum(e_c, axis=-1, keepdims=True) + jnp.sum(e_n, axis=-1, keepdims=True) + jnp.exp(sink - m)
    o = (bdot((e_c / den).astype(BF16), vc_ref[...].astype(BF16), ((2,), (1,)))
         + bdot((e_n / den).astype(BF16), vn_ref[...].astype(BF16), ((2,), (1,))))
    o_ref[...] = o


def _swa_sample(q_st, k_cache, v_cache, k_new, v_new, sink_rows, layer, steps, bt):
    nb, rows, _ = q_st.shape
    nn = k_new.shape[1]
    blk = lambda r: pl.BlockSpec((bt, r, LANES), lambda i: (i, 0, 0))
    cache = pl.BlockSpec((None, bt, WINDOW, LANES), lambda i: (layer, i, 0, 0))
    return pl.pallas_call(
        functools.partial(_swa_sample_kernel, steps=steps),
        grid=(nb // bt,),
        in_specs=[blk(rows), cache, cache, blk(nn), blk(nn), _layer_spec((rows, LANES), layer)],
        out_specs=blk(rows),
        out_shape=jax.ShapeDtypeStruct((nb, rows, LANES), F32),
        compiler_params=_params(("arbitrary",)),
        name="swa_sample",
    )(q_st, k_cache, v_cache, k_new, v_new, sink_rows)


def _merge_kernel(x_ref, y_ref, bonus_ref, g_ref, oa_ref, gr_ref, ga_ref, lng_ref, lnb_ref, ones_ref,
                  wa_ref, wb_ref, wo_ref, o_ref):
    y = y_ref[...]
    ones_bd = ones_ref[...]
    mean = _head_sum(y, ones_bd) * (1.0 / HEAD_DIM)
    d = y - mean
    var = _head_sum(d * d, ones_bd) * (1.0 / HEAD_DIM)
    yn = d * lax.rsqrt(var + GN_EPS) * lng_ref[...] + lnb_ref[...]
    o_r = (yn + bonus_ref[...]) * g_ref[...]
    merged = gr_ref[...] * _mm(o_r, wa_ref[...]) + ga_ref[...] * _mm(oa_ref[...], wb_ref[...])
    o_ref[...] = x_ref[...] + _mm(merged, wo_ref[...])


def _merge(x, y, bonus, g, o_a, gr, ga, P, tm):
    n = x.shape[0]
    layer = P['layer']
    row = lambda i: (i, 0)
    wide = pl.BlockSpec((tm, D_MODEL), row)
    half = pl.BlockSpec((tm, RWKV_DIM), row)
    return pl.pallas_call(
        _merge_kernel,
        grid=(n // tm,),
        in_specs=[wide, half, half, half, half, wide, wide,
                  _layer_spec((1, RWKV_DIM), layer), _layer_spec((1, RWKV_DIM), layer),
                  _const_spec((RWKV_DIM, RWKV_DIM)),
                  _layer_spec((RWKV_DIM, D_MODEL), layer), _layer_spec((ATT_DIM, D_MODEL), layer),
                  _layer_spec((D_MODEL, D_MODEL), layer)],
        out_specs=wide,
        out_shape=jax.ShapeDtypeStruct((n, D_MODEL), F32),
        compiler_params=_params(("arbitrary",)),
        name="merge",
    )(x, y, bonus, g, o_a, gr, ga, P['ln_g'], P['ln_b'], P['ones_bd'], P['w_br_rwkv'], P['w_br_attn'], P['w_out'])


def _gelu(x):
    return 0.5 * x * (1.0 + lax.erf(x * (2.0 ** -0.5)))


def _ffn_tail(x, c, up, c1, c2, cw_ref, cb_ref, wd_ref, gf_ref, o_ref, final):
    conv = cb_ref[...] + c2 * cw_ref[0:1, :] + c1 * cw_ref[1:2, :] + c * cw_ref[2:3, :]
    out = x + _mm(_gelu(conv) * up, wd_ref[...])
    o_ref[...] = _rmsnorm(out, gf_ref[...]) if final else out


def _ffn_prompt_kernel(x_ref, gn_ref, wi_ref, cw_ref, cb_ref, wd_ref, gf_ref, o_ref, ct_ref, carry, *, final):
    @pl.when(pl.program_id(1) == 0)
    def _():
        carry[...] = jnp.zeros_like(carry)

    x = x_ref[...]
    h = _rmsnorm(x, gn_ref[...]).astype(BF16)
    c = jnp.dot(h, wi_ref[:, 0:D_FF], preferred_element_type=F32)
    up = jnp.dot(h, wi_ref[:, D_FF:2 * D_FF], preferred_element_type=F32)
    tm = c.shape[0]
    row = lax.broadcasted_iota(jnp.int32, (tm, 1), 0)
    last = carry[SUBLANES - 1:SUBLANES, :]
    c1 = jnp.where(row == 0, last, pltpu.roll(c, 1, 0))
    c2 = jnp.where(row == 0, carry[SUBLANES - 2:SUBLANES - 1, :],
                   jnp.where(row == 1, last, pltpu.roll(c, 2, 0)))
    tail = c[tm - SUBLANES:, :]
    carry[...] = tail
    ct_ref[0] = tail
    _ffn_tail(x, c, up, c1, c2, cw_ref, cb_ref, wd_ref, gf_ref, o_ref, final)


def _ffn_sample_kernel(x_ref, cp_ref, gn_ref, wi_ref, cw_ref, cb_ref, wd_ref, gf_ref, o_ref, ct_ref, *, final):
    x = x_ref[...]
    h = _rmsnorm(x, gn_ref[...]).astype(BF16)
    c = jnp.dot(h, wi_ref[:, 0:D_FF], preferred_element_type=F32)
    up = jnp.dot(h, wi_ref[:, D_FF:2 * D_FF], preferred_element_type=F32)
    n = c.shape[0]
    nb = cp_ref.shape[1]
    p0, p1 = cp_ref[0], cp_ref[1]
    c1 = jnp.concatenate([p1, c[:n - nb]], axis=0)
    c2 = jnp.concatenate([p0, p1, c[:n - 2 * nb]], axis=0)
    ct_ref[0] = c[n - 2 * nb:n - nb, :]
    ct_ref[1] = c[n - nb:, :]
    _ffn_tail(x, c, up, c1, c2, cw_ref, cb_ref, wd_ref, gf_ref, o_ref, final)


def _ffn_weight_specs(layer):
    return [_layer_spec((1, D_MODEL), layer), _layer_spec((D_MODEL, 2 * D_FF), layer),
            _layer_spec((CONV_W, D_FF), layer), _layer_spec((1, D_FF), layer), _layer_spec((D_FF, D_MODEL), layer),
            _const_spec((1, D_MODEL))]


def _ffn_weights(P, g_final):
    return [P['norm_ffn_g'], P['ffn_w_in'], P['ffn_conv_w'], P['ffn_conv_b'], P['ffn_w_down'], g_final]


def _ffn_prompt(x, P, g_final, final, batch, tm):
    n = x.shape[0]
    nt = n // batch // tm
    row = lambda b, t: (b * nt + t, 0)
    return pl.pallas_call(
        functools.partial(_ffn_prompt_kernel, final=final),
        grid=(batch, nt),
        in_specs=[pl.BlockSpec((tm, D_MODEL), row)] + _ffn_weight_specs(P['layer']),
        out_specs=[pl.BlockSpec((tm, D_MODEL), row), pl.BlockSpec((1, SUBLANES, D_FF), lambda b, t: (b, 0, 0))],
        out_shape=[jax.ShapeDtypeStruct((n, D_MODEL), F32), jax.ShapeDtypeStruct((batch, SUBLANES, D_FF), F32)],
        scratch_shapes=[pltpu.VMEM((SUBLANES, D_FF), F32)],
        compiler_params=_params(("arbitrary", "arbitrary")),
        name="ffn_prompt",
    )(x, *_ffn_weights(P, g_final))


def _ffn_sample(x, conv_prev, P, g_final, final):
    n = x.shape[0]
    nb = conv_prev.shape[2]
    return pl.pallas_call(
        functools.partial(_ffn_sample_kernel, final=final),
        grid=(1,),
        in_specs=[_const_spec((n, D_MODEL)), _layer_spec((CONV_W - 1, nb, D_FF), P['layer'])]
        + _ffn_weight_specs(P['layer']),
        out_specs=[_const_spec((n, D_MODEL)), _const_spec((CONV_W - 1, nb, D_FF))],
        out_shape=[jax.ShapeDtypeStruct((n, D_MODEL), F32), jax.ShapeDtypeStruct((CONV_W - 1, nb, D_FF), F32)],
        compiler_params=_params(("arbitrary",)),
        name="ffn_sample",
    )(x, conv_prev, *_ffn_weights(P, g_final))


def _rope_tables(pos):
    inv = ROPE_THETA ** (-jnp.arange(0, HEAD_DIM, 2, dtype=F32) / HEAD_DIM)
    ang = pos.astype(F32)[:, None] * inv[None, :]
    cos, sin = jnp.cos(ang), jnp.sin(ang)
    reps = LANES // HEAD_DIM
    return jnp.tile(jnp.concatenate([cos, cos], axis=1), (1, reps)), jnp.tile(jnp.concatenate([-sin, sin], axis=1),
                                                                             (1, reps))


def _stacked_params(norm_mix_g, w_in, rwkv_mu, rwkv_w0, rwkv_w2, rwkv_a0, rwkv_a2, rwkv_g2, rwkv_k_k, rwkv_k_a,
                    rwkv_r_k, rwkv_ln_g, rwkv_ln_b, attn_sinks, w_br_rwkv, w_br_attn, w_out, norm_ffn_g, ffn_w_in,
                    ffn_conv_w, ffn_conv_b, ffn_w_down, steps):
    d = RWKV_DIM
    depth = w_in.shape[0]
    head = jnp.arange(d) // HEAD_DIM
    row = lambda t: t.reshape(depth, 1, -1)
    sink_rows = jnp.broadcast_to(jnp.repeat(attn_sinks, steps, axis=1)[:, :, None],
                                 (depth, ATT_Q_HEADS * steps, LANES))
    return {
        'norm_mix_g': row(norm_mix_g), 'w_in': w_in.astype(BF16),
        'mu': row(rwkv_mu), 'w0': row(rwkv_w0),
        'w2p': jnp.pad(rwkv_w2, ((0, 0), (0, D_AAA_LORA), (0, 0))).astype(BF16),
        'a0': row(rwkv_a0),
        'a2p': jnp.pad(rwkv_a2, ((0, 0), (D_DECAY_LORA, 0), (0, 0))).astype(BF16),
        'g2': rwkv_g2.astype(BF16), 'k_k': row(rwkv_k_k), 'k_a': row(rwkv_k_a),
        'r_k': row(rwkv_r_k), 'ln_g': row(rwkv_ln_g), 'ln_b': row(rwkv_ln_b),
        'ones_bd': (head[:, None] == head[None, :]).astype(BF16),
        'sinks': row(attn_sinks), 'sink_rows': sink_rows,
        'w_br_rwkv': w_br_rwkv.astype(BF16), 'w_br_attn': w_br_attn.astype(BF16),
        'w_out': w_out.astype(BF16), 'norm_ffn_g': row(norm_ffn_g),
        'ffn_w_in': ffn_w_in.astype(BF16), 'ffn_conv_w': ffn_conv_w, 'ffn_conv_b': row(ffn_conv_b),
        'ffn_w_down': ffn_w_down.astype(BF16),
    }


def _prompt_layer(x, P, tables, g_final, final, batch, seq):
    tm = min(256, seq)
    z, q, k, v, gr, ga = _proj(x, P, tables[0], tables[1], tm)
    r, lw, kk, b, kf, vv, g, bonus = _prep_prompt(z, P, batch, tm)
    y, s_t = _wkv_prompt(r, lw, kk, b, kf, vv, batch, min(256, seq))
    o_a = _swa_prompt(q, k, v, P, batch, min(2, seq // WINDOW))
    x1 = _merge(x, y, bonus, g, o_a, gr, ga, P, tm)
    x2, c_tail = _ffn_prompt(x1, P, g_final, final, batch, tm)
    last = lambda t: t.reshape(batch, seq, -1)[:, -WINDOW:].reshape(batch, WINDOW, ATT_KV_HEADS, HEAD_DIM)
    state = (z.reshape(batch, seq, RWKV_PROJ)[:, -1], s_t, last(k), last(v), c_tail[:, SUBLANES - (CONV_W - 1):])
    return x2, state


def _sample_layer(x, P, tables, g_final, final, shift_prev, wkv0, k_cache, v_cache, conv_prev, nb, steps):
    n = nb * steps
    layer = P['layer']
    z, q, k, v, gr, ga = _proj(x, P, tables[0], tables[1], n)
    r, lw, kk, b, kf, vv, g, bonus = _prep_sample(z, shift_prev, P)
    to_lanes = lambda t: t.reshape(steps, nb, RWKV_HEADS, HEAD_DIM).transpose(0, 2, 3, 1)
    y_l, s_l = _wkv_sample(*(to_lanes(t) for t in (r, lw, kk, b, kf, vv)), wkv0, layer, steps)
    y = y_l.transpose(0, 3, 1, 2).reshape(n, RWKV_DIM)

    rows = ATT_Q_HEADS * steps
    q5 = q.reshape(steps, nb, ATT_KV_HEADS, ATT_GROUP, HEAD_DIM).transpose(1, 2, 3, 0, 4)
    sel = jnp.eye(ATT_KV_HEADS, dtype=F32)
    q_st = (q5[:, :, :, :, None, :] * sel[None, :, None, None, :, None]).reshape(nb, rows, ATT_KV_DIM)
    pad = ((0, 0), (0, SUBLANES - steps), (0, 0))
    k_new = k.reshape(steps, nb, ATT_KV_DIM).transpose(1, 0, 2)
    v_new = v.reshape(steps, nb, ATT_KV_DIM).transpose(1, 0, 2)
    o_st = _swa_sample(q_st, k_cache, v_cache, jnp.pad(k_new, pad), jnp.pad(v_new, pad), P['sink_rows'], layer,
                       steps, 16)
    o5 = o_st.reshape(nb, ATT_KV_HEADS, ATT_GROUP, steps, ATT_KV_HEADS, HEAD_DIM)
    o_a = jnp.stack([o5[:, j, :, :, j, :] for j in range(ATT_KV_HEADS)], axis=0)
    o_a = o_a.transpose(3, 1, 0, 2, 4).reshape(n, ATT_DIM)

    x1 = _merge(x, y, bonus, g, o_a, gr, ga, P, n)
    x2, c_tail = _ffn_sample(x1, conv_prev, P, g_final, final)
    state = (z[n - nb:], s_l, k_new, v_new, c_tail)
    return x2, state


def kernel(x_prompt, x_sample, state_rwkv_shift, state_rwkv_wkv, cache_swa_k, cache_swa_v, state_ffn_conv, norm_mix_g, w_in, rwkv_mu, rwkv_w0, rwkv_w2, rwkv_a0, rwkv_a2, rwkv_g2, rwkv_k_k, rwkv_k_a, rwkv_r_k, rwkv_ln_g, rwkv_ln_b, attn_sinks, w_br_rwkv, w_br_attn, w_out, norm_ffn_g, ffn_w_in, ffn_conv_w, ffn_conv_b, ffn_w_down, norm_final_g):
    bp, tp, _ = x_prompt.shape
    nb, steps, _ = x_sample.shape
    depth = w_in.shape[0]
    tab_p = _rope_tables(jnp.arange(tp, dtype=jnp.int32))
    tab_s = _rope_tables(PAST_LEN + jnp.repeat(jnp.arange(steps, dtype=jnp.int32), nb))
    g_final = norm_final_g[None]
    params = _stacked_params(norm_mix_g, w_in, rwkv_mu, rwkv_w0, rwkv_w2, rwkv_a0, rwkv_a2, rwkv_g2, rwkv_k_k,
                             rwkv_k_a, rwkv_r_k, rwkv_ln_g, rwkv_ln_b, attn_sinks, w_br_rwkv, w_br_attn, w_out,
                             norm_ffn_g, ffn_w_in, ffn_conv_w, ffn_conv_b, ffn_w_down, steps)
    wkv0 = state_rwkv_wkv.transpose(0, 2, 3, 4, 1)
    k_cache = cache_swa_k.reshape(depth, nb, WINDOW, ATT_KV_DIM)
    v_cache = cache_swa_v.reshape(depth, nb, WINDOW, ATT_KV_DIM)
    conv_prev = state_ffn_conv.transpose(0, 2, 1, 3)
    xp = x_prompt.reshape(bp * tp, D_MODEL)
    xs = x_sample.transpose(1, 0, 2).reshape(steps * nb, D_MODEL)
    outs_p, outs_s = [], []
    for l in range(depth):
        P = dict(params, layer=l)
        final = l == depth - 1
        xp, sp = _prompt_layer(xp, P, tab_p, g_final, final, bp, tp)
        xs, ss = _sample_layer(xs, P, tab_s, g_final, final, state_rwkv_shift, wkv0, k_cache, v_cache, conv_prev,
                               nb, steps)
        outs_p.append(sp)
        outs_s.append(ss)
    y_prompt = xp.reshape(bp, tp, D_MODEL)
    y_sample = xs.reshape(steps, nb, D_MODEL).transpose(1, 0, 2)
    stack = lambda outs, i: jnp.stack([o[i] for o in outs])
    kv_shape = (depth, nb, steps, ATT_KV_HEADS, HEAD_DIM)
    s_k = jnp.concatenate([cache_swa_k[:, :, steps:], stack(outs_s, 2).reshape(kv_shape)], axis=2)
    s_v = jnp.concatenate([cache_swa_v[:, :, steps:], stack(outs_s, 3).reshape(kv_shape)], axis=2)
    s_wkv = stack(outs_s, 1).transpose(0, 4, 1, 2, 3)
    s_conv = stack(outs_s, 4).transpose(0, 2, 1, 3)
    return (y_prompt, y_sample, *(stack(outs_p, i) for i in range(5)),
            stack(outs_s, 0), s_wkv, s_k, s_v, s_conv)
```

```python
import functools

import jax
import jax.numpy as jnp
from jax import lax
from jax.experimental import pallas as pl
from jax.experimental.pallas import tpu as pltpu

F32 = jnp.float32
BF16 = jnp.bfloat16

D_MODEL = 1024
HEAD_DIM = 64
RWKV_HEADS = 8
RWKV_DIM = RWKV_HEADS * HEAD_DIM
D_DECAY_LORA = 64
D_AAA_LORA = 64
D_GATE_LORA = 128
RWKV_PROJ = 3 * RWKV_DIM + D_DECAY_LORA + D_AAA_LORA + D_GATE_LORA
ATT_Q_HEADS = 8
ATT_KV_HEADS = 2
ATT_GROUP = ATT_Q_HEADS // ATT_KV_HEADS
ATT_DIM = ATT_Q_HEADS * HEAD_DIM
ATT_KV_DIM = ATT_KV_HEADS * HEAD_DIM
WINDOW = 128
IN_PROJ = RWKV_PROJ + ATT_DIM + 2 * ATT_KV_DIM + 2 * D_MODEL
D_FF = 2816
CONV_W = 3
ROPE_THETA = 10000.0
RMS_EPS = 1e-6
GN_EPS = 64e-5
PAST_LEN = 16384

LANES = 128
SUBLANES = 8
MXU_DIM = 256
CHUNK = 64
HEADS_PER_GROUP = MXU_DIM // HEAD_DIM
VMEM_LIMIT = 56 * 1024 * 1024


def _params(sem):
    return pltpu.CompilerParams(dimension_semantics=sem, vmem_limit_bytes=VMEM_LIMIT)


def _const_spec(shape):
    zeros = (0,) * len(shape)
    return pl.BlockSpec(shape, lambda *_: zeros, pipeline_mode=pl.Buffered(1))


def _layer_spec(shape, layer):
    index = (layer,) + (0,) * len(shape)
    return pl.BlockSpec((None,) + tuple(shape), lambda *_: index, pipeline_mode=pl.Buffered(1))


def _mm(a, b):
    return jnp.dot(a.astype(BF16), b.astype(BF16), preferred_element_type=F32)


def _mm_nt(a, b):
    return lax.dot_general(a.astype(BF16), b.astype(BF16), (((1,), (1,)), ((), ())),
                           preferred_element_type=F32)


def _mm_f32(a, b):
    return jnp.dot(a, b, precision=lax.Precision.HIGHEST, preferred_element_type=F32)


def _mm_tn(a, b):
    return lax.dot_general(a.astype(BF16), b.astype(BF16), (((0,), (0,)), ((), ())),
                           preferred_element_type=F32)


def _head_sum(x, ones_bd):
    hi = x.astype(BF16)
    lo = (x - hi.astype(F32)).astype(BF16)
    return (jnp.dot(hi, ones_bd, preferred_element_type=F32)
            + jnp.dot(lo, ones_bd, preferred_element_type=F32))


def _rmsnorm(x, g):
    return x * lax.rsqrt(jnp.mean(x * x, axis=-1, keepdims=True) + RMS_EPS) * g


def _rope(x, cos, sin_signed, first_half):
    partner = jnp.where(first_half, pltpu.roll(x, LANES - HEAD_DIM // 2, 1), pltpu.roll(x, HEAD_DIM // 2, 1))
    return x * cos + partner * sin_signed


def _prep_math(z, zp, mu, w0, w2p, a0, a2p, g2, k_k, k_a, r_k, ones_bd, outs):
    r_ref, lw_ref, kk_ref, b_ref, kf_ref, v_ref, g_ref, bonus_ref = outs
    zs = z + mu * (zp - z)
    d = RWKV_DIM
    r, k, v = zs[:, 0:d], zs[:, d:2 * d], zs[:, 2 * d:3 * d]
    zwa = zs[:, 3 * d:3 * d + LANES]
    zg = zs[:, 3 * d + LANES:3 * d + 2 * LANES]
    w_pre = w0 + _mm(jnp.tanh(zwa), w2p)
    neg = -w_pre
    softplus = jnp.maximum(neg, 0.0) + jnp.log1p(jnp.exp(-jnp.abs(neg)))
    w_log = -softplus - 0.5
    lw_ref[...] = -jnp.exp(w_log)
    a = jax.nn.sigmoid(a0 + _mm(zwa, a2p))
    g_ref[...] = _mm(jax.nn.sigmoid(zg), g2).astype(g_ref.dtype)
    kk0 = k * k_k
    norm = jnp.sqrt(_head_sum(kk0 * kk0, ones_bd))
    kk = kk0 / jnp.maximum(norm, 1e-12)
    kf = k * (1.0 + (a - 1.0) * k_a)
    r_ref[...] = r.astype(r_ref.dtype)
    kk_ref[...] = kk.astype(kk_ref.dtype)
    b_ref[...] = (kk * a).astype(b_ref.dtype)
    kf_ref[...] = kf.astype(kf_ref.dtype)
    v_ref[...] = v.astype(v_ref.dtype)
    bonus_ref[...] = (_head_sum(r * kf * r_k, ones_bd) * v).astype(bonus_ref.dtype)


N_PREP_CONSTS = 10
N_FRONT_OUTS = 14


def _front_kernel(*refs, sample):
    if sample:
        x_ref, sp_ref, g_ref, w_ref, cos_ref, sin_ref = refs[:6]
        rest = refs[6:]
    else:
        x_ref, g_ref, w_ref, cos_ref, sin_ref = refs[:5]
        rest = refs[5:]
    consts = [c[...] for c in rest[:N_PREP_CONSTS]]
    outs = rest[N_PREP_CONSTS:N_PREP_CONSTS + N_FRONT_OUTS]
    q_ref, k_ref, v_ref, gr_ref, ga_ref, zt_ref = outs[:6]

    h = _rmsnorm(x_ref[...], g_ref[...]).astype(BF16)
    c0, c1, c2, c3, c4 = (RWKV_PROJ, RWKV_PROJ + ATT_DIM, RWKV_PROJ + ATT_DIM + ATT_KV_DIM,
                          RWKV_PROJ + ATT_DIM + 2 * ATT_KV_DIM, RWKV_PROJ + ATT_DIM + 2 * ATT_KV_DIM + D_MODEL)
    z = jnp.dot(h, w_ref[:, 0:c0], preferred_element_type=F32)
    n = z.shape[0]
    if sample:
        nb = sp_ref.shape[0]
        zp = jnp.concatenate([sp_ref[...], z[:n - nb, :]], axis=0)
        zt_ref[...] = z[n - nb:, :]
    else:
        carry = rest[N_PREP_CONSTS + N_FRONT_OUTS]

        @pl.when(pl.program_id(1) == 0)
        def _():
            carry[...] = jnp.zeros_like(carry)

        row = lax.broadcasted_iota(jnp.int32, (n, 1), 0)
        zp = jnp.where(row == 0, carry[SUBLANES - 1:SUBLANES, :], pltpu.roll(z, 1, 0))
        tail = z[n - SUBLANES:, :]
        carry[...] = tail
        zt_ref[0] = tail
    _prep_math(z, zp, *consts, outs[6:])

    cos = cos_ref[...]
    sin = sin_ref[...]
    lane = lax.broadcasted_iota(jnp.int32, (1, LANES), 1)
    first_half = (lane % HEAD_DIM) < (HEAD_DIM // 2)
    q = jnp.dot(h, w_ref[:, c0:c1], preferred_element_type=F32)
    for grp in range(ATT_DIM // LANES):
        sl = slice(grp * LANES, (grp + 1) * LANES)
        q_ref[:, sl] = _rope(q[:, sl], cos, sin, first_half).astype(q_ref.dtype)
    k = jnp.dot(h, w_ref[:, c1:c2], preferred_element_type=F32)
    k_ref[...] = _rope(k, cos, sin, first_half)
    v_ref[...] = jnp.dot(h, w_ref[:, c2:c3], preferred_element_type=F32)
    gr_ref[...] = jax.nn.sigmoid(jnp.dot(h, w_ref[:, c3:c4], preferred_element_type=F32)).astype(gr_ref.dtype)
    ga_ref[...] = jax.nn.sigmoid(jnp.dot(h, w_ref[:, c4:IN_PROJ], preferred_element_type=F32)).astype(ga_ref.dtype)


def _prep_consts(P):
    return [P['mu'], P['w0'], P['w2p'], P['a0'], P['a2p'], P['g2'], P['k_k'], P['k_a'], P['r_k'], P['ones_bd']]


def _prep_const_specs(layer):
    d = RWKV_DIM
    shapes = [(1, RWKV_PROJ), (1, d), (LANES, d), (1, d), (LANES, d), (D_GATE_LORA, d), (1, d), (1, d), (1, d)]
    return [_layer_spec(s, layer) for s in shapes] + [_const_spec((d, d))]


def _front_out_types(n, tail_shape):
    d = RWKV_DIM
    widths = [(ATT_DIM, BF16), (ATT_KV_DIM, F32), (ATT_KV_DIM, F32), (D_MODEL, BF16), (D_MODEL, BF16)]
    prep = [(d, BF16), (d, F32), (d, BF16), (d, BF16), (d, BF16), (d, BF16), (d, BF16), (d, BF16)]
    shapes = ([jax.ShapeDtypeStruct((n, w), t) for w, t in widths] + [jax.ShapeDtypeStruct(tail_shape, F32)]
              + [jax.ShapeDtypeStruct((n, w), t) for w, t in prep])
    return shapes, [w for w, _ in widths], [w for w, _ in prep]


def _front_prompt(x2d, P, cos_t, sin_t, batch, tm):
    n = x2d.shape[0]
    layer = P['layer']
    nt = n // batch // tm
    row = lambda b, t: (b * nt + t, 0)
    pos = lambda b, t: (t, 0)
    shapes, widths, prep = _front_out_types(n, (batch, SUBLANES, RWKV_PROJ))
    return pl.pallas_call(
        functools.partial(_front_kernel, sample=False),
        grid=(batch, nt),
        in_specs=[pl.BlockSpec((tm, D_MODEL), row), _layer_spec((1, D_MODEL), layer),
                  _layer_spec((D_MODEL, IN_PROJ), layer),
                  pl.BlockSpec((tm, LANES), pos), pl.BlockSpec((tm, LANES), pos)] + _prep_const_specs(layer),
        out_specs=([pl.BlockSpec((tm, w), row) for w in widths]
                   + [pl.BlockSpec((1, SUBLANES, RWKV_PROJ), lambda b, t: (b, 0, 0))]
                   + [pl.BlockSpec((tm, w), row) for w in prep]),
        out_shape=shapes,
        scratch_shapes=[pltpu.VMEM((SUBLANES, RWKV_PROJ), F32)],
        compiler_params=_params(("arbitrary", "arbitrary")),
        name="front_prompt",
    )(x2d, P['norm_mix_g'], P['w_in'], cos_t, sin_t, *_prep_consts(P))


def _front_sample(x2d, shift_prev, P, cos_t, sin_t):
    n = x2d.shape[0]
    layer = P['layer']
    nb = shift_prev.shape[1]
    shapes, _, _ = _front_out_types(n, (nb, RWKV_PROJ))
    return pl.pallas_call(
        functools.partial(_front_kernel, sample=True),
        grid=(1,),
        in_specs=[_const_spec((n, D_MODEL)), _layer_spec((nb, RWKV_PROJ), layer), _layer_spec((1, D_MODEL), layer),
                  _layer_spec((D_MODEL, IN_PROJ), layer), _const_spec((n, LANES)), _const_spec((n, LANES))]
        + _prep_const_specs(layer),
        out_specs=[_const_spec(sh.shape) for sh in shapes],
        out_shape=shapes,
        compiler_params=_params(("arbitrary",)),
        name="front_sample",
    )(x2d, shift_prev, P['norm_mix_g'], P['w_in'], cos_t, sin_t, *_prep_consts(P))


def _wkv_prompt_kernel(r_ref, lw_ref, kk_ref, b_ref, kf_ref, v_ref, y_ref, st_ref, s_scr, *, chunks):
    C, G, M = CHUNK, HEADS_PER_GROUP, MXU_DIM
    nbatch = r_ref.shape[0]
    ngrp = RWKV_HEADS // G

    @pl.when(pl.program_id(0) == 0)
    def _():
        s_scr[...] = jnp.zeros_like(s_scr)

    ri = lax.broadcasted_iota(jnp.int32, (M, M), 0)
    ci = lax.broadcasted_iota(jnp.int32, (M, M), 1)
    strict = (ri % C) > (ci % C)
    incl = (ri % C) >= (ci % C)
    eye = (ri == ci).astype(F32)
    ti = lax.broadcasted_iota(jnp.int32, (C, C), 0)
    si = lax.broadcasted_iota(jnp.int32, (C, C), 1)
    tri = (ti >= si).astype(F32)
    lane_head = lax.broadcasted_iota(jnp.int32, (1, M), 1) // HEAD_DIM

    def stack(x):
        xb = x.astype(BF16)
        return jnp.concatenate([jnp.where(lane_head == h, xb, jnp.zeros_like(xb)) for h in range(G)], axis=0)

    def chunk_body(c, carry):
        rows = pl.ds(pl.multiple_of(c * C, C), C)
        probs = []
        for bi in range(nbatch):
            lw = lw_ref[bi, rows, :]
            cum = _mm_f32(tri, lw)
            w_end = jnp.exp(cum[C - 1:C, :])
            e_neg = jnp.exp(-cum)
            a_hat = -kk_ref[bi, rows, :] * jnp.exp(cum - lw)
            r_hat = r_ref[bi, rows, :] * jnp.exp(cum)
            b_til = b_ref[bi, rows, :] * e_neg
            k_til = kf_ref[bi, rows, :] * e_neg
            v = v_ref[bi, rows, :]
            for grp in range(ngrp):
                ls = slice(grp * M, (grp + 1) * M)
                we = w_end[:, ls]
                probs.append(dict(
                    bi=bi, grp=grp, ls=ls, we=we, a=stack(a_hat[:, ls]), r=stack(r_hat[:, ls]),
                    b=stack(b_til[:, ls]), k=stack(k_til[:, ls]), v=stack(v[:, ls]),
                    bb=stack(b_til[:, ls] * we), kb=stack(k_til[:, ls] * we)))
        l_ab = [jnp.where(strict, _mm_nt(p['a'], p['b']), 0.0) for p in probs]
        l_ak = [jnp.where(strict, _mm_nt(p['a'], p['k']), 0.0).astype(BF16) for p in probs]
        p_rb = [jnp.where(incl, _mm_nt(p['r'], p['b']), 0.0).astype(BF16) for p in probs]
        p_rk = [jnp.where(incl, _mm_nt(p['r'], p['k']), 0.0).astype(BF16) for p in probs]
        t_inv = [eye + x for x in l_ab]
        pw = [x.astype(BF16) for x in l_ab]
        for _ in range(C.bit_length() - 2):
            pw = [_mm(x, x).astype(BF16) for x in pw]
            t_inv = [t + _mm(x, t) for x, t in zip(pw, t_inv)]
        lv = [_mm(x, p['v']) for x, p in zip(l_ak, probs)]
        s0 = [s_scr[p['bi'], p['grp']] for p in probs]
        s0b = [x.astype(BF16) for x in s0]
        rhs = [_mm_nt(p['a'], x) + y for p, x, y in zip(probs, s0b, lv)]
        u = [_mm(t, x).astype(BF16) for t, x in zip(t_inv, rhs)]
        y_s = [_mm_nt(p['r'], x) + _mm(q, w) + _mm(z, p['v'])
               for p, x, q, w, z in zip(probs, s0b, p_rb, u, p_rk)]
        for p, ys, x, w in zip(probs, y_s, s0, u):
            y = ys[0:C]
            for h in range(1, G):
                y = y + ys[h * C:(h + 1) * C]
            y_ref[p['bi'], rows, p['ls']] = y
            s_scr[p['bi'], p['grp']] = x * p['we'] + _mm_tn(w, p['bb']) + _mm_tn(p['v'], p['kb'])
        return carry

    lax.fori_loop(0, chunks, chunk_body, 0)

    @pl.when(pl.program_id(0) == pl.num_programs(0) - 1)
    def _():
        st_ref[...] = s_scr[...]


def _wkv_prompt(r, lw, kk, b, kf, v, batch, tl):
    n = r.shape[0]
    seq = n // batch
    ngrp = RWKV_HEADS // HEADS_PER_GROUP
    blk = pl.BlockSpec((batch, tl, RWKV_DIM), lambda t: (0, t, 0))
    y, st = pl.pallas_call(
        functools.partial(_wkv_prompt_kernel, chunks=tl // CHUNK),
        grid=(seq // tl,),
        in_specs=[blk] * 6,
        out_specs=[blk, _const_spec((batch, ngrp, MXU_DIM, MXU_DIM))],
        out_shape=[jax.ShapeDtypeStruct((batch, seq, RWKV_DIM), F32),
                   jax.ShapeDtypeStruct((batch, ngrp, MXU_DIM, MXU_DIM), F32)],
        scratch_shapes=[pltpu.VMEM((batch, ngrp, MXU_DIM, MXU_DIM), F32)],
        compiler_params=_params(("arbitrary",)),
        name="wkv_prompt",
    )(*(t.reshape(batch, seq, RWKV_DIM) for t in (r, lw, kk, b, kf, v)))
    st = st.reshape(batch, ngrp, HEADS_PER_GROUP, HEAD_DIM, HEADS_PER_GROUP, HEAD_DIM)
    st = jnp.stack([st[:, :, h, :, h, :] for h in range(HEADS_PER_GROUP)], axis=2)
    st = st.reshape(batch, RWKV_HEADS, HEAD_DIM, HEAD_DIM)
    return y.reshape(n, RWKV_DIM), st


def _wkv_sample_kernel(r_ref, lw_ref, kk_ref, b_ref, kf_ref, v_ref, s_ref, y_ref, so_ref, *, steps):
    def value_row(vi, carry):
        s = s_ref[0, vi]
        for t in range(steps):
            sk = jnp.sum(s * kk_ref[t, 0].astype(F32), axis=0, keepdims=True)
            v_row = v_ref[t, 0, pl.ds(vi, 1), :]
            s = (s * jnp.exp(lw_ref[t, 0]) - sk * b_ref[t, 0].astype(F32)
                 + v_row * kf_ref[t, 0].astype(F32))
            y_ref[t, 0, pl.ds(vi, 1), :] = jnp.sum(s * r_ref[t, 0].astype(F32), axis=0, keepdims=True)
        so_ref[0, vi] = s
        return carry

    lax.fori_loop(0, HEAD_DIM, value_row, 0)


def _wkv_sample(r, lw, kk, b, kf, v, s0, layer, steps):
    nb = r.shape[-1]
    vec = pl.BlockSpec((steps, 1, HEAD_DIM, nb), lambda h: (0, h, 0, 0))
    st = pl.BlockSpec((1, HEAD_DIM, HEAD_DIM, nb), lambda h: (h, 0, 0, 0))
    st_in = pl.BlockSpec((None, 1, HEAD_DIM, HEAD_DIM, nb), lambda h: (layer, h, 0, 0, 0))
    return pl.pallas_call(
        functools.partial(_wkv_sample_kernel, steps=steps),
        grid=(RWKV_HEADS,),
        in_specs=[vec] * 6 + [st_in],
        out_specs=[vec, st],
        out_shape=[jax.ShapeDtypeStruct(r.shape, F32), jax.ShapeDtypeStruct(s0.shape[1:], F32)],
        compiler_params=_params(("arbitrary",)),
        name="wkv_sample",
    )(r, lw, kk, b, kf, v, s0)


def _swa_prompt_kernel(q_ref, kc_ref, kp_ref, vc_ref, vp_ref, sink_ref, o_ref, *, sub):
    blk = WINDOW
    first = pl.program_id(1) == 0
    k_all = jnp.concatenate([kp_ref[...], kc_ref[...]], axis=0)
    v_all = jnp.concatenate([vp_ref[...], vc_ref[...]], axis=0).astype(BF16)
    v_ext = jnp.concatenate([v_all, jnp.ones_like(v_all)], axis=1)
    k_swapped = pltpu.roll(k_all, HEAD_DIM, 1)
    lane = lax.broadcasted_iota(jnp.int32, (1, LANES), 1)
    low = lane < HEAD_DIM
    k_var = {(0, 0): jnp.where(low, k_all, 0.0), (0, 1): jnp.where(low, 0.0, k_swapped),
             (1, 1): jnp.where(low, 0.0, k_all), (1, 0): jnp.where(low, k_swapped, 0.0)}
    k_var = {key: val.astype(BF16) for key, val in k_var.items()}
    order = [j * ATT_GROUP + g for j in range(ATT_KV_HEADS) for half in range(2)
             for g in range(ATT_GROUP) if g % 2 == half]
    n_rows = ATT_Q_HEADS * blk
    qi = lax.broadcasted_iota(jnp.int32, (n_rows, 2 * blk), 0) % blk
    kj = lax.broadcasted_iota(jnp.int32, (n_rows, 2 * blk), 1)
    diff = blk + qi - kj
    band = (diff >= 0) & (diff < WINDOW)
    row_head = lax.broadcasted_iota(jnp.int32, (n_rows, 1), 0) // blk
    sink = jnp.zeros((n_rows, 1), F32)
    for idx, head in enumerate(order):
        sink = jnp.where(row_head == idx, sink_ref[0:1, head:head + 1], sink)
    scale = HEAD_DIM ** -0.5
    for sb in range(sub):
        qs = slice(sb * blk, (sb + 1) * blk)
        ks = slice(sb * blk, (sb + 2) * blk)
        parts = []
        for j in range(ATT_KV_HEADS):
            for half in range(2):
                heads = [j * ATT_GROUP + g for g in range(ATT_GROUP) if g % 2 == half]
                q_cat = jnp.concatenate([q_ref[qs, (h // 2) * LANES:(h // 2 + 1) * LANES] for h in heads], axis=0)
                parts.append(_mm_nt(q_cat * scale, k_var[(j, half)][ks]))
        s = jnp.concatenate(parts, axis=0)
        ok = band & jnp.logical_or(kj >= blk, jnp.logical_not(first)) if sb == 0 else band
        s = jnp.where(ok, s, -jnp.inf)
        m = jnp.maximum(jnp.max(s, axis=-1, keepdims=True), sink)
        e = jnp.exp(s - m).astype(BF16)
        ev = jnp.dot(e, v_ext[ks], preferred_element_type=F32)
        o = ev[:, 0:LANES] / (ev[:, LANES:LANES + 1] + jnp.exp(sink - m))
        by_head = {}
        for idx, head in enumerate(order):
            o_h = o[idx * blk:(idx + 1) * blk]
            by_head[head] = pltpu.roll(o_h, HEAD_DIM, 1) if head // ATT_GROUP != head % 2 else o_h
        for grp in range(ATT_DIM // LANES):
            o_ref[qs, grp * LANES:(grp + 1) * LANES] = jnp.where(low, by_head[2 * grp],
                                                                 by_head[2 * grp + 1]).astype(o_ref.dtype)


def _swa_prompt(q, k, v, P, batch, sub):
    n = q.shape[0]
    rows = sub * WINDOW
    nb = n // batch // rows
    cur = lambda b, i: (b * nb + i, 0)
    prev = lambda b, i: ((b * nb + i) * sub - jnp.minimum(i, 1), 0)
    return pl.pallas_call(
        functools.partial(_swa_prompt_kernel, sub=sub),
        grid=(batch, nb),
        in_specs=[pl.BlockSpec((rows, ATT_DIM), cur),
                  pl.BlockSpec((rows, ATT_KV_DIM), cur), pl.BlockSpec((WINDOW, ATT_KV_DIM), prev),
                  pl.BlockSpec((rows, ATT_KV_DIM), cur), pl.BlockSpec((WINDOW, ATT_KV_DIM), prev),
                  _layer_spec((1, ATT_Q_HEADS), P['layer'])],
        out_specs=pl.BlockSpec((rows, ATT_DIM), cur),
        out_shape=jax.ShapeDtypeStruct((n, ATT_DIM), BF16),
        compiler_params=_params(("arbitrary", "arbitrary")),
        name="swa_prompt",
    )(q, k, k, v, v, P['sinks'])


def _swa_sample_kernel(q_ref, kc_ref, vc_ref, kn_ref, vn_ref, sink_ref, o_ref, *, steps):
    q = q_ref[...].astype(BF16)
    rows = q.shape[1]
    nk = kc_ref.shape[1]
    nn = kn_ref.shape[1]
    bdot = lambda a, b, dims: lax.dot_general(a, b, (dims, ((0,), (0,))), preferred_element_type=F32)
    s_c = bdot(q, kc_ref[...].astype(BF16), ((2,), (2,))) * (HEAD_DIM ** -0.5)
    s_n = bdot(q, kn_ref[...].astype(BF16), ((2,), (2,))) * (HEAD_DIM ** -0.5)
    t_c = lax.broadcasted_iota(jnp.int32, (1, rows, nk), 1) % steps
    w_c = lax.broadcasted_iota(jnp.int32, (1, rows, nk), 2)
    s_c = jnp.where(w_c > t_c, s_c, -jnp.inf)
    t_n = lax.broadcasted_iota(jnp.int32, (1, rows, nn), 1) % steps
    w_n = lax.broadcasted_iota(jnp.int32, (1, rows, nn), 2)
    s_n = jnp.where(w_n <= t_n, s_n, -jnp.inf)
    sink = sink_ref[...][None, :, 0:1]
    m = jnp.maximum(jnp.maximum(jnp.max(s_c, axis=-1, keepdims=True), jnp.max(s_n, axis=-1, keepdims=True)), sink)
    e_c = jnp.exp(s_c - m)
    e_n = jnp.exp(s_n - m)
    den = jnp.sum(e_c, axis=-1, keepdims=True) + jnp.sum(e_n, axis=-1, keepdims=True) + jnp.exp(sink - m)
    o = (bdot((e_c / den).astype(BF16), vc_ref[...].astype(BF16), ((2,), (1,)))
         + bdot((e_n / den).astype(BF16), vn_ref[...].astype(BF16), ((2,), (1,))))
    o_ref[...] = o


def _swa_sample(q_st, k_cache, v_cache, k_new, v_new, sink_rows, layer, steps, bt):
    nb, rows, _ = q_st.shape
    nn = k_new.shape[1]
    blk = lambda r: pl.BlockSpec((bt, r, LANES), lambda i: (i, 0, 0))
    cache = pl.BlockSpec((None, bt, WINDOW, LANES), lambda i: (layer, i, 0, 0))
    return pl.pallas_call(
        functools.partial(_swa_sample_kernel, steps=steps),
        grid=(nb // bt,),
        in_specs=[blk(rows), cache, cache, blk(nn), blk(nn), _layer_spec((rows, LANES), layer)],
        out_specs=blk(rows),
        out_shape=jax.ShapeDtypeStruct((nb, rows, LANES), F32),
        compiler_params=_params(("arbitrary",)),
        name="swa_sample",
    )(q_st, k_cache, v_cache, k_new, v_new, sink_rows)


def _gelu(x):
    return 0.5 * x * (1.0 + lax.erf(x * (2.0 ** -0.5)))


def _back_kernel(*refs, final, sample):
    x_ref, y_ref, bonus_ref, g_ref, oa_ref, gr_ref, ga_ref = refs[:7]
    at = 7
    if sample:
        cp_ref = refs[at]
        at += 1
    (lng_ref, lnb_ref, ones_ref, wa_ref, wb_ref, wo_ref, gn_ref, wi_ref, cw_ref, cb_ref, wd_ref,
     gf_ref) = refs[at:at + 12]
    o_ref, ct_ref, carry = refs[at + 12:at + 15]

    y = y_ref[...]
    ones_bd = ones_ref[...]
    mean = _head_sum(y, ones_bd) * (1.0 / HEAD_DIM)
    d = y - mean
    var = _head_sum(d * d, ones_bd) * (1.0 / HEAD_DIM)
    yn = d * lax.rsqrt(var + GN_EPS) * lng_ref[...] + lnb_ref[...]
    o_r = (yn + bonus_ref[...]) * g_ref[...]
    merged = gr_ref[...] * _mm(o_r, wa_ref[...]) + ga_ref[...] * _mm(oa_ref[...], wb_ref[...])
    x = x_ref[...] + _mm(merged, wo_ref[...])

    h = _rmsnorm(x, gn_ref[...]).astype(BF16)
    c = jnp.dot(h, wi_ref[:, 0:D_FF], preferred_element_type=F32)
    up = jnp.dot(h, wi_ref[:, D_FF:2 * D_FF], preferred_element_type=F32)
    if sample:
        @pl.when(pl.program_id(0) == 0)
        def _():
            carry[...] = cp_ref[...]

        c2, c1 = carry[0], carry[1]
        carry[0] = c1
        carry[1] = c
        ct_ref[0] = c
    else:
        @pl.when(pl.program_id(1) == 0)
        def _():
            carry[...] = jnp.zeros_like(carry)

        tm = c.shape[0]
        row = lax.broadcasted_iota(jnp.int32, (tm, 1), 0)
        last = carry[SUBLANES - 1:SUBLANES, :]
        c1 = jnp.where(row == 0, last, pltpu.roll(c, 1, 0))
        c2 = jnp.where(row == 0, carry[SUBLANES - 2:SUBLANES - 1, :],
                       jnp.where(row == 1, last, pltpu.roll(c, 2, 0)))
        tail = c[tm - SUBLANES:, :]
        carry[...] = tail
        ct_ref[0] = tail
    conv = cb_ref[...] + c2 * cw_ref[0:1, :] + c1 * cw_ref[1:2, :] + c * cw_ref[2:3, :]
    out = x + _mm(_gelu(conv) * up, wd_ref[...])
    o_ref[...] = _rmsnorm(out, gf_ref[...]) if final else out


def _back_weight_specs(layer):
    return [_layer_spec((1, RWKV_DIM), layer), _layer_spec((1, RWKV_DIM), layer), _const_spec((RWKV_DIM, RWKV_DIM)),
            _layer_spec((RWKV_DIM, D_MODEL), layer), _layer_spec((ATT_DIM, D_MODEL), layer),
            _layer_spec((D_MODEL, D_MODEL), layer),
            _layer_spec((1, D_MODEL), layer), _layer_spec((D_MODEL, 2 * D_FF), layer),
            _layer_spec((CONV_W, D_FF), layer), _layer_spec((1, D_FF), layer), _layer_spec((D_FF, D_MODEL), layer),
            _const_spec((1, D_MODEL))]


def _back_weights(P, g_final):
    return [P['ln_g'], P['ln_b'], P['ones_bd'], P['w_br_rwkv'], P['w_br_attn'], P['w_out'],
            P['norm_ffn_g'], P['ffn_w_in'], P['ffn_conv_w'], P['ffn_conv_b'], P['ffn_w_down'], g_final]


def _back_prompt(x, y, bonus, g, o_a, gr, ga, P, g_final, final, batch, tm):
    n = x.shape[0]
    nt = n // batch // tm
    row = lambda b, t: (b * nt + t, 0)
    wide = pl.BlockSpec((tm, D_MODEL), row)
    half = pl.BlockSpec((tm, RWKV_DIM), row)
    return pl.pallas_call(
        functools.partial(_back_kernel, final=final, sample=False),
        grid=(batch, nt),
        in_specs=[wide, half, half, half, half, wide, wide] + _back_weight_specs(P['layer']),
        out_specs=[wide, pl.BlockSpec((1, SUBLANES, D_FF), lambda b, t: (b, 0, 0))],
        out_shape=[jax.ShapeDtypeStruct((n, D_MODEL), F32), jax.ShapeDtypeStruct((batch, SUBLANES, D_FF), F32)],
        scratch_shapes=[pltpu.VMEM((SUBLANES, D_FF), F32)],
        compiler_params=_params(("arbitrary", "arbitrary")),
        name="back_prompt",
    )(x, y, bonus, g, o_a, gr, ga, *_back_weights(P, g_final))


def _back_sample(x, y, bonus, g, o_a, gr, ga, conv_prev, P, g_final, final):
    n = x.shape[0]
    nb = conv_prev.shape[2]
    steps = n // nb
    taps = CONV_W - 1
    row = lambda t: (t, 0)
    wide = pl.BlockSpec((nb, D_MODEL), row)
    half = pl.BlockSpec((nb, RWKV_DIM), row)
    return pl.pallas_call(
        functools.partial(_back_kernel, final=final, sample=True),
        grid=(steps,),
        in_specs=[wide, half, half, half, half, wide, wide, _layer_spec((taps, nb, D_FF), P['layer'])]
        + _back_weight_specs(P['layer']),
        out_specs=[wide, pl.BlockSpec((1, nb, D_FF), lambda t: (jnp.maximum(t - (steps - taps), 0), 0, 0))],
        out_shape=[jax.ShapeDtypeStruct((n, D_MODEL), F32), jax.ShapeDtypeStruct((taps, nb, D_FF), F32)],
        scratch_shapes=[pltpu.VMEM((taps, nb, D_FF), F32)],
        compiler_params=_params(("arbitrary",)),
        name="back_sample",
    )(x, y, bonus, g, o_a, gr, ga, conv_prev, *_back_weights(P, g_final))


def _rope_tables(pos):
    inv = ROPE_THETA ** (-jnp.arange(0, HEAD_DIM, 2, dtype=F32) / HEAD_DIM)
    ang = pos.astype(F32)[:, None] * inv[None, :]
    cos, sin = jnp.cos(ang), jnp.sin(ang)
    reps = LANES // HEAD_DIM
    return jnp.tile(jnp.concatenate([cos, cos], axis=1), (1, reps)), jnp.tile(jnp.concatenate([-sin, sin], axis=1),
                                                                             (1, reps))


def _stacked_params(norm_mix_g, w_in, rwkv_mu, rwkv_w0, rwkv_w2, rwkv_a0, rwkv_a2, rwkv_g2, rwkv_k_k, rwkv_k_a,
                    rwkv_r_k, rwkv_ln_g, rwkv_ln_b, attn_sinks, w_br_rwkv, w_br_attn, w_out, norm_ffn_g, ffn_w_in,
                    ffn_conv_w, ffn_conv_b, ffn_w_down, steps):
    d = RWKV_DIM
    depth = w_in.shape[0]
    head = jnp.arange(d) // HEAD_DIM
    row = lambda t: t.reshape(depth, 1, -1)
    sink_rows = jnp.broadcast_to(jnp.repeat(attn_sinks, steps, axis=1)[:, :, None],
                                 (depth, ATT_Q_HEADS * steps, LANES))
    return {
        'norm_mix_g': row(norm_mix_g), 'w_in': w_in.astype(BF16),
        'mu': row(rwkv_mu), 'w0': row(rwkv_w0),
        'w2p': jnp.pad(rwkv_w2, ((0, 0), (0, D_AAA_LORA), (0, 0))).astype(BF16),
        'a0': row(rwkv_a0),
        'a2p': jnp.pad(rwkv_a2, ((0, 0), (D_DECAY_LORA, 0), (0, 0))).astype(BF16),
        'g2': rwkv_g2.astype(BF16), 'k_k': row(rwkv_k_k), 'k_a': row(rwkv_k_a),
        'r_k': row(rwkv_r_k), 'ln_g': row(rwkv_ln_g), 'ln_b': row(rwkv_ln_b),
        'ones_bd': (head[:, None] == head[None, :]).astype(BF16),
        'sinks': row(attn_sinks), 'sink_rows': sink_rows,
        'w_br_rwkv': w_br_rwkv.astype(BF16), 'w_br_attn': w_br_attn.astype(BF16),
        'w_out': w_out.astype(BF16), 'norm_ffn_g': row(norm_ffn_g),
        'ffn_w_in': ffn_w_in.astype(BF16), 'ffn_conv_w': ffn_conv_w, 'ffn_conv_b': row(ffn_conv_b),
        'ffn_w_down': ffn_w_down.astype(BF16),
    }


def _prompt_layer(x, P, tables, g_final, final, batch, seq):
    tm = min(256, seq)
    q, k, v, gr, ga, z_tail, r, lw, kk, b, kf, vv, g, bonus = _front_prompt(x, P, tables[0], tables[1], batch, tm)
    y, s_t = _wkv_prompt(r, lw, kk, b, kf, vv, batch, min(256, seq))
    o_a = _swa_prompt(q, k, v, P, batch, min(2, seq // WINDOW))
    x2, c_tail = _back_prompt(x, y, bonus, g, o_a, gr, ga, P, g_final, final, batch, tm)
    last = lambda t: t.reshape(batch, seq, -1)[:, -WINDOW:].reshape(batch, WINDOW, ATT_KV_HEADS, HEAD_DIM)
    state = (z_tail[:, SUBLANES - 1], s_t, last(k), last(v), c_tail[:, SUBLANES - (CONV_W - 1):])
    return x2, state


def _sample_layer(x, P, tables, g_final, final, shift_prev, wkv0, k_cache, v_cache, conv_prev, nb, steps):
    n = nb * steps
    layer = P['layer']
    q, k, v, gr, ga, z_tail, r, lw, kk, b, kf, vv, g, bonus = _front_sample(x, shift_prev, P, tables[0], tables[1])
    to_lanes = lambda t: t.reshape(steps, nb, RWKV_HEADS, HEAD_DIM).transpose(0, 2, 3, 1)
    y_l, s_l = _wkv_sample(*(to_lanes(t) for t in (r, lw, kk, b, kf)), to_lanes(vv).astype(F32), wkv0, layer, steps)
    y = y_l.transpose(0, 3, 1, 2).reshape(n, RWKV_DIM)

    rows = ATT_Q_HEADS * steps
    q5 = q.reshape(steps, nb, ATT_KV_HEADS, ATT_GROUP, HEAD_DIM).transpose(1, 2, 3, 0, 4)
    sel = jnp.eye(ATT_KV_HEADS, dtype=q.dtype)
    q_st = (q5[:, :, :, :, None, :] * sel[None, :, None, None, :, None]).reshape(nb, rows, ATT_KV_DIM)
    pad = ((0, 0), (0, SUBLANES - steps), (0, 0))
    k_new = k.reshape(steps, nb, ATT_KV_DIM).transpose(1, 0, 2)
    v_new = v.reshape(steps, nb, ATT_KV_DIM).transpose(1, 0, 2)
    o_st = _swa_sample(q_st, k_cache, v_cache, jnp.pad(k_new, pad), jnp.pad(v_new, pad), P['sink_rows'], layer,
                       steps, 16)
    o5 = o_st.reshape(nb, ATT_KV_HEADS, ATT_GROUP, steps, ATT_KV_HEADS, HEAD_DIM)
    o_a = jnp.stack([o5[:, j, :, :, j, :] for j in range(ATT_KV_HEADS)], axis=0)
    o_a = o_a.transpose(3, 1, 0, 2, 4).reshape(n, ATT_DIM)

    x2, c_tail = _back_sample(x, y, bonus, g, o_a, gr, ga, conv_prev, P, g_final, final)
    state = (z_tail, s_l, k_new, v_new, c_tail)
    return x2, state


def kernel(x_prompt, x_sample, state_rwkv_shift, state_rwkv_wkv, cache_swa_k, cache_swa_v, state_ffn_conv, norm_mix_g, w_in, rwkv_mu, rwkv_w0, rwkv_w2, rwkv_a0, rwkv_a2, rwkv_g2, rwkv_k_k, rwkv_k_a, rwkv_r_k, rwkv_ln_g, rwkv_ln_b, attn_sinks, w_br_rwkv, w_br_attn, w_out, norm_ffn_g, ffn_w_in, ffn_conv_w, ffn_conv_b, ffn_w_down, norm_final_g):
    bp, tp, _ = x_prompt.shape
    nb, steps, _ = x_sample.shape
    depth = w_in.shape[0]
    tab_p = _rope_tables(jnp.arange(tp, dtype=jnp.int32))
    tab_s = _rope_tables(PAST_LEN + jnp.repeat(jnp.arange(steps, dtype=jnp.int32), nb))
    g_final = norm_final_g[None]
    params = _stacked_params(norm_mix_g, w_in, rwkv_mu, rwkv_w0, rwkv_w2, rwkv_a0, rwkv_a2, rwkv_g2, rwkv_k_k,
                             rwkv_k_a, rwkv_r_k, rwkv_ln_g, rwkv_ln_b, attn_sinks, w_br_rwkv, w_br_attn, w_out,
                             norm_ffn_g, ffn_w_in, ffn_conv_w, ffn_conv_b, ffn_w_down, steps)
    wkv0 = state_rwkv_wkv.transpose(0, 2, 3, 4, 1)
    k_cache = cache_swa_k.reshape(depth, nb, WINDOW, ATT_KV_DIM)
    v_cache = cache_swa_v.reshape(depth, nb, WINDOW, ATT_KV_DIM)
    conv_prev = state_ffn_conv.transpose(0, 2, 1, 3)
    xp = x_prompt.reshape(bp * tp, D_MODEL)
    xs = x_sample.transpose(1, 0, 2).reshape(steps * nb, D_MODEL)
    outs_p, outs_s = [], []
    for l in range(depth):
        P = dict(params, layer=l)
        final = l == depth - 1
        xp, sp = _prompt_layer(xp, P, tab_p, g_final, final, bp, tp)
        xs, ss = _sample_layer(xs, P, tab_s, g_final, final, state_rwkv_shift, wkv0, k_cache, v_cache, conv_prev,
                               nb, steps)
        outs_p.append(sp)
        outs_s.append(ss)
    y_prompt = xp.reshape(bp, tp, D_MODEL)
    y_sample = xs.reshape(steps, nb, D_MODEL).transpose(1, 0, 2)
    stack = lambda outs, i: jnp.stack([o[i] for o in outs])
    kv_shape = (depth, nb, steps, ATT_KV_HEADS, HEAD_DIM)
    s_k = jnp.concatenate([cache_swa_k[:, :, steps:], stack(outs_s, 2).reshape(kv_shape)], axis=2)
    s_v = jnp.concatenate([cache_swa_v[:, :, steps:], stack(outs_s, 3).reshape(kv_shape)], axis=2)
    s_wkv = stack(outs_s, 1).transpose(0, 4, 1, 2, 3)
    s_conv = stack(outs_s, 4).transpose(0, 2, 1, 3)
    return (y_prompt, y_sample, *(stack(outs_p, i) for i in range(5)),
            stack(outs_s, 0), s_wkv, s_k, s_v, s_conv)
```

```python
import functools

import jax
import jax.numpy as jnp
from jax import lax
from jax.experimental import pallas as pl
from jax.experimental.pallas import tpu as pltpu

F32 = jnp.float32
BF16 = jnp.bfloat16

D_MODEL = 1024
HEAD_DIM = 64
RWKV_HEADS = 8
RWKV_DIM = RWKV_HEADS * HEAD_DIM
D_DECAY_LORA = 64
D_AAA_LORA = 64
D_GATE_LORA = 128
RWKV_PROJ = 3 * RWKV_DIM + D_DECAY_LORA + D_AAA_LORA + D_GATE_LORA
ATT_Q_HEADS = 8
ATT_KV_HEADS = 2
ATT_GROUP = ATT_Q_HEADS // ATT_KV_HEADS
ATT_DIM = ATT_Q_HEADS * HEAD_DIM
ATT_KV_DIM = ATT_KV_HEADS * HEAD_DIM
WINDOW = 128
IN_PROJ = RWKV_PROJ + ATT_DIM + 2 * ATT_KV_DIM + 2 * D_MODEL
D_FF = 2816
CONV_W = 3
ROPE_THETA = 10000.0
RMS_EPS = 1e-6
GN_EPS = 64e-5
PAST_LEN = 16384

LANES = 128
SUBLANES = 8
MXU_DIM = 256
CHUNK = 64
HEADS_PER_GROUP = MXU_DIM // HEAD_DIM
VMEM_LIMIT = 56 * 1024 * 1024


def _params(sem):
    return pltpu.CompilerParams(dimension_semantics=sem, vmem_limit_bytes=VMEM_LIMIT)


def _const_spec(shape):
    zeros = (0,) * len(shape)
    return pl.BlockSpec(shape, lambda *_: zeros, pipeline_mode=pl.Buffered(1))


def _layer_spec(shape, layer):
    index = (layer,) + (0,) * len(shape)
    return pl.BlockSpec((None,) + tuple(shape), lambda *_: index, pipeline_mode=pl.Buffered(1))


def _mm(a, b):
    return jnp.dot(a.astype(BF16), b.astype(BF16), preferred_element_type=F32)


def _mm_nt(a, b):
    return lax.dot_general(a.astype(BF16), b.astype(BF16), (((1,), (1,)), ((), ())),
                           preferred_element_type=F32)


def _mm_f32(a, b):
    return jnp.dot(a, b, precision=lax.Precision.HIGHEST, preferred_element_type=F32)


def _mm_tn(a, b):
    return lax.dot_general(a.astype(BF16), b.astype(BF16), (((0,), (0,)), ((), ())),
                           preferred_element_type=F32)


def _head_sum(x, ones_bd):
    return jnp.dot(x.astype(BF16), ones_bd, preferred_element_type=F32)


def _rmsnorm(x, g):
    return x * lax.rsqrt(jnp.mean(x * x, axis=-1, keepdims=True) + RMS_EPS) * g


def _rope(x, cos, sin_signed, first_half):
    partner = jnp.where(first_half, pltpu.roll(x, LANES - HEAD_DIM // 2, 1), pltpu.roll(x, HEAD_DIM // 2, 1))
    return x * cos + partner * sin_signed


def _prep_math(z, zp, mu, w0, w2p, a0, a2p, g2, k_k, k_a, r_k, ones_bd, outs):
    r_ref, lw_ref, kk_ref, b_ref, kf_ref, v_ref, g_ref, bonus_ref = outs
    zs = z + mu * (zp - z)
    d = RWKV_DIM
    r, k, v = zs[:, 0:d], zs[:, d:2 * d], zs[:, 2 * d:3 * d]
    zwa = zs[:, 3 * d:3 * d + LANES]
    zg = zs[:, 3 * d + LANES:3 * d + 2 * LANES]
    w_pre = w0 + _mm(jnp.tanh(zwa), w2p)
    neg = -w_pre
    softplus = jnp.maximum(neg, 0.0) + jnp.log1p(jnp.exp(-jnp.abs(neg)))
    w_log = -softplus - 0.5
    lw_ref[...] = -jnp.exp(w_log)
    a = jax.nn.sigmoid(a0 + _mm(zwa, a2p))
    g_ref[...] = _mm(jax.nn.sigmoid(zg), g2).astype(g_ref.dtype)
    kk0 = k * k_k
    norm = jnp.sqrt(_head_sum(kk0 * kk0, ones_bd))
    kk = kk0 / jnp.maximum(norm, 1e-12)
    kf = k * (1.0 + (a - 1.0) * k_a)
    r_ref[...] = r.astype(r_ref.dtype)
    kk_ref[...] = kk.astype(kk_ref.dtype)
    b_ref[...] = (kk * a).astype(b_ref.dtype)
    kf_ref[...] = kf.astype(kf_ref.dtype)
    v_ref[...] = v.astype(v_ref.dtype)
    bonus_ref[...] = (_head_sum(r * kf * r_k, ones_bd) * v).astype(bonus_ref.dtype)


N_PREP_CONSTS = 10
N_FRONT_OUTS = 14


def _front_kernel(*refs, sample):
    if sample:
        x_ref, sp_ref, g_ref, w_ref, cos_ref, sin_ref = refs[:6]
        rest = refs[6:]
    else:
        x_ref, g_ref, w_ref, cos_ref, sin_ref = refs[:5]
        rest = refs[5:]
    consts = [c[...] for c in rest[:N_PREP_CONSTS]]
    outs = rest[N_PREP_CONSTS:N_PREP_CONSTS + N_FRONT_OUTS]
    q_ref, k_ref, v_ref, gr_ref, ga_ref, zt_ref = outs[:6]

    h = _rmsnorm(x_ref[...], g_ref[...]).astype(BF16)
    c0, c1, c2, c3, c4 = (RWKV_PROJ, RWKV_PROJ + ATT_DIM, RWKV_PROJ + ATT_DIM + ATT_KV_DIM,
                          RWKV_PROJ + ATT_DIM + 2 * ATT_KV_DIM, RWKV_PROJ + ATT_DIM + 2 * ATT_KV_DIM + D_MODEL)
    z = jnp.dot(h, w_ref[:, 0:c0], preferred_element_type=F32)
    n = z.shape[0]
    if sample:
        nb = sp_ref.shape[0]
        zp = jnp.concatenate([sp_ref[...], z[:n - nb, :]], axis=0)
        zt_ref[...] = z[n - nb:, :]
    else:
        carry = rest[N_PREP_CONSTS + N_FRONT_OUTS]

        @pl.when(pl.program_id(1) == 0)
        def _():
            carry[...] = jnp.zeros_like(carry)

        row = lax.broadcasted_iota(jnp.int32, (n, 1), 0)
        zp = jnp.where(row == 0, carry[SUBLANES - 1:SUBLANES, :], pltpu.roll(z, 1, 0))
        tail = z[n - SUBLANES:, :]
        carry[...] = tail
        zt_ref[0] = tail
    _prep_math(z, zp, *consts, outs[6:])

    cos = cos_ref[...]
    sin = sin_ref[...]
    lane = lax.broadcasted_iota(jnp.int32, (1, LANES), 1)
    first_half = (lane % HEAD_DIM) < (HEAD_DIM // 2)
    q = jnp.dot(h, w_ref[:, c0:c1], preferred_element_type=F32)
    for grp in range(ATT_DIM // LANES):
        sl = slice(grp * LANES, (grp + 1) * LANES)
        q_ref[:, sl] = _rope(q[:, sl], cos, sin, first_half).astype(q_ref.dtype)
    k = jnp.dot(h, w_ref[:, c1:c2], preferred_element_type=F32)
    k_ref[...] = _rope(k, cos, sin, first_half)
    v_ref[...] = jnp.dot(h, w_ref[:, c2:c3], preferred_element_type=F32)
    gr_ref[...] = jax.nn.sigmoid(jnp.dot(h, w_ref[:, c3:c4], preferred_element_type=F32)).astype(gr_ref.dtype)
    ga_ref[...] = jax.nn.sigmoid(jnp.dot(h, w_ref[:, c4:IN_PROJ], preferred_element_type=F32)).astype(ga_ref.dtype)


def _prep_consts(P):
    return [P['mu'], P['w0'], P['w2p'], P['a0'], P['a2p'], P['g2'], P['k_k'], P['k_a'], P['r_k'], P['ones_bd']]


def _prep_const_specs(layer):
    d = RWKV_DIM
    shapes = [(1, RWKV_PROJ), (1, d), (LANES, d), (1, d), (LANES, d), (D_GATE_LORA, d), (1, d), (1, d), (1, d)]
    return [_layer_spec(s, layer) for s in shapes] + [_const_spec((d, d))]


def _front_out_types(n, tail_shape):
    d = RWKV_DIM
    widths = [(ATT_DIM, BF16), (ATT_KV_DIM, F32), (ATT_KV_DIM, F32), (D_MODEL, BF16), (D_MODEL, BF16)]
    prep = [(d, BF16), (d, F32), (d, BF16), (d, BF16), (d, BF16), (d, BF16), (d, BF16), (d, BF16)]
    shapes = ([jax.ShapeDtypeStruct((n, w), t) for w, t in widths] + [jax.ShapeDtypeStruct(tail_shape, F32)]
              + [jax.ShapeDtypeStruct((n, w), t) for w, t in prep])
    return shapes, [w for w, _ in widths], [w for w, _ in prep]


def _front_prompt(x2d, P, cos_t, sin_t, batch, tm):
    n = x2d.shape[0]
    layer = P['layer']
    nt = n // batch // tm
    row = lambda b, t: (b * nt + t, 0)
    pos = lambda b, t: (t, 0)
    shapes, widths, prep = _front_out_types(n, (batch, SUBLANES, RWKV_PROJ))
    return pl.pallas_call(
        functools.partial(_front_kernel, sample=False),
        grid=(batch, nt),
        in_specs=[pl.BlockSpec((tm, D_MODEL), row), _layer_spec((1, D_MODEL), layer),
                  _layer_spec((D_MODEL, IN_PROJ), layer),
                  pl.BlockSpec((tm, LANES), pos), pl.BlockSpec((tm, LANES), pos)] + _prep_const_specs(layer),
        out_specs=([pl.BlockSpec((tm, w), row) for w in widths]
                   + [pl.BlockSpec((1, SUBLANES, RWKV_PROJ), lambda b, t: (b, 0, 0))]
                   + [pl.BlockSpec((tm, w), row) for w in prep]),
        out_shape=shapes,
        scratch_shapes=[pltpu.VMEM((SUBLANES, RWKV_PROJ), F32)],
        compiler_params=_params(("arbitrary", "arbitrary")),
        name="front_prompt",
    )(x2d, P['norm_mix_g'], P['w_in'], cos_t, sin_t, *_prep_consts(P))


def _front_sample(x2d, shift_prev, P, cos_t, sin_t):
    n = x2d.shape[0]
    layer = P['layer']
    nb = shift_prev.shape[1]
    shapes, _, _ = _front_out_types(n, (nb, RWKV_PROJ))
    return pl.pallas_call(
        functools.partial(_front_kernel, sample=True),
        grid=(1,),
        in_specs=[_const_spec((n, D_MODEL)), _layer_spec((nb, RWKV_PROJ), layer), _layer_spec((1, D_MODEL), layer),
                  _layer_spec((D_MODEL, IN_PROJ), layer), _const_spec((n, LANES)), _const_spec((n, LANES))]
        + _prep_const_specs(layer),
        out_specs=[_const_spec(sh.shape) for sh in shapes],
        out_shape=shapes,
        compiler_params=_params(("arbitrary",)),
        name="front_sample",
    )(x2d, shift_prev, P['norm_mix_g'], P['w_in'], cos_t, sin_t, *_prep_consts(P))


def _wkv_prompt_kernel(r_ref, lw_ref, kk_ref, b_ref, kf_ref, v_ref, y_ref, st_ref, s_scr, *, chunks):
    C, G, M = CHUNK, HEADS_PER_GROUP, MXU_DIM
    nbatch = r_ref.shape[0]
    ngrp = RWKV_HEADS // G

    @pl.when(pl.program_id(0) == 0)
    def _():
        s_scr[...] = jnp.zeros_like(s_scr)

    ti = lax.broadcasted_iota(jnp.int32, (C, M), 0)
    si = lax.broadcasted_iota(jnp.int32, (C, M), 1) % C
    strict = ti > si
    incl = ti >= si
    eye = (ti == si).astype(F32)
    tri = (lax.broadcasted_iota(jnp.int32, (C, C), 0) >= lax.broadcasted_iota(jnp.int32, (C, C), 1)).astype(F32)
    lane_head = lax.broadcasted_iota(jnp.int32, (1, M), 1) // HEAD_DIM
    same_head = (lax.broadcasted_iota(jnp.int32, (M, M), 0) // HEAD_DIM
                 == lax.broadcasted_iota(jnp.int32, (M, M), 1) // HEAD_DIM)

    def stack(x):
        xb = x.astype(BF16)
        return jnp.concatenate([jnp.where(lane_head == h, xb, jnp.zeros_like(xb)) for h in range(G)], axis=0)

    def rows2(top, bottom):
        return jnp.concatenate([top.astype(BF16), bottom.astype(BF16)], axis=0)

    def chunk_body(c, carry):
        rows = pl.ds(pl.multiple_of(c * C, C), C)
        probs = []
        for bi in range(nbatch):
            lw = lw_ref[bi, rows, :]
            cum = _mm_f32(tri, lw)
            w_end = jnp.exp(cum[C - 1:C, :])
            e_neg = jnp.exp(-cum)
            a_hat = -kk_ref[bi, rows, :] * jnp.exp(cum - lw)
            r_hat = r_ref[bi, rows, :] * jnp.exp(cum)
            b_til = b_ref[bi, rows, :] * e_neg
            k_til = kf_ref[bi, rows, :] * e_neg
            v = v_ref[bi, rows, :]
            for grp in range(ngrp):
                ls = slice(grp * M, (grp + 1) * M)
                we = w_end[:, ls]
                probs.append(dict(
                    bi=bi, grp=grp, ls=ls, we=we, ar=rows2(a_hat[:, ls], r_hat[:, ls]),
                    b=stack(b_til[:, ls]), k=stack(k_til[:, ls]), v=v[:, ls], vs=stack(v[:, ls]),
                    bk=rows2(b_til[:, ls] * we, k_til[:, ls] * we)))
        mb = [_mm_nt(p['ar'], p['b']) for p in probs]
        mk = [_mm_nt(p['ar'], p['k']) for p in probs]
        l_ab = [jnp.where(strict, x[0:C], 0.0) for x in mb]
        p_rb = [jnp.where(incl, x[C:2 * C], 0.0).astype(BF16) for x in mb]
        lp_k = [rows2(jnp.where(strict, x[0:C], 0.0), jnp.where(incl, x[C:2 * C], 0.0)) for x in mk]
        t_inv = [eye + x for x in l_ab]
        pw = [_mm(x, stack(x)) for x in l_ab]
        for _ in range(C.bit_length() - 3):
            both = [_mm(rows2(t, x), stack(x)) for t, x in zip(t_inv, pw)]
            t_inv = [t + x[0:C] for t, x in zip(t_inv, both)]
            pw = [x[C:2 * C] for x in both]
        t_inv = [t + _mm(t, stack(x)) for t, x in zip(t_inv, pw)]
        s0 = [s_scr[p['bi'], p['grp']] for p in probs]
        d0 = [_mm_nt(p['ar'], x) for p, x in zip(probs, s0)]
        e0 = [_mm(x, p['vs']) for x, p in zip(lp_k, probs)]
        u = [_mm(t, stack(x[0:C] + y[0:C])) for t, x, y in zip(t_inv, d0, e0)]
        for p, x, y, w, q, st in zip(probs, d0, e0, u, p_rb, s0):
            y_ref[p['bi'], rows, p['ls']] = x[C:2 * C] + y[C:2 * C] + _mm(q, stack(w))
            upd = _mm_tn(rows2(w, p['v']), p['bk'])
            s_scr[p['bi'], p['grp']] = st * p['we'] + jnp.where(same_head, upd, 0.0)
        return carry

    lax.fori_loop(0, chunks, chunk_body, 0)

    @pl.when(pl.program_id(0) == pl.num_programs(0) - 1)
    def _():
        st_ref[...] = s_scr[...]


def _wkv_prompt(r, lw, kk, b, kf, v, batch, tl):
    n = r.shape[0]
    seq = n // batch
    ngrp = RWKV_HEADS // HEADS_PER_GROUP
    blk = pl.BlockSpec((batch, tl, RWKV_DIM), lambda t: (0, t, 0))
    y, st = pl.pallas_call(
        functools.partial(_wkv_prompt_kernel, chunks=tl // CHUNK),
        grid=(seq // tl,),
        in_specs=[blk] * 6,
        out_specs=[blk, _const_spec((batch, ngrp, MXU_DIM, MXU_DIM))],
        out_shape=[jax.ShapeDtypeStruct((batch, seq, RWKV_DIM), F32),
                   jax.ShapeDtypeStruct((batch, ngrp, MXU_DIM, MXU_DIM), F32)],
        scratch_shapes=[pltpu.VMEM((batch, ngrp, MXU_DIM, MXU_DIM), F32)],
        compiler_params=_params(("arbitrary",)),
        name="wkv_prompt",
    )(*(t.reshape(batch, seq, RWKV_DIM) for t in (r, lw, kk, b, kf, v)))
    st = st.reshape(batch, ngrp, HEADS_PER_GROUP, HEAD_DIM, HEADS_PER_GROUP, HEAD_DIM)
    st = jnp.stack([st[:, :, h, :, h, :] for h in range(HEADS_PER_GROUP)], axis=2)
    st = st.reshape(batch, RWKV_HEADS, HEAD_DIM, HEAD_DIM)
    return y.reshape(n, RWKV_DIM), st


def _wkv_sample_kernel(r_ref, lw_ref, kk_ref, b_ref, kf_ref, v_ref, s_ref, y_ref, so_ref, *, steps):
    def value_row(vi, carry):
        s = s_ref[0, vi]
        for t in range(steps):
            sk = jnp.sum(s * kk_ref[t, 0].astype(F32), axis=0, keepdims=True)
            v_row = v_ref[t, 0, pl.ds(vi, 1), :]
            s = (s * jnp.exp(lw_ref[t, 0]) - sk * b_ref[t, 0].astype(F32)
                 + v_row * kf_ref[t, 0].astype(F32))
            y_ref[t, 0, pl.ds(vi, 1), :] = jnp.sum(s * r_ref[t, 0].astype(F32), axis=0, keepdims=True)
        so_ref[0, vi] = s
        return carry

    lax.fori_loop(0, HEAD_DIM, value_row, 0)


def _wkv_sample(r, lw, kk, b, kf, v, s0, layer, steps):
    nb = r.shape[-1]
    vec = pl.BlockSpec((steps, 1, HEAD_DIM, nb), lambda h: (0, h, 0, 0))
    st = pl.BlockSpec((1, HEAD_DIM, HEAD_DIM, nb), lambda h: (h, 0, 0, 0))
    st_in = pl.BlockSpec((None, 1, HEAD_DIM, HEAD_DIM, nb), lambda h: (layer, h, 0, 0, 0))
    return pl.pallas_call(
        functools.partial(_wkv_sample_kernel, steps=steps),
        grid=(RWKV_HEADS,),
        in_specs=[vec] * 6 + [st_in],
        out_specs=[vec, st],
        out_shape=[jax.ShapeDtypeStruct(r.shape, F32), jax.ShapeDtypeStruct(s0.shape[1:], F32)],
        compiler_params=_params(("arbitrary",)),
        name="wkv_sample",
    )(r, lw, kk, b, kf, v, s0)


def _swa_prompt_kernel(q_ref, kc_ref, kp_ref, vc_ref, vp_ref, sink_ref, o_ref, *, sub):
    blk = WINDOW
    first = pl.program_id(1) == 0
    k_all = jnp.concatenate([kp_ref[...], kc_ref[...]], axis=0)
    v_all = jnp.concatenate([vp_ref[...], vc_ref[...]], axis=0).astype(BF16)
    v_ext = jnp.concatenate([v_all, jnp.ones_like(v_all)], axis=1)
    k_swapped = pltpu.roll(k_all, HEAD_DIM, 1)
    lane = lax.broadcasted_iota(jnp.int32, (1, LANES), 1)
    low = lane < HEAD_DIM
    k_var = {(0, 0): jnp.where(low, k_all, 0.0), (0, 1): jnp.where(low, 0.0, k_swapped),
             (1, 1): jnp.where(low, 0.0, k_all), (1, 0): jnp.where(low, k_swapped, 0.0)}
    k_var = {key: val.astype(BF16) for key, val in k_var.items()}
    order = [j * ATT_GROUP + g for j in range(ATT_KV_HEADS) for half in range(2)
             for g in range(ATT_GROUP) if g % 2 == half]
    n_rows = ATT_Q_HEADS * blk
    qi = lax.broadcasted_iota(jnp.int32, (n_rows, 2 * blk), 0) % blk
    kj = lax.broadcasted_iota(jnp.int32, (n_rows, 2 * blk), 1)
    diff = blk + qi - kj
    band = (diff >= 0) & (diff < WINDOW)
    row_head = lax.broadcasted_iota(jnp.int32, (n_rows, 1), 0) // blk
    sink = jnp.zeros((n_rows, 1), F32)
    for idx, head in enumerate(order):
        sink = jnp.where(row_head == idx, sink_ref[0:1, head:head + 1], sink)
    scale = HEAD_DIM ** -0.5
    for sb in range(sub):
        qs = slice(sb * blk, (sb + 1) * blk)
        ks = slice(sb * blk, (sb + 2) * blk)
        parts = []
        for j in range(ATT_KV_HEADS):
            for half in range(2):
                heads = [j * ATT_GROUP + g for g in range(ATT_GROUP) if g % 2 == half]
                q_cat = jnp.concatenate([q_ref[qs, (h // 2) * LANES:(h // 2 + 1) * LANES] for h in heads], axis=0)
                parts.append(_mm_nt(q_cat * scale, k_var[(j, half)][ks]))
        s = jnp.concatenate(parts, axis=0)
        ok = band & jnp.logical_or(kj >= blk, jnp.logical_not(first)) if sb == 0 else band
        s = jnp.where(ok, s, -jnp.inf)
        m = jnp.maximum(jnp.max(s, axis=-1, keepdims=True), sink)
        e = jnp.exp(s - m).astype(BF16)
        ev = jnp.dot(e, v_ext[ks], preferred_element_type=F32)
        o = ev[:, 0:LANES] / (ev[:, LANES:LANES + 1] + jnp.exp(sink - m))
        by_head = {}
        for idx, head in enumerate(order):
            o_h = o[idx * blk:(idx + 1) * blk]
            by_head[head] = pltpu.roll(o_h, HEAD_DIM, 1) if head // ATT_GROUP != head % 2 else o_h
        for grp in range(ATT_DIM // LANES):
            o_ref[qs, grp * LANES:(grp + 1) * LANES] = jnp.where(low, by_head[2 * grp],
                                                                 by_head[2 * grp + 1]).astype(o_ref.dtype)


def _swa_prompt(q, k, v, P, batch, sub):
    n = q.shape[0]
    rows = sub * WINDOW
    nb = n // batch // rows
    cur = lambda b, i: (b * nb + i, 0)
    prev = lambda b, i: ((b * nb + i) * sub - jnp.minimum(i, 1), 0)
    return pl.pallas_call(
        functools.partial(_swa_prompt_kernel, sub=sub),
        grid=(batch, nb),
        in_specs=[pl.BlockSpec((rows, ATT_DIM), cur),
                  pl.BlockSpec((rows, ATT_KV_DIM), cur), pl.BlockSpec((WINDOW, ATT_KV_DIM), prev),
                  pl.BlockSpec((rows, ATT_KV_DIM), cur), pl.BlockSpec((WINDOW, ATT_KV_DIM), prev),
                  _layer_spec((1, ATT_Q_HEADS), P['layer'])],
        out_specs=pl.BlockSpec((rows, ATT_DIM), cur),
        out_shape=jax.ShapeDtypeStruct((n, ATT_DIM), BF16),
        compiler_params=_params(("arbitrary", "arbitrary")),
        name="swa_prompt",
    )(q, k, k, v, v, P['sinks'])


def _swa_sample_kernel(q_ref, kc_ref, vc_ref, kn_ref, vn_ref, sink_ref, o_ref, *, steps):
    q = q_ref[...].astype(BF16)
    rows = q.shape[1]
    nk = kc_ref.shape[1]
    nn = kn_ref.shape[1]
    bdot = lambda a, b, dims: lax.dot_general(a, b, (dims, ((0,), (0,))), preferred_element_type=F32)
    s_c = bdot(q, kc_ref[...].astype(BF16), ((2,), (2,))) * (HEAD_DIM ** -0.5)
    s_n = bdot(q, kn_ref[...].astype(BF16), ((2,), (2,))) * (HEAD_DIM ** -0.5)
    t_c = lax.broadcasted_iota(jnp.int32, (1, rows, nk), 1) % steps
    w_c = lax.broadcasted_iota(jnp.int32, (1, rows, nk), 2)
    s_c = jnp.where(w_c > t_c, s_c, -jnp.inf)
    t_n = lax.broadcasted_iota(jnp.int32, (1, rows, nn), 1) % steps
    w_n = lax.broadcasted_iota(jnp.int32, (1, rows, nn), 2)
    s_n = jnp.where(w_n <= t_n, s_n, -jnp.inf)
    sink = sink_ref[...][None, :, 0:1]
    m = jnp.maximum(jnp.maximum(jnp.max(s_c, axis=-1, keepdims=True), jnp.max(s_n, axis=-1, keepdims=True)), sink)
    e_c = jnp.exp(s_c - m)
    e_n = jnp.exp(s_n - m)
    den = jnp.sum(e_c, axis=-1, keepdims=True) + jnp.sum(e_n, axis=-1, keepdims=True) + jnp.exp(sink - m)
    o = (bdot((e_c / den).astype(BF16), vc_ref[...].astype(BF16), ((2,), (1,)))
         + bdot((e_n / den).astype(BF16), vn_ref[...].astype(BF16), ((2,), (1,))))
    o_ref[...] = o


def _swa_sample(q_st, k_cache, v_cache, k_new, v_new, sink_rows, layer, steps, bt):
    nb, rows, _ = q_st.shape
    nn = k_new.shape[1]
    blk = lambda r: pl.BlockSpec((bt, r, LANES), lambda i: (i, 0, 0))
    cache = pl.BlockSpec((None, bt, WINDOW, LANES), lambda i: (layer, i, 0, 0))
    return pl.pallas_call(
        functools.partial(_swa_sample_kernel, steps=steps),
        grid=(nb // bt,),
        in_specs=[blk(rows), cache, cache, blk(nn), blk(nn), _layer_spec((rows, LANES), layer)],
        out_specs=blk(rows),
        out_shape=jax.ShapeDtypeStruct((nb, rows, LANES), F32),
        compiler_params=_params(("arbitrary",)),
        name="swa_sample",
    )(q_st, k_cache, v_cache, k_new, v_new, sink_rows)


def _gelu(x):
    return 0.5 * x * (1.0 + lax.erf(x * (2.0 ** -0.5)))


def _back_kernel(*refs, final, sample):
    x_ref, y_ref, bonus_ref, g_ref, oa_ref, gr_ref, ga_ref = refs[:7]
    at = 7
    if sample:
        cp_ref = refs[at]
        at += 1
    (lng_ref, lnb_ref, ones_ref, wa_ref, wb_ref, wo_ref, gn_ref, wi_ref, cw_ref, cb_ref, wd_ref,
     gf_ref) = refs[at:at + 12]
    o_ref, ct_ref, carry = refs[at + 12:at + 15]

    y = y_ref[...]
    ones_bd = ones_ref[...]
    mean = _head_sum(y, ones_bd) * (1.0 / HEAD_DIM)
    d = y - mean
    var = _head_sum(d * d, ones_bd) * (1.0 / HEAD_DIM)
    yn = d * lax.rsqrt(var + GN_EPS) * lng_ref[...] + lnb_ref[...]
    o_r = (yn + bonus_ref[...]) * g_ref[...]
    merged = gr_ref[...] * _mm(o_r, wa_ref[...]) + ga_ref[...] * _mm(oa_ref[...], wb_ref[...])
    x = x_ref[...] + _mm(merged, wo_ref[...])

    h = _rmsnorm(x, gn_ref[...]).astype(BF16)
    c = jnp.dot(h, wi_ref[:, 0:D_FF], preferred_element_type=F32)
    up = jnp.dot(h, wi_ref[:, D_FF:2 * D_FF], preferred_element_type=F32)
    if sample:
        @pl.when(pl.program_id(0) == 0)
        def _():
            carry[...] = cp_ref[...]

        c2, c1 = carry[0], carry[1]
        carry[0] = c1
        carry[1] = c
        ct_ref[0] = c
    else:
        @pl.when(pl.program_id(1) == 0)
        def _():
            carry[...] = jnp.zeros_like(carry)

        tm = c.shape[0]
        row = lax.broadcasted_iota(jnp.int32, (tm, 1), 0)
        last = carry[SUBLANES - 1:SUBLANES, :]
        c1 = jnp.where(row == 0, last, pltpu.roll(c, 1, 0))
        c2 = jnp.where(row == 0, carry[SUBLANES - 2:SUBLANES - 1, :],
                       jnp.where(row == 1, last, pltpu.roll(c, 2, 0)))
        tail = c[tm - SUBLANES:, :]
        carry[...] = tail
        ct_ref[0] = tail
    conv = cb_ref[...] + c2 * cw_ref[0:1, :] + c1 * cw_ref[1:2, :] + c * cw_ref[2:3, :]
    out = x + _mm(_gelu(conv) * up, wd_ref[...])
    o_ref[...] = _rmsnorm(out, gf_ref[...]) if final else out


def _back_weight_specs(layer):
    return [_layer_spec((1, RWKV_DIM), layer), _layer_spec((1, RWKV_DIM), layer), _const_spec((RWKV_DIM, RWKV_DIM)),
            _layer_spec((RWKV_DIM, D_MODEL), layer), _layer_spec((ATT_DIM, D_MODEL), layer),
            _layer_spec((D_MODEL, D_MODEL), layer),
            _layer_spec((1, D_MODEL), layer), _layer_spec((D_MODEL, 2 * D_FF), layer),
            _layer_spec((CONV_W, D_FF), layer), _layer_spec((1, D_FF), layer), _layer_spec((D_FF, D_MODEL), layer),
            _const_spec((1, D_MODEL))]


def _back_weights(P, g_final):
    return [P['ln_g'], P['ln_b'], P['ones_bd'], P['w_br_rwkv'], P['w_br_attn'], P['w_out'],
            P['norm_ffn_g'], P['ffn_w_in'], P['ffn_conv_w'], P['ffn_conv_b'], P['ffn_w_down'], g_final]


def _back_prompt(x, y, bonus, g, o_a, gr, ga, P, g_final, final, batch, tm):
    n = x.shape[0]
    nt = n // batch // tm
    row = lambda b, t: (b * nt + t, 0)
    wide = pl.BlockSpec((tm, D_MODEL), row)
    half = pl.BlockSpec((tm, RWKV_DIM), row)
    return pl.pallas_call(
        functools.partial(_back_kernel, final=final, sample=False),
        grid=(batch, nt),
        in_specs=[wide, half, half, half, half, wide, wide] + _back_weight_specs(P['layer']),
        out_specs=[wide, pl.BlockSpec((1, SUBLANES, D_FF), lambda b, t: (b, 0, 0))],
        out_shape=[jax.ShapeDtypeStruct((n, D_MODEL), F32), jax.ShapeDtypeStruct((batch, SUBLANES, D_FF), F32)],
        scratch_shapes=[pltpu.VMEM((SUBLANES, D_FF), F32)],
        compiler_params=_params(("arbitrary", "arbitrary")),
        name="back_prompt",
    )(x, y, bonus, g, o_a, gr, ga, *_back_weights(P, g_final))


def _back_sample(x, y, bonus, g, o_a, gr, ga, conv_prev, P, g_final, final):
    n = x.shape[0]
    nb = conv_prev.shape[2]
    steps = n // nb
    taps = CONV_W - 1
    row = lambda t: (t, 0)
    wide = pl.BlockSpec((nb, D_MODEL), row)
    half = pl.BlockSpec((nb, RWKV_DIM), row)
    return pl.pallas_call(
        functools.partial(_back_kernel, final=final, sample=True),
        grid=(steps,),
        in_specs=[wide, half, half, half, half, wide, wide, _layer_spec((taps, nb, D_FF), P['layer'])]
        + _back_weight_specs(P['layer']),
        out_specs=[wide, pl.BlockSpec((1, nb, D_FF), lambda t: (jnp.maximum(t - (steps - taps), 0), 0, 0))],
        out_shape=[jax.ShapeDtypeStruct((n, D_MODEL), F32), jax.ShapeDtypeStruct((taps, nb, D_FF), F32)],
        scratch_shapes=[pltpu.VMEM((taps, nb, D_FF), F32)],
        compiler_params=_params(("arbitrary",)),
        name="back_sample",
    )(x, y, bonus, g, o_a, gr, ga, conv_prev, *_back_weights(P, g_final))


def _rope_tables(pos):
    inv = ROPE_THETA ** (-jnp.arange(0, HEAD_DIM, 2, dtype=F32) / HEAD_DIM)
    ang = pos.astype(F32)[:, None] * inv[None, :]
    cos, sin = jnp.cos(ang), jnp.sin(ang)
    reps = LANES // HEAD_DIM
    return jnp.tile(jnp.concatenate([cos, cos], axis=1), (1, reps)), jnp.tile(jnp.concatenate([-sin, sin], axis=1),
                                                                             (1, reps))


def _stacked_params(norm_mix_g, w_in, rwkv_mu, rwkv_w0, rwkv_w2, rwkv_a0, rwkv_a2, rwkv_g2, rwkv_k_k, rwkv_k_a,
                    rwkv_r_k, rwkv_ln_g, rwkv_ln_b, attn_sinks, w_br_rwkv, w_br_attn, w_out, norm_ffn_g, ffn_w_in,
                    ffn_conv_w, ffn_conv_b, ffn_w_down, steps):
    d = RWKV_DIM
    depth = w_in.shape[0]
    head = jnp.arange(d) // HEAD_DIM
    row = lambda t: t.reshape(depth, 1, -1)
    sink_rows = jnp.broadcast_to(jnp.repeat(attn_sinks, steps, axis=1)[:, :, None],
                                 (depth, ATT_Q_HEADS * steps, LANES))
    return {
        'norm_mix_g': row(norm_mix_g), 'w_in': w_in.astype(BF16),
        'mu': row(rwkv_mu), 'w0': row(rwkv_w0),
        'w2p': jnp.pad(rwkv_w2, ((0, 0), (0, D_AAA_LORA), (0, 0))).astype(BF16),
        'a0': row(rwkv_a0),
        'a2p': jnp.pad(rwkv_a2, ((0, 0), (D_DECAY_LORA, 0), (0, 0))).astype(BF16),
        'g2': rwkv_g2.astype(BF16), 'k_k': row(rwkv_k_k), 'k_a': row(rwkv_k_a),
        'r_k': row(rwkv_r_k), 'ln_g': row(rwkv_ln_g), 'ln_b': row(rwkv_ln_b),
        'ones_bd': (head[:, None] == head[None, :]).astype(BF16),
        'sinks': row(attn_sinks), 'sink_rows': sink_rows,
        'w_br_rwkv': w_br_rwkv.astype(BF16), 'w_br_attn': w_br_attn.astype(BF16),
        'w_out': w_out.astype(BF16), 'norm_ffn_g': row(norm_ffn_g),
        'ffn_w_in': ffn_w_in.astype(BF16), 'ffn_conv_w': ffn_conv_w, 'ffn_conv_b': row(ffn_conv_b),
        'ffn_w_down': ffn_w_down.astype(BF16),
    }


def _prompt_layer(x, P, tables, g_final, final, batch, seq):
    tm = min(256, seq)
    q, k, v, gr, ga, z_tail, r, lw, kk, b, kf, vv, g, bonus = _front_prompt(x, P, tables[0], tables[1], batch, tm)
    y, s_t = _wkv_prompt(r, lw, kk, b, kf, vv, batch, min(256, seq))
    o_a = _swa_prompt(q, k, v, P, batch, min(2, seq // WINDOW))
    x2, c_tail = _back_prompt(x, y, bonus, g, o_a, gr, ga, P, g_final, final, batch, tm)
    last = lambda t: t.reshape(batch, seq, -1)[:, -WINDOW:].reshape(batch, WINDOW, ATT_KV_HEADS, HEAD_DIM)
    state = (z_tail[:, SUBLANES - 1], s_t, last(k), last(v), c_tail[:, SUBLANES - (CONV_W - 1):])
    return x2, state


def _sample_layer(x, P, tables, g_final, final, shift_prev, wkv0, k_cache, v_cache, conv_prev, nb, steps):
    n = nb * steps
    layer = P['layer']
    q, k, v, gr, ga, z_tail, r, lw, kk, b, kf, vv, g, bonus = _front_sample(x, shift_prev, P, tables[0], tables[1])
    to_lanes = lambda t: t.reshape(steps, nb, RWKV_HEADS, HEAD_DIM).transpose(0, 2, 3, 1)
    y_l, s_l = _wkv_sample(*(to_lanes(t) for t in (r, lw, kk, b, kf)), to_lanes(vv).astype(F32), wkv0, layer, steps)
    y = y_l.transpose(0, 3, 1, 2).reshape(n, RWKV_DIM)

    rows = ATT_Q_HEADS * steps
    q5 = q.reshape(steps, nb, ATT_KV_HEADS, ATT_GROUP, HEAD_DIM).transpose(1, 2, 3, 0, 4)
    sel = jnp.eye(ATT_KV_HEADS, dtype=q.dtype)
    q_st = (q5[:, :, :, :, None, :] * sel[None, :, None, None, :, None]).reshape(nb, rows, ATT_KV_DIM)
    pad = ((0, 0), (0, SUBLANES - steps), (0, 0))
    k_new = k.reshape(steps, nb, ATT_KV_DIM).transpose(1, 0, 2)
    v_new = v.reshape(steps, nb, ATT_KV_DIM).transpose(1, 0, 2)
    o_st = _swa_sample(q_st, k_cache, v_cache, jnp.pad(k_new, pad), jnp.pad(v_new, pad), P['sink_rows'], layer,
                       steps, 16)
    o5 = o_st.reshape(nb, ATT_KV_HEADS, ATT_GROUP, steps, ATT_KV_HEADS, HEAD_DIM)
    o_a = jnp.stack([o5[:, j, :, :, j, :] for j in range(ATT_KV_HEADS)], axis=0)
    o_a = o_a.transpose(3, 1, 0, 2, 4).reshape(n, ATT_DIM)

    x2, c_tail = _back_sample(x, y, bonus, g, o_a, gr, ga, conv_prev, P, g_final, final)
    state = (z_tail, s_l, k_new, v_new, c_tail)
    return x2, state


def kernel(x_prompt, x_sample, state_rwkv_shift, state_rwkv_wkv, cache_swa_k, cache_swa_v, state_ffn_conv, norm_mix_g, w_in, rwkv_mu, rwkv_w0, rwkv_w2, rwkv_a0, rwkv_a2, rwkv_g2, rwkv_k_k, rwkv_k_a, rwkv_r_k, rwkv_ln_g, rwkv_ln_b, attn_sinks, w_br_rwkv, w_br_attn, w_out, norm_ffn_g, ffn_w_in, ffn_conv_w, ffn_conv_b, ffn_w_down, norm_final_g):
    bp, tp, _ = x_prompt.shape
    nb, steps, _ = x_sample.shape
    depth = w_in.shape[0]
    tab_p = _rope_tables(jnp.arange(tp, dtype=jnp.int32))
    tab_s = _rope_tables(PAST_LEN + jnp.repeat(jnp.arange(steps, dtype=jnp.int32), nb))
    g_final = norm_final_g[None]
    params = _stacked_params(norm_mix_g, w_in, rwkv_mu, rwkv_w0, rwkv_w2, rwkv_a0, rwkv_a2, rwkv_g2, rwkv_k_k,
                             rwkv_k_a, rwkv_r_k, rwkv_ln_g, rwkv_ln_b, attn_sinks, w_br_rwkv, w_br_attn, w_out,
                             norm_ffn_g, ffn_w_in, ffn_conv_w, ffn_conv_b, ffn_w_down, steps)
    wkv0 = state_rwkv_wkv.transpose(0, 2, 3, 4, 1)
    k_cache = cache_swa_k.reshape(depth, nb, WINDOW, ATT_KV_DIM)
    v_cache = cache_swa_v.reshape(depth, nb, WINDOW, ATT_KV_DIM)
    conv_prev = state_ffn_conv.transpose(0, 2, 1, 3)
    xp = x_prompt.reshape(bp * tp, D_MODEL)
    xs = x_sample.transpose(1, 0, 2).reshape(steps * nb, D_MODEL)
    outs_p, outs_s = [], []
    for l in range(depth):
        P = dict(params, layer=l)
        final = l == depth - 1
        xp, sp = _prompt_layer(xp, P, tab_p, g_final, final, bp, tp)
        xs, ss = _sample_layer(xs, P, tab_s, g_final, final, state_rwkv_shift, wkv0, k_cache, v_cache, conv_prev,
                               nb, steps)
        outs_p.append(sp)
        outs_s.append(ss)
    y_prompt = xp.reshape(bp, tp, D_MODEL)
    y_sample = xs.reshape(steps, nb, D_MODEL).transpose(1, 0, 2)
    stack = lambda outs, i: jnp.stack([o[i] for o in outs])
    kv_shape = (depth, nb, steps, ATT_KV_HEADS, HEAD_DIM)
    s_k = jnp.concatenate([cache_swa_k[:, :, steps:], stack(outs_s, 2).reshape(kv_shape)], axis=2)
    s_v = jnp.concatenate([cache_swa_v[:, :, steps:], stack(outs_s, 3).reshape(kv_shape)], axis=2)
    s_wkv = stack(outs_s, 1).transpose(0, 4, 1, 2, 3)
    s_conv = stack(outs_s, 4).transpose(0, 2, 1, 3)
    return (y_prompt, y_sample, *(stack(outs_p, i) for i in range(5)),
            stack(outs_s, 0), s_wkv, s_k, s_v, s_conv)
```

```python
import functools

import jax
import jax.numpy as jnp
from jax import lax
from jax.experimental import pallas as pl
from jax.experimental.pallas import tpu as pltpu

F32 = jnp.float32
BF16 = jnp.bfloat16

D_MODEL = 1024
HEAD_DIM = 64
RWKV_HEADS = 8
RWKV_DIM = RWKV_HEADS * HEAD_DIM
D_DECAY_LORA = 64
D_AAA_LORA = 64
D_GATE_LORA = 128
RWKV_PROJ = 3 * RWKV_DIM + D_DECAY_LORA + D_AAA_LORA + D_GATE_LORA
ATT_Q_HEADS = 8
ATT_KV_HEADS = 2
ATT_GROUP = ATT_Q_HEADS // ATT_KV_HEADS
ATT_DIM = ATT_Q_HEADS * HEAD_DIM
ATT_KV_DIM = ATT_KV_HEADS * HEAD_DIM
WINDOW = 128
IN_PROJ = RWKV_PROJ + ATT_DIM + 2 * ATT_KV_DIM + 2 * D_MODEL
D_FF = 2816
CONV_W = 3
ROPE_THETA = 10000.0
RMS_EPS = 1e-6
GN_EPS = 64e-5
PAST_LEN = 16384

LANES = 128
SUBLANES = 8
MXU_DIM = 256
CHUNK = 64
HEADS_PER_GROUP = MXU_DIM // HEAD_DIM
VMEM_LIMIT = 56 * 1024 * 1024


def _params(sem):
    return pltpu.CompilerParams(dimension_semantics=sem, vmem_limit_bytes=VMEM_LIMIT)


def _const_spec(shape):
    zeros = (0,) * len(shape)
    return pl.BlockSpec(shape, lambda *_: zeros, pipeline_mode=pl.Buffered(1))


def _layer_spec(shape, layer):
    index = (layer,) + (0,) * len(shape)
    return pl.BlockSpec((None,) + tuple(shape), lambda *_: index, pipeline_mode=pl.Buffered(1))


def _mm(a, b):
    return jnp.dot(a.astype(BF16), b.astype(BF16), preferred_element_type=F32)


def _mm_nt(a, b):
    return lax.dot_general(a.astype(BF16), b.astype(BF16), (((1,), (1,)), ((), ())),
                           preferred_element_type=F32)


def _mm_f32(a, b):
    return jnp.dot(a, b, precision=lax.Precision.HIGHEST, preferred_element_type=F32)


def _mm_tn(a, b):
    return lax.dot_general(a.astype(BF16), b.astype(BF16), (((0,), (0,)), ((), ())),
                           preferred_element_type=F32)


def _head_sum(x, ones_bd):
    return jnp.dot(x.astype(BF16), ones_bd, preferred_element_type=F32)


def _rmsnorm(x, g):
    return x * lax.rsqrt(jnp.mean(x * x, axis=-1, keepdims=True) + RMS_EPS) * g


def _rope(x, cos, sin_signed, first_half):
    partner = jnp.where(first_half, pltpu.roll(x, LANES - HEAD_DIM // 2, 1), pltpu.roll(x, HEAD_DIM // 2, 1))
    return x * cos + partner * sin_signed


def _prep_math(z, zp, mu, w0, w2p, a0, a2p, g2, k_k, k_a, r_k, ones_bd, outs, rs):
    r_ref, lw_ref, kk_ref, b_ref, kf_ref, v_ref, g_ref, bonus_ref = outs
    zs = z + mu * (zp - z)
    d = RWKV_DIM
    r, k, v = zs[:, 0:d], zs[:, d:2 * d], zs[:, 2 * d:3 * d]
    zwa = zs[:, 3 * d:3 * d + LANES]
    zg = zs[:, 3 * d + LANES:3 * d + 2 * LANES]
    w_pre = w0 + _mm(jnp.tanh(zwa), w2p)
    neg = -w_pre
    softplus = jnp.maximum(neg, 0.0) + jnp.log1p(jnp.exp(-jnp.abs(neg)))
    w_log = -softplus - 0.5
    lw_ref[rs, :] = -jnp.exp(w_log)
    a = jax.nn.sigmoid(a0 + _mm(zwa, a2p))
    g_ref[rs, :] = _mm(jax.nn.sigmoid(zg), g2).astype(g_ref.dtype)
    kk0 = k * k_k
    norm = jnp.sqrt(_head_sum(kk0 * kk0, ones_bd))
    kk = kk0 / jnp.maximum(norm, 1e-12)
    kf = k * (1.0 + (a - 1.0) * k_a)
    r_ref[rs, :] = r.astype(r_ref.dtype)
    kk_ref[rs, :] = kk.astype(kk_ref.dtype)
    b_ref[rs, :] = (kk * a).astype(b_ref.dtype)
    kf_ref[rs, :] = kf.astype(kf_ref.dtype)
    v_ref[rs, :] = v.astype(v_ref.dtype)
    bonus_ref[rs, :] = (_head_sum(r * kf * r_k, ones_bd) * v).astype(bonus_ref.dtype)


N_PREP_CONSTS = 10
N_FRONT_OUTS = 14


def _row_blocks(n, nsub):
    rb = n // nsub
    return [slice(i * rb, (i + 1) * rb) for i in range(nsub)], rb


def _staggered(stages, nsub):
    for step in range(nsub + len(stages) - 1):
        for i in range(nsub):
            if 0 <= step - i < len(stages):
                stages[step - i](i)


def _front_kernel(*refs, sample, nsub):
    if sample:
        x_ref, sp_ref, g_ref, w_ref, cos_ref, sin_ref = refs[:6]
        rest = refs[6:]
    else:
        x_ref, g_ref, w_ref, cos_ref, sin_ref = refs[:5]
        rest = refs[5:]
    consts = [c[...] for c in rest[:N_PREP_CONSTS]]
    outs = rest[N_PREP_CONSTS:N_PREP_CONSTS + N_FRONT_OUTS]
    q_ref, k_ref, v_ref, gr_ref, ga_ref, zt_ref = outs[:6]
    blocks, rb = _row_blocks(x_ref.shape[0], nsub)
    c0, c1, c2, c3, c4 = (RWKV_PROJ, RWKV_PROJ + ATT_DIM, RWKV_PROJ + ATT_DIM + ATT_KV_DIM,
                          RWKV_PROJ + ATT_DIM + 2 * ATT_KV_DIM, RWKV_PROJ + ATT_DIM + 2 * ATT_KV_DIM + D_MODEL)
    if not sample:
        carry = rest[N_PREP_CONSTS + N_FRONT_OUTS]

        @pl.when(pl.program_id(1) == 0)
        def _():
            carry[...] = jnp.zeros_like(carry)

    row = lax.broadcasted_iota(jnp.int32, (rb, 1), 0)
    lane = lax.broadcasted_iota(jnp.int32, (1, LANES), 1)
    first_half = (lane % HEAD_DIM) < (HEAD_DIM // 2)
    hs, zs = {}, {}

    def project(i):
        hs[i] = _rmsnorm(x_ref[blocks[i], :], g_ref[...]).astype(BF16)
        zs[i] = jnp.dot(hs[i], w_ref[:, 0:c0], preferred_element_type=F32)

    def rwkv_operands(i):
        z = zs[i]
        if sample:
            zp = sp_ref[...] if i == 0 else zs[i - 1]
            if i == nsub - 1:
                zt_ref[...] = z
        else:
            tail = carry[...] if i == 0 else zs[i - 1][rb - SUBLANES:, :]
            zp = jnp.where(row == 0, tail[SUBLANES - 1:SUBLANES, :], pltpu.roll(z, 1, 0))
            if i == nsub - 1:
                carry[...] = z[rb - SUBLANES:, :]
                zt_ref[0] = z[rb - SUBLANES:, :]
        _prep_math(z, zp, *consts, outs[6:], blocks[i])

    def attention_operands(i):
        rs, h = blocks[i], hs[i]
        cos = cos_ref[rs, :]
        sin = sin_ref[rs, :]
        q = jnp.dot(h, w_ref[:, c0:c1], preferred_element_type=F32)
        for grp in range(ATT_DIM // LANES):
            sl = slice(grp * LANES, (grp + 1) * LANES)
            q_ref[rs, sl] = _rope(q[:, sl], cos, sin, first_half).astype(q_ref.dtype)
        k = jnp.dot(h, w_ref[:, c1:c2], preferred_element_type=F32)
        k_ref[rs, :] = _rope(k, cos, sin, first_half)
        v_ref[rs, :] = jnp.dot(h, w_ref[:, c2:c3], preferred_element_type=F32)
        gr_ref[rs, :] = jax.nn.sigmoid(jnp.dot(h, w_ref[:, c3:c4], preferred_element_type=F32)).astype(gr_ref.dtype)
        ga_ref[rs, :] = jax.nn.sigmoid(jnp.dot(h, w_ref[:, c4:IN_PROJ], preferred_element_type=F32)).astype(ga_ref.dtype)

    _staggered([project, rwkv_operands, attention_operands], nsub)


def _prep_consts(P):
    return [P['mu'], P['w0'], P['w2p'], P['a0'], P['a2p'], P['g2'], P['k_k'], P['k_a'], P['r_k'], P['ones_bd']]


def _prep_const_specs(layer):
    d = RWKV_DIM
    shapes = [(1, RWKV_PROJ), (1, d), (LANES, d), (1, d), (LANES, d), (D_GATE_LORA, d), (1, d), (1, d), (1, d)]
    return [_layer_spec(s, layer) for s in shapes] + [_const_spec((d, d))]


def _front_out_types(n, tail_shape):
    d = RWKV_DIM
    widths = [(ATT_DIM, BF16), (ATT_KV_DIM, F32), (ATT_KV_DIM, F32), (D_MODEL, BF16), (D_MODEL, BF16)]
    prep = [(d, BF16), (d, F32), (d, BF16), (d, BF16), (d, BF16), (d, BF16), (d, BF16), (d, BF16)]
    shapes = ([jax.ShapeDtypeStruct((n, w), t) for w, t in widths] + [jax.ShapeDtypeStruct(tail_shape, F32)]
              + [jax.ShapeDtypeStruct((n, w), t) for w, t in prep])
    return shapes, [w for w, _ in widths], [w for w, _ in prep]


def _front_prompt(x2d, P, cos_t, sin_t, batch, tm):
    n = x2d.shape[0]
    layer = P['layer']
    nt = n // batch // tm
    row = lambda b, t: (b * nt + t, 0)
    pos = lambda b, t: (t, 0)
    shapes, widths, prep = _front_out_types(n, (batch, SUBLANES, RWKV_PROJ))
    return pl.pallas_call(
        functools.partial(_front_kernel, sample=False, nsub=tm // 128),
        grid=(batch, nt),
        in_specs=[pl.BlockSpec((tm, D_MODEL), row), _layer_spec((1, D_MODEL), layer),
                  _layer_spec((D_MODEL, IN_PROJ), layer),
                  pl.BlockSpec((tm, LANES), pos), pl.BlockSpec((tm, LANES), pos)] + _prep_const_specs(layer),
        out_specs=([pl.BlockSpec((tm, w), row) for w in widths]
                   + [pl.BlockSpec((1, SUBLANES, RWKV_PROJ), lambda b, t: (b, 0, 0))]
                   + [pl.BlockSpec((tm, w), row) for w in prep]),
        out_shape=shapes,
        scratch_shapes=[pltpu.VMEM((SUBLANES, RWKV_PROJ), F32)],
        compiler_params=_params(("arbitrary", "arbitrary")),
        name="front_prompt",
    )(x2d, P['norm_mix_g'], P['w_in'], cos_t, sin_t, *_prep_consts(P))


def _front_sample(x2d, shift_prev, P, cos_t, sin_t):
    n = x2d.shape[0]
    layer = P['layer']
    nb = shift_prev.shape[1]
    shapes, _, _ = _front_out_types(n, (nb, RWKV_PROJ))
    return pl.pallas_call(
        functools.partial(_front_kernel, sample=True, nsub=n // nb),
        grid=(1,),
        in_specs=[_const_spec((n, D_MODEL)), _layer_spec((nb, RWKV_PROJ), layer), _layer_spec((1, D_MODEL), layer),
                  _layer_spec((D_MODEL, IN_PROJ), layer), _const_spec((n, LANES)), _const_spec((n, LANES))]
        + _prep_const_specs(layer),
        out_specs=[_const_spec(sh.shape) for sh in shapes],
        out_shape=shapes,
        compiler_params=_params(("arbitrary",)),
        name="front_sample",
    )(x2d, shift_prev, P['norm_mix_g'], P['w_in'], cos_t, sin_t, *_prep_consts(P))


def _wkv_prompt_kernel(r_ref, lw_ref, kk_ref, b_ref, kf_ref, v_ref, y_ref, st_ref, s_scr, *, chunks):
    C, G, M = CHUNK, HEADS_PER_GROUP, MXU_DIM
    nbatch = r_ref.shape[0]
    ngrp = RWKV_HEADS // G

    @pl.when(pl.program_id(0) == 0)
    def _():
        s_scr[...] = jnp.zeros_like(s_scr)

    ti = lax.broadcasted_iota(jnp.int32, (C, M), 0)
    si = lax.broadcasted_iota(jnp.int32, (C, M), 1) % C
    strict = ti > si
    incl = ti >= si
    eye = (ti == si).astype(F32)
    tri = (lax.broadcasted_iota(jnp.int32, (C, C), 0) >= lax.broadcasted_iota(jnp.int32, (C, C), 1)).astype(F32)
    lane_head = lax.broadcasted_iota(jnp.int32, (1, M), 1) // HEAD_DIM
    same_head = (lax.broadcasted_iota(jnp.int32, (M, M), 0) // HEAD_DIM
                 == lax.broadcasted_iota(jnp.int32, (M, M), 1) // HEAD_DIM)

    def stack(x):
        xb = x.astype(BF16)
        return jnp.concatenate([jnp.where(lane_head == h, xb, jnp.zeros_like(xb)) for h in range(G)], axis=0)

    def rows2(top, bottom):
        return jnp.concatenate([top.astype(BF16), bottom.astype(BF16)], axis=0)

    def chunk_body(c, carry):
        rows = pl.ds(pl.multiple_of(c * C, C), C)
        probs = []
        for bi in range(nbatch):
            lw = lw_ref[bi, rows, :]
            cum = _mm_f32(tri, lw)
            w_end = jnp.exp(cum[C - 1:C, :])
            e_neg = jnp.exp(-cum)
            a_hat = -kk_ref[bi, rows, :] * jnp.exp(cum - lw)
            r_hat = r_ref[bi, rows, :] * jnp.exp(cum)
            b_til = b_ref[bi, rows, :] * e_neg
            k_til = kf_ref[bi, rows, :] * e_neg
            v = v_ref[bi, rows, :]
            for grp in range(ngrp):
                ls = slice(grp * M, (grp + 1) * M)
                we = w_end[:, ls]
                probs.append(dict(
                    bi=bi, grp=grp, ls=ls, we=we, ar=rows2(a_hat[:, ls], r_hat[:, ls]),
                    b=stack(b_til[:, ls]), k=stack(k_til[:, ls]), v=v[:, ls], vs=stack(v[:, ls]),
                    bk=rows2(b_til[:, ls] * we, k_til[:, ls] * we)))
        mb = [_mm_nt(p['ar'], p['b']) for p in probs]
        mk = [_mm_nt(p['ar'], p['k']) for p in probs]
        l_ab = [jnp.where(strict, x[0:C], 0.0) for x in mb]
        p_rb = [jnp.where(incl, x[C:2 * C], 0.0).astype(BF16) for x in mb]
        lp_k = [rows2(jnp.where(strict, x[0:C], 0.0), jnp.where(incl, x[C:2 * C], 0.0)) for x in mk]
        t_inv = [eye + x for x in l_ab]
        pw = [_mm(x, stack(x)) for x in l_ab]
        for _ in range(C.bit_length() - 3):
            both = [_mm(rows2(t, x), stack(x)) for t, x in zip(t_inv, pw)]
            t_inv = [t + x[0:C] for t, x in zip(t_inv, both)]
            pw = [x[C:2 * C] for x in both]
        t_inv = [t + _mm(t, stack(x)) for t, x in zip(t_inv, pw)]
        s0 = [s_scr[p['bi'], p['grp']] for p in probs]
        d0 = [_mm_nt(p['ar'], x) for p, x in zip(probs, s0)]
        e0 = [_mm(x, p['vs']) for x, p in zip(lp_k, probs)]
        u = [_mm(t, stack(x[0:C] + y[0:C])) for t, x, y in zip(t_inv, d0, e0)]
        for p, x, y, w, q, st in zip(probs, d0, e0, u, p_rb, s0):
            y_ref[p['bi'], rows, p['ls']] = x[C:2 * C] + y[C:2 * C] + _mm(q, stack(w))
            upd = _mm_tn(rows2(w, p['v']), p['bk'])
            s_scr[p['bi'], p['grp']] = st * p['we'] + jnp.where(same_head, upd, 0.0)
        return carry

    lax.fori_loop(0, chunks, chunk_body, 0)

    @pl.when(pl.program_id(0) == pl.num_programs(0) - 1)
    def _():
        st_ref[...] = s_scr[...]


def _wkv_prompt(r, lw, kk, b, kf, v, batch, tl):
    n = r.shape[0]
    seq = n // batch
    ngrp = RWKV_HEADS // HEADS_PER_GROUP
    blk = pl.BlockSpec((batch, tl, RWKV_DIM), lambda t: (0, t, 0))
    y, st = pl.pallas_call(
        functools.partial(_wkv_prompt_kernel, chunks=tl // CHUNK),
        grid=(seq // tl,),
        in_specs=[blk] * 6,
        out_specs=[blk, _const_spec((batch, ngrp, MXU_DIM, MXU_DIM))],
        out_shape=[jax.ShapeDtypeStruct((batch, seq, RWKV_DIM), F32),
                   jax.ShapeDtypeStruct((batch, ngrp, MXU_DIM, MXU_DIM), F32)],
        scratch_shapes=[pltpu.VMEM((batch, ngrp, MXU_DIM, MXU_DIM), F32)],
        compiler_params=_params(("arbitrary",)),
        name="wkv_prompt",
    )(*(t.reshape(batch, seq, RWKV_DIM) for t in (r, lw, kk, b, kf, v)))
    st = st.reshape(batch, ngrp, HEADS_PER_GROUP, HEAD_DIM, HEADS_PER_GROUP, HEAD_DIM)
    st = jnp.stack([st[:, :, h, :, h, :] for h in range(HEADS_PER_GROUP)], axis=2)
    st = st.reshape(batch, RWKV_HEADS, HEAD_DIM, HEAD_DIM)
    return y.reshape(n, RWKV_DIM), st


def _wkv_sample_kernel(r_ref, lw_ref, kk_ref, b_ref, kf_ref, v_ref, s_ref, y_ref, so_ref, *, steps):
    def value_row(vi, carry):
        s = s_ref[0, vi]
        for t in range(steps):
            sk = jnp.sum(s * kk_ref[t, 0].astype(F32), axis=0, keepdims=True)
            v_row = v_ref[t, 0, pl.ds(vi, 1), :]
            s = (s * jnp.exp(lw_ref[t, 0]) - sk * b_ref[t, 0].astype(F32)
                 + v_row * kf_ref[t, 0].astype(F32))
            y_ref[t, 0, pl.ds(vi, 1), :] = jnp.sum(s * r_ref[t, 0].astype(F32), axis=0, keepdims=True)
        so_ref[0, vi] = s
        return carry

    lax.fori_loop(0, HEAD_DIM, value_row, 0)


def _wkv_sample(r, lw, kk, b, kf, v, s0, layer, steps):
    nb = r.shape[-1]
    vec = pl.BlockSpec((steps, 1, HEAD_DIM, nb), lambda h: (0, h, 0, 0))
    st = pl.BlockSpec((1, HEAD_DIM, HEAD_DIM, nb), lambda h: (h, 0, 0, 0))
    st_in = pl.BlockSpec((None, 1, HEAD_DIM, HEAD_DIM, nb), lambda h: (layer, h, 0, 0, 0))
    return pl.pallas_call(
        functools.partial(_wkv_sample_kernel, steps=steps),
        grid=(RWKV_HEADS,),
        in_specs=[vec] * 6 + [st_in],
        out_specs=[vec, st],
        out_shape=[jax.ShapeDtypeStruct(r.shape, F32), jax.ShapeDtypeStruct(s0.shape[1:], F32)],
        compiler_params=_params(("arbitrary",)),
        name="wkv_sample",
    )(r, lw, kk, b, kf, v, s0)


def _swa_prompt_kernel(q_ref, kc_ref, kp_ref, vc_ref, vp_ref, sink_ref, o_ref, *, sub):
    blk = WINDOW
    first = pl.program_id(1) == 0
    k_all = jnp.concatenate([kp_ref[...], kc_ref[...]], axis=0)
    v_all = jnp.concatenate([vp_ref[...], vc_ref[...]], axis=0).astype(BF16)
    v_ext = jnp.concatenate([v_all, jnp.ones_like(v_all)], axis=1)
    k_swapped = pltpu.roll(k_all, HEAD_DIM, 1)
    lane = lax.broadcasted_iota(jnp.int32, (1, LANES), 1)
    low = lane < HEAD_DIM
    k_var = {(0, 0): jnp.where(low, k_all, 0.0), (0, 1): jnp.where(low, 0.0, k_swapped),
             (1, 1): jnp.where(low, 0.0, k_all), (1, 0): jnp.where(low, k_swapped, 0.0)}
    k_var = {key: val.astype(BF16) for key, val in k_var.items()}
    order = [j * ATT_GROUP + g for j in range(ATT_KV_HEADS) for half in range(2)
             for g in range(ATT_GROUP) if g % 2 == half]
    n_rows = ATT_Q_HEADS * blk
    qi = lax.broadcasted_iota(jnp.int32, (n_rows, 2 * blk), 0) % blk
    kj = lax.broadcasted_iota(jnp.int32, (n_rows, 2 * blk), 1)
    diff = blk + qi - kj
    band = (diff >= 0) & (diff < WINDOW)
    row_head = lax.broadcasted_iota(jnp.int32, (n_rows, 1), 0) // blk
    sink = jnp.zeros((n_rows, 1), F32)
    for idx, head in enumerate(order):
        sink = jnp.where(row_head == idx, sink_ref[0:1, head:head + 1], sink)
    scale = HEAD_DIM ** -0.5
    for sb in range(sub):
        qs = slice(sb * blk, (sb + 1) * blk)
        ks = slice(sb * blk, (sb + 2) * blk)
        parts = []
        for j in range(ATT_KV_HEADS):
            for half in range(2):
                heads = [j * ATT_GROUP + g for g in range(ATT_GROUP) if g % 2 == half]
                q_cat = jnp.concatenate([q_ref[qs, (h // 2) * LANES:(h // 2 + 1) * LANES] for h in heads], axis=0)
                parts.append(_mm_nt(q_cat * scale, k_var[(j, half)][ks]))
        s = jnp.concatenate(parts, axis=0)
        ok = band & jnp.logical_or(kj >= blk, jnp.logical_not(first)) if sb == 0 else band
        s = jnp.where(ok, s, -jnp.inf)
        m = jnp.maximum(jnp.max(s, axis=-1, keepdims=True), sink)
        e = jnp.exp(s - m).astype(BF16)
        ev = jnp.dot(e, v_ext[ks], preferred_element_type=F32)
        o = ev[:, 0:LANES] / (ev[:, LANES:LANES + 1] + jnp.exp(sink - m))
        by_head = {}
        for idx, head in enumerate(order):
            o_h = o[idx * blk:(idx + 1) * blk]
            by_head[head] = pltpu.roll(o_h, HEAD_DIM, 1) if head // ATT_GROUP != head % 2 else o_h
        for grp in range(ATT_DIM // LANES):
            o_ref[qs, grp * LANES:(grp + 1) * LANES] = jnp.where(low, by_head[2 * grp],
                                                                 by_head[2 * grp + 1]).astype(o_ref.dtype)


def _swa_prompt(q, k, v, P, batch, sub):
    n = q.shape[0]
    rows = sub * WINDOW
    nb = n // batch // rows
    cur = lambda b, i: (b * nb + i, 0)
    prev = lambda b, i: ((b * nb + i) * sub - jnp.minimum(i, 1), 0)
    return pl.pallas_call(
        functools.partial(_swa_prompt_kernel, sub=sub),
        grid=(batch, nb),
        in_specs=[pl.BlockSpec((rows, ATT_DIM), cur),
                  pl.BlockSpec((rows, ATT_KV_DIM), cur), pl.BlockSpec((WINDOW, ATT_KV_DIM), prev),
                  pl.BlockSpec((rows, ATT_KV_DIM), cur), pl.BlockSpec((WINDOW, ATT_KV_DIM), prev),
                  _layer_spec((1, ATT_Q_HEADS), P['layer'])],
        out_specs=pl.BlockSpec((rows, ATT_DIM), cur),
        out_shape=jax.ShapeDtypeStruct((n, ATT_DIM), BF16),
        compiler_params=_params(("arbitrary", "arbitrary")),
        name="swa_prompt",
    )(q, k, k, v, v, P['sinks'])


def _swa_sample_kernel(q_ref, kc_ref, vc_ref, kn_ref, vn_ref, sink_ref, o_ref, *, steps):
    q = q_ref[...].astype(BF16)
    rows = q.shape[1]
    nk = kc_ref.shape[1]
    nn = kn_ref.shape[1]
    bdot = lambda a, b, dims: lax.dot_general(a, b, (dims, ((0,), (0,))), preferred_element_type=F32)
    s_c = bdot(q, kc_ref[...].astype(BF16), ((2,), (2,))) * (HEAD_DIM ** -0.5)
    s_n = bdot(q, kn_ref[...].astype(BF16), ((2,), (2,))) * (HEAD_DIM ** -0.5)
    t_c = lax.broadcasted_iota(jnp.int32, (1, rows, nk), 1) % steps
    w_c = lax.broadcasted_iota(jnp.int32, (1, rows, nk), 2)
    s_c = jnp.where(w_c > t_c, s_c, -jnp.inf)
    t_n = lax.broadcasted_iota(jnp.int32, (1, rows, nn), 1) % steps
    w_n = lax.broadcasted_iota(jnp.int32, (1, rows, nn), 2)
    s_n = jnp.where(w_n <= t_n, s_n, -jnp.inf)
    sink = sink_ref[...][None, :, 0:1]
    m = jnp.maximum(jnp.maximum(jnp.max(s_c, axis=-1, keepdims=True), jnp.max(s_n, axis=-1, keepdims=True)), sink)
    e_c = jnp.exp(s_c - m)
    e_n = jnp.exp(s_n - m)
    den = jnp.sum(e_c, axis=-1, keepdims=True) + jnp.sum(e_n, axis=-1, keepdims=True) + jnp.exp(sink - m)
    o = (bdot((e_c / den).astype(BF16), vc_ref[...].astype(BF16), ((2,), (1,)))
         + bdot((e_n / den).astype(BF16), vn_ref[...].astype(BF16), ((2,), (1,))))
    o_ref[...] = o


def _swa_sample(q_st, k_cache, v_cache, k_new, v_new, sink_rows, layer, steps, bt):
    nb, rows, _ = q_st.shape
    nn = k_new.shape[1]
    blk = lambda r: pl.BlockSpec((bt, r, LANES), lambda i: (i, 0, 0))
    cache = pl.BlockSpec((None, bt, WINDOW, LANES), lambda i: (layer, i, 0, 0))
    return pl.pallas_call(
        functools.partial(_swa_sample_kernel, steps=steps),
        grid=(nb // bt,),
        in_specs=[blk(rows), cache, cache, blk(nn), blk(nn), _layer_spec((rows, LANES), layer)],
        out_specs=blk(rows),
        out_shape=jax.ShapeDtypeStruct((nb, rows, LANES), F32),
        compiler_params=_params(("arbitrary",)),
        name="swa_sample",
    )(q_st, k_cache, v_cache, k_new, v_new, sink_rows)


def _gelu(x):
    return 0.5 * x * (1.0 + lax.erf(x * (2.0 ** -0.5)))


def _back_kernel(*refs, final, sample, nsub):
    x_ref, y_ref, bonus_ref, g_ref, oa_ref, gr_ref, ga_ref = refs[:7]
    at = 7
    if sample:
        cp_ref = refs[at]
        at += 1
    (lng_ref, lnb_ref, ones_ref, wa_ref, wb_ref, wo_ref, gn_ref, wi_ref, cw_ref, cb_ref, wd_ref,
     gf_ref) = refs[at:at + 12]
    o_ref, ct_ref, carry = refs[at + 12:at + 15]
    blocks, rb = _row_blocks(x_ref.shape[0], nsub)
    ones_bd = ones_ref[...]

    if sample:
        @pl.when(pl.program_id(0) == 0)
        def _():
            carry[...] = cp_ref[...]
    else:
        @pl.when(pl.program_id(1) == 0)
        def _():
            carry[...] = jnp.zeros_like(carry)

    row = lax.broadcasted_iota(jnp.int32, (rb, 1), 0)
    xs, cs, ups, acts = {}, {}, {}, {}

    def mix(i):
        rs = blocks[i]
        y = y_ref[rs, :]
        mean = _head_sum(y, ones_bd) * (1.0 / HEAD_DIM)
        d = y - mean
        var = _head_sum(d * d, ones_bd) * (1.0 / HEAD_DIM)
        yn = d * lax.rsqrt(var + GN_EPS) * lng_ref[...] + lnb_ref[...]
        o_r = (yn + bonus_ref[rs, :]) * g_ref[rs, :]
        merged = gr_ref[rs, :] * _mm(o_r, wa_ref[...]) + ga_ref[rs, :] * _mm(oa_ref[rs, :], wb_ref[...])
        xs[i] = x_ref[rs, :] + _mm(merged, wo_ref[...])

    def up_project(i):
        h = _rmsnorm(xs[i], gn_ref[...]).astype(BF16)
        cs[i] = jnp.dot(h, wi_ref[:, 0:D_FF], preferred_element_type=F32)
        ups[i] = jnp.dot(h, wi_ref[:, D_FF:2 * D_FF], preferred_element_type=F32)

    def conv_gate(i):
        c = cs[i]
        if sample:
            c2, c1 = carry[0], carry[1]
            carry[0] = c1
            carry[1] = c
            ct_ref[0] = c
        else:
            tail = carry[...] if i == 0 else cs[i - 1][rb - SUBLANES:, :]
            last = tail[SUBLANES - 1:SUBLANES, :]
            c1 = jnp.where(row == 0, last, pltpu.roll(c, 1, 0))
            c2 = jnp.where(row == 0, tail[SUBLANES - 2:SUBLANES - 1, :],
                           jnp.where(row == 1, last, pltpu.roll(c, 2, 0)))
            if i == nsub - 1:
                carry[...] = c[rb - SUBLANES:, :]
                ct_ref[0] = c[rb - SUBLANES:, :]
        conv = cb_ref[...] + c2 * cw_ref[0:1, :] + c1 * cw_ref[1:2, :] + c * cw_ref[2:3, :]
        acts[i] = (_gelu(conv) * ups[i]).astype(BF16)

    def down_project(i):
        out = xs[i] + jnp.dot(acts[i], wd_ref[...], preferred_element_type=F32)
        o_ref[blocks[i], :] = _rmsnorm(out, gf_ref[...]) if final else out

    _staggered([mix, up_project, conv_gate, down_project], nsub)


def _back_weight_specs(layer):
    return [_layer_spec((1, RWKV_DIM), layer), _layer_spec((1, RWKV_DIM), layer), _const_spec((RWKV_DIM, RWKV_DIM)),
            _layer_spec((RWKV_DIM, D_MODEL), layer), _layer_spec((ATT_DIM, D_MODEL), layer),
            _layer_spec((D_MODEL, D_MODEL), layer),
            _layer_spec((1, D_MODEL), layer), _layer_spec((D_MODEL, 2 * D_FF), layer),
            _layer_spec((CONV_W, D_FF), layer), _layer_spec((1, D_FF), layer), _layer_spec((D_FF, D_MODEL), layer),
            _const_spec((1, D_MODEL))]


def _back_weights(P, g_final):
    return [P['ln_g'], P['ln_b'], P['ones_bd'], P['w_br_rwkv'], P['w_br_attn'], P['w_out'],
            P['norm_ffn_g'], P['ffn_w_in'], P['ffn_conv_w'], P['ffn_conv_b'], P['ffn_w_down'], g_final]


def _back_prompt(x, y, bonus, g, o_a, gr, ga, P, g_final, final, batch, tm):
    n = x.shape[0]
    nt = n // batch // tm
    row = lambda b, t: (b * nt + t, 0)
    wide = pl.BlockSpec((tm, D_MODEL), row)
    half = pl.BlockSpec((tm, RWKV_DIM), row)
    return pl.pallas_call(
        functools.partial(_back_kernel, final=final, sample=False, nsub=tm // 128),
        grid=(batch, nt),
        in_specs=[wide, half, half, half, half, wide, wide] + _back_weight_specs(P['layer']),
        out_specs=[wide, pl.BlockSpec((1, SUBLANES, D_FF), lambda b, t: (b, 0, 0))],
        out_shape=[jax.ShapeDtypeStruct((n, D_MODEL), F32), jax.ShapeDtypeStruct((batch, SUBLANES, D_FF), F32)],
        scratch_shapes=[pltpu.VMEM((SUBLANES, D_FF), F32)],
        compiler_params=_params(("arbitrary", "arbitrary")),
        name="back_prompt",
    )(x, y, bonus, g, o_a, gr, ga, *_back_weights(P, g_final))


def _back_sample(x, y, bonus, g, o_a, gr, ga, conv_prev, P, g_final, final):
    n = x.shape[0]
    nb = conv_prev.shape[2]
    steps = n // nb
    taps = CONV_W - 1
    row = lambda t: (t, 0)
    wide = pl.BlockSpec((nb, D_MODEL), row)
    half = pl.BlockSpec((nb, RWKV_DIM), row)
    return pl.pallas_call(
        functools.partial(_back_kernel, final=final, sample=True, nsub=1),
        grid=(steps,),
        in_specs=[wide, half, half, half, half, wide, wide, _layer_spec((taps, nb, D_FF), P['layer'])]
        + _back_weight_specs(P['layer']),
        out_specs=[wide, pl.BlockSpec((1, nb, D_FF), lambda t: (jnp.maximum(t - (steps - taps), 0), 0, 0))],
        out_shape=[jax.ShapeDtypeStruct((n, D_MODEL), F32), jax.ShapeDtypeStruct((taps, nb, D_FF), F32)],
        scratch_shapes=[pltpu.VMEM((taps, nb, D_FF), F32)],
        compiler_params=_params(("arbitrary",)),
        name="back_sample",
    )(x, y, bonus, g, o_a, gr, ga, conv_prev, *_back_weights(P, g_final))


def _rope_tables(pos):
    inv = ROPE_THETA ** (-jnp.arange(0, HEAD_DIM, 2, dtype=F32) / HEAD_DIM)
    ang = pos.astype(F32)[:, None] * inv[None, :]
    cos, sin = jnp.cos(ang), jnp.sin(ang)
    reps = LANES // HEAD_DIM
    return jnp.tile(jnp.concatenate([cos, cos], axis=1), (1, reps)), jnp.tile(jnp.concatenate([-sin, sin], axis=1),
                                                                             (1, reps))


def _stacked_params(norm_mix_g, w_in, rwkv_mu, rwkv_w0, rwkv_w2, rwkv_a0, rwkv_a2, rwkv_g2, rwkv_k_k, rwkv_k_a,
                    rwkv_r_k, rwkv_ln_g, rwkv_ln_b, attn_sinks, w_br_rwkv, w_br_attn, w_out, norm_ffn_g, ffn_w_in,
                    ffn_conv_w, ffn_conv_b, ffn_w_down, steps):
    d = RWKV_DIM
    depth = w_in.shape[0]
    head = jnp.arange(d) // HEAD_DIM
    row = lambda t: t.reshape(depth, 1, -1)
    sink_rows = jnp.broadcast_to(jnp.repeat(attn_sinks, steps, axis=1)[:, :, None],
                                 (depth, ATT_Q_HEADS * steps, LANES))
    return {
        'norm_mix_g': row(norm_mix_g), 'w_in': w_in.astype(BF16),
        'mu': row(rwkv_mu), 'w0': row(rwkv_w0),
        'w2p': jnp.pad(rwkv_w2, ((0, 0), (0, D_AAA_LORA), (0, 0))).astype(BF16),
        'a0': row(rwkv_a0),
        'a2p': jnp.pad(rwkv_a2, ((0, 0), (D_DECAY_LORA, 0), (0, 0))).astype(BF16),
        'g2': rwkv_g2.astype(BF16), 'k_k': row(rwkv_k_k), 'k_a': row(rwkv_k_a),
        'r_k': row(rwkv_r_k), 'ln_g': row(rwkv_ln_g), 'ln_b': row(rwkv_ln_b),
        'ones_bd': (head[:, None] == head[None, :]).astype(BF16),
        'sinks': row(attn_sinks), 'sink_rows': sink_rows,
        'w_br_rwkv': w_br_rwkv.astype(BF16), 'w_br_attn': w_br_attn.astype(BF16),
        'w_out': w_out.astype(BF16), 'norm_ffn_g': row(norm_ffn_g),
        'ffn_w_in': ffn_w_in.astype(BF16), 'ffn_conv_w': ffn_conv_w, 'ffn_conv_b': row(ffn_conv_b),
        'ffn_w_down': ffn_w_down.astype(BF16),
    }


def _prompt_layer(x, P, tables, g_final, final, batch, seq):
    tm = min(512, seq)
    q, k, v, gr, ga, z_tail, r, lw, kk, b, kf, vv, g, bonus = _front_prompt(x, P, tables[0], tables[1], batch, tm)
    y, s_t = _wkv_prompt(r, lw, kk, b, kf, vv, batch, min(256, seq))
    o_a = _swa_prompt(q, k, v, P, batch, min(2, seq // WINDOW))
    x2, c_tail = _back_prompt(x, y, bonus, g, o_a, gr, ga, P, g_final, final, batch, tm)
    last = lambda t: t.reshape(batch, seq, -1)[:, -WINDOW:].reshape(batch, WINDOW, ATT_KV_HEADS, HEAD_DIM)
    state = (z_tail[:, SUBLANES - 1], s_t, last(k), last(v), c_tail[:, SUBLANES - (CONV_W - 1):])
    return x2, state


def _sample_layer(x, P, tables, g_final, final, shift_prev, wkv0, k_cache, v_cache, conv_prev, nb, steps):
    n = nb * steps
    layer = P['layer']
    q, k, v, gr, ga, z_tail, r, lw, kk, b, kf, vv, g, bonus = _front_sample(x, shift_prev, P, tables[0], tables[1])
    to_lanes = lambda t: t.reshape(steps, nb, RWKV_HEADS, HEAD_DIM).transpose(0, 2, 3, 1)
    y_l, s_l = _wkv_sample(*(to_lanes(t) for t in (r, lw, kk, b, kf)), to_lanes(vv).astype(F32), wkv0, layer, steps)
    y = y_l.transpose(0, 3, 1, 2).reshape(n, RWKV_DIM)

    rows = ATT_Q_HEADS * steps
    q5 = q.reshape(steps, nb, ATT_KV_HEADS, ATT_GROUP, HEAD_DIM).transpose(1, 2, 3, 0, 4)
    sel = jnp.eye(ATT_KV_HEADS, dtype=q.dtype)
    q_st = (q5[:, :, :, :, None, :] * sel[None, :, None, None, :, None]).reshape(nb, rows, ATT_KV_DIM)
    pad = ((0, 0), (0, SUBLANES - steps), (0, 0))
    k_new = k.reshape(steps, nb, ATT_KV_DIM).transpose(1, 0, 2)
    v_new = v.reshape(steps, nb, ATT_KV_DIM).transpose(1, 0, 2)
    o_st = _swa_sample(q_st, k_cache, v_cache, jnp.pad(k_new, pad), jnp.pad(v_new, pad), P['sink_rows'], layer,
                       steps, 16)
    o5 = o_st.reshape(nb, ATT_KV_HEADS, ATT_GROUP, steps, ATT_KV_HEADS, HEAD_DIM)
    o_a = jnp.stack([o5[:, j, :, :, j, :] for j in range(ATT_KV_HEADS)], axis=0)
    o_a = o_a.transpose(3, 1, 0, 2, 4).reshape(n, ATT_DIM)

    x2, c_tail = _back_sample(x, y, bonus, g, o_a, gr, ga, conv_prev, P, g_final, final)
    state = (z_tail, s_l, k_new, v_new, c_tail)
    return x2, state


def kernel(x_prompt, x_sample, state_rwkv_shift, state_rwkv_wkv, cache_swa_k, cache_swa_v, state_ffn_conv, norm_mix_g, w_in, rwkv_mu, rwkv_w0, rwkv_w2, rwkv_a0, rwkv_a2, rwkv_g2, rwkv_k_k, rwkv_k_a, rwkv_r_k, rwkv_ln_g, rwkv_ln_b, attn_sinks, w_br_rwkv, w_br_attn, w_out, norm_ffn_g, ffn_w_in, ffn_conv_w, ffn_conv_b, ffn_w_down, norm_final_g):
    bp, tp, _ = x_prompt.shape
    nb, steps, _ = x_sample.shape
    depth = w_in.shape[0]
    tab_p = _rope_tables(jnp.arange(tp, dtype=jnp.int32))
    tab_s = _rope_tables(PAST_LEN + jnp.repeat(jnp.arange(steps, dtype=jnp.int32), nb))
    g_final = norm_final_g[None]
    params = _stacked_params(norm_mix_g, w_in, rwkv_mu, rwkv_w0, rwkv_w2, rwkv_a0, rwkv_a2, rwkv_g2, rwkv_k_k,
                             rwkv_k_a, rwkv_r_k, rwkv_ln_g, rwkv_ln_b, attn_sinks, w_br_rwkv, w_br_attn, w_out,
                             norm_ffn_g, ffn_w_in, ffn_conv_w, ffn_conv_b, ffn_w_down, steps)
    wkv0 = state_rwkv_wkv.transpose(0, 2, 3, 4, 1)
    k_cache = cache_swa_k.reshape(depth, nb, WINDOW, ATT_KV_DIM)
    v_cache = cache_swa_v.reshape(depth, nb, WINDOW, ATT_KV_DIM)
    conv_prev = state_ffn_conv.transpose(0, 2, 1, 3)
    xp = x_prompt.reshape(bp * tp, D_MODEL)
    xs = x_sample.transpose(1, 0, 2).reshape(steps * nb, D_MODEL)
    outs_p, outs_s = [], []
    for l in range(depth):
        P = dict(params, layer=l)
        final = l == depth - 1
        xp, sp = _prompt_layer(xp, P, tab_p, g_final, final, bp, tp)
        xs, ss = _sample_layer(xs, P, tab_s, g_final, final, state_rwkv_shift, wkv0, k_cache, v_cache, conv_prev,
                               nb, steps)
        outs_p.append(sp)
        outs_s.append(ss)
    y_prompt = xp.reshape(bp, tp, D_MODEL)
    y_sample = xs.reshape(steps, nb, D_MODEL).transpose(1, 0, 2)
    stack = lambda outs, i: jnp.stack([o[i] for o in outs])
    kv_shape = (depth, nb, steps, ATT_KV_HEADS, HEAD_DIM)
    s_k = jnp.concatenate([cache_swa_k[:, :, steps:], stack(outs_s, 2).reshape(kv_shape)], axis=2)
    s_v = jnp.concatenate([cache_swa_v[:, :, steps:], stack(outs_s, 3).reshape(kv_shape)], axis=2)
    s_wkv = stack(outs_s, 1).transpose(0, 4, 1, 2, 3)
    s_conv = stack(outs_s, 4).transpose(0, 2, 1, 3)
    return (y_prompt, y_sample, *(stack(outs_p, i) for i in range(5)),
            stack(outs_s, 0), s_wkv, s_k, s_v, s_conv)
```

```python
import functools

import jax
import jax.numpy as jnp
from jax import lax
from jax.experimental import pallas as pl
from jax.experimental.pallas import tpu as pltpu

F32 = jnp.float32
BF16 = jnp.bfloat16

D_MODEL = 1024
HEAD_DIM = 64
RWKV_HEADS = 8
RWKV_DIM = RWKV_HEADS * HEAD_DIM
D_DECAY_LORA = 64
D_AAA_LORA = 64
D_GATE_LORA = 128
RWKV_PROJ = 3 * RWKV_DIM + D_DECAY_LORA + D_AAA_LORA + D_GATE_LORA
ATT_Q_HEADS = 8
ATT_KV_HEADS = 2
ATT_GROUP = ATT_Q_HEADS // ATT_KV_HEADS
ATT_DIM = ATT_Q_HEADS * HEAD_DIM
ATT_KV_DIM = ATT_KV_HEADS * HEAD_DIM
WINDOW = 128
IN_PROJ = RWKV_PROJ + ATT_DIM + 2 * ATT_KV_DIM + 2 * D_MODEL
D_FF = 2816
CONV_W = 3
ROPE_THETA = 10000.0
RMS_EPS = 1e-6
GN_EPS = 64e-5
PAST_LEN = 16384

LANES = 128
SUBLANES = 8
MXU_DIM = 256
CHUNK = 64
HEADS_PER_GROUP = MXU_DIM // HEAD_DIM
VMEM_LIMIT = 56 * 1024 * 1024


def _params(sem):
    return pltpu.CompilerParams(dimension_semantics=sem, vmem_limit_bytes=VMEM_LIMIT)


def _const_spec(shape):
    zeros = (0,) * len(shape)
    return pl.BlockSpec(shape, lambda *_: zeros, pipeline_mode=pl.Buffered(1))


def _layer_spec(shape, layer):
    index = (layer,) + (0,) * len(shape)
    return pl.BlockSpec((None,) + tuple(shape), lambda *_: index, pipeline_mode=pl.Buffered(1))


def _mm(a, b):
    return jnp.dot(a.astype(BF16), b.astype(BF16), preferred_element_type=F32)


def _mm_nt(a, b):
    return lax.dot_general(a.astype(BF16), b.astype(BF16), (((1,), (1,)), ((), ())),
                           preferred_element_type=F32)


def _mm_f32(a, b):
    return jnp.dot(a, b, precision=lax.Precision.HIGHEST, preferred_element_type=F32)


def _mm_tn(a, b):
    return lax.dot_general(a.astype(BF16), b.astype(BF16), (((0,), (0,)), ((), ())),
                           preferred_element_type=F32)


def _head_sum(x, ones_bd):
    return jnp.dot(x.astype(BF16), ones_bd, preferred_element_type=F32)


def _rmsnorm(x, g):
    return x * lax.rsqrt(jnp.mean(x * x, axis=-1, keepdims=True) + RMS_EPS) * g


def _rope(x, cos, sin_signed, first_half):
    partner = jnp.where(first_half, pltpu.roll(x, LANES - HEAD_DIM // 2, 1), pltpu.roll(x, HEAD_DIM // 2, 1))
    return x * cos + partner * sin_signed


def _prep_math(z, zp, mu, w0, w2p, a0, a2p, g2, k_k, k_a, r_k, ones_bd, outs, rs):
    r_ref, lw_ref, kk_ref, b_ref, kf_ref, v_ref, g_ref, bonus_ref = outs
    zs = z + mu * (zp - z)
    d = RWKV_DIM
    r, k, v = zs[:, 0:d], zs[:, d:2 * d], zs[:, 2 * d:3 * d]
    zwa = zs[:, 3 * d:3 * d + LANES]
    zg = zs[:, 3 * d + LANES:3 * d + 2 * LANES]
    w_pre = w0 + _mm(jnp.tanh(zwa), w2p)
    neg = -w_pre
    softplus = jnp.maximum(neg, 0.0) + jnp.log1p(jnp.exp(-jnp.abs(neg)))
    w_log = -softplus - 0.5
    lw_ref[rs, :] = -jnp.exp(w_log)
    a = jax.nn.sigmoid(a0 + _mm(zwa, a2p))
    g_ref[rs, :] = _mm(jax.nn.sigmoid(zg), g2).astype(g_ref.dtype)
    kk0 = k * k_k
    norm = jnp.sqrt(_head_sum(kk0 * kk0, ones_bd))
    kk = kk0 / jnp.maximum(norm, 1e-12)
    kf = k * (1.0 + (a - 1.0) * k_a)
    r_ref[rs, :] = r.astype(r_ref.dtype)
    kk_ref[rs, :] = kk.astype(kk_ref.dtype)
    b_ref[rs, :] = (kk * a).astype(b_ref.dtype)
    kf_ref[rs, :] = kf.astype(kf_ref.dtype)
    v_ref[rs, :] = v.astype(v_ref.dtype)
    bonus_ref[rs, :] = (_head_sum(r * kf * r_k, ones_bd) * v).astype(bonus_ref.dtype)


N_PREP_CONSTS = 10
N_FRONT_OUTS = 14


def _row_blocks(n, nsub):
    rb = n // nsub
    return [slice(i * rb, (i + 1) * rb) for i in range(nsub)], rb


def _staggered(stages, nsub):
    for step in range(nsub + len(stages) - 1):
        for i in range(nsub):
            if 0 <= step - i < len(stages):
                stages[step - i](i)


def _front_kernel(*refs, sample, nsub):
    if sample:
        x_ref, sp_ref, g_ref, w_ref, cos_ref, sin_ref = refs[:6]
        rest = refs[6:]
    else:
        x_ref, g_ref, w_ref, cos_ref, sin_ref = refs[:5]
        rest = refs[5:]
    consts = [c[...] for c in rest[:N_PREP_CONSTS]]
    outs = rest[N_PREP_CONSTS:N_PREP_CONSTS + N_FRONT_OUTS]
    q_ref, k_ref, v_ref, gr_ref, ga_ref, zt_ref = outs[:6]
    blocks, rb = _row_blocks(x_ref.shape[0], nsub)
    c0, c1, c2, c3, c4 = (RWKV_PROJ, RWKV_PROJ + ATT_DIM, RWKV_PROJ + ATT_DIM + ATT_KV_DIM,
                          RWKV_PROJ + ATT_DIM + 2 * ATT_KV_DIM, RWKV_PROJ + ATT_DIM + 2 * ATT_KV_DIM + D_MODEL)
    if not sample:
        carry = rest[N_PREP_CONSTS + N_FRONT_OUTS]

        @pl.when(pl.program_id(1) == 0)
        def _():
            carry[...] = jnp.zeros_like(carry)

    row = lax.broadcasted_iota(jnp.int32, (rb, 1), 0)
    lane = lax.broadcasted_iota(jnp.int32, (1, LANES), 1)
    first_half = (lane % HEAD_DIM) < (HEAD_DIM // 2)
    hs, zs = {}, {}

    def project(i):
        hs[i] = _rmsnorm(x_ref[blocks[i], :], g_ref[...]).astype(BF16)
        zs[i] = jnp.dot(hs[i], w_ref[:, 0:c0], preferred_element_type=F32)

    def rwkv_operands(i):
        z = zs[i]
        if sample:
            zp = sp_ref[...] if i == 0 else zs[i - 1]
            if i == nsub - 1:
                zt_ref[...] = z
        else:
            tail = carry[...] if i == 0 else zs[i - 1][rb - SUBLANES:, :]
            zp = jnp.where(row == 0, tail[SUBLANES - 1:SUBLANES, :], pltpu.roll(z, 1, 0))
            if i == nsub - 1:
                carry[...] = z[rb - SUBLANES:, :]
                zt_ref[0] = z[rb - SUBLANES:, :]
        _prep_math(z, zp, *consts, outs[6:], blocks[i])

    def attention_operands(i):
        rs, h = blocks[i], hs[i]
        cos = cos_ref[rs, :]
        sin = sin_ref[rs, :]
        q = jnp.dot(h, w_ref[:, c0:c1], preferred_element_type=F32)
        for grp in range(ATT_DIM // LANES):
            sl = slice(grp * LANES, (grp + 1) * LANES)
            q_ref[rs, sl] = _rope(q[:, sl], cos, sin, first_half).astype(q_ref.dtype)
        k = jnp.dot(h, w_ref[:, c1:c2], preferred_element_type=F32)
        k_ref[rs, :] = _rope(k, cos, sin, first_half)
        v_ref[rs, :] = jnp.dot(h, w_ref[:, c2:c3], preferred_element_type=F32)
        gr_ref[rs, :] = jax.nn.sigmoid(jnp.dot(h, w_ref[:, c3:c4], preferred_element_type=F32)).astype(gr_ref.dtype)
        ga_ref[rs, :] = jax.nn.sigmoid(jnp.dot(h, w_ref[:, c4:IN_PROJ], preferred_element_type=F32)).astype(ga_ref.dtype)

    _staggered([project, rwkv_operands, attention_operands], nsub)


def _prep_consts(P):
    return [P['mu'], P['w0'], P['w2p'], P['a0'], P['a2p'], P['g2'], P['k_k'], P['k_a'], P['r_k'], P['ones_bd']]


def _prep_const_specs(layer):
    d = RWKV_DIM
    shapes = [(1, RWKV_PROJ), (1, d), (LANES, d), (1, d), (LANES, d), (D_GATE_LORA, d), (1, d), (1, d), (1, d)]
    return [_layer_spec(s, layer) for s in shapes] + [_const_spec((d, d))]


def _front_out_types(n, tail_shape):
    d = RWKV_DIM
    widths = [(ATT_DIM, BF16), (ATT_KV_DIM, F32), (ATT_KV_DIM, F32), (D_MODEL, BF16), (D_MODEL, BF16)]
    prep = [(d, BF16), (d, F32), (d, BF16), (d, BF16), (d, BF16), (d, BF16), (d, BF16), (d, BF16)]
    shapes = ([jax.ShapeDtypeStruct((n, w), t) for w, t in widths] + [jax.ShapeDtypeStruct(tail_shape, F32)]
              + [jax.ShapeDtypeStruct((n, w), t) for w, t in prep])
    return shapes, [w for w, _ in widths], [w for w, _ in prep]


def _front_prompt(x2d, P, cos_t, sin_t, batch, tm):
    n = x2d.shape[0]
    layer = P['layer']
    nt = n // batch // tm
    row = lambda b, t: (b * nt + t, 0)
    pos = lambda b, t: (t, 0)
    shapes, widths, prep = _front_out_types(n, (batch, SUBLANES, RWKV_PROJ))
    return pl.pallas_call(
        functools.partial(_front_kernel, sample=False, nsub=tm // 128),
        grid=(batch, nt),
        in_specs=[pl.BlockSpec((tm, D_MODEL), row), _layer_spec((1, D_MODEL), layer),
                  _layer_spec((D_MODEL, IN_PROJ), layer),
                  pl.BlockSpec((tm, LANES), pos), pl.BlockSpec((tm, LANES), pos)] + _prep_const_specs(layer),
        out_specs=([pl.BlockSpec((tm, w), row) for w in widths]
                   + [pl.BlockSpec((1, SUBLANES, RWKV_PROJ), lambda b, t: (b, 0, 0))]
                   + [pl.BlockSpec((tm, w), row) for w in prep]),
        out_shape=shapes,
        scratch_shapes=[pltpu.VMEM((SUBLANES, RWKV_PROJ), F32)],
        compiler_params=_params(("arbitrary", "arbitrary")),
        name="front_prompt",
    )(x2d, P['norm_mix_g'], P['w_in'], cos_t, sin_t, *_prep_consts(P))


def _front_sample(x2d, shift_prev, P, cos_t, sin_t):
    n = x2d.shape[0]
    layer = P['layer']
    nb = shift_prev.shape[1]
    shapes, _, _ = _front_out_types(n, (nb, RWKV_PROJ))
    return pl.pallas_call(
        functools.partial(_front_kernel, sample=True, nsub=n // nb),
        grid=(1,),
        in_specs=[_const_spec((n, D_MODEL)), _layer_spec((nb, RWKV_PROJ), layer), _layer_spec((1, D_MODEL), layer),
                  _layer_spec((D_MODEL, IN_PROJ), layer), _const_spec((n, LANES)), _const_spec((n, LANES))]
        + _prep_const_specs(layer),
        out_specs=[_const_spec(sh.shape) for sh in shapes],
        out_shape=shapes,
        compiler_params=_params(("arbitrary",)),
        name="front_sample",
    )(x2d, shift_prev, P['norm_mix_g'], P['w_in'], cos_t, sin_t, *_prep_consts(P))


def _wkv_prompt_kernel(r_ref, lw_ref, kk_ref, b_ref, kf_ref, v_ref, y_ref, st_ref, s_scr, *, chunks, together):
    C, G, M = CHUNK, HEADS_PER_GROUP, MXU_DIM
    nbatch = r_ref.shape[0]
    ngrp = RWKV_HEADS // G

    @pl.when(pl.program_id(0) == 0)
    def _():
        s_scr[...] = jnp.zeros_like(s_scr)

    ti = lax.broadcasted_iota(jnp.int32, (C, M), 0)
    si = lax.broadcasted_iota(jnp.int32, (C, M), 1) % C
    strict = ti > si
    incl = ti >= si
    eye = (ti == si).astype(F32)
    tri = (lax.broadcasted_iota(jnp.int32, (C, C), 0) >= lax.broadcasted_iota(jnp.int32, (C, C), 1)).astype(F32)
    lane_head = lax.broadcasted_iota(jnp.int32, (1, M), 1) // HEAD_DIM
    same_head = (lax.broadcasted_iota(jnp.int32, (M, M), 0) // HEAD_DIM
                 == lax.broadcasted_iota(jnp.int32, (M, M), 1) // HEAD_DIM)

    def stack(x):
        xb = x.astype(BF16)
        return jnp.concatenate([jnp.where(lane_head == h, xb, jnp.zeros_like(xb)) for h in range(G)], axis=0)

    def rows2(top, bottom):
        return jnp.concatenate([top.astype(BF16), bottom.astype(BF16)], axis=0)

    def load(rows):
        probs = []
        for bi in range(nbatch):
            lw = lw_ref[bi, rows, :]
            cum = _mm_f32(tri, lw)
            w_end = jnp.exp(cum[C - 1:C, :])
            e_neg = jnp.exp(-cum)
            a_hat = -kk_ref[bi, rows, :] * jnp.exp(cum - lw)
            r_hat = r_ref[bi, rows, :] * jnp.exp(cum)
            b_til = b_ref[bi, rows, :] * e_neg
            k_til = kf_ref[bi, rows, :] * e_neg
            v = v_ref[bi, rows, :]
            for grp in range(ngrp):
                ls = slice(grp * M, (grp + 1) * M)
                we = w_end[:, ls]
                probs.append(dict(
                    bi=bi, grp=grp, ls=ls, we=we, rows=rows, ar=rows2(a_hat[:, ls], r_hat[:, ls]),
                    b=stack(b_til[:, ls]), k=stack(k_til[:, ls]), v=v[:, ls], vs=stack(v[:, ls]),
                    bk=rows2(b_til[:, ls] * we, k_til[:, ls] * we)))
        return probs

    def state_free(probs):
        mb = [_mm_nt(p['ar'], p['b']) for p in probs]
        mk = [_mm_nt(p['ar'], p['k']) for p in probs]
        l_ab = [jnp.where(strict, x[0:C], 0.0) for x in mb]
        p_rb = [jnp.where(incl, x[C:2 * C], 0.0).astype(BF16) for x in mb]
        lp_k = [rows2(jnp.where(strict, x[0:C], 0.0), jnp.where(incl, x[C:2 * C], 0.0)) for x in mk]
        t_inv = [eye + x for x in l_ab]
        pw = [_mm(x, stack(x)) for x in l_ab]
        for _ in range(C.bit_length() - 3):
            both = [_mm(rows2(t, x), stack(x)) for t, x in zip(t_inv, pw)]
            t_inv = [t + x[0:C] for t, x in zip(t_inv, both)]
            pw = [x[C:2 * C] for x in both]
        t_inv = [t + _mm(t, stack(x)) for t, x in zip(t_inv, pw)]
        e0 = [_mm(x, p['vs']) for x, p in zip(lp_k, probs)]
        return t_inv, p_rb, e0

    def advance(probs, t_inv, p_rb, e0):
        s0 = [s_scr[p['bi'], p['grp']] for p in probs]
        d0 = [_mm_nt(p['ar'], x) for p, x in zip(probs, s0)]
        u = [_mm(t, stack(x[0:C] + y[0:C])) for t, x, y in zip(t_inv, d0, e0)]
        for p, x, y, w, q, st in zip(probs, d0, e0, u, p_rb, s0):
            y_ref[p['bi'], p['rows'], p['ls']] = x[C:2 * C] + y[C:2 * C] + _mm(q, stack(w))
            upd = _mm_tn(rows2(w, p['v']), p['bk'])
            s_scr[p['bi'], p['grp']] = st * p['we'] + jnp.where(same_head, upd, 0.0)

    def chunk_group(c, carry):
        chunk_probs = [load(pl.ds(pl.multiple_of((c * together + j) * C, C), C)) for j in range(together)]
        t_inv, p_rb, e0 = state_free([p for probs in chunk_probs for p in probs])
        n = len(chunk_probs[0])
        for j, probs in enumerate(chunk_probs):
            sl = slice(j * n, (j + 1) * n)
            advance(probs, t_inv[sl], p_rb[sl], e0[sl])
        return carry

    lax.fori_loop(0, chunks // together, chunk_group, 0)

    @pl.when(pl.program_id(0) == pl.num_programs(0) - 1)
    def _():
        st_ref[...] = s_scr[...]


def _wkv_prompt(r, lw, kk, b, kf, v, batch, tl):
    n = r.shape[0]
    seq = n // batch
    ngrp = RWKV_HEADS // HEADS_PER_GROUP
    blk = pl.BlockSpec((batch, tl, RWKV_DIM), lambda t: (0, t, 0))
    y, st = pl.pallas_call(
        functools.partial(_wkv_prompt_kernel, chunks=tl // CHUNK, together=2),
        grid=(seq // tl,),
        in_specs=[blk] * 6,
        out_specs=[blk, _const_spec((batch, ngrp, MXU_DIM, MXU_DIM))],
        out_shape=[jax.ShapeDtypeStruct((batch, seq, RWKV_DIM), F32),
                   jax.ShapeDtypeStruct((batch, ngrp, MXU_DIM, MXU_DIM), F32)],
        scratch_shapes=[pltpu.VMEM((batch, ngrp, MXU_DIM, MXU_DIM), F32)],
        compiler_params=_params(("arbitrary",)),
        name="wkv_prompt",
    )(*(t.reshape(batch, seq, RWKV_DIM) for t in (r, lw, kk, b, kf, v)))
    st = st.reshape(batch, ngrp, HEADS_PER_GROUP, HEAD_DIM, HEADS_PER_GROUP, HEAD_DIM)
    st = jnp.stack([st[:, :, h, :, h, :] for h in range(HEADS_PER_GROUP)], axis=2)
    st = st.reshape(batch, RWKV_HEADS, HEAD_DIM, HEAD_DIM)
    return y.reshape(n, RWKV_DIM), st


def _wkv_sample_kernel(r_ref, lw_ref, kk_ref, b_ref, kf_ref, v_ref, s_ref, y_ref, so_ref, *, steps):
    def value_row(vi, carry):
        s = s_ref[0, vi]
        for t in range(steps):
            sk = jnp.sum(s * kk_ref[t, 0].astype(F32), axis=0, keepdims=True)
            v_row = v_ref[t, 0, pl.ds(vi, 1), :]
            s = (s * jnp.exp(lw_ref[t, 0]) - sk * b_ref[t, 0].astype(F32)
                 + v_row * kf_ref[t, 0].astype(F32))
            y_ref[t, 0, pl.ds(vi, 1), :] = jnp.sum(s * r_ref[t, 0].astype(F32), axis=0, keepdims=True)
        so_ref[0, vi] = s
        return carry

    lax.fori_loop(0, HEAD_DIM, value_row, 0)


def _wkv_sample(r, lw, kk, b, kf, v, s0, layer, steps):
    nb = r.shape[-1]
    vec = pl.BlockSpec((steps, 1, HEAD_DIM, nb), lambda h: (0, h, 0, 0))
    st = pl.BlockSpec((1, HEAD_DIM, HEAD_DIM, nb), lambda h: (h, 0, 0, 0))
    st_in = pl.BlockSpec((None, 1, HEAD_DIM, HEAD_DIM, nb), lambda h: (layer, h, 0, 0, 0))
    return pl.pallas_call(
        functools.partial(_wkv_sample_kernel, steps=steps),
        grid=(RWKV_HEADS,),
        in_specs=[vec] * 6 + [st_in],
        out_specs=[vec, st],
        out_shape=[jax.ShapeDtypeStruct(r.shape, F32), jax.ShapeDtypeStruct(s0.shape[1:], F32)],
        compiler_params=_params(("arbitrary",)),
        name="wkv_sample",
    )(r, lw, kk, b, kf, v, s0)


def _swa_prompt_kernel(q_ref, kc_ref, kp_ref, vc_ref, vp_ref, sink_ref, o_ref, *, sub):
    blk = WINDOW
    first = pl.program_id(1) == 0
    k_all = jnp.concatenate([kp_ref[...], kc_ref[...]], axis=0)
    v_all = jnp.concatenate([vp_ref[...], vc_ref[...]], axis=0).astype(BF16)
    v_ext = jnp.concatenate([v_all, jnp.ones_like(v_all)], axis=1)
    k_swapped = pltpu.roll(k_all, HEAD_DIM, 1)
    lane = lax.broadcasted_iota(jnp.int32, (1, LANES), 1)
    low = lane < HEAD_DIM
    k_var = {(0, 0): jnp.where(low, k_all, 0.0), (0, 1): jnp.where(low, 0.0, k_swapped),
             (1, 1): jnp.where(low, 0.0, k_all), (1, 0): jnp.where(low, k_swapped, 0.0)}
    k_var = {key: val.astype(BF16) for key, val in k_var.items()}
    order = [j * ATT_GROUP + g for j in range(ATT_KV_HEADS) for half in range(2)
             for g in range(ATT_GROUP) if g % 2 == half]
    n_rows = ATT_Q_HEADS * blk
    qi = lax.broadcasted_iota(jnp.int32, (n_rows, 2 * blk), 0) % blk
    kj = lax.broadcasted_iota(jnp.int32, (n_rows, 2 * blk), 1)
    diff = blk + qi - kj
    band = (diff >= 0) & (diff < WINDOW)
    row_head = lax.broadcasted_iota(jnp.int32, (n_rows, 1), 0) // blk
    sink = jnp.zeros((n_rows, 1), F32)
    for idx, head in enumerate(order):
        sink = jnp.where(row_head == idx, sink_ref[0:1, head:head + 1], sink)
    scale = HEAD_DIM ** -0.5
    for sb in range(sub):
        qs = slice(sb * blk, (sb + 1) * blk)
        ks = slice(sb * blk, (sb + 2) * blk)
        parts = []
        for j in range(ATT_KV_HEADS):
            for half in range(2):
                heads = [j * ATT_GROUP + g for g in range(ATT_GROUP) if g % 2 == half]
                q_cat = jnp.concatenate([q_ref[qs, (h // 2) * LANES:(h // 2 + 1) * LANES] for h in heads], axis=0)
                parts.append(_mm_nt(q_cat * scale, k_var[(j, half)][ks]))
        s = jnp.concatenate(parts, axis=0)
        ok = band & jnp.logical_or(kj >= blk, jnp.logical_not(first)) if sb == 0 else band
        s = jnp.where(ok, s, -jnp.inf)
        m = jnp.maximum(jnp.max(s, axis=-1, keepdims=True), sink)
        e = jnp.exp(s - m).astype(BF16)
        ev = jnp.dot(e, v_ext[ks], preferred_element_type=F32)
        o = ev[:, 0:LANES] / (ev[:, LANES:LANES + 1] + jnp.exp(sink - m))
        by_head = {}
        for idx, head in enumerate(order):
            o_h = o[idx * blk:(idx + 1) * blk]
            by_head[head] = pltpu.roll(o_h, HEAD_DIM, 1) if head // ATT_GROUP != head % 2 else o_h
        for grp in range(ATT_DIM // LANES):
            o_ref[qs, grp * LANES:(grp + 1) * LANES] = jnp.where(low, by_head[2 * grp],
                                                                 by_head[2 * grp + 1]).astype(o_ref.dtype)


def _swa_prompt(q, k, v, P, batch, sub):
    n = q.shape[0]
    rows = sub * WINDOW
    nb = n // batch // rows
    cur = lambda b, i: (b * nb + i, 0)
    prev = lambda b, i: ((b * nb + i) * sub - jnp.minimum(i, 1), 0)
    return pl.pallas_call(
        functools.partial(_swa_prompt_kernel, sub=sub),
        grid=(batch, nb),
        in_specs=[pl.BlockSpec((rows, ATT_DIM), cur),
                  pl.BlockSpec((rows, ATT_KV_DIM), cur), pl.BlockSpec((WINDOW, ATT_KV_DIM), prev),
                  pl.BlockSpec((rows, ATT_KV_DIM), cur), pl.BlockSpec((WINDOW, ATT_KV_DIM), prev),
                  _layer_spec((1, ATT_Q_HEADS), P['layer'])],
        out_specs=pl.BlockSpec((rows, ATT_DIM), cur),
        out_shape=jax.ShapeDtypeStruct((n, ATT_DIM), BF16),
        compiler_params=_params(("arbitrary", "arbitrary")),
        name="swa_prompt",
    )(q, k, k, v, v, P['sinks'])


def _swa_sample_kernel(q_ref, kc_ref, vc_ref, kn_ref, vn_ref, sink_ref, o_ref, *, steps):
    q = q_ref[...].astype(BF16)
    rows = q.shape[1]
    nk = kc_ref.shape[1]
    nn = kn_ref.shape[1]
    bdot = lambda a, b, dims: lax.dot_general(a, b, (dims, ((0,), (0,))), preferred_element_type=F32)
    s_c = bdot(q, kc_ref[...].astype(BF16), ((2,), (2,))) * (HEAD_DIM ** -0.5)
    s_n = bdot(q, kn_ref[...].astype(BF16), ((2,), (2,))) * (HEAD_DIM ** -0.5)
    t_c = lax.broadcasted_iota(jnp.int32, (1, rows, nk), 1) % steps
    w_c = lax.broadcasted_iota(jnp.int32, (1, rows, nk), 2)
    s_c = jnp.where(w_c > t_c, s_c, -jnp.inf)
    t_n = lax.broadcasted_iota(jnp.int32, (1, rows, nn), 1) % steps
    w_n = lax.broadcasted_iota(jnp.int32, (1, rows, nn), 2)
    s_n = jnp.where(w_n <= t_n, s_n, -jnp.inf)
    sink = sink_ref[...][None, :, 0:1]
    m = jnp.maximum(jnp.maximum(jnp.max(s_c, axis=-1, keepdims=True), jnp.max(s_n, axis=-1, keepdims=True)), sink)
    e_c = jnp.exp(s_c - m)
    e_n = jnp.exp(s_n - m)
    den = jnp.sum(e_c, axis=-1, keepdims=True) + jnp.sum(e_n, axis=-1, keepdims=True) + jnp.exp(sink - m)
    o = (bdot((e_c / den).astype(BF16), vc_ref[...].astype(BF16), ((2,), (1,)))
         + bdot((e_n / den).astype(BF16), vn_ref[...].astype(BF16), ((2,), (1,))))
    o_ref[...] = o


def _swa_sample(q_st, k_cache, v_cache, k_new, v_new, sink_rows, layer, steps, bt):
    nb, rows, _ = q_st.shape
    nn = k_new.shape[1]
    blk = lambda r: pl.BlockSpec((bt, r, LANES), lambda i: (i, 0, 0))
    cache = pl.BlockSpec((None, bt, WINDOW, LANES), lambda i: (layer, i, 0, 0))
    return pl.pallas_call(
        functools.partial(_swa_sample_kernel, steps=steps),
        grid=(nb // bt,),
        in_specs=[blk(rows), cache, cache, blk(nn), blk(nn), _layer_spec((rows, LANES), layer)],
        out_specs=blk(rows),
        out_shape=jax.ShapeDtypeStruct((nb, rows, LANES), F32),
        compiler_params=_params(("arbitrary",)),
        name="swa_sample",
    )(q_st, k_cache, v_cache, k_new, v_new, sink_rows)


def _gelu(x):
    return 0.5 * x * (1.0 + lax.erf(x * (2.0 ** -0.5)))


def _back_kernel(*refs, final, sample, nsub):
    x_ref, y_ref, bonus_ref, g_ref, oa_ref, gr_ref, ga_ref = refs[:7]
    at = 7
    if sample:
        cp_ref = refs[at]
        at += 1
    (lng_ref, lnb_ref, ones_ref, wa_ref, wb_ref, wo_ref, gn_ref, wi_ref, cw_ref, cb_ref, wd_ref,
     gf_ref) = refs[at:at + 12]
    o_ref, ct_ref, carry = refs[at + 12:at + 15]
    blocks, rb = _row_blocks(x_ref.shape[0], nsub)
    ones_bd = ones_ref[...]

    if sample:
        @pl.when(pl.program_id(0) == 0)
        def _():
            carry[...] = cp_ref[...]
    else:
        @pl.when(pl.program_id(1) == 0)
        def _():
            carry[...] = jnp.zeros_like(carry)

    row = lax.broadcasted_iota(jnp.int32, (rb, 1), 0)
    xs, cs, ups, acts = {}, {}, {}, {}

    def mix(i):
        rs = blocks[i]
        y = y_ref[rs, :]
        mean = _head_sum(y, ones_bd) * (1.0 / HEAD_DIM)
        d = y - mean
        var = _head_sum(d * d, ones_bd) * (1.0 / HEAD_DIM)
        yn = d * lax.rsqrt(var + GN_EPS) * lng_ref[...] + lnb_ref[...]
        o_r = (yn + bonus_ref[rs, :]) * g_ref[rs, :]
        merged = gr_ref[rs, :] * _mm(o_r, wa_ref[...]) + ga_ref[rs, :] * _mm(oa_ref[rs, :], wb_ref[...])
        xs[i] = x_ref[rs, :] + _mm(merged, wo_ref[...])

    def up_project(i):
        h = _rmsnorm(xs[i], gn_ref[...]).astype(BF16)
        cs[i] = jnp.dot(h, wi_ref[:, 0:D_FF], preferred_element_type=F32)
        ups[i] = jnp.dot(h, wi_ref[:, D_FF:2 * D_FF], preferred_element_type=F32)

    def conv_gate(i):
        c = cs[i]
        if sample:
            c2, c1 = carry[0], carry[1]
            carry[0] = c1
            carry[1] = c
            ct_ref[0] = c
        else:
            tail = carry[...] if i == 0 else cs[i - 1][rb - SUBLANES:, :]
            last = tail[SUBLANES - 1:SUBLANES, :]
            c1 = jnp.where(row == 0, last, pltpu.roll(c, 1, 0))
            c2 = jnp.where(row == 0, tail[SUBLANES - 2:SUBLANES - 1, :],
                           jnp.where(row == 1, last, pltpu.roll(c, 2, 0)))
            if i == nsub - 1:
                carry[...] = c[rb - SUBLANES:, :]
                ct_ref[0] = c[rb - SUBLANES:, :]
        conv = cb_ref[...] + c2 * cw_ref[0:1, :] + c1 * cw_ref[1:2, :] + c * cw_ref[2:3, :]
        acts[i] = (_gelu(conv) * ups[i]).astype(BF16)

    def down_project(i):
        out = xs[i] + jnp.dot(acts[i], wd_ref[...], preferred_element_type=F32)
        o_ref[blocks[i], :] = _rmsnorm(out, gf_ref[...]) if final else out

    _staggered([mix, up_project, conv_gate, down_project], nsub)


def _back_weight_specs(layer):
    return [_layer_spec((1, RWKV_DIM), layer), _layer_spec((1, RWKV_DIM), layer), _const_spec((RWKV_DIM, RWKV_DIM)),
            _layer_spec((RWKV_DIM, D_MODEL), layer), _layer_spec((ATT_DIM, D_MODEL), layer),
            _layer_spec((D_MODEL, D_MODEL), layer),
            _layer_spec((1, D_MODEL), layer), _layer_spec((D_MODEL, 2 * D_FF), layer),
            _layer_spec((CONV_W, D_FF), layer), _layer_spec((1, D_FF), layer), _layer_spec((D_FF, D_MODEL), layer),
            _const_spec((1, D_MODEL))]


def _back_weights(P, g_final):
    return [P['ln_g'], P['ln_b'], P['ones_bd'], P['w_br_rwkv'], P['w_br_attn'], P['w_out'],
            P['norm_ffn_g'], P['ffn_w_in'], P['ffn_conv_w'], P['ffn_conv_b'], P['ffn_w_down'], g_final]


def _back_prompt(x, y, bonus, g, o_a, gr, ga, P, g_final, final, batch, tm):
    n = x.shape[0]
    nt = n // batch // tm
    row = lambda b, t: (b * nt + t, 0)
    wide = pl.BlockSpec((tm, D_MODEL), row)
    half = pl.BlockSpec((tm, RWKV_DIM), row)
    return pl.pallas_call(
        functools.partial(_back_kernel, final=final, sample=False, nsub=tm // 128),
        grid=(batch, nt),
        in_specs=[wide, half, half, half, half, wide, wide] + _back_weight_specs(P['layer']),
        out_specs=[wide, pl.BlockSpec((1, SUBLANES, D_FF), lambda b, t: (b, 0, 0))],
        out_shape=[jax.ShapeDtypeStruct((n, D_MODEL), F32), jax.ShapeDtypeStruct((batch, SUBLANES, D_FF), F32)],
        scratch_shapes=[pltpu.VMEM((SUBLANES, D_FF), F32)],
        compiler_params=_params(("arbitrary", "arbitrary")),
        name="back_prompt",
    )(x, y, bonus, g, o_a, gr, ga, *_back_weights(P, g_final))


def _back_sample(x, y, bonus, g, o_a, gr, ga, conv_prev, P, g_final, final):
    n = x.shape[0]
    nb = conv_prev.shape[2]
    steps = n // nb
    taps = CONV_W - 1
    row = lambda t: (t, 0)
    wide = pl.BlockSpec((nb, D_MODEL), row)
    half = pl.BlockSpec((nb, RWKV_DIM), row)
    return pl.pallas_call(
        functools.partial(_back_kernel, final=final, sample=True, nsub=1),
        grid=(steps,),
        in_specs=[wide, half, half, half, half, wide, wide, _layer_spec((taps, nb, D_FF), P['layer'])]
        + _back_weight_specs(P['layer']),
        out_specs=[wide, pl.BlockSpec((1, nb, D_FF), lambda t: (jnp.maximum(t - (steps - taps), 0), 0, 0))],
        out_shape=[jax.ShapeDtypeStruct((n, D_MODEL), F32), jax.ShapeDtypeStruct((taps, nb, D_FF), F32)],
        scratch_shapes=[pltpu.VMEM((taps, nb, D_FF), F32)],
        compiler_params=_params(("arbitrary",)),
        name="back_sample",
    )(x, y, bonus, g, o_a, gr, ga, conv_prev, *_back_weights(P, g_final))


def _rope_tables(pos):
    inv = ROPE_THETA ** (-jnp.arange(0, HEAD_DIM, 2, dtype=F32) / HEAD_DIM)
    ang = pos.astype(F32)[:, None] * inv[None, :]
    cos, sin = jnp.cos(ang), jnp.sin(ang)
    reps = LANES // HEAD_DIM
    return jnp.tile(jnp.concatenate([cos, cos], axis=1), (1, reps)), jnp.tile(jnp.concatenate([-sin, sin], axis=1),
                                                                             (1, reps))


def _stacked_params(norm_mix_g, w_in, rwkv_mu, rwkv_w0, rwkv_w2, rwkv_a0, rwkv_a2, rwkv_g2, rwkv_k_k, rwkv_k_a,
                    rwkv_r_k, rwkv_ln_g, rwkv_ln_b, attn_sinks, w_br_rwkv, w_br_attn, w_out, norm_ffn_g, ffn_w_in,
                    ffn_conv_w, ffn_conv_b, ffn_w_down, steps):
    d = RWKV_DIM
    depth = w_in.shape[0]
    head = jnp.arange(d) // HEAD_DIM
    row = lambda t: t.reshape(depth, 1, -1)
    sink_rows = jnp.broadcast_to(jnp.repeat(attn_sinks, steps, axis=1)[:, :, None],
                                 (depth, ATT_Q_HEADS * steps, LANES))
    return {
        'norm_mix_g': row(norm_mix_g), 'w_in': w_in.astype(BF16),
        'mu': row(rwkv_mu), 'w0': row(rwkv_w0),
        'w2p': jnp.pad(rwkv_w2, ((0, 0), (0, D_AAA_LORA), (0, 0))).astype(BF16),
        'a0': row(rwkv_a0),
        'a2p': jnp.pad(rwkv_a2, ((0, 0), (D_DECAY_LORA, 0), (0, 0))).astype(BF16),
        'g2': rwkv_g2.astype(BF16), 'k_k': row(rwkv_k_k), 'k_a': row(rwkv_k_a),
        'r_k': row(rwkv_r_k), 'ln_g': row(rwkv_ln_g), 'ln_b': row(rwkv_ln_b),
        'ones_bd': (head[:, None] == head[None, :]).astype(BF16),
        'sinks': row(attn_sinks), 'sink_rows': sink_rows,
        'w_br_rwkv': w_br_rwkv.astype(BF16), 'w_br_attn': w_br_attn.astype(BF16),
        'w_out': w_out.astype(BF16), 'norm_ffn_g': row(norm_ffn_g),
        'ffn_w_in': ffn_w_in.astype(BF16), 'ffn_conv_w': ffn_conv_w, 'ffn_conv_b': row(ffn_conv_b),
        'ffn_w_down': ffn_w_down.astype(BF16),
    }


def _prompt_layer(x, P, tables, g_final, final, batch, seq):
    tm = min(512, seq)
    q, k, v, gr, ga, z_tail, r, lw, kk, b, kf, vv, g, bonus = _front_prompt(x, P, tables[0], tables[1], batch, tm)
    y, s_t = _wkv_prompt(r, lw, kk, b, kf, vv, batch, min(256, seq))
    o_a = _swa_prompt(q, k, v, P, batch, min(2, seq // WINDOW))
    x2, c_tail = _back_prompt(x, y, bonus, g, o_a, gr, ga, P, g_final, final, batch, tm)
    last = lambda t: t.reshape(batch, seq, -1)[:, -WINDOW:].reshape(batch, WINDOW, ATT_KV_HEADS, HEAD_DIM)
    state = (z_tail[:, SUBLANES - 1], s_t, last(k), last(v), c_tail[:, SUBLANES - (CONV_W - 1):])
    return x2, state


def _sample_layer(x, P, tables, g_final, final, shift_prev, wkv0, k_cache, v_cache, conv_prev, nb, steps):
    n = nb * steps
    layer = P['layer']
    q, k, v, gr, ga, z_tail, r, lw, kk, b, kf, vv, g, bonus = _front_sample(x, shift_prev, P, tables[0], tables[1])
    to_lanes = lambda t: t.reshape(steps, nb, RWKV_HEADS, HEAD_DIM).transpose(0, 2, 3, 1)
    y_l, s_l = _wkv_sample(*(to_lanes(t) for t in (r, lw, kk, b, kf)), to_lanes(vv).astype(F32), wkv0, layer, steps)
    y = y_l.transpose(0, 3, 1, 2).reshape(n, RWKV_DIM)

    rows = ATT_Q_HEADS * steps
    q5 = q.reshape(steps, nb, ATT_KV_HEADS, ATT_GROUP, HEAD_DIM).transpose(1, 2, 3, 0, 4)
    sel = jnp.eye(ATT_KV_HEADS, dtype=q.dtype)
    q_st = (q5[:, :, :, :, None, :] * sel[None, :, None, None, :, None]).reshape(nb, rows, ATT_KV_DIM)
    pad = ((0, 0), (0, SUBLANES - steps), (0, 0))
    k_new = k.reshape(steps, nb, ATT_KV_DIM).transpose(1, 0, 2)
    v_new = v.reshape(steps, nb, ATT_KV_DIM).transpose(1, 0, 2)
    o_st = _swa_sample(q_st, k_cache, v_cache, jnp.pad(k_new, pad), jnp.pad(v_new, pad), P['sink_rows'], layer,
                       steps, 16)
    o5 = o_st.reshape(nb, ATT_KV_HEADS, ATT_GROUP, steps, ATT_KV_HEADS, HEAD_DIM)
    o_a = jnp.stack([o5[:, j, :, :, j, :] for j in range(ATT_KV_HEADS)], axis=0)
    o_a = o_a.transpose(3, 1, 0, 2, 4).reshape(n, ATT_DIM)

    x2, c_tail = _back_sample(x, y, bonus, g, o_a, gr, ga, conv_prev, P, g_final, final)
    state = (z_tail, s_l, k_new, v_new, c_tail)
    return x2, state


def kernel(x_prompt, x_sample, state_rwkv_shift, state_rwkv_wkv, cache_swa_k, cache_swa_v, state_ffn_conv, norm_mix_g, w_in, rwkv_mu, rwkv_w0, rwkv_w2, rwkv_a0, rwkv_a2, rwkv_g2, rwkv_k_k, rwkv_k_a, rwkv_r_k, rwkv_ln_g, rwkv_ln_b, attn_sinks, w_br_rwkv, w_br_attn, w_out, norm_ffn_g, ffn_w_in, ffn_conv_w, ffn_conv_b, ffn_w_down, norm_final_g):
    bp, tp, _ = x_prompt.shape
    nb, steps, _ = x_sample.shape
    depth = w_in.shape[0]
    tab_p = _rope_tables(jnp.arange(tp, dtype=jnp.int32))
    tab_s = _rope_tables(PAST_LEN + jnp.repeat(jnp.arange(steps, dtype=jnp.int32), nb))
    g_final = norm_final_g[None]
    params = _stacked_params(norm_mix_g, w_in, rwkv_mu, rwkv_w0, rwkv_w2, rwkv_a0, rwkv_a2, rwkv_g2, rwkv_k_k,
                             rwkv_k_a, rwkv_r_k, rwkv_ln_g, rwkv_ln_b, attn_sinks, w_br_rwkv, w_br_attn, w_out,
                             norm_ffn_g, ffn_w_in, ffn_conv_w, ffn_conv_b, ffn_w_down, steps)
    wkv0 = state_rwkv_wkv.transpose(0, 2, 3, 4, 1)
    k_cache = cache_swa_k.reshape(depth, nb, WINDOW, ATT_KV_DIM)
    v_cache = cache_swa_v.reshape(depth, nb, WINDOW, ATT_KV_DIM)
    conv_prev = state_ffn_conv.transpose(0, 2, 1, 3)
    xp = x_prompt.reshape(bp * tp, D_MODEL)
    xs = x_sample.transpose(1, 0, 2).reshape(steps * nb, D_MODEL)
    outs_p, outs_s = [], []
    for l in range(depth):
        P = dict(params, layer=l)
        final = l == depth - 1
        xp, sp = _prompt_layer(xp, P, tab_p, g_final, final, bp, tp)
        xs, ss = _sample_layer(xs, P, tab_s, g_final, final, state_rwkv_shift, wkv0, k_cache, v_cache, conv_prev,
                               nb, steps)
        outs_p.append(sp)
        outs_s.append(ss)
    y_prompt = xp.reshape(bp, tp, D_MODEL)
    y_sample = xs.reshape(steps, nb, D_MODEL).transpose(1, 0, 2)
    stack = lambda outs, i: jnp.stack([o[i] for o in outs])
    kv_shape = (depth, nb, steps, ATT_KV_HEADS, HEAD_DIM)
    drop_oldest = [(0, 0, 0), (0, 0, 0), (-steps, steps, 0), (0, 0, 0), (0, 0, 0)]
    newest = (0, 0, WINDOW - steps, 0, 0)

    def slide(cache, new_rows):
        return lax.dynamic_update_slice(lax.pad(cache, jnp.zeros((), cache.dtype), drop_oldest),
                                        new_rows.reshape(kv_shape), newest)

    s_k = slide(cache_swa_k, stack(outs_s, 2))
    s_v = slide(cache_swa_v, stack(outs_s, 3))
    s_wkv = stack(outs_s, 1).transpose(0, 4, 1, 2, 3)
    s_conv = stack(outs_s, 4).transpose(0, 2, 1, 3)
    return (y_prompt, y_sample, *(stack(outs_p, i) for i in range(5)),
            stack(outs_s, 0), s_wkv, s_k, s_v, s_conv)
```

```python
import functools

import jax
import jax.numpy as jnp
from jax import lax
from jax.experimental import pallas as pl
from jax.experimental.pallas import tpu as pltpu

F32 = jnp.float32
BF16 = jnp.bfloat16

D_MODEL = 1024
HEAD_DIM = 64
RWKV_HEADS = 8
RWKV_DIM = RWKV_HEADS * HEAD_DIM
D_DECAY_LORA = 64
D_AAA_LORA = 64
D_GATE_LORA = 128
RWKV_PROJ = 3 * RWKV_DIM + D_DECAY_LORA + D_AAA_LORA + D_GATE_LORA
ATT_Q_HEADS = 8
ATT_KV_HEADS = 2
ATT_GROUP = ATT_Q_HEADS // ATT_KV_HEADS
ATT_DIM = ATT_Q_HEADS * HEAD_DIM
ATT_KV_DIM = ATT_KV_HEADS * HEAD_DIM
WINDOW = 128
IN_PROJ = RWKV_PROJ + ATT_DIM + 2 * ATT_KV_DIM + 2 * D_MODEL
D_FF = 2816
CONV_W = 3
ROPE_THETA = 10000.0
RMS_EPS = 1e-6
GN_EPS = 64e-5
PAST_LEN = 16384

LANES = 128
SUBLANES = 8
MXU_DIM = 256
CHUNK = 64
HEADS_PER_GROUP = MXU_DIM // HEAD_DIM
VMEM_LIMIT = 56 * 1024 * 1024


def _params(sem):
    return pltpu.CompilerParams(dimension_semantics=sem, vmem_limit_bytes=VMEM_LIMIT)


def _const_spec(shape):
    zeros = (0,) * len(shape)
    return pl.BlockSpec(shape, lambda *_: zeros, pipeline_mode=pl.Buffered(1))


def _layer_spec(shape, layer):
    index = (layer,) + (0,) * len(shape)
    return pl.BlockSpec((None,) + tuple(shape), lambda *_: index, pipeline_mode=pl.Buffered(1))


def _mm(a, b):
    return jnp.dot(a.astype(BF16), b.astype(BF16), preferred_element_type=F32)


def _mm_nt(a, b):
    return lax.dot_general(a.astype(BF16), b.astype(BF16), (((1,), (1,)), ((), ())),
                           preferred_element_type=F32)


def _mm_f32(a, b):
    return jnp.dot(a, b, precision=lax.Precision.HIGHEST, preferred_element_type=F32)


def _mm_tn(a, b):
    return lax.dot_general(a.astype(BF16), b.astype(BF16), (((0,), (0,)), ((), ())),
                           preferred_element_type=F32)


def _head_sum(x, ones_bd):
    return jnp.dot(x.astype(BF16), ones_bd, preferred_element_type=F32)


def _rmsnorm(x, g):
    return x * lax.rsqrt(jnp.mean(x * x, axis=-1, keepdims=True) + RMS_EPS) * g


def _rope(x, cos, sin_signed, first_half):
    partner = jnp.where(first_half, pltpu.roll(x, LANES - HEAD_DIM // 2, 1), pltpu.roll(x, HEAD_DIM // 2, 1))
    return x * cos + partner * sin_signed


def _prep_math(z, zp, mu, w0, w2p, a0, a2p, g2, k_k, k_a, r_k, ones_bd, outs, rs):
    r_ref, lw_ref, kk_ref, b_ref, kf_ref, v_ref, g_ref, bonus_ref = outs
    zs = z + mu * (zp - z)
    d = RWKV_DIM
    r, k, v = zs[:, 0:d], zs[:, d:2 * d], zs[:, 2 * d:3 * d]
    zwa = zs[:, 3 * d:3 * d + LANES]
    zg = zs[:, 3 * d + LANES:3 * d + 2 * LANES]
    w_pre = w0 + _mm(jnp.tanh(zwa), w2p)
    neg = -w_pre
    softplus = jnp.maximum(neg, 0.0) + jnp.log1p(jnp.exp(-jnp.abs(neg)))
    w_log = -softplus - 0.5
    lw_ref[rs, :] = -jnp.exp(w_log)
    a = jax.nn.sigmoid(a0 + _mm(zwa, a2p))
    g_ref[rs, :] = _mm(jax.nn.sigmoid(zg), g2).astype(g_ref.dtype)
    kk0 = k * k_k
    norm = jnp.sqrt(_head_sum(kk0 * kk0, ones_bd))
    kk = kk0 / jnp.maximum(norm, 1e-12)
    kf = k * (1.0 + (a - 1.0) * k_a)
    r_ref[rs, :] = r.astype(r_ref.dtype)
    kk_ref[rs, :] = kk.astype(kk_ref.dtype)
    b_ref[rs, :] = (kk * a).astype(b_ref.dtype)
    kf_ref[rs, :] = kf.astype(kf_ref.dtype)
    v_ref[rs, :] = v.astype(v_ref.dtype)
    bonus_ref[rs, :] = (_head_sum(r * kf * r_k, ones_bd) * v).astype(bonus_ref.dtype)


N_PREP_CONSTS = 10
N_FRONT_OUTS = 14


def _row_blocks(n, nsub):
    rb = n // nsub
    return [slice(i * rb, (i + 1) * rb) for i in range(nsub)], rb


def _staggered(stages, nsub):
    for step in range(nsub + len(stages) - 1):
        for i in range(nsub):
            if 0 <= step - i < len(stages):
                stages[step - i](i)


def _front_kernel(*refs, sample, nsub):
    if sample:
        x_ref, sp_ref, g_ref, w_ref, cos_ref, sin_ref = refs[:6]
        rest = refs[6:]
    else:
        x_ref, g_ref, w_ref, cos_ref, sin_ref = refs[:5]
        rest = refs[5:]
    consts = [c[...] for c in rest[:N_PREP_CONSTS]]
    outs = rest[N_PREP_CONSTS:N_PREP_CONSTS + N_FRONT_OUTS]
    q_ref, k_ref, v_ref, gr_ref, ga_ref, zt_ref = outs[:6]
    blocks, rb = _row_blocks(x_ref.shape[0], nsub)
    c0, c1, c2, c3, c4 = (RWKV_PROJ, RWKV_PROJ + ATT_DIM, RWKV_PROJ + ATT_DIM + ATT_KV_DIM,
                          RWKV_PROJ + ATT_DIM + 2 * ATT_KV_DIM, RWKV_PROJ + ATT_DIM + 2 * ATT_KV_DIM + D_MODEL)
    if not sample:
        carry = rest[N_PREP_CONSTS + N_FRONT_OUTS]

        @pl.when(pl.program_id(1) == 0)
        def _():
            carry[...] = jnp.zeros_like(carry)

    row = lax.broadcasted_iota(jnp.int32, (rb, 1), 0)
    lane = lax.broadcasted_iota(jnp.int32, (1, LANES), 1)
    first_half = (lane % HEAD_DIM) < (HEAD_DIM // 2)
    hs, zs = {}, {}

    def project(i):
        hs[i] = _rmsnorm(x_ref[blocks[i], :], g_ref[...]).astype(BF16)
        zs[i] = jnp.dot(hs[i], w_ref[:, 0:c0], preferred_element_type=F32)

    def rwkv_operands(i):
        z = zs[i]
        if sample:
            zp = sp_ref[...] if i == 0 else zs[i - 1]
            if i == nsub - 1:
                zt_ref[...] = z
        else:
            tail = carry[...] if i == 0 else zs[i - 1][rb - SUBLANES:, :]
            zp = jnp.where(row == 0, tail[SUBLANES - 1:SUBLANES, :], pltpu.roll(z, 1, 0))
            if i == nsub - 1:
                carry[...] = z[rb - SUBLANES:, :]
                zt_ref[0] = z[rb - SUBLANES:, :]
        _prep_math(z, zp, *consts, outs[6:], blocks[i])

    def attention_operands(i):
        rs, h = blocks[i], hs[i]
        cos = cos_ref[rs, :]
        sin = sin_ref[rs, :]
        q = jnp.dot(h, w_ref[:, c0:c1], preferred_element_type=F32)
        for grp in range(ATT_DIM // LANES):
            sl = slice(grp * LANES, (grp + 1) * LANES)
            q_ref[rs, sl] = _rope(q[:, sl], cos, sin, first_half).astype(q_ref.dtype)
        k = jnp.dot(h, w_ref[:, c1:c2], preferred_element_type=F32)
        k_ref[rs, :] = _rope(k, cos, sin, first_half)
        v_ref[rs, :] = jnp.dot(h, w_ref[:, c2:c3], preferred_element_type=F32)
        gr_ref[rs, :] = jax.nn.sigmoid(jnp.dot(h, w_ref[:, c3:c4], preferred_element_type=F32)).astype(gr_ref.dtype)
        ga_ref[rs, :] = jax.nn.sigmoid(jnp.dot(h, w_ref[:, c4:IN_PROJ], preferred_element_type=F32)).astype(ga_ref.dtype)

    _staggered([project, rwkv_operands, attention_operands], nsub)


def _prep_consts(P):
    return [P['mu'], P['w0'], P['w2p'], P['a0'], P['a2p'], P['g2'], P['k_k'], P['k_a'], P['r_k'], P['ones_bd']]


def _prep_const_specs(layer):
    d = RWKV_DIM
    shapes = [(1, RWKV_PROJ), (1, d), (LANES, d), (1, d), (LANES, d), (D_GATE_LORA, d), (1, d), (1, d), (1, d)]
    return [_layer_spec(s, layer) for s in shapes] + [_const_spec((d, d))]


def _front_out_types(n, tail_shape):
    d = RWKV_DIM
    widths = [(ATT_DIM, BF16), (ATT_KV_DIM, F32), (ATT_KV_DIM, F32), (D_MODEL, BF16), (D_MODEL, BF16)]
    prep = [(d, BF16), (d, F32), (d, BF16), (d, BF16), (d, BF16), (d, BF16), (d, BF16), (d, BF16)]
    shapes = ([jax.ShapeDtypeStruct((n, w), t) for w, t in widths] + [jax.ShapeDtypeStruct(tail_shape, F32)]
              + [jax.ShapeDtypeStruct((n, w), t) for w, t in prep])
    return shapes, [w for w, _ in widths], [w for w, _ in prep]


def _front_prompt(x2d, P, cos_t, sin_t, batch, tm):
    n = x2d.shape[0]
    layer = P['layer']
    nt = n // batch // tm
    row = lambda b, t: (b * nt + t, 0)
    pos = lambda b, t: (t, 0)
    shapes, widths, prep = _front_out_types(n, (batch, SUBLANES, RWKV_PROJ))
    return pl.pallas_call(
        functools.partial(_front_kernel, sample=False, nsub=tm // 128),
        grid=(batch, nt),
        in_specs=[pl.BlockSpec((tm, D_MODEL), row), _layer_spec((1, D_MODEL), layer),
                  _layer_spec((D_MODEL, IN_PROJ), layer),
                  pl.BlockSpec((tm, LANES), pos), pl.BlockSpec((tm, LANES), pos)] + _prep_const_specs(layer),
        out_specs=([pl.BlockSpec((tm, w), row) for w in widths]
                   + [pl.BlockSpec((1, SUBLANES, RWKV_PROJ), lambda b, t: (b, 0, 0))]
                   + [pl.BlockSpec((tm, w), row) for w in prep]),
        out_shape=shapes,
        scratch_shapes=[pltpu.VMEM((SUBLANES, RWKV_PROJ), F32)],
        compiler_params=_params(("arbitrary", "arbitrary")),
        name="front_prompt",
    )(x2d, P['norm_mix_g'], P['w_in'], cos_t, sin_t, *_prep_consts(P))


def _front_sample(x2d, shift_prev, P, cos_t, sin_t):
    n = x2d.shape[0]
    layer = P['layer']
    nb = shift_prev.shape[1]
    shapes, _, _ = _front_out_types(n, (nb, RWKV_PROJ))
    return pl.pallas_call(
        functools.partial(_front_kernel, sample=True, nsub=n // nb),
        grid=(1,),
        in_specs=[_const_spec((n, D_MODEL)), _layer_spec((nb, RWKV_PROJ), layer), _layer_spec((1, D_MODEL), layer),
                  _layer_spec((D_MODEL, IN_PROJ), layer), _const_spec((n, LANES)), _const_spec((n, LANES))]
        + _prep_const_specs(layer),
        out_specs=[_const_spec(sh.shape) for sh in shapes],
        out_shape=shapes,
        compiler_params=_params(("arbitrary",)),
        name="front_sample",
    )(x2d, shift_prev, P['norm_mix_g'], P['w_in'], cos_t, sin_t, *_prep_consts(P))


def _wkv_prompt_kernel(r_ref, lw_ref, kk_ref, b_ref, kf_ref, v_ref, y_ref, st_ref, s_scr, *, chunks, together):
    C, G, M = CHUNK, HEADS_PER_GROUP, MXU_DIM
    nbatch = r_ref.shape[0]
    ngrp = RWKV_HEADS // G

    @pl.when(pl.program_id(0) == 0)
    def _():
        s_scr[...] = jnp.zeros_like(s_scr)

    ti = lax.broadcasted_iota(jnp.int32, (C, M), 0)
    si = lax.broadcasted_iota(jnp.int32, (C, M), 1) % C
    strict = ti > si
    incl = ti >= si
    eye = (ti == si).astype(F32)
    tri = (lax.broadcasted_iota(jnp.int32, (C, C), 0) >= lax.broadcasted_iota(jnp.int32, (C, C), 1)).astype(F32)
    lane_head = lax.broadcasted_iota(jnp.int32, (1, M), 1) // HEAD_DIM
    same_head = (lax.broadcasted_iota(jnp.int32, (M, M), 0) // HEAD_DIM
                 == lax.broadcasted_iota(jnp.int32, (M, M), 1) // HEAD_DIM)

    def stack(x):
        xb = x.astype(BF16)
        return jnp.concatenate([jnp.where(lane_head == h, xb, jnp.zeros_like(xb)) for h in range(G)], axis=0)

    def rows2(top, bottom):
        return jnp.concatenate([top.astype(BF16), bottom.astype(BF16)], axis=0)

    def load(rows):
        probs = []
        for bi in range(nbatch):
            lw = lw_ref[bi, rows, :]
            cum = _mm_f32(tri, lw)
            w_end = jnp.exp(cum[C - 1:C, :])
            e_neg = jnp.exp(-cum)
            a_hat = -kk_ref[bi, rows, :] * jnp.exp(cum - lw)
            r_hat = r_ref[bi, rows, :] * jnp.exp(cum)
            b_til = b_ref[bi, rows, :] * e_neg
            k_til = kf_ref[bi, rows, :] * e_neg
            v = v_ref[bi, rows, :]
            for grp in range(ngrp):
                ls = slice(grp * M, (grp + 1) * M)
                we = w_end[:, ls]
                probs.append(dict(
                    bi=bi, grp=grp, ls=ls, we=we, rows=rows, ar=rows2(a_hat[:, ls], r_hat[:, ls]),
                    b=stack(b_til[:, ls]), k=stack(k_til[:, ls]), v=v[:, ls], vs=stack(v[:, ls]),
                    bk=rows2(b_til[:, ls] * we, k_til[:, ls] * we)))
        return probs

    def state_free(probs):
        mb = [_mm_nt(p['ar'], p['b']) for p in probs]
        mk = [_mm_nt(p['ar'], p['k']) for p in probs]
        l_ab = [jnp.where(strict, x[0:C], 0.0) for x in mb]
        p_rb = [jnp.where(incl, x[C:2 * C], 0.0).astype(BF16) for x in mb]
        lp_k = [rows2(jnp.where(strict, x[0:C], 0.0), jnp.where(incl, x[C:2 * C], 0.0)) for x in mk]
        t_inv = [eye + x for x in l_ab]
        pw = [_mm(x, stack(x)) for x in l_ab]
        for _ in range(C.bit_length() - 3):
            both = [_mm(rows2(t, x), stack(x)) for t, x in zip(t_inv, pw)]
            t_inv = [t + x[0:C] for t, x in zip(t_inv, both)]
            pw = [x[C:2 * C] for x in both]
        t_inv = [t + _mm(t, stack(x)) for t, x in zip(t_inv, pw)]
        e0 = [_mm(x, p['vs']) for x, p in zip(lp_k, probs)]
        return t_inv, p_rb, e0

    def advance(probs, t_inv, p_rb, e0):
        s0 = [s_scr[p['bi'], p['grp']] for p in probs]
        d0 = [_mm_nt(p['ar'], x) for p, x in zip(probs, s0)]
        u = [_mm(t, stack(x[0:C] + y[0:C])) for t, x, y in zip(t_inv, d0, e0)]
        for p, x, y, w, q, st in zip(probs, d0, e0, u, p_rb, s0):
            y_ref[p['bi'], p['rows'], p['ls']] = x[C:2 * C] + y[C:2 * C] + _mm(q, stack(w))
            upd = _mm_tn(rows2(w, p['v']), p['bk'])
            s_scr[p['bi'], p['grp']] = st * p['we'] + jnp.where(same_head, upd, 0.0)

    def chunk_group(c, carry):
        chunk_probs = [load(pl.ds(pl.multiple_of((c * together + j) * C, C), C)) for j in range(together)]
        t_inv, p_rb, e0 = state_free([p for probs in chunk_probs for p in probs])
        n = len(chunk_probs[0])
        for j, probs in enumerate(chunk_probs):
            sl = slice(j * n, (j + 1) * n)
            advance(probs, t_inv[sl], p_rb[sl], e0[sl])
        return carry

    lax.fori_loop(0, chunks // together, chunk_group, 0)

    @pl.when(pl.program_id(0) == pl.num_programs(0) - 1)
    def _():
        st_ref[...] = s_scr[...]


def _wkv_prompt(r, lw, kk, b, kf, v, batch, tl):
    n = r.shape[0]
    seq = n // batch
    ngrp = RWKV_HEADS // HEADS_PER_GROUP
    blk = pl.BlockSpec((batch, tl, RWKV_DIM), lambda t: (0, t, 0))
    y, st = pl.pallas_call(
        functools.partial(_wkv_prompt_kernel, chunks=tl // CHUNK, together=2),
        grid=(seq // tl,),
        in_specs=[blk] * 6,
        out_specs=[blk, _const_spec((batch, ngrp, MXU_DIM, MXU_DIM))],
        out_shape=[jax.ShapeDtypeStruct((batch, seq, RWKV_DIM), F32),
                   jax.ShapeDtypeStruct((batch, ngrp, MXU_DIM, MXU_DIM), F32)],
        scratch_shapes=[pltpu.VMEM((batch, ngrp, MXU_DIM, MXU_DIM), F32)],
        compiler_params=_params(("arbitrary",)),
        name="wkv_prompt",
    )(*(t.reshape(batch, seq, RWKV_DIM) for t in (r, lw, kk, b, kf, v)))
    st = st.reshape(batch, ngrp, HEADS_PER_GROUP, HEAD_DIM, HEADS_PER_GROUP, HEAD_DIM)
    st = jnp.stack([st[:, :, h, :, h, :] for h in range(HEADS_PER_GROUP)], axis=2)
    st = st.reshape(batch, RWKV_HEADS, HEAD_DIM, HEAD_DIM)
    return y.reshape(n, RWKV_DIM), st


def _wkv_sample_kernel(r_ref, lw_ref, kk_ref, b_ref, kf_ref, v_ref, s_ref, y_ref, so_ref, *, steps):
    group = SUBLANES // 2

    def value_rows(gi, carry):
        vis = [gi * group + j for j in range(group)]
        ss = [s_ref[0, vi] for vi in vis]
        for t in range(steps):
            kk, w, b = kk_ref[t, 0].astype(F32), jnp.exp(lw_ref[t, 0]), b_ref[t, 0].astype(F32)
            kf, r = kf_ref[t, 0].astype(F32), r_ref[t, 0].astype(F32)
            sks = [jnp.sum(s * kk, axis=0, keepdims=True) for s in ss]
            ss = [s * w - sk * b + v_ref[t, 0, pl.ds(vi, 1), :] * kf for s, sk, vi in zip(ss, sks, vis)]
            for s, vi in zip(ss, vis):
                y_ref[t, 0, pl.ds(vi, 1), :] = jnp.sum(s * r, axis=0, keepdims=True)
        for s, vi in zip(ss, vis):
            so_ref[0, vi] = s
        return carry

    lax.fori_loop(0, HEAD_DIM // group, value_rows, 0)


def _wkv_sample(r, lw, kk, b, kf, v, s0, layer, steps):
    nb = r.shape[-1]
    vec = pl.BlockSpec((steps, 1, HEAD_DIM, nb), lambda h: (0, h, 0, 0))
    st = pl.BlockSpec((1, HEAD_DIM, HEAD_DIM, nb), lambda h: (h, 0, 0, 0))
    st_in = pl.BlockSpec((None, 1, HEAD_DIM, HEAD_DIM, nb), lambda h: (layer, h, 0, 0, 0))
    return pl.pallas_call(
        functools.partial(_wkv_sample_kernel, steps=steps),
        grid=(RWKV_HEADS,),
        in_specs=[vec] * 6 + [st_in],
        out_specs=[vec, st],
        out_shape=[jax.ShapeDtypeStruct(r.shape, F32), jax.ShapeDtypeStruct(s0.shape[1:], F32)],
        compiler_params=_params(("arbitrary",)),
        name="wkv_sample",
    )(r, lw, kk, b, kf, v, s0)


def _swa_prompt_kernel(q_ref, kc_ref, kp_ref, vc_ref, vp_ref, sink_ref, o_ref, *, sub):
    blk = WINDOW
    first = pl.program_id(1) == 0
    k_all = jnp.concatenate([kp_ref[...], kc_ref[...]], axis=0)
    v_all = jnp.concatenate([vp_ref[...], vc_ref[...]], axis=0).astype(BF16)
    v_ext = jnp.concatenate([v_all, jnp.ones_like(v_all)], axis=1)
    k_swapped = pltpu.roll(k_all, HEAD_DIM, 1)
    lane = lax.broadcasted_iota(jnp.int32, (1, LANES), 1)
    low = lane < HEAD_DIM
    k_var = {(0, 0): jnp.where(low, k_all, 0.0), (0, 1): jnp.where(low, 0.0, k_swapped),
             (1, 1): jnp.where(low, 0.0, k_all), (1, 0): jnp.where(low, k_swapped, 0.0)}
    k_var = {key: val.astype(BF16) for key, val in k_var.items()}
    order = [j * ATT_GROUP + g for j in range(ATT_KV_HEADS) for half in range(2)
             for g in range(ATT_GROUP) if g % 2 == half]
    n_rows = ATT_Q_HEADS * blk
    qi = lax.broadcasted_iota(jnp.int32, (n_rows, 2 * blk), 0) % blk
    kj = lax.broadcasted_iota(jnp.int32, (n_rows, 2 * blk), 1)
    diff = blk + qi - kj
    band = (diff >= 0) & (diff < WINDOW)
    row_head = lax.broadcasted_iota(jnp.int32, (n_rows, 1), 0) // blk
    sink = jnp.zeros((n_rows, 1), F32)
    for idx, head in enumerate(order):
        sink = jnp.where(row_head == idx, sink_ref[0:1, head:head + 1], sink)
    scale = HEAD_DIM ** -0.5
    scores, maxes, weights, sums = {}, {}, {}, {}

    def score(sb):
        qs = slice(sb * blk, (sb + 1) * blk)
        ks = slice(sb * blk, (sb + 2) * blk)
        parts = []
        for j in range(ATT_KV_HEADS):
            for half in range(2):
                heads = [j * ATT_GROUP + g for g in range(ATT_GROUP) if g % 2 == half]
                q_cat = jnp.concatenate([q_ref[qs, (h // 2) * LANES:(h // 2 + 1) * LANES] for h in heads], axis=0)
                parts.append(_mm_nt(q_cat * scale, k_var[(j, half)][ks]))
        s = jnp.concatenate(parts, axis=0)
        ok = band & jnp.logical_or(kj >= blk, jnp.logical_not(first)) if sb == 0 else band
        scores[sb] = jnp.where(ok, s, -jnp.inf)

    def exponentiate(sb):
        maxes[sb] = jnp.maximum(jnp.max(scores[sb], axis=-1, keepdims=True), sink)
        weights[sb] = jnp.exp(scores[sb] - maxes[sb]).astype(BF16)

    def weigh(sb):
        ks = slice(sb * blk, (sb + 2) * blk)
        sums[sb] = jnp.dot(weights[sb], v_ext[ks], preferred_element_type=F32)

    def emit(sb):
        qs = slice(sb * blk, (sb + 1) * blk)
        ev = sums[sb]
        o = ev[:, 0:LANES] / (ev[:, LANES:LANES + 1] + jnp.exp(sink - maxes[sb]))
        by_head = {}
        for idx, head in enumerate(order):
            o_h = o[idx * blk:(idx + 1) * blk]
            by_head[head] = pltpu.roll(o_h, HEAD_DIM, 1) if head // ATT_GROUP != head % 2 else o_h
        for grp in range(ATT_DIM // LANES):
            o_ref[qs, grp * LANES:(grp + 1) * LANES] = jnp.where(low, by_head[2 * grp],
                                                                 by_head[2 * grp + 1]).astype(o_ref.dtype)

    _staggered([score, exponentiate, weigh, emit], sub)


def _swa_prompt(q, k, v, P, batch, sub):
    n = q.shape[0]
    rows = sub * WINDOW
    nb = n // batch // rows
    cur = lambda b, i: (b * nb + i, 0)
    prev = lambda b, i: ((b * nb + i) * sub - jnp.minimum(i, 1), 0)
    return pl.pallas_call(
        functools.partial(_swa_prompt_kernel, sub=sub),
        grid=(batch, nb),
        in_specs=[pl.BlockSpec((rows, ATT_DIM), cur),
                  pl.BlockSpec((rows, ATT_KV_DIM), cur), pl.BlockSpec((WINDOW, ATT_KV_DIM), prev),
                  pl.BlockSpec((rows, ATT_KV_DIM), cur), pl.BlockSpec((WINDOW, ATT_KV_DIM), prev),
                  _layer_spec((1, ATT_Q_HEADS), P['layer'])],
        out_specs=pl.BlockSpec((rows, ATT_DIM), cur),
        out_shape=jax.ShapeDtypeStruct((n, ATT_DIM), BF16),
        compiler_params=_params(("arbitrary", "arbitrary")),
        name="swa_prompt",
    )(q, k, k, v, v, P['sinks'])


def _swa_sample_kernel(q_ref, kc_ref, vc_ref, kn_ref, vn_ref, sink_ref, o_ref, *, steps):
    q = q_ref[...].astype(BF16)
    rows = q.shape[1]
    nk = kc_ref.shape[1]
    nn = kn_ref.shape[1]
    bdot = lambda a, b, dims: lax.dot_general(a, b, (dims, ((0,), (0,))), preferred_element_type=F32)
    s_c = bdot(q, kc_ref[...].astype(BF16), ((2,), (2,))) * (HEAD_DIM ** -0.5)
    s_n = bdot(q, kn_ref[...].astype(BF16), ((2,), (2,))) * (HEAD_DIM ** -0.5)
    t_c = lax.broadcasted_iota(jnp.int32, (1, rows, nk), 1) % steps
    w_c = lax.broadcasted_iota(jnp.int32, (1, rows, nk), 2)
    s_c = jnp.where(w_c > t_c, s_c, -jnp.inf)
    t_n = lax.broadcasted_iota(jnp.int32, (1, rows, nn), 1) % steps
    w_n = lax.broadcasted_iota(jnp.int32, (1, rows, nn), 2)
    s_n = jnp.where(w_n <= t_n, s_n, -jnp.inf)
    sink = sink_ref[...][None, :, 0:1]
    m = jnp.maximum(jnp.maximum(jnp.max(s_c, axis=-1, keepdims=True), jnp.max(s_n, axis=-1, keepdims=True)), sink)
    e_c = jnp.exp(s_c - m)
    e_n = jnp.exp(s_n - m)
    den = jnp.sum(e_c, axis=-1, keepdims=True) + jnp.sum(e_n, axis=-1, keepdims=True) + jnp.exp(sink - m)
    o = (bdot((e_c / den).astype(BF16), vc_ref[...].astype(BF16), ((2,), (1,)))
         + bdot((e_n / den).astype(BF16), vn_ref[...].astype(BF16), ((2,), (1,))))
    o_ref[...] = o


def _swa_sample(q_st, k_cache, v_cache, k_new, v_new, sink_rows, layer, steps, bt):
    nb, rows, _ = q_st.shape
    nn = k_new.shape[1]
    blk = lambda r: pl.BlockSpec((bt, r, LANES), lambda i: (i, 0, 0))
    cache = pl.BlockSpec((None, bt, WINDOW, LANES), lambda i: (layer, i, 0, 0))
    return pl.pallas_call(
        functools.partial(_swa_sample_kernel, steps=steps),
        grid=(nb // bt,),
        in_specs=[blk(rows), cache, cache, blk(nn), blk(nn), _layer_spec((rows, LANES), layer)],
        out_specs=blk(rows),
        out_shape=jax.ShapeDtypeStruct((nb, rows, LANES), F32),
        compiler_params=_params(("arbitrary",)),
        name="swa_sample",
    )(q_st, k_cache, v_cache, k_new, v_new, sink_rows)


def _gelu(x):
    return 0.5 * x * (1.0 + lax.erf(x * (2.0 ** -0.5)))


def _back_kernel(*refs, final, sample, nsub):
    x_ref, y_ref, bonus_ref, g_ref, oa_ref, gr_ref, ga_ref = refs[:7]
    at = 7
    if sample:
        cp_ref = refs[at]
        at += 1
    (lng_ref, lnb_ref, ones_ref, wa_ref, wb_ref, wo_ref, gn_ref, wi_ref, cw_ref, cb_ref, wd_ref,
     gf_ref) = refs[at:at + 12]
    o_ref, ct_ref, carry = refs[at + 12:at + 15]
    blocks, rb = _row_blocks(x_ref.shape[0], nsub)
    ones_bd = ones_ref[...]

    if sample:
        @pl.when(pl.program_id(0) == 0)
        def _():
            carry[...] = cp_ref[...]
    else:
        @pl.when(pl.program_id(1) == 0)
        def _():
            carry[...] = jnp.zeros_like(carry)

    row = lax.broadcasted_iota(jnp.int32, (rb, 1), 0)
    xs, cs, ups, acts = {}, {}, {}, {}

    def mix(i):
        rs = blocks[i]
        y = y_ref[rs, :]
        mean = _head_sum(y, ones_bd) * (1.0 / HEAD_DIM)
        d = y - mean
        var = _head_sum(d * d, ones_bd) * (1.0 / HEAD_DIM)
        yn = d * lax.rsqrt(var + GN_EPS) * lng_ref[...] + lnb_ref[...]
        o_r = (yn + bonus_ref[rs, :]) * g_ref[rs, :]
        merged = gr_ref[rs, :] * _mm(o_r, wa_ref[...]) + ga_ref[rs, :] * _mm(oa_ref[rs, :], wb_ref[...])
        xs[i] = x_ref[rs, :] + _mm(merged, wo_ref[...])

    def up_project(i):
        h = _rmsnorm(xs[i], gn_ref[...]).astype(BF16)
        cs[i] = jnp.dot(h, wi_ref[:, 0:D_FF], preferred_element_type=F32)
        ups[i] = jnp.dot(h, wi_ref[:, D_FF:2 * D_FF], preferred_element_type=F32)

    def conv_gate(i):
        c = cs[i]
        if sample:
            c2, c1 = carry[0], carry[1]
            carry[0] = c1
            carry[1] = c
            ct_ref[0] = c
        else:
            tail = carry[...] if i == 0 else cs[i - 1][rb - SUBLANES:, :]
            last = tail[SUBLANES - 1:SUBLANES, :]
            c1 = jnp.where(row == 0, last, pltpu.roll(c, 1, 0))
            c2 = jnp.where(row == 0, tail[SUBLANES - 2:SUBLANES - 1, :],
                           jnp.where(row == 1, last, pltpu.roll(c, 2, 0)))
            if i == nsub - 1:
                carry[...] = c[rb - SUBLANES:, :]
                ct_ref[0] = c[rb - SUBLANES:, :]
        conv = cb_ref[...] + c2 * cw_ref[0:1, :] + c1 * cw_ref[1:2, :] + c * cw_ref[2:3, :]
        acts[i] = (_gelu(conv) * ups[i]).astype(BF16)

    def down_project(i):
        out = xs[i] + jnp.dot(acts[i], wd_ref[...], preferred_element_type=F32)
        o_ref[blocks[i], :] = _rmsnorm(out, gf_ref[...]) if final else out

    _staggered([mix, up_project, conv_gate, down_project], nsub)


def _back_weight_specs(layer):
    return [_layer_spec((1, RWKV_DIM), layer), _layer_spec((1, RWKV_DIM), layer), _const_spec((RWKV_DIM, RWKV_DIM)),
            _layer_spec((RWKV_DIM, D_MODEL), layer), _layer_spec((ATT_DIM, D_MODEL), layer),
            _layer_spec((D_MODEL, D_MODEL), layer),
            _layer_spec((1, D_MODEL), layer), _layer_spec((D_MODEL, 2 * D_FF), layer),
            _layer_spec((CONV_W, D_FF), layer), _layer_spec((1, D_FF), layer), _layer_spec((D_FF, D_MODEL), layer),
            _const_spec((1, D_MODEL))]


def _back_weights(P, g_final):
    return [P['ln_g'], P['ln_b'], P['ones_bd'], P['w_br_rwkv'], P['w_br_attn'], P['w_out'],
            P['norm_ffn_g'], P['ffn_w_in'], P['ffn_conv_w'], P['ffn_conv_b'], P['ffn_w_down'], g_final]


def _back_prompt(x, y, bonus, g, o_a, gr, ga, P, g_final, final, batch, tm):
    n = x.shape[0]
    nt = n // batch // tm
    row = lambda b, t: (b * nt + t, 0)
    wide = pl.BlockSpec((tm, D_MODEL), row)
    half = pl.BlockSpec((tm, RWKV_DIM), row)
    return pl.pallas_call(
        functools.partial(_back_kernel, final=final, sample=False, nsub=tm // 128),
        grid=(batch, nt),
        in_specs=[wide, half, half, half, half, wide, wide] + _back_weight_specs(P['layer']),
        out_specs=[wide, pl.BlockSpec((1, SUBLANES, D_FF), lambda b, t: (b, 0, 0))],
        out_shape=[jax.ShapeDtypeStruct((n, D_MODEL), F32), jax.ShapeDtypeStruct((batch, SUBLANES, D_FF), F32)],
        scratch_shapes=[pltpu.VMEM((SUBLANES, D_FF), F32)],
        compiler_params=_params(("arbitrary", "arbitrary")),
        name="back_prompt",
    )(x, y, bonus, g, o_a, gr, ga, *_back_weights(P, g_final))


def _back_sample(x, y, bonus, g, o_a, gr, ga, conv_prev, P, g_final, final):
    n = x.shape[0]
    nb = conv_prev.shape[2]
    steps = n // nb
    taps = CONV_W - 1
    row = lambda t: (t, 0)
    wide = pl.BlockSpec((nb, D_MODEL), row)
    half = pl.BlockSpec((nb, RWKV_DIM), row)
    return pl.pallas_call(
        functools.partial(_back_kernel, final=final, sample=True, nsub=1),
        grid=(steps,),
        in_specs=[wide, half, half, half, half, wide, wide, _layer_spec((taps, nb, D_FF), P['layer'])]
        + _back_weight_specs(P['layer']),
        out_specs=[wide, pl.BlockSpec((1, nb, D_FF), lambda t: (jnp.maximum(t - (steps - taps), 0), 0, 0))],
        out_shape=[jax.ShapeDtypeStruct((n, D_MODEL), F32), jax.ShapeDtypeStruct((taps, nb, D_FF), F32)],
        scratch_shapes=[pltpu.VMEM((taps, nb, D_FF), F32)],
        compiler_params=_params(("arbitrary",)),
        name="back_sample",
    )(x, y, bonus, g, o_a, gr, ga, conv_prev, *_back_weights(P, g_final))


def _rope_tables(pos):
    inv = ROPE_THETA ** (-jnp.arange(0, HEAD_DIM, 2, dtype=F32) / HEAD_DIM)
    ang = pos.astype(F32)[:, None] * inv[None, :]
    cos, sin = jnp.cos(ang), jnp.sin(ang)
    reps = LANES // HEAD_DIM
    return jnp.tile(jnp.concatenate([cos, cos], axis=1), (1, reps)), jnp.tile(jnp.concatenate([-sin, sin], axis=1),
                                                                             (1, reps))


def _stacked_params(norm_mix_g, w_in, rwkv_mu, rwkv_w0, rwkv_w2, rwkv_a0, rwkv_a2, rwkv_g2, rwkv_k_k, rwkv_k_a,
                    rwkv_r_k, rwkv_ln_g, rwkv_ln_b, attn_sinks, w_br_rwkv, w_br_attn, w_out, norm_ffn_g, ffn_w_in,
                    ffn_conv_w, ffn_conv_b, ffn_w_down, steps):
    d = RWKV_DIM
    depth = w_in.shape[0]
    head = jnp.arange(d) // HEAD_DIM
    row = lambda t: t.reshape(depth, 1, -1)
    sink_rows = jnp.broadcast_to(jnp.repeat(attn_sinks, steps, axis=1)[:, :, None],
                                 (depth, ATT_Q_HEADS * steps, LANES))
    return {
        'norm_mix_g': row(norm_mix_g), 'w_in': w_in.astype(BF16),
        'mu': row(rwkv_mu), 'w0': row(rwkv_w0),
        'w2p': jnp.pad(rwkv_w2, ((0, 0), (0, D_AAA_LORA), (0, 0))).astype(BF16),
        'a0': row(rwkv_a0),
        'a2p': jnp.pad(rwkv_a2, ((0, 0), (D_DECAY_LORA, 0), (0, 0))).astype(BF16),
        'g2': rwkv_g2.astype(BF16), 'k_k': row(rwkv_k_k), 'k_a': row(rwkv_k_a),
        'r_k': row(rwkv_r_k), 'ln_g': row(rwkv_ln_g), 'ln_b': row(rwkv_ln_b),
        'ones_bd': (head[:, None] == head[None, :]).astype(BF16),
        'sinks': row(attn_sinks), 'sink_rows': sink_rows,
        'w_br_rwkv': w_br_rwkv.astype(BF16), 'w_br_attn': w_br_attn.astype(BF16),
        'w_out': w_out.astype(BF16), 'norm_ffn_g': row(norm_ffn_g),
        'ffn_w_in': ffn_w_in.astype(BF16), 'ffn_conv_w': ffn_conv_w, 'ffn_conv_b': row(ffn_conv_b),
        'ffn_w_down': ffn_w_down.astype(BF16),
    }


def _prompt_layer(x, P, tables, g_final, final, batch, seq):
    tm = min(512, seq)
    q, k, v, gr, ga, z_tail, r, lw, kk, b, kf, vv, g, bonus = _front_prompt(x, P, tables[0], tables[1], batch, tm)
    y, s_t = _wkv_prompt(r, lw, kk, b, kf, vv, batch, min(256, seq))
    o_a = _swa_prompt(q, k, v, P, batch, min(4, seq // WINDOW))
    x2, c_tail = _back_prompt(x, y, bonus, g, o_a, gr, ga, P, g_final, final, batch, tm)
    last = lambda t: t.reshape(batch, seq, -1)[:, -WINDOW:].reshape(batch, WINDOW, ATT_KV_HEADS, HEAD_DIM)
    state = (z_tail[:, SUBLANES - 1], s_t, last(k), last(v), c_tail[:, SUBLANES - (CONV_W - 1):])
    return x2, state


def _sample_layer(x, P, tables, g_final, final, shift_prev, wkv0, k_cache, v_cache, conv_prev, nb, steps):
    n = nb * steps
    layer = P['layer']
    q, k, v, gr, ga, z_tail, r, lw, kk, b, kf, vv, g, bonus = _front_sample(x, shift_prev, P, tables[0], tables[1])
    to_lanes = lambda t: t.reshape(steps, nb, RWKV_HEADS, HEAD_DIM).transpose(0, 2, 3, 1)
    y_l, s_l = _wkv_sample(*(to_lanes(t) for t in (r, lw, kk, b, kf)), to_lanes(vv).astype(F32), wkv0, layer, steps)
    y = y_l.transpose(0, 3, 1, 2).reshape(n, RWKV_DIM)

    rows = ATT_Q_HEADS * steps
    q5 = q.reshape(steps, nb, ATT_KV_HEADS, ATT_GROUP, HEAD_DIM).transpose(1, 2, 3, 0, 4)
    sel = jnp.eye(ATT_KV_HEADS, dtype=q.dtype)
    q_st = (q5[:, :, :, :, None, :] * sel[None, :, None, None, :, None]).reshape(nb, rows, ATT_KV_DIM)
    pad = ((0, 0), (0, SUBLANES - steps), (0, 0))
    k_new = k.reshape(steps, nb, ATT_KV_DIM).transpose(1, 0, 2)
    v_new = v.reshape(steps, nb, ATT_KV_DIM).transpose(1, 0, 2)
    o_st = _swa_sample(q_st, k_cache, v_cache, jnp.pad(k_new, pad), jnp.pad(v_new, pad), P['sink_rows'], layer,
                       steps, 16)
    o5 = o_st.reshape(nb, ATT_KV_HEADS, ATT_GROUP, steps, ATT_KV_HEADS, HEAD_DIM)
    o_a = jnp.stack([o5[:, j, :, :, j, :] for j in range(ATT_KV_HEADS)], axis=0)
    o_a = o_a.transpose(3, 1, 0, 2, 4).reshape(n, ATT_DIM)

    x2, c_tail = _back_sample(x, y, bonus, g, o_a, gr, ga, conv_prev, P, g_final, final)
    state = (z_tail, s_l, k_new, v_new, c_tail)
    return x2, state


def kernel(x_prompt, x_sample, state_rwkv_shift, state_rwkv_wkv, cache_swa_k, cache_swa_v, state_ffn_conv, norm_mix_g, w_in, rwkv_mu, rwkv_w0, rwkv_w2, rwkv_a0, rwkv_a2, rwkv_g2, rwkv_k_k, rwkv_k_a, rwkv_r_k, rwkv_ln_g, rwkv_ln_b, attn_sinks, w_br_rwkv, w_br_attn, w_out, norm_ffn_g, ffn_w_in, ffn_conv_w, ffn_conv_b, ffn_w_down, norm_final_g):
    bp, tp, _ = x_prompt.shape
    nb, steps, _ = x_sample.shape
    depth = w_in.shape[0]
    tab_p = _rope_tables(jnp.arange(tp, dtype=jnp.int32))
    tab_s = _rope_tables(PAST_LEN + jnp.repeat(jnp.arange(steps, dtype=jnp.int32), nb))
    g_final = norm_final_g[None]
    params = _stacked_params(norm_mix_g, w_in, rwkv_mu, rwkv_w0, rwkv_w2, rwkv_a0, rwkv_a2, rwkv_g2, rwkv_k_k,
                             rwkv_k_a, rwkv_r_k, rwkv_ln_g, rwkv_ln_b, attn_sinks, w_br_rwkv, w_br_attn, w_out,
                             norm_ffn_g, ffn_w_in, ffn_conv_w, ffn_conv_b, ffn_w_down, steps)
    wkv0 = state_rwkv_wkv.transpose(0, 2, 3, 4, 1)
    k_cache = cache_swa_k.reshape(depth, nb, WINDOW, ATT_KV_DIM)
    v_cache = cache_swa_v.reshape(depth, nb, WINDOW, ATT_KV_DIM)
    conv_prev = state_ffn_conv.transpose(0, 2, 1, 3)
    xp = x_prompt.reshape(bp * tp, D_MODEL)
    xs = x_sample.transpose(1, 0, 2).reshape(steps * nb, D_MODEL)
    outs_p, outs_s = [], []
    for l in range(depth):
        P = dict(params, layer=l)
        final = l == depth - 1
        xp, sp = _prompt_layer(xp, P, tab_p, g_final, final, bp, tp)
        xs, ss = _sample_layer(xs, P, tab_s, g_final, final, state_rwkv_shift, wkv0, k_cache, v_cache, conv_prev,
                               nb, steps)
        outs_p.append(sp)
        outs_s.append(ss)
    y_prompt = xp.reshape(bp, tp, D_MODEL)
    y_sample = xs.reshape(steps, nb, D_MODEL).transpose(1, 0, 2)
    stack = lambda outs, i: jnp.stack([o[i] for o in outs])
    kv_shape = (depth, nb, steps, ATT_KV_HEADS, HEAD_DIM)
    s_k = jnp.concatenate([cache_swa_k[:, :, steps:], stack(outs_s, 2).reshape(kv_shape)], axis=2)
    s_v = jnp.concatenate([cache_swa_v[:, :, steps:], stack(outs_s, 3).reshape(kv_shape)], axis=2)
    s_wkv = stack(outs_s, 1).transpose(0, 4, 1, 2, 3)
    s_conv = stack(outs_s, 4).transpose(0, 2, 1, 3)
    return (y_prompt, y_sample, *(stack(outs_p, i) for i in range(5)),
            stack(outs_s, 0), s_wkv, s_k, s_v, s_conv)
```

```python
import functools

import jax
import jax.numpy as jnp
from jax import lax
from jax.experimental import pallas as pl
from jax.experimental.pallas import tpu as pltpu

F32 = jnp.float32
BF16 = jnp.bfloat16

D_MODEL = 1024
HEAD_DIM = 64
RWKV_HEADS = 8
RWKV_DIM = RWKV_HEADS * HEAD_DIM
D_DECAY_LORA = 64
D_AAA_LORA = 64
D_GATE_LORA = 128
RWKV_PROJ = 3 * RWKV_DIM + D_DECAY_LORA + D_AAA_LORA + D_GATE_LORA
ATT_Q_HEADS = 8
ATT_KV_HEADS = 2
ATT_GROUP = ATT_Q_HEADS // ATT_KV_HEADS
ATT_DIM = ATT_Q_HEADS * HEAD_DIM
ATT_KV_DIM = ATT_KV_HEADS * HEAD_DIM
WINDOW = 128
IN_PROJ = RWKV_PROJ + ATT_DIM + 2 * ATT_KV_DIM + 2 * D_MODEL
D_FF = 2816
CONV_W = 3
ROPE_THETA = 10000.0
RMS_EPS = 1e-6
GN_EPS = 64e-5
PAST_LEN = 16384

LANES = 128
SUBLANES = 8
MXU_DIM = 256
CHUNK = 64
HEADS_PER_GROUP = MXU_DIM // HEAD_DIM
VMEM_LIMIT = 56 * 1024 * 1024


def _params(sem):
    return pltpu.CompilerParams(dimension_semantics=sem, vmem_limit_bytes=VMEM_LIMIT)


def _const_spec(shape):
    zeros = (0,) * len(shape)
    return pl.BlockSpec(shape, lambda *_: zeros, pipeline_mode=pl.Buffered(1))


def _layer_spec(shape, layer):
    index = (layer,) + (0,) * len(shape)
    return pl.BlockSpec((None,) + tuple(shape), lambda *_: index, pipeline_mode=pl.Buffered(1))


def _mm(a, b):
    return jnp.dot(a.astype(BF16), b.astype(BF16), preferred_element_type=F32)


def _mm_nt(a, b):
    return lax.dot_general(a.astype(BF16), b.astype(BF16), (((1,), (1,)), ((), ())),
                           preferred_element_type=F32)


def _mm_f32(a, b):
    return jnp.dot(a, b, precision=lax.Precision.HIGHEST, preferred_element_type=F32)


def _mm_tn(a, b):
    return lax.dot_general(a.astype(BF16), b.astype(BF16), (((0,), (0,)), ((), ())),
                           preferred_element_type=F32)


def _head_sum(x, ones_bd):
    return jnp.dot(x.astype(BF16), ones_bd, preferred_element_type=F32)


def _rmsnorm(x, g):
    return x * lax.rsqrt(jnp.mean(x * x, axis=-1, keepdims=True) + RMS_EPS) * g


def _rope(x, cos, sin_signed, first_half):
    partner = jnp.where(first_half, pltpu.roll(x, LANES - HEAD_DIM // 2, 1), pltpu.roll(x, HEAD_DIM // 2, 1))
    return x * cos + partner * sin_signed


def _prep_math(z, zp, mu, w0, w2p, a0, a2p, g2, k_k, k_a, r_k, ones_bd, outs, rs):
    r_ref, lw_ref, kk_ref, b_ref, kf_ref, v_ref, g_ref, bonus_ref = outs
    zs = z + mu * (zp - z)
    d = RWKV_DIM
    r, k, v = zs[:, 0:d], zs[:, d:2 * d], zs[:, 2 * d:3 * d]
    zwa = zs[:, 3 * d:3 * d + LANES]
    zg = zs[:, 3 * d + LANES:3 * d + 2 * LANES]
    w_pre = w0 + _mm(jnp.tanh(zwa), w2p)
    neg = -w_pre
    softplus = jnp.maximum(neg, 0.0) + jnp.log1p(jnp.exp(-jnp.abs(neg)))
    w_log = -softplus - 0.5
    lw_ref[rs, :] = -jnp.exp(w_log)
    a = jax.nn.sigmoid(a0 + _mm(zwa, a2p))
    g_ref[rs, :] = _mm(jax.nn.sigmoid(zg), g2).astype(g_ref.dtype)
    kk0 = k * k_k
    norm = jnp.sqrt(_head_sum(kk0 * kk0, ones_bd))
    kk = kk0 / jnp.maximum(norm, 1e-12)
    kf = k * (1.0 + (a - 1.0) * k_a)
    r_ref[rs, :] = r.astype(r_ref.dtype)
    kk_ref[rs, :] = kk.astype(kk_ref.dtype)
    b_ref[rs, :] = (kk * a).astype(b_ref.dtype)
    kf_ref[rs, :] = kf.astype(kf_ref.dtype)
    v_ref[rs, :] = v.astype(v_ref.dtype)
    bonus_ref[rs, :] = (_head_sum(r * kf * r_k, ones_bd) * v).astype(bonus_ref.dtype)


N_PREP_CONSTS = 10
N_FRONT_OUTS = 14


def _row_blocks(n, nsub):
    rb = n // nsub
    return [slice(i * rb, (i + 1) * rb) for i in range(nsub)], rb


def _staggered(stages, nsub):
    for step in range(nsub + len(stages) - 1):
        for i in range(nsub):
            if 0 <= step - i < len(stages):
                stages[step - i](i)


def _front_kernel(*refs, sample, nsub):
    if sample:
        x_ref, sp_ref, g_ref, w_ref, cos_ref, sin_ref = refs[:6]
        rest = refs[6:]
    else:
        x_ref, g_ref, w_ref, cos_ref, sin_ref = refs[:5]
        rest = refs[5:]
    consts = [c[...] for c in rest[:N_PREP_CONSTS]]
    outs = rest[N_PREP_CONSTS:N_PREP_CONSTS + N_FRONT_OUTS]
    q_ref, k_ref, v_ref, gr_ref, ga_ref, zt_ref = outs[:6]
    blocks, rb = _row_blocks(x_ref.shape[0], nsub)
    c0, c1, c2, c3, c4 = (RWKV_PROJ, RWKV_PROJ + ATT_DIM, RWKV_PROJ + ATT_DIM + ATT_KV_DIM,
                          RWKV_PROJ + ATT_DIM + 2 * ATT_KV_DIM, RWKV_PROJ + ATT_DIM + 2 * ATT_KV_DIM + D_MODEL)
    if not sample:
        carry = rest[N_PREP_CONSTS + N_FRONT_OUTS]

        @pl.when(pl.program_id(1) == 0)
        def _():
            carry[...] = jnp.zeros_like(carry)

    row = lax.broadcasted_iota(jnp.int32, (rb, 1), 0)
    lane = lax.broadcasted_iota(jnp.int32, (1, LANES), 1)
    first_half = (lane % HEAD_DIM) < (HEAD_DIM // 2)
    hs, zs = {}, {}

    def project(i):
        hs[i] = _rmsnorm(x_ref[blocks[i], :], g_ref[...]).astype(BF16)
        zs[i] = jnp.dot(hs[i], w_ref[:, 0:c0], preferred_element_type=F32)

    def rwkv_operands(i):
        z = zs[i]
        if sample:
            zp = sp_ref[...] if i == 0 else zs[i - 1]
            if i == nsub - 1:
                zt_ref[...] = z
        else:
            tail = carry[...] if i == 0 else zs[i - 1][rb - SUBLANES:, :]
            zp = jnp.where(row == 0, tail[SUBLANES - 1:SUBLANES, :], pltpu.roll(z, 1, 0))
            if i == nsub - 1:
                carry[...] = z[rb - SUBLANES:, :]
                zt_ref[0] = z[rb - SUBLANES:, :]
        _prep_math(z, zp, *consts, outs[6:], blocks[i])

    def attention_operands(i):
        rs, h = blocks[i], hs[i]
        cos = cos_ref[rs, :]
        sin = sin_ref[rs, :]
        q = jnp.dot(h, w_ref[:, c0:c1], preferred_element_type=F32)
        for grp in range(ATT_DIM // LANES):
            sl = slice(grp * LANES, (grp + 1) * LANES)
            q_ref[rs, sl] = _rope(q[:, sl], cos, sin, first_half).astype(q_ref.dtype)
        k = jnp.dot(h, w_ref[:, c1:c2], preferred_element_type=F32)
        k_ref[rs, :] = _rope(k, cos, sin, first_half)
        v_ref[rs, :] = jnp.dot(h, w_ref[:, c2:c3], preferred_element_type=F32)
        gr_ref[rs, :] = jax.nn.sigmoid(jnp.dot(h, w_ref[:, c3:c4], preferred_element_type=F32)).astype(gr_ref.dtype)
        ga_ref[rs, :] = jax.nn.sigmoid(jnp.dot(h, w_ref[:, c4:IN_PROJ], preferred_element_type=F32)).astype(ga_ref.dtype)

    _staggered([project, rwkv_operands, attention_operands], nsub)


def _prep_consts(P):
    return [P['mu'], P['w0'], P['w2p'], P['a0'], P['a2p'], P['g2'], P['k_k'], P['k_a'], P['r_k'], P['ones_bd']]


def _prep_const_specs(layer):
    d = RWKV_DIM
    shapes = [(1, RWKV_PROJ), (1, d), (LANES, d), (1, d), (LANES, d), (D_GATE_LORA, d), (1, d), (1, d), (1, d)]
    return [_layer_spec(s, layer) for s in shapes] + [_const_spec((d, d))]


def _front_out_types(n, tail_shape):
    d = RWKV_DIM
    widths = [(ATT_DIM, BF16), (ATT_KV_DIM, F32), (ATT_KV_DIM, F32), (D_MODEL, BF16), (D_MODEL, BF16)]
    prep = [(d, BF16), (d, F32), (d, BF16), (d, BF16), (d, BF16), (d, BF16), (d, BF16), (d, BF16)]
    shapes = ([jax.ShapeDtypeStruct((n, w), t) for w, t in widths] + [jax.ShapeDtypeStruct(tail_shape, F32)]
              + [jax.ShapeDtypeStruct((n, w), t) for w, t in prep])
    return shapes, [w for w, _ in widths], [w for w, _ in prep]


def _front_prompt(x2d, P, cos_t, sin_t, batch, tm):
    n = x2d.shape[0]
    layer = P['layer']
    nt = n // batch // tm
    row = lambda b, t: (b * nt + t, 0)
    pos = lambda b, t: (t, 0)
    shapes, widths, prep = _front_out_types(n, (batch, SUBLANES, RWKV_PROJ))
    return pl.pallas_call(
        functools.partial(_front_kernel, sample=False, nsub=tm // 128),
        grid=(batch, nt),
        in_specs=[pl.BlockSpec((tm, D_MODEL), row), _layer_spec((1, D_MODEL), layer),
                  _layer_spec((D_MODEL, IN_PROJ), layer),
                  pl.BlockSpec((tm, LANES), pos), pl.BlockSpec((tm, LANES), pos)] + _prep_const_specs(layer),
        out_specs=([pl.BlockSpec((tm, w), row) for w in widths]
                   + [pl.BlockSpec((1, SUBLANES, RWKV_PROJ), lambda b, t: (b, 0, 0))]
                   + [pl.BlockSpec((tm, w), row) for w in prep]),
        out_shape=shapes,
        scratch_shapes=[pltpu.VMEM((SUBLANES, RWKV_PROJ), F32)],
        compiler_params=_params(("arbitrary", "arbitrary")),
        name="front_prompt",
    )(x2d, P['norm_mix_g'], P['w_in'], cos_t, sin_t, *_prep_consts(P))


def _front_sample(x2d, shift_prev, P, cos_t, sin_t):
    n = x2d.shape[0]
    layer = P['layer']
    nb = shift_prev.shape[1]
    shapes, _, _ = _front_out_types(n, (nb, RWKV_PROJ))
    return pl.pallas_call(
        functools.partial(_front_kernel, sample=True, nsub=n // nb),
        grid=(1,),
        in_specs=[_const_spec((n, D_MODEL)), _layer_spec((nb, RWKV_PROJ), layer), _layer_spec((1, D_MODEL), layer),
                  _layer_spec((D_MODEL, IN_PROJ), layer), _const_spec((n, LANES)), _const_spec((n, LANES))]
        + _prep_const_specs(layer),
        out_specs=[_const_spec(sh.shape) for sh in shapes],
        out_shape=shapes,
        compiler_params=_params(("arbitrary",)),
        name="front_sample",
    )(x2d, shift_prev, P['norm_mix_g'], P['w_in'], cos_t, sin_t, *_prep_consts(P))


def _wkv_prompt_kernel(r_ref, lw_ref, kk_ref, b_ref, kf_ref, v_ref, y_ref, st_ref, s_scr, *, chunks, together):
    C, G, M = CHUNK, HEADS_PER_GROUP, MXU_DIM
    nbatch = r_ref.shape[0]
    ngrp = RWKV_HEADS // G

    @pl.when(pl.program_id(0) == 0)
    def _():
        s_scr[...] = jnp.zeros_like(s_scr)

    ti = lax.broadcasted_iota(jnp.int32, (C, M), 0)
    si = lax.broadcasted_iota(jnp.int32, (C, M), 1) % C
    strict = ti > si
    incl = ti >= si
    eye = (ti == si).astype(F32)
    tri = (lax.broadcasted_iota(jnp.int32, (C, C), 0) >= lax.broadcasted_iota(jnp.int32, (C, C), 1)).astype(F32)
    lane_head = lax.broadcasted_iota(jnp.int32, (1, M), 1) // HEAD_DIM
    same_head = (lax.broadcasted_iota(jnp.int32, (M, M), 0) // HEAD_DIM
                 == lax.broadcasted_iota(jnp.int32, (M, M), 1) // HEAD_DIM)

    def stack(x):
        xb = x.astype(BF16)
        return jnp.concatenate([jnp.where(lane_head == h, xb, jnp.zeros_like(xb)) for h in range(G)], axis=0)

    def rows2(top, bottom):
        return jnp.concatenate([top.astype(BF16), bottom.astype(BF16)], axis=0)

    def load(rows):
        probs = []
        for bi in range(nbatch):
            lw = lw_ref[bi, rows, :]
            cum = _mm_f32(tri, lw)
            w_end = jnp.exp(cum[C - 1:C, :])
            e_neg = jnp.exp(-cum)
            a_hat = -kk_ref[bi, rows, :] * jnp.exp(cum - lw)
            r_hat = r_ref[bi, rows, :] * jnp.exp(cum)
            b_til = b_ref[bi, rows, :] * e_neg
            k_til = kf_ref[bi, rows, :] * e_neg
            v = v_ref[bi, rows, :]
            for grp in range(ngrp):
                ls = slice(grp * M, (grp + 1) * M)
                we = w_end[:, ls]
                probs.append(dict(
                    bi=bi, grp=grp, ls=ls, we=we, rows=rows, ar=rows2(a_hat[:, ls], r_hat[:, ls]),
                    b=stack(b_til[:, ls]), k=stack(k_til[:, ls]), v=v[:, ls], vs=stack(v[:, ls]),
                    bk=rows2(b_til[:, ls] * we, k_til[:, ls] * we)))
        return probs

    def state_free(probs):
        mb = [_mm_nt(p['ar'], p['b']) for p in probs]
        mk = [_mm_nt(p['ar'], p['k']) for p in probs]
        l_ab = [jnp.where(strict, x[0:C], 0.0) for x in mb]
        p_rb = [jnp.where(incl, x[C:2 * C], 0.0).astype(BF16) for x in mb]
        lp_k = [rows2(jnp.where(strict, x[0:C], 0.0), jnp.where(incl, x[C:2 * C], 0.0)) for x in mk]
        t_inv = [eye + x for x in l_ab]
        pw = [_mm(x, stack(x)) for x in l_ab]
        for _ in range(C.bit_length() - 3):
            both = [_mm(rows2(t, x), stack(x)) for t, x in zip(t_inv, pw)]
            t_inv = [t + x[0:C] for t, x in zip(t_inv, both)]
            pw = [x[C:2 * C] for x in both]
        t_inv = [t + _mm(t, stack(x)) for t, x in zip(t_inv, pw)]
        e0 = [_mm(x, p['vs']) for x, p in zip(lp_k, probs)]
        return t_inv, p_rb, e0

    def advance(probs, t_inv, p_rb, e0):
        s0 = [s_scr[p['bi'], p['grp']] for p in probs]
        d0 = [_mm_nt(p['ar'], x) for p, x in zip(probs, s0)]
        u = [_mm(t, stack(x[0:C] + y[0:C])) for t, x, y in zip(t_inv, d0, e0)]
        for p, x, y, w, q, st in zip(probs, d0, e0, u, p_rb, s0):
            y_ref[p['bi'], p['rows'], p['ls']] = x[C:2 * C] + y[C:2 * C] + _mm(q, stack(w))
            upd = _mm_tn(rows2(w, p['v']), p['bk'])
            s_scr[p['bi'], p['grp']] = st * p['we'] + jnp.where(same_head, upd, 0.0)

    def chunk_group(c, carry):
        chunk_probs = [load(pl.ds(pl.multiple_of((c * together + j) * C, C), C)) for j in range(together)]
        t_inv, p_rb, e0 = state_free([p for probs in chunk_probs for p in probs])
        n = len(chunk_probs[0])
        for j, probs in enumerate(chunk_probs):
            sl = slice(j * n, (j + 1) * n)
            advance(probs, t_inv[sl], p_rb[sl], e0[sl])
        return carry

    lax.fori_loop(0, chunks // together, chunk_group, 0)

    @pl.when(pl.program_id(0) == pl.num_programs(0) - 1)
    def _():
        st_ref[...] = s_scr[...]


def _wkv_prompt(r, lw, kk, b, kf, v, batch, tl):
    n = r.shape[0]
    seq = n // batch
    ngrp = RWKV_HEADS // HEADS_PER_GROUP
    blk = pl.BlockSpec((batch, tl, RWKV_DIM), lambda t: (0, t, 0))
    y, st = pl.pallas_call(
        functools.partial(_wkv_prompt_kernel, chunks=tl // CHUNK, together=2),
        grid=(seq // tl,),
        in_specs=[blk] * 6,
        out_specs=[blk, _const_spec((batch, ngrp, MXU_DIM, MXU_DIM))],
        out_shape=[jax.ShapeDtypeStruct((batch, seq, RWKV_DIM), F32),
                   jax.ShapeDtypeStruct((batch, ngrp, MXU_DIM, MXU_DIM), F32)],
        scratch_shapes=[pltpu.VMEM((batch, ngrp, MXU_DIM, MXU_DIM), F32)],
        compiler_params=_params(("arbitrary",)),
        name="wkv_prompt",
    )(*(t.reshape(batch, seq, RWKV_DIM) for t in (r, lw, kk, b, kf, v)))
    st = st.reshape(batch, ngrp, HEADS_PER_GROUP, HEAD_DIM, HEADS_PER_GROUP, HEAD_DIM)
    st = jnp.stack([st[:, :, h, :, h, :] for h in range(HEADS_PER_GROUP)], axis=2)
    st = st.reshape(batch, RWKV_HEADS, HEAD_DIM, HEAD_DIM)
    return y.reshape(n, RWKV_DIM), st


def _wkv_sample_kernel(r_ref, lw_ref, kk_ref, b_ref, kf_ref, v_ref, s_ref, y_ref, so_ref, *, steps):
    group = SUBLANES // 2

    def value_rows(gi, carry):
        vis = [gi * group + j for j in range(group)]
        ss = [s_ref[0, vi] for vi in vis]
        for t in range(steps):
            kk, w, b = kk_ref[t, 0].astype(F32), jnp.exp(lw_ref[t, 0]), b_ref[t, 0].astype(F32)
            kf, r = kf_ref[t, 0].astype(F32), r_ref[t, 0].astype(F32)
            sks = [jnp.sum(s * kk, axis=0, keepdims=True) for s in ss]
            ss = [s * w - sk * b + v_ref[t, 0, pl.ds(vi, 1), :] * kf for s, sk, vi in zip(ss, sks, vis)]
            for s, vi in zip(ss, vis):
                y_ref[t, 0, pl.ds(vi, 1), :] = jnp.sum(s * r, axis=0, keepdims=True)
        for s, vi in zip(ss, vis):
            so_ref[0, vi] = s
        return carry

    lax.fori_loop(0, HEAD_DIM // group, value_rows, 0)


def _wkv_sample(r, lw, kk, b, kf, v, s0, layer, steps):
    nb = r.shape[-1]
    vec = pl.BlockSpec((steps, 1, HEAD_DIM, nb), lambda h: (0, h, 0, 0))
    st = pl.BlockSpec((1, HEAD_DIM, HEAD_DIM, nb), lambda h: (h, 0, 0, 0))
    st_in = pl.BlockSpec((None, 1, HEAD_DIM, HEAD_DIM, nb), lambda h: (layer, h, 0, 0, 0))
    return pl.pallas_call(
        functools.partial(_wkv_sample_kernel, steps=steps),
        grid=(RWKV_HEADS,),
        in_specs=[vec] * 6 + [st_in],
        out_specs=[vec, st],
        out_shape=[jax.ShapeDtypeStruct(r.shape, F32), jax.ShapeDtypeStruct(s0.shape[1:], F32)],
        compiler_params=_params(("arbitrary",)),
        name="wkv_sample",
    )(r, lw, kk, b, kf, v, s0)


def _swa_prompt_kernel(q_ref, kc_ref, kp_ref, vc_ref, vp_ref, sink_ref, o_ref, *, sub):
    blk = WINDOW
    first = pl.program_id(1) == 0
    k_all = jnp.concatenate([kp_ref[...], kc_ref[...]], axis=0)
    v_all = jnp.concatenate([vp_ref[...], vc_ref[...]], axis=0).astype(BF16)
    v_ext = jnp.concatenate([v_all, jnp.ones_like(v_all)], axis=1)
    k_swapped = pltpu.roll(k_all, HEAD_DIM, 1)
    lane = lax.broadcasted_iota(jnp.int32, (1, LANES), 1)
    low = lane < HEAD_DIM
    k_var = {(0, 0): jnp.where(low, k_all, 0.0), (0, 1): jnp.where(low, 0.0, k_swapped),
             (1, 1): jnp.where(low, 0.0, k_all), (1, 0): jnp.where(low, k_swapped, 0.0)}
    k_var = {key: val.astype(BF16) for key, val in k_var.items()}
    order = [j * ATT_GROUP + g for j in range(ATT_KV_HEADS) for half in range(2)
             for g in range(ATT_GROUP) if g % 2 == half]
    n_rows = ATT_Q_HEADS * blk
    qi = lax.broadcasted_iota(jnp.int32, (n_rows, 2 * blk), 0) % blk
    kj = lax.broadcasted_iota(jnp.int32, (n_rows, 2 * blk), 1)
    diff = blk + qi - kj
    band = (diff >= 0) & (diff < WINDOW)
    row_head = lax.broadcasted_iota(jnp.int32, (n_rows, 1), 0) // blk
    sink = jnp.zeros((n_rows, 1), F32)
    for idx, head in enumerate(order):
        sink = jnp.where(row_head == idx, sink_ref[0:1, head:head + 1], sink)
    scale = HEAD_DIM ** -0.5
    scores, maxes, weights, sums = {}, {}, {}, {}

    def score(sb):
        qs = slice(sb * blk, (sb + 1) * blk)
        ks = slice(sb * blk, (sb + 2) * blk)
        parts = []
        for j in range(ATT_KV_HEADS):
            for half in range(2):
                heads = [j * ATT_GROUP + g for g in range(ATT_GROUP) if g % 2 == half]
                q_cat = jnp.concatenate([q_ref[qs, (h // 2) * LANES:(h // 2 + 1) * LANES] for h in heads], axis=0)
                parts.append(_mm_nt(q_cat * scale, k_var[(j, half)][ks]))
        s = jnp.concatenate(parts, axis=0)
        ok = band & jnp.logical_or(kj >= blk, jnp.logical_not(first)) if sb == 0 else band
        scores[sb] = jnp.where(ok, s, -jnp.inf)

    def exponentiate(sb):
        maxes[sb] = jnp.maximum(jnp.max(scores[sb], axis=-1, keepdims=True), sink)
        weights[sb] = jnp.exp(scores[sb] - maxes[sb]).astype(BF16)

    def weigh(sb):
        ks = slice(sb * blk, (sb + 2) * blk)
        sums[sb] = jnp.dot(weights[sb], v_ext[ks], preferred_element_type=F32)

    def emit(sb):
        qs = slice(sb * blk, (sb + 1) * blk)
        ev = sums[sb]
        o = ev[:, 0:LANES] / (ev[:, LANES:LANES + 1] + jnp.exp(sink - maxes[sb]))
        by_head = {}
        for idx, head in enumerate(order):
            o_h = o[idx * blk:(idx + 1) * blk]
            by_head[head] = pltpu.roll(o_h, HEAD_DIM, 1) if head // ATT_GROUP != head % 2 else o_h
        for grp in range(ATT_DIM // LANES):
            o_ref[qs, grp * LANES:(grp + 1) * LANES] = jnp.where(low, by_head[2 * grp],
                                                                 by_head[2 * grp + 1]).astype(o_ref.dtype)

    _staggered([score, exponentiate, weigh, emit], sub)


def _swa_prompt(q, k, v, P, batch, sub):
    n = q.shape[0]
    rows = sub * WINDOW
    nb = n // batch // rows
    cur = lambda b, i: (b * nb + i, 0)
    prev = lambda b, i: ((b * nb + i) * sub - jnp.minimum(i, 1), 0)
    return pl.pallas_call(
        functools.partial(_swa_prompt_kernel, sub=sub),
        grid=(batch, nb),
        in_specs=[pl.BlockSpec((rows, ATT_DIM), cur),
                  pl.BlockSpec((rows, ATT_KV_DIM), cur), pl.BlockSpec((WINDOW, ATT_KV_DIM), prev),
                  pl.BlockSpec((rows, ATT_KV_DIM), cur), pl.BlockSpec((WINDOW, ATT_KV_DIM), prev),
                  _layer_spec((1, ATT_Q_HEADS), P['layer'])],
        out_specs=pl.BlockSpec((rows, ATT_DIM), cur),
        out_shape=jax.ShapeDtypeStruct((n, ATT_DIM), BF16),
        compiler_params=_params(("arbitrary", "arbitrary")),
        name="swa_prompt",
    )(q, k, k, v, v, P['sinks'])


def _swa_sample_kernel(q_ref, kc_ref, vc_ref, kn_ref, vn_ref, sink_ref, o_ref, *, steps):
    q = q_ref[...].astype(BF16)
    rows = q.shape[1]
    nk = kc_ref.shape[1]
    nn = kn_ref.shape[1]
    bdot = lambda a, b, dims: lax.dot_general(a, b, (dims, ((0,), (0,))), preferred_element_type=F32)
    s_c = bdot(q, kc_ref[...].astype(BF16), ((2,), (2,))) * (HEAD_DIM ** -0.5)
    s_n = bdot(q, kn_ref[...].astype(BF16), ((2,), (2,))) * (HEAD_DIM ** -0.5)
    t_c = lax.broadcasted_iota(jnp.int32, (1, rows, nk), 1) % steps
    w_c = lax.broadcasted_iota(jnp.int32, (1, rows, nk), 2)
    s_c = jnp.where(w_c > t_c, s_c, -jnp.inf)
    t_n = lax.broadcasted_iota(jnp.int32, (1, rows, nn), 1) % steps
    w_n = lax.broadcasted_iota(jnp.int32, (1, rows, nn), 2)
    s_n = jnp.where(w_n <= t_n, s_n, -jnp.inf)
    sink = sink_ref[...][None, :, 0:1]
    m = jnp.maximum(jnp.maximum(jnp.max(s_c, axis=-1, keepdims=True), jnp.max(s_n, axis=-1, keepdims=True)), sink)
    e_c = jnp.exp(s_c - m)
    e_n = jnp.exp(s_n - m)
    den = jnp.sum(e_c, axis=-1, keepdims=True) + jnp.sum(e_n, axis=-1, keepdims=True) + jnp.exp(sink - m)
    o = (bdot((e_c / den).astype(BF16), vc_ref[...].astype(BF16), ((2,), (1,)))
         + bdot((e_n / den).astype(BF16), vn_ref[...].astype(BF16), ((2,), (1,))))
    o_ref[...] = o


def _swa_sample(q_st, k_cache, v_cache, k_new, v_new, sink_rows, layer, steps, bt):
    nb, rows, _ = q_st.shape
    nn = k_new.shape[1]
    blk = lambda r: pl.BlockSpec((bt, r, LANES), lambda i: (i, 0, 0))
    cache = pl.BlockSpec((None, bt, WINDOW, LANES), lambda i: (layer, i, 0, 0))
    return pl.pallas_call(
        functools.partial(_swa_sample_kernel, steps=steps),
        grid=(nb // bt,),
        in_specs=[blk(rows), cache, cache, blk(nn), blk(nn), _layer_spec((rows, LANES), layer)],
        out_specs=blk(rows),
        out_shape=jax.ShapeDtypeStruct((nb, rows, LANES), F32),
        compiler_params=_params(("arbitrary",)),
        name="swa_sample",
    )(q_st, k_cache, v_cache, k_new, v_new, sink_rows)


def _gelu(x):
    return 0.5 * x * (1.0 + lax.erf(x * (2.0 ** -0.5)))


def _back_kernel(*refs, final, sample, nsub):
    x_ref, y_ref, bonus_ref, g_ref, oa_ref, gr_ref, ga_ref = refs[:7]
    at = 7
    if sample:
        cp_ref = refs[at]
        at += 1
    (lng_ref, lnb_ref, ones_ref, wa_ref, wb_ref, wo_ref, gn_ref, wi_ref, cw_ref, cb_ref, wd_ref,
     gf_ref) = refs[at:at + 12]
    o_ref, ct_ref, carry = refs[at + 12:at + 15]
    blocks, rb = _row_blocks(x_ref.shape[0], nsub)
    ones_bd = ones_ref[...]

    if sample:
        @pl.when(pl.program_id(0) == 0)
        def _():
            carry[...] = cp_ref[...]
    else:
        @pl.when(pl.program_id(1) == 0)
        def _():
            carry[...] = jnp.zeros_like(carry)

    row = lax.broadcasted_iota(jnp.int32, (rb, 1), 0)
    xs, cs, ups, acts = {}, {}, {}, {}

    def mix(i):
        rs = blocks[i]
        y = y_ref[rs, :]
        mean = _head_sum(y, ones_bd) * (1.0 / HEAD_DIM)
        d = y - mean
        var = _head_sum(d * d, ones_bd) * (1.0 / HEAD_DIM)
        yn = d * lax.rsqrt(var + GN_EPS) * lng_ref[...] + lnb_ref[...]
        o_r = (yn + bonus_ref[rs, :]) * g_ref[rs, :]
        merged = gr_ref[rs, :] * _mm(o_r, wa_ref[...]) + ga_ref[rs, :] * _mm(oa_ref[rs, :], wb_ref[...])
        xs[i] = x_ref[rs, :] + _mm(merged, wo_ref[...])

    def up_project(i):
        h = _rmsnorm(xs[i], gn_ref[...]).astype(BF16)
        cs[i] = jnp.dot(h, wi_ref[:, 0:D_FF], preferred_element_type=F32)
        ups[i] = jnp.dot(h, wi_ref[:, D_FF:2 * D_FF], preferred_element_type=F32)

    def conv_gate(i):
        c = cs[i]
        if sample:
            c2, c1 = carry[0], carry[1]
            carry[0] = c1
            carry[1] = c
            ct_ref[0] = c
        else:
            tail = carry[...] if i == 0 else cs[i - 1][rb - SUBLANES:, :]
            last = tail[SUBLANES - 1:SUBLANES, :]
            c1 = jnp.where(row == 0, last, pltpu.roll(c, 1, 0))
            c2 = jnp.where(row == 0, tail[SUBLANES - 2:SUBLANES - 1, :],
                           jnp.where(row == 1, last, pltpu.roll(c, 2, 0)))
            if i == nsub - 1:
                carry[...] = c[rb - SUBLANES:, :]
                ct_ref[0] = c[rb - SUBLANES:, :]
        conv = cb_ref[...] + c2 * cw_ref[0:1, :] + c1 * cw_ref[1:2, :] + c * cw_ref[2:3, :]
        acts[i] = (_gelu(conv) * ups[i]).astype(BF16)

    def down_project(i):
        out = xs[i] + jnp.dot(acts[i], wd_ref[...], preferred_element_type=F32)
        o_ref[blocks[i], :] = _rmsnorm(out, gf_ref[...]) if final else out

    _staggered([mix, up_project, conv_gate, down_project], nsub)


def _back_weight_specs(layer):
    return [_layer_spec((1, RWKV_DIM), layer), _layer_spec((1, RWKV_DIM), layer), _const_spec((RWKV_DIM, RWKV_DIM)),
            _layer_spec((RWKV_DIM, D_MODEL), layer), _layer_spec((ATT_DIM, D_MODEL), layer),
            _layer_spec((D_MODEL, D_MODEL), layer),
            _layer_spec((1, D_MODEL), layer), _layer_spec((D_MODEL, 2 * D_FF), layer),
            _layer_spec((CONV_W, D_FF), layer), _layer_spec((1, D_FF), layer), _layer_spec((D_FF, D_MODEL), layer),
            _const_spec((1, D_MODEL))]


def _back_weights(P, g_final):
    return [P['ln_g'], P['ln_b'], P['ones_bd'], P['w_br_rwkv'], P['w_br_attn'], P['w_out'],
            P['norm_ffn_g'], P['ffn_w_in'], P['ffn_conv_w'], P['ffn_conv_b'], P['ffn_w_down'], g_final]


def _back_prompt(x, y, bonus, g, o_a, gr, ga, P, g_final, final, batch, tm):
    n = x.shape[0]
    nt = n // batch // tm
    row = lambda b, t: (b * nt + t, 0)
    wide = pl.BlockSpec((tm, D_MODEL), row)
    half = pl.BlockSpec((tm, RWKV_DIM), row)
    return pl.pallas_call(
        functools.partial(_back_kernel, final=final, sample=False, nsub=tm // 256),
        grid=(batch, nt),
        in_specs=[wide, half, half, half, half, wide, wide] + _back_weight_specs(P['layer']),
        out_specs=[wide, pl.BlockSpec((1, SUBLANES, D_FF), lambda b, t: (b, 0, 0))],
        out_shape=[jax.ShapeDtypeStruct((n, D_MODEL), F32), jax.ShapeDtypeStruct((batch, SUBLANES, D_FF), F32)],
        scratch_shapes=[pltpu.VMEM((SUBLANES, D_FF), F32)],
        compiler_params=_params(("arbitrary", "arbitrary")),
        name="back_prompt",
    )(x, y, bonus, g, o_a, gr, ga, *_back_weights(P, g_final))


def _back_sample(x, y, bonus, g, o_a, gr, ga, conv_prev, P, g_final, final):
    n = x.shape[0]
    nb = conv_prev.shape[2]
    steps = n // nb
    taps = CONV_W - 1
    row = lambda t: (t, 0)
    wide = pl.BlockSpec((nb, D_MODEL), row)
    half = pl.BlockSpec((nb, RWKV_DIM), row)
    return pl.pallas_call(
        functools.partial(_back_kernel, final=final, sample=True, nsub=1),
        grid=(steps,),
        in_specs=[wide, half, half, half, half, wide, wide, _layer_spec((taps, nb, D_FF), P['layer'])]
        + _back_weight_specs(P['layer']),
        out_specs=[wide, pl.BlockSpec((1, nb, D_FF), lambda t: (jnp.maximum(t - (steps - taps), 0), 0, 0))],
        out_shape=[jax.ShapeDtypeStruct((n, D_MODEL), F32), jax.ShapeDtypeStruct((taps, nb, D_FF), F32)],
        scratch_shapes=[pltpu.VMEM((taps, nb, D_FF), F32)],
        compiler_params=_params(("arbitrary",)),
        name="back_sample",
    )(x, y, bonus, g, o_a, gr, ga, conv_prev, *_back_weights(P, g_final))


def _rope_tables(pos):
    inv = ROPE_THETA ** (-jnp.arange(0, HEAD_DIM, 2, dtype=F32) / HEAD_DIM)
    ang = pos.astype(F32)[:, None] * inv[None, :]
    cos, sin = jnp.cos(ang), jnp.sin(ang)
    reps = LANES // HEAD_DIM
    return jnp.tile(jnp.concatenate([cos, cos], axis=1), (1, reps)), jnp.tile(jnp.concatenate([-sin, sin], axis=1),
                                                                             (1, reps))


def _stacked_params(norm_mix_g, w_in, rwkv_mu, rwkv_w0, rwkv_w2, rwkv_a0, rwkv_a2, rwkv_g2, rwkv_k_k, rwkv_k_a,
                    rwkv_r_k, rwkv_ln_g, rwkv_ln_b, attn_sinks, w_br_rwkv, w_br_attn, w_out, norm_ffn_g, ffn_w_in,
                    ffn_conv_w, ffn_conv_b, ffn_w_down, steps):
    d = RWKV_DIM
    depth = w_in.shape[0]
    head = jnp.arange(d) // HEAD_DIM
    row = lambda t: t.reshape(depth, 1, -1)
    sink_rows = jnp.broadcast_to(jnp.repeat(attn_sinks, steps, axis=1)[:, :, None],
                                 (depth, ATT_Q_HEADS * steps, LANES))
    return {
        'norm_mix_g': row(norm_mix_g), 'w_in': w_in.astype(BF16),
        'mu': row(rwkv_mu), 'w0': row(rwkv_w0),
        'w2p': jnp.pad(rwkv_w2, ((0, 0), (0, D_AAA_LORA), (0, 0))).astype(BF16),
        'a0': row(rwkv_a0),
        'a2p': jnp.pad(rwkv_a2, ((0, 0), (D_DECAY_LORA, 0), (0, 0))).astype(BF16),
        'g2': rwkv_g2.astype(BF16), 'k_k': row(rwkv_k_k), 'k_a': row(rwkv_k_a),
        'r_k': row(rwkv_r_k), 'ln_g': row(rwkv_ln_g), 'ln_b': row(rwkv_ln_b),
        'ones_bd': (head[:, None] == head[None, :]).astype(BF16),
        'sinks': row(attn_sinks), 'sink_rows': sink_rows,
        'w_br_rwkv': w_br_rwkv.astype(BF16), 'w_br_attn': w_br_attn.astype(BF16),
        'w_out': w_out.astype(BF16), 'norm_ffn_g': row(norm_ffn_g),
        'ffn_w_in': ffn_w_in.astype(BF16), 'ffn_conv_w': ffn_conv_w, 'ffn_conv_b': row(ffn_conv_b),
        'ffn_w_down': ffn_w_down.astype(BF16),
    }


def _prompt_layer(x, P, tables, g_final, final, batch, seq):
    tm = min(512, seq)
    q, k, v, gr, ga, z_tail, r, lw, kk, b, kf, vv, g, bonus = _front_prompt(x, P, tables[0], tables[1], batch, tm)
    y, s_t = _wkv_prompt(r, lw, kk, b, kf, vv, batch, min(256, seq))
    o_a = _swa_prompt(q, k, v, P, batch, min(4, seq // WINDOW))
    x2, c_tail = _back_prompt(x, y, bonus, g, o_a, gr, ga, P, g_final, final, batch, tm)
    last = lambda t: t.reshape(batch, seq, -1)[:, -WINDOW:].reshape(batch, WINDOW, ATT_KV_HEADS, HEAD_DIM)
    state = (z_tail[:, SUBLANES - 1], s_t, last(k), last(v), c_tail[:, SUBLANES - (CONV_W - 1):])
    return x2, state


def _sample_layer(x, P, tables, g_final, final, shift_prev, wkv0, k_cache, v_cache, conv_prev, nb, steps):
    n = nb * steps
    layer = P['layer']
    q, k, v, gr, ga, z_tail, r, lw, kk, b, kf, vv, g, bonus = _front_sample(x, shift_prev, P, tables[0], tables[1])
    to_lanes = lambda t: t.reshape(steps, nb, RWKV_HEADS, HEAD_DIM).transpose(0, 2, 3, 1)
    y_l, s_l = _wkv_sample(*(to_lanes(t) for t in (r, lw, kk, b, kf)), to_lanes(vv).astype(F32), wkv0, layer, steps)
    y = y_l.transpose(0, 3, 1, 2).reshape(n, RWKV_DIM)

    rows = ATT_Q_HEADS * steps
    q5 = q.reshape(steps, nb, ATT_KV_HEADS, ATT_GROUP, HEAD_DIM).transpose(1, 2, 3, 0, 4)
    sel = jnp.eye(ATT_KV_HEADS, dtype=q.dtype)
    q_st = (q5[:, :, :, :, None, :] * sel[None, :, None, None, :, None]).reshape(nb, rows, ATT_KV_DIM)
    pad = ((0, 0), (0, SUBLANES - steps), (0, 0))
    k_new = k.reshape(steps, nb, ATT_KV_DIM).transpose(1, 0, 2)
    v_new = v.reshape(steps, nb, ATT_KV_DIM).transpose(1, 0, 2)
    o_st = _swa_sample(q_st, k_cache, v_cache, jnp.pad(k_new, pad), jnp.pad(v_new, pad), P['sink_rows'], layer,
                       steps, 16)
    o5 = o_st.reshape(nb, ATT_KV_HEADS, ATT_GROUP, steps, ATT_KV_HEADS, HEAD_DIM)
    o_a = jnp.stack([o5[:, j, :, :, j, :] for j in range(ATT_KV_HEADS)], axis=0)
    o_a = o_a.transpose(3, 1, 0, 2, 4).reshape(n, ATT_DIM)

    x2, c_tail = _back_sample(x, y, bonus, g, o_a, gr, ga, conv_prev, P, g_final, final)
    state = (z_tail, s_l, k_new, v_new, c_tail)
    return x2, state


def kernel(x_prompt, x_sample, state_rwkv_shift, state_rwkv_wkv, cache_swa_k, cache_swa_v, state_ffn_conv, norm_mix_g, w_in, rwkv_mu, rwkv_w0, rwkv_w2, rwkv_a0, rwkv_a2, rwkv_g2, rwkv_k_k, rwkv_k_a, rwkv_r_k, rwkv_ln_g, rwkv_ln_b, attn_sinks, w_br_rwkv, w_br_attn, w_out, norm_ffn_g, ffn_w_in, ffn_conv_w, ffn_conv_b, ffn_w_down, norm_final_g):
    bp, tp, _ = x_prompt.shape
    nb, steps, _ = x_sample.shape
    depth = w_in.shape[0]
    tab_p = _rope_tables(jnp.arange(tp, dtype=jnp.int32))
    tab_s = _rope_tables(PAST_LEN + jnp.repeat(jnp.arange(steps, dtype=jnp.int32), nb))
    g_final = norm_final_g[None]
    params = _stacked_params(norm_mix_g, w_in, rwkv_mu, rwkv_w0, rwkv_w2, rwkv_a0, rwkv_a2, rwkv_g2, rwkv_k_k,
                             rwkv_k_a, rwkv_r_k, rwkv_ln_g, rwkv_ln_b, attn_sinks, w_br_rwkv, w_br_attn, w_out,
                             norm_ffn_g, ffn_w_in, ffn_conv_w, ffn_conv_b, ffn_w_down, steps)
    wkv0 = state_rwkv_wkv.transpose(0, 2, 3, 4, 1)
    k_cache = cache_swa_k.reshape(depth, nb, WINDOW, ATT_KV_DIM)
    v_cache = cache_swa_v.reshape(depth, nb, WINDOW, ATT_KV_DIM)
    conv_prev = state_ffn_conv.transpose(0, 2, 1, 3)
    xp = x_prompt.reshape(bp * tp, D_MODEL)
    xs = x_sample.transpose(1, 0, 2).reshape(steps * nb, D_MODEL)
    outs_p, outs_s = [], []
    for l in range(depth):
        P = dict(params, layer=l)
        final = l == depth - 1
        xp, sp = _prompt_layer(xp, P, tab_p, g_final, final, bp, tp)
        xs, ss = _sample_layer(xs, P, tab_s, g_final, final, state_rwkv_shift, wkv0, k_cache, v_cache, conv_prev,
                               nb, steps)
        outs_p.append(sp)
        outs_s.append(ss)
    y_prompt = xp.reshape(bp, tp, D_MODEL)
    y_sample = xs.reshape(steps, nb, D_MODEL).transpose(1, 0, 2)
    stack = lambda outs, i: jnp.stack([o[i] for o in outs])
    kv_shape = (depth, nb, steps, ATT_KV_HEADS, HEAD_DIM)
    s_k = jnp.concatenate([cache_swa_k[:, :, steps:], stack(outs_s, 2).reshape(kv_shape)], axis=2)
    s_v = jnp.concatenate([cache_swa_v[:, :, steps:], stack(outs_s, 3).reshape(kv_shape)], axis=2)
    s_wkv = stack(outs_s, 1).transpose(0, 4, 1, 2, 3)
    s_conv = stack(outs_s, 4).transpose(0, 2, 1, 3)
    return (y_prompt, y_sample, *(stack(outs_p, i) for i in range(5)),
            stack(outs_s, 0), s_wkv, s_k, s_v, s_conv)
```

```python
import functools

import jax
import jax.numpy as jnp
from jax import lax
from jax.experimental import pallas as pl
from jax.experimental.pallas import tpu as pltpu

F32 = jnp.float32
BF16 = jnp.bfloat16

D_MODEL = 1024
HEAD_DIM = 64
RWKV_HEADS = 8
RWKV_DIM = RWKV_HEADS * HEAD_DIM
D_DECAY_LORA = 64
D_AAA_LORA = 64
D_GATE_LORA = 128
RWKV_PROJ = 3 * RWKV_DIM + D_DECAY_LORA + D_AAA_LORA + D_GATE_LORA
ATT_Q_HEADS = 8
ATT_KV_HEADS = 2
ATT_GROUP = ATT_Q_HEADS // ATT_KV_HEADS
ATT_DIM = ATT_Q_HEADS * HEAD_DIM
ATT_KV_DIM = ATT_KV_HEADS * HEAD_DIM
WINDOW = 128
IN_PROJ = RWKV_PROJ + ATT_DIM + 2 * ATT_KV_DIM + 2 * D_MODEL
D_FF = 2816
CONV_W = 3
ROPE_THETA = 10000.0
RMS_EPS = 1e-6
GN_EPS = 64e-5
PAST_LEN = 16384

LANES = 128
SUBLANES = 8
MXU_DIM = 256
CHUNK = 64
HEADS_PER_GROUP = MXU_DIM // HEAD_DIM
VMEM_LIMIT = 56 * 1024 * 1024


def _params(sem):
    return pltpu.CompilerParams(dimension_semantics=sem, vmem_limit_bytes=VMEM_LIMIT)


def _const_spec(shape):
    zeros = (0,) * len(shape)
    return pl.BlockSpec(shape, lambda *_: zeros, pipeline_mode=pl.Buffered(1))


def _layer_spec(shape, layer):
    index = (layer,) + (0,) * len(shape)
    return pl.BlockSpec((None,) + tuple(shape), lambda *_: index, pipeline_mode=pl.Buffered(1))


def _mm(a, b):
    return jnp.dot(a.astype(BF16), b.astype(BF16), preferred_element_type=F32)


def _mm_nt(a, b):
    return lax.dot_general(a.astype(BF16), b.astype(BF16), (((1,), (1,)), ((), ())),
                           preferred_element_type=F32)


def _mm_f32(a, b):
    return jnp.dot(a, b, precision=lax.Precision.HIGHEST, preferred_element_type=F32)


def _mm_tn(a, b):
    return lax.dot_general(a.astype(BF16), b.astype(BF16), (((0,), (0,)), ((), ())),
                           preferred_element_type=F32)


def _head_sum(x, ones_bd):
    return jnp.dot(x.astype(BF16), ones_bd, preferred_element_type=F32)


def _rmsnorm(x, g):
    return x * lax.rsqrt(jnp.mean(x * x, axis=-1, keepdims=True) + RMS_EPS) * g


def _rope(x, cos, sin_signed, first_half):
    partner = jnp.where(first_half, pltpu.roll(x, LANES - HEAD_DIM // 2, 1), pltpu.roll(x, HEAD_DIM // 2, 1))
    return x * cos + partner * sin_signed


def _prep_math(z, zp, mu, w0, w2p, a0, a2p, g2, k_k, k_a, r_k, ones_bd, outs, rs):
    r_ref, lw_ref, kk_ref, b_ref, kf_ref, v_ref, g_ref, bonus_ref = outs
    zs = z + mu * (zp - z)
    d = RWKV_DIM
    r, k, v = zs[:, 0:d], zs[:, d:2 * d], zs[:, 2 * d:3 * d]
    zwa = zs[:, 3 * d:3 * d + LANES]
    zg = zs[:, 3 * d + LANES:3 * d + 2 * LANES]
    w_pre = w0 + _mm(jnp.tanh(zwa), w2p)
    neg = -w_pre
    softplus = jnp.maximum(neg, 0.0) + jnp.log1p(jnp.exp(-jnp.abs(neg)))
    w_log = -softplus - 0.5
    lw_ref[rs, :] = -jnp.exp(w_log)
    a = jax.nn.sigmoid(a0 + _mm(zwa, a2p))
    g_ref[rs, :] = _mm(jax.nn.sigmoid(zg), g2).astype(g_ref.dtype)
    kk0 = k * k_k
    norm = jnp.sqrt(_head_sum(kk0 * kk0, ones_bd))
    kk = kk0 / jnp.maximum(norm, 1e-12)
    kf = k * (1.0 + (a - 1.0) * k_a)
    r_ref[rs, :] = r.astype(r_ref.dtype)
    kk_ref[rs, :] = kk.astype(kk_ref.dtype)
    b_ref[rs, :] = (kk * a).astype(b_ref.dtype)
    kf_ref[rs, :] = kf.astype(kf_ref.dtype)
    v_ref[rs, :] = v.astype(v_ref.dtype)
    bonus_ref[rs, :] = (_head_sum(r * kf * r_k, ones_bd) * v).astype(bonus_ref.dtype)


N_PREP_CONSTS = 10
N_FRONT_OUTS = 14


def _row_blocks(n, nsub):
    rb = n // nsub
    return [slice(i * rb, (i + 1) * rb) for i in range(nsub)], rb


def _staggered(stages, nsub):
    for step in range(nsub + len(stages) - 1):
        for i in range(nsub):
            if 0 <= step - i < len(stages):
                stages[step - i](i)


def _front_kernel(*refs, sample, nsub):
    if sample:
        x_ref, sp_ref, g_ref, w_ref, cos_ref, sin_ref = refs[:6]
        rest = refs[6:]
    else:
        x_ref, g_ref, w_ref, cos_ref, sin_ref = refs[:5]
        rest = refs[5:]
    consts = [c[...] for c in rest[:N_PREP_CONSTS]]
    outs = rest[N_PREP_CONSTS:N_PREP_CONSTS + N_FRONT_OUTS]
    q_ref, k_ref, v_ref, gr_ref, ga_ref, zt_ref = outs[:6]
    blocks, rb = _row_blocks(x_ref.shape[0], nsub)
    c0, c1, c2, c3, c4 = (RWKV_PROJ, RWKV_PROJ + ATT_DIM, RWKV_PROJ + ATT_DIM + ATT_KV_DIM,
                          RWKV_PROJ + ATT_DIM + 2 * ATT_KV_DIM, RWKV_PROJ + ATT_DIM + 2 * ATT_KV_DIM + D_MODEL)
    if not sample:
        carry = rest[N_PREP_CONSTS + N_FRONT_OUTS]

        @pl.when(pl.program_id(1) == 0)
        def _():
            carry[...] = jnp.zeros_like(carry)

    row = lax.broadcasted_iota(jnp.int32, (rb, 1), 0)
    lane = lax.broadcasted_iota(jnp.int32, (1, LANES), 1)
    first_half = (lane % HEAD_DIM) < (HEAD_DIM // 2)
    hs, zs = {}, {}

    def project(i):
        hs[i] = _rmsnorm(x_ref[blocks[i], :], g_ref[...]).astype(BF16)
        zs[i] = jnp.dot(hs[i], w_ref[:, 0:c0], preferred_element_type=F32)

    def rwkv_operands(i):
        z = zs[i]
        if sample:
            zp = sp_ref[...] if i == 0 else zs[i - 1]
            if i == nsub - 1:
                zt_ref[...] = z
        else:
            tail = carry[...] if i == 0 else zs[i - 1][rb - SUBLANES:, :]
            zp = jnp.where(row == 0, tail[SUBLANES - 1:SUBLANES, :], pltpu.roll(z, 1, 0))
            if i == nsub - 1:
                carry[...] = z[rb - SUBLANES:, :]
                zt_ref[0] = z[rb - SUBLANES:, :]
        _prep_math(z, zp, *consts, outs[6:], blocks[i])

    def attention_operands(i):
        rs, h = blocks[i], hs[i]
        cos = cos_ref[rs, :]
        sin = sin_ref[rs, :]
        q = jnp.dot(h, w_ref[:, c0:c1], preferred_element_type=F32)
        for grp in range(ATT_DIM // LANES):
            sl = slice(grp * LANES, (grp + 1) * LANES)
            q_ref[rs, sl] = _rope(q[:, sl], cos, sin, first_half).astype(q_ref.dtype)
        k = jnp.dot(h, w_ref[:, c1:c2], preferred_element_type=F32)
        k_ref[rs, :] = _rope(k, cos, sin, first_half)
        v_ref[rs, :] = jnp.dot(h, w_ref[:, c2:c3], preferred_element_type=F32)
        gr_ref[rs, :] = jax.nn.sigmoid(jnp.dot(h, w_ref[:, c3:c4], preferred_element_type=F32)).astype(gr_ref.dtype)
        ga_ref[rs, :] = jax.nn.sigmoid(jnp.dot(h, w_ref[:, c4:IN_PROJ], preferred_element_type=F32)).astype(ga_ref.dtype)

    _staggered([project, rwkv_operands, attention_operands], nsub)


def _prep_consts(P):
    return [P['mu'], P['w0'], P['w2p'], P['a0'], P['a2p'], P['g2'], P['k_k'], P['k_a'], P['r_k'], P['ones_bd']]


def _prep_const_specs(layer):
    d = RWKV_DIM
    shapes = [(1, RWKV_PROJ), (1, d), (LANES, d), (1, d), (LANES, d), (D_GATE_LORA, d), (1, d), (1, d), (1, d)]
    return [_layer_spec(s, layer) for s in shapes] + [_const_spec((d, d))]


def _front_out_types(n, tail_shape):
    d = RWKV_DIM
    widths = [(ATT_DIM, BF16), (ATT_KV_DIM, F32), (ATT_KV_DIM, F32), (D_MODEL, BF16), (D_MODEL, BF16)]
    prep = [(d, BF16), (d, F32), (d, BF16), (d, BF16), (d, BF16), (d, BF16), (d, BF16), (d, BF16)]
    shapes = ([jax.ShapeDtypeStruct((n, w), t) for w, t in widths] + [jax.ShapeDtypeStruct(tail_shape, F32)]
              + [jax.ShapeDtypeStruct((n, w), t) for w, t in prep])
    return shapes, [w for w, _ in widths], [w for w, _ in prep]


def _front_prompt(x2d, P, cos_t, sin_t, batch, tm):
    n = x2d.shape[0]
    layer = P['layer']
    nt = n // batch // tm
    row = lambda b, t: (b * nt + t, 0)
    pos = lambda b, t: (t, 0)
    shapes, widths, prep = _front_out_types(n, (batch, SUBLANES, RWKV_PROJ))
    return pl.pallas_call(
        functools.partial(_front_kernel, sample=False, nsub=tm // 128),
        grid=(batch, nt),
        in_specs=[pl.BlockSpec((tm, D_MODEL), row), _layer_spec((1, D_MODEL), layer),
                  _layer_spec((D_MODEL, IN_PROJ), layer),
                  pl.BlockSpec((tm, LANES), pos), pl.BlockSpec((tm, LANES), pos)] + _prep_const_specs(layer),
        out_specs=([pl.BlockSpec((tm, w), row) for w in widths]
                   + [pl.BlockSpec((1, SUBLANES, RWKV_PROJ), lambda b, t: (b, 0, 0))]
                   + [pl.BlockSpec((tm, w), row) for w in prep]),
        out_shape=shapes,
        scratch_shapes=[pltpu.VMEM((SUBLANES, RWKV_PROJ), F32)],
        compiler_params=_params(("arbitrary", "arbitrary")),
        name="front_prompt",
    )(x2d, P['norm_mix_g'], P['w_in'], cos_t, sin_t, *_prep_consts(P))


def _front_sample(x2d, shift_prev, P, cos_t, sin_t):
    n = x2d.shape[0]
    layer = P['layer']
    nb = shift_prev.shape[1]
    shapes, _, _ = _front_out_types(n, (nb, RWKV_PROJ))
    return pl.pallas_call(
        functools.partial(_front_kernel, sample=True, nsub=n // nb),
        grid=(1,),
        in_specs=[_const_spec((n, D_MODEL)), _layer_spec((nb, RWKV_PROJ), layer), _layer_spec((1, D_MODEL), layer),
                  _layer_spec((D_MODEL, IN_PROJ), layer), _const_spec((n, LANES)), _const_spec((n, LANES))]
        + _prep_const_specs(layer),
        out_specs=[_const_spec(sh.shape) for sh in shapes],
        out_shape=shapes,
        compiler_params=_params(("arbitrary",)),
        name="front_sample",
    )(x2d, shift_prev, P['norm_mix_g'], P['w_in'], cos_t, sin_t, *_prep_consts(P))


def _wkv_prompt_kernel(r_ref, lw_ref, kk_ref, b_ref, kf_ref, v_ref, y_ref, st_ref, s_scr, *, chunks, together):
    C, G, M = CHUNK, HEADS_PER_GROUP, MXU_DIM
    nbatch = r_ref.shape[0]
    ngrp = RWKV_HEADS // G

    @pl.when(pl.program_id(0) == 0)
    def _():
        s_scr[...] = jnp.zeros_like(s_scr)

    ti = lax.broadcasted_iota(jnp.int32, (C, M), 0)
    si = lax.broadcasted_iota(jnp.int32, (C, M), 1) % C
    strict = ti > si
    incl = ti >= si
    eye = (ti == si).astype(F32)
    tri = (lax.broadcasted_iota(jnp.int32, (C, C), 0) >= lax.broadcasted_iota(jnp.int32, (C, C), 1)).astype(F32)
    lane_head = lax.broadcasted_iota(jnp.int32, (1, M), 1) // HEAD_DIM
    same_head = (lax.broadcasted_iota(jnp.int32, (M, M), 0) // HEAD_DIM
                 == lax.broadcasted_iota(jnp.int32, (M, M), 1) // HEAD_DIM)

    def stack(x):
        xb = x.astype(BF16)
        return jnp.concatenate([jnp.where(lane_head == h, xb, jnp.zeros_like(xb)) for h in range(G)], axis=0)

    def rows2(top, bottom):
        return jnp.concatenate([top.astype(BF16), bottom.astype(BF16)], axis=0)

    def load(rows):
        probs = []
        for bi in range(nbatch):
            lw = lw_ref[bi, rows, :]
            cum = _mm_f32(tri, lw)
            w_end = jnp.exp(cum[C - 1:C, :])
            e_neg = jnp.exp(-cum)
            a_hat = -kk_ref[bi, rows, :] * jnp.exp(cum - lw)
            r_hat = r_ref[bi, rows, :] * jnp.exp(cum)
            b_til = b_ref[bi, rows, :] * e_neg
            k_til = kf_ref[bi, rows, :] * e_neg
            v = v_ref[bi, rows, :]
            for grp in range(ngrp):
                ls = slice(grp * M, (grp + 1) * M)
                we = w_end[:, ls]
                probs.append(dict(
                    bi=bi, grp=grp, ls=ls, we=we, rows=rows, ar=rows2(a_hat[:, ls], r_hat[:, ls]),
                    b=stack(b_til[:, ls]), k=stack(k_til[:, ls]), v=v[:, ls], vs=stack(v[:, ls]),
                    bk=rows2(b_til[:, ls] * we, k_til[:, ls] * we)))
        return probs

    def state_free(probs):
        mb = [_mm_nt(p['ar'], p['b']) for p in probs]
        mk = [_mm_nt(p['ar'], p['k']) for p in probs]
        l_ab = [jnp.where(strict, x[0:C], 0.0) for x in mb]
        p_rb = [jnp.where(incl, x[C:2 * C], 0.0).astype(BF16) for x in mb]
        lp_k = [rows2(jnp.where(strict, x[0:C], 0.0), jnp.where(incl, x[C:2 * C], 0.0)) for x in mk]
        t_inv = [eye + x for x in l_ab]
        pw = [_mm(x, stack(x)) for x in l_ab]
        for _ in range(C.bit_length() - 3):
            both = [_mm(rows2(t, x), stack(x)) for t, x in zip(t_inv, pw)]
            t_inv = [t + x[0:C] for t, x in zip(t_inv, both)]
            pw = [x[C:2 * C] for x in both]
        t_inv = [t + _mm(t, stack(x)) for t, x in zip(t_inv, pw)]
        e0 = [_mm(x, p['vs']) for x, p in zip(lp_k, probs)]
        return t_inv, p_rb, e0

    def advance(probs, t_inv, p_rb, e0):
        s0 = [s_scr[p['bi'], p['grp']] for p in probs]
        d0 = [_mm_nt(p['ar'], x) for p, x in zip(probs, s0)]
        u = [_mm(t, stack(x[0:C] + y[0:C])) for t, x, y in zip(t_inv, d0, e0)]
        for p, x, y, w, q, st in zip(probs, d0, e0, u, p_rb, s0):
            y_ref[p['bi'], p['rows'], p['ls']] = x[C:2 * C] + y[C:2 * C] + _mm(q, stack(w))
            upd = _mm_tn(rows2(w, p['v']), p['bk'])
            s_scr[p['bi'], p['grp']] = st * p['we'] + jnp.where(same_head, upd, 0.0)

    def chunk_group(c, carry):
        chunk_probs = [load(pl.ds(pl.multiple_of((c * together + j) * C, C), C)) for j in range(together)]
        t_inv, p_rb, e0 = state_free([p for probs in chunk_probs for p in probs])
        n = len(chunk_probs[0])
        for j, probs in enumerate(chunk_probs):
            sl = slice(j * n, (j + 1) * n)
            advance(probs, t_inv[sl], p_rb[sl], e0[sl])
        return carry

    lax.fori_loop(0, chunks // together, chunk_group, 0)

    @pl.when(pl.program_id(0) == pl.num_programs(0) - 1)
    def _():
        st_ref[...] = s_scr[...]


def _wkv_prompt(r, lw, kk, b, kf, v, batch, tl):
    n = r.shape[0]
    seq = n // batch
    ngrp = RWKV_HEADS // HEADS_PER_GROUP
    blk = pl.BlockSpec((batch, tl, RWKV_DIM), lambda t: (0, t, 0))
    y, st = pl.pallas_call(
        functools.partial(_wkv_prompt_kernel, chunks=tl // CHUNK, together=2),
        grid=(seq // tl,),
        in_specs=[blk] * 6,
        out_specs=[blk, _const_spec((batch, ngrp, MXU_DIM, MXU_DIM))],
        out_shape=[jax.ShapeDtypeStruct((batch, seq, RWKV_DIM), F32),
                   jax.ShapeDtypeStruct((batch, ngrp, MXU_DIM, MXU_DIM), F32)],
        scratch_shapes=[pltpu.VMEM((batch, ngrp, MXU_DIM, MXU_DIM), F32)],
        compiler_params=_params(("arbitrary",)),
        name="wkv_prompt",
    )(*(t.reshape(batch, seq, RWKV_DIM) for t in (r, lw, kk, b, kf, v)))
    st = st.reshape(batch, ngrp, HEADS_PER_GROUP, HEAD_DIM, HEADS_PER_GROUP, HEAD_DIM)
    st = jnp.stack([st[:, :, h, :, h, :] for h in range(HEADS_PER_GROUP)], axis=2)
    st = st.reshape(batch, RWKV_HEADS, HEAD_DIM, HEAD_DIM)
    return y.reshape(n, RWKV_DIM), st


def _wkv_sample_kernel(r_ref, lw_ref, kk_ref, b_ref, kf_ref, v_ref, s_ref, y_ref, so_ref, *, steps):
    group = SUBLANES // 2

    def value_rows(gi, carry):
        vis = [gi * group + j for j in range(group)]
        ss = [s_ref[0, vi] for vi in vis]
        for t in range(steps):
            kk, w, b = kk_ref[t, 0].astype(F32), jnp.exp(lw_ref[t, 0]), b_ref[t, 0].astype(F32)
            kf, r = kf_ref[t, 0].astype(F32), r_ref[t, 0].astype(F32)
            sks = [jnp.sum(s * kk, axis=0, keepdims=True) for s in ss]
            ss = [s * w - sk * b + v_ref[t, 0, pl.ds(vi, 1), :] * kf for s, sk, vi in zip(ss, sks, vis)]
            for s, vi in zip(ss, vis):
                y_ref[t, 0, pl.ds(vi, 1), :] = jnp.sum(s * r, axis=0, keepdims=True)
        for s, vi in zip(ss, vis):
            so_ref[0, vi] = s
        return carry

    lax.fori_loop(0, HEAD_DIM // group, value_rows, 0)


def _wkv_sample(r, lw, kk, b, kf, v, s0, layer, steps):
    nb = r.shape[-1]
    vec = pl.BlockSpec((steps, 1, HEAD_DIM, nb), lambda h: (0, h, 0, 0))
    st = pl.BlockSpec((1, HEAD_DIM, HEAD_DIM, nb), lambda h: (h, 0, 0, 0))
    st_in = pl.BlockSpec((None, 1, HEAD_DIM, HEAD_DIM, nb), lambda h: (layer, h, 0, 0, 0))
    return pl.pallas_call(
        functools.partial(_wkv_sample_kernel, steps=steps),
        grid=(RWKV_HEADS,),
        in_specs=[vec] * 6 + [st_in],
        out_specs=[vec, st],
        out_shape=[jax.ShapeDtypeStruct(r.shape, F32), jax.ShapeDtypeStruct(s0.shape[1:], F32)],
        compiler_params=_params(("arbitrary",)),
        name="wkv_sample",
    )(r, lw, kk, b, kf, v, s0)


def _swa_prompt_kernel(q_ref, kc_ref, kp_ref, vc_ref, vp_ref, sink_ref, o_ref, *, sub):
    blk = WINDOW
    first = pl.program_id(1) == 0
    k_all = jnp.concatenate([kp_ref[...], kc_ref[...]], axis=0)
    v_all = jnp.concatenate([vp_ref[...], vc_ref[...]], axis=0).astype(BF16)
    v_ext = jnp.concatenate([v_all, jnp.ones_like(v_all)], axis=1)
    k_swapped = pltpu.roll(k_all, HEAD_DIM, 1)
    lane = lax.broadcasted_iota(jnp.int32, (1, LANES), 1)
    low = lane < HEAD_DIM
    k_var = {(0, 0): jnp.where(low, k_all, 0.0), (0, 1): jnp.where(low, 0.0, k_swapped),
             (1, 1): jnp.where(low, 0.0, k_all), (1, 0): jnp.where(low, k_swapped, 0.0)}
    k_var = {key: val.astype(BF16) for key, val in k_var.items()}
    order = [j * ATT_GROUP + g for j in range(ATT_KV_HEADS) for half in range(2)
             for g in range(ATT_GROUP) if g % 2 == half]
    n_rows = ATT_Q_HEADS * blk
    qi = lax.broadcasted_iota(jnp.int32, (n_rows, 2 * blk), 0) % blk
    kj = lax.broadcasted_iota(jnp.int32, (n_rows, 2 * blk), 1)
    diff = blk + qi - kj
    band = (diff >= 0) & (diff < WINDOW)
    row_head = lax.broadcasted_iota(jnp.int32, (n_rows, 1), 0) // blk
    sink = jnp.zeros((n_rows, 1), F32)
    for idx, head in enumerate(order):
        sink = jnp.where(row_head == idx, sink_ref[0:1, head:head + 1], sink)
    scale = HEAD_DIM ** -0.5
    scores, maxes, weights, sums = {}, {}, {}, {}

    def score(sb):
        qs = slice(sb * blk, (sb + 1) * blk)
        ks = slice(sb * blk, (sb + 2) * blk)
        parts = []
        for j in range(ATT_KV_HEADS):
            for half in range(2):
                heads = [j * ATT_GROUP + g for g in range(ATT_GROUP) if g % 2 == half]
                q_cat = jnp.concatenate([q_ref[qs, (h // 2) * LANES:(h // 2 + 1) * LANES] for h in heads], axis=0)
                parts.append(_mm_nt(q_cat * scale, k_var[(j, half)][ks]))
        s = jnp.concatenate(parts, axis=0)
        ok = band & jnp.logical_or(kj >= blk, jnp.logical_not(first)) if sb == 0 else band
        scores[sb] = jnp.where(ok, s, -jnp.inf)

    def exponentiate(sb):
        maxes[sb] = jnp.maximum(jnp.max(scores[sb], axis=-1, keepdims=True), sink)
        weights[sb] = jnp.exp(scores[sb] - maxes[sb]).astype(BF16)

    def weigh(sb):
        ks = slice(sb * blk, (sb + 2) * blk)
        sums[sb] = jnp.dot(weights[sb], v_ext[ks], preferred_element_type=F32)

    def emit(sb):
        qs = slice(sb * blk, (sb + 1) * blk)
        ev = sums[sb]
        o = ev[:, 0:LANES] / (ev[:, LANES:LANES + 1] + jnp.exp(sink - maxes[sb]))
        by_head = {}
        for idx, head in enumerate(order):
            o_h = o[idx * blk:(idx + 1) * blk]
            by_head[head] = pltpu.roll(o_h, HEAD_DIM, 1) if head // ATT_GROUP != head % 2 else o_h
        for grp in range(ATT_DIM // LANES):
            o_ref[qs, grp * LANES:(grp + 1) * LANES] = jnp.where(low, by_head[2 * grp],
                                                                 by_head[2 * grp + 1]).astype(o_ref.dtype)

    _staggered([score, exponentiate, weigh, emit], sub)


def _swa_prompt(q, k, v, P, batch, sub):
    n = q.shape[0]
    rows = sub * WINDOW
    nb = n // batch // rows
    cur = lambda b, i: (b * nb + i, 0)
    prev = lambda b, i: ((b * nb + i) * sub - jnp.minimum(i, 1), 0)
    return pl.pallas_call(
        functools.partial(_swa_prompt_kernel, sub=sub),
        grid=(batch, nb),
        in_specs=[pl.BlockSpec((rows, ATT_DIM), cur),
                  pl.BlockSpec((rows, ATT_KV_DIM), cur), pl.BlockSpec((WINDOW, ATT_KV_DIM), prev),
                  pl.BlockSpec((rows, ATT_KV_DIM), cur), pl.BlockSpec((WINDOW, ATT_KV_DIM), prev),
                  _layer_spec((1, ATT_Q_HEADS), P['layer'])],
        out_specs=pl.BlockSpec((rows, ATT_DIM), cur),
        out_shape=jax.ShapeDtypeStruct((n, ATT_DIM), BF16),
        compiler_params=_params(("arbitrary", "arbitrary")),
        name="swa_prompt",
    )(q, k, k, v, v, P['sinks'])


def _swa_sample_kernel(q_ref, kc_ref, vc_ref, kn_ref, vn_ref, sink_ref, o_ref, *, steps):
    q = q_ref[...].astype(BF16)
    rows = q.shape[1]
    nk = kc_ref.shape[1]
    nn = kn_ref.shape[1]
    bdot = lambda a, b, dims: lax.dot_general(a, b, (dims, ((0,), (0,))), preferred_element_type=F32)
    s_c = bdot(q, kc_ref[...].astype(BF16), ((2,), (2,))) * (HEAD_DIM ** -0.5)
    s_n = bdot(q, kn_ref[...].astype(BF16), ((2,), (2,))) * (HEAD_DIM ** -0.5)
    t_c = lax.broadcasted_iota(jnp.int32, (1, rows, nk), 1) % steps
    w_c = lax.broadcasted_iota(jnp.int32, (1, rows, nk), 2)
    s_c = jnp.where(w_c > t_c, s_c, -jnp.inf)
    t_n = lax.broadcasted_iota(jnp.int32, (1, rows, nn), 1) % steps
    w_n = lax.broadcasted_iota(jnp.int32, (1, rows, nn), 2)
    s_n = jnp.where(w_n <= t_n, s_n, -jnp.inf)
    sink = sink_ref[...][None, :, 0:1]
    m = jnp.maximum(jnp.maximum(jnp.max(s_c, axis=-1, keepdims=True), jnp.max(s_n, axis=-1, keepdims=True)), sink)
    e_c = jnp.exp(s_c - m)
    e_n = jnp.exp(s_n - m)
    den = jnp.sum(e_c, axis=-1, keepdims=True) + jnp.sum(e_n, axis=-1, keepdims=True) + jnp.exp(sink - m)
    o = (bdot((e_c / den).astype(BF16), vc_ref[...].astype(BF16), ((2,), (1,)))
         + bdot((e_n / den).astype(BF16), vn_ref[...].astype(BF16), ((2,), (1,))))
    o_ref[...] = o


def _swa_sample(q_st, k_cache, v_cache, k_new, v_new, sink_rows, layer, steps, bt):
    nb, rows, _ = q_st.shape
    nn = k_new.shape[1]
    blk = lambda r: pl.BlockSpec((bt, r, LANES), lambda i: (i, 0, 0))
    cache = pl.BlockSpec((None, bt, WINDOW, LANES), lambda i: (layer, i, 0, 0))
    return pl.pallas_call(
        functools.partial(_swa_sample_kernel, steps=steps),
        grid=(nb // bt,),
        in_specs=[blk(rows), cache, cache, blk(nn), blk(nn), _layer_spec((rows, LANES), layer)],
        out_specs=blk(rows),
        out_shape=jax.ShapeDtypeStruct((nb, rows, LANES), F32),
        compiler_params=_params(("arbitrary",)),
        name="swa_sample",
    )(q_st, k_cache, v_cache, k_new, v_new, sink_rows)


def _gelu(x):
    return 0.5 * x * (1.0 + lax.erf(x * (2.0 ** -0.5)))


def _back_kernel(*refs, final, sample, nsub):
    x_ref, y_ref, bonus_ref, g_ref, oa_ref, gr_ref, ga_ref = refs[:7]
    at = 7
    if sample:
        cp_ref = refs[at]
        at += 1
    (lng_ref, lnb_ref, ones_ref, wa_ref, wb_ref, wo_ref, gn_ref, wi_ref, cw_ref, cb_ref, wd_ref,
     gf_ref) = refs[at:at + 12]
    o_ref, ct_ref, carry = refs[at + 12:at + 15]
    blocks, rb = _row_blocks(x_ref.shape[0], nsub)
    ones_bd = ones_ref[...]

    if sample:
        @pl.when(pl.program_id(0) == 0)
        def _():
            carry[...] = cp_ref[...]
    else:
        @pl.when(pl.program_id(1) == 0)
        def _():
            carry[...] = jnp.zeros_like(carry)

    row = lax.broadcasted_iota(jnp.int32, (rb, 1), 0)
    xs, cs, ups, acts = {}, {}, {}, {}

    def mix(i):
        rs = blocks[i]
        y = y_ref[rs, :]
        mean = _head_sum(y, ones_bd) * (1.0 / HEAD_DIM)
        d = y - mean
        var = _head_sum(d * d, ones_bd) * (1.0 / HEAD_DIM)
        yn = d * lax.rsqrt(var + GN_EPS) * lng_ref[...] + lnb_ref[...]
        o_r = (yn + bonus_ref[rs, :]) * g_ref[rs, :]
        merged = gr_ref[rs, :] * _mm(o_r, wa_ref[...]) + ga_ref[rs, :] * _mm(oa_ref[rs, :], wb_ref[...])
        xs[i] = x_ref[rs, :] + _mm(merged, wo_ref[...])

    def up_project(i):
        h = _rmsnorm(xs[i], gn_ref[...]).astype(BF16)
        cs[i] = jnp.dot(h, wi_ref[:, 0:D_FF], preferred_element_type=F32)
        ups[i] = jnp.dot(h, wi_ref[:, D_FF:2 * D_FF], preferred_element_type=F32)

    def conv_gate(i):
        c = cs[i]
        if sample:
            c2, c1 = carry[0], carry[1]
            carry[0] = c1
            carry[1] = c
            ct_ref[0] = c
        else:
            tail = carry[...] if i == 0 else cs[i - 1][rb - SUBLANES:, :]
            last = tail[SUBLANES - 1:SUBLANES, :]
            c1 = jnp.where(row == 0, last, pltpu.roll(c, 1, 0))
            c2 = jnp.where(row == 0, tail[SUBLANES - 2:SUBLANES - 1, :],
                           jnp.where(row == 1, last, pltpu.roll(c, 2, 0)))
            if i == nsub - 1:
                carry[...] = c[rb - SUBLANES:, :]
                ct_ref[0] = c[rb - SUBLANES:, :]
        conv = cb_ref[...] + c2 * cw_ref[0:1, :] + c1 * cw_ref[1:2, :] + c * cw_ref[2:3, :]
        acts[i] = (_gelu(conv) * ups[i]).astype(BF16)

    def down_project(i):
        out = xs[i] + jnp.dot(acts[i], wd_ref[...], preferred_element_type=F32)
        o_ref[blocks[i], :] = _rmsnorm(out, gf_ref[...]) if final else out

    _staggered([mix, up_project, conv_gate, down_project], nsub)


def _back_weight_specs(layer):
    return [_layer_spec((1, RWKV_DIM), layer), _layer_spec((1, RWKV_DIM), layer), _const_spec((RWKV_DIM, RWKV_DIM)),
            _layer_spec((RWKV_DIM, D_MODEL), layer), _layer_spec((ATT_DIM, D_MODEL), layer),
            _layer_spec((D_MODEL, D_MODEL), layer),
            _layer_spec((1, D_MODEL), layer), _layer_spec((D_MODEL, 2 * D_FF), layer),
            _layer_spec((CONV_W, D_FF), layer), _layer_spec((1, D_FF), layer), _layer_spec((D_FF, D_MODEL), layer),
            _const_spec((1, D_MODEL))]


def _back_weights(P, g_final):
    return [P['ln_g'], P['ln_b'], P['ones_bd'], P['w_br_rwkv'], P['w_br_attn'], P['w_out'],
            P['norm_ffn_g'], P['ffn_w_in'], P['ffn_conv_w'], P['ffn_conv_b'], P['ffn_w_down'], g_final]


def _back_prompt(x, y, bonus, g, o_a, gr, ga, P, g_final, final, batch, tm):
    n = x.shape[0]
    nt = n // batch // tm
    row = lambda b, t: (b * nt + t, 0)
    wide = pl.BlockSpec((tm, D_MODEL), row)
    half = pl.BlockSpec((tm, RWKV_DIM), row)
    return pl.pallas_call(
        functools.partial(_back_kernel, final=final, sample=False, nsub=tm // 256),
        grid=(batch, nt),
        in_specs=[wide, half, half, half, half, wide, wide] + _back_weight_specs(P['layer']),
        out_specs=[wide, pl.BlockSpec((1, SUBLANES, D_FF), lambda b, t: (b, 0, 0))],
        out_shape=[jax.ShapeDtypeStruct((n, D_MODEL), F32), jax.ShapeDtypeStruct((batch, SUBLANES, D_FF), F32)],
        scratch_shapes=[pltpu.VMEM((SUBLANES, D_FF), F32)],
        compiler_params=_params(("arbitrary", "arbitrary")),
        name="back_prompt",
    )(x, y, bonus, g, o_a, gr, ga, *_back_weights(P, g_final))


def _back_sample(x, y, bonus, g, o_a, gr, ga, conv_prev, P, g_final, final):
    n = x.shape[0]
    nb = conv_prev.shape[2]
    steps = n // nb
    taps = CONV_W - 1
    row = lambda t: (t, 0)
    wide = pl.BlockSpec((nb, D_MODEL), row)
    half = pl.BlockSpec((nb, RWKV_DIM), row)
    return pl.pallas_call(
        functools.partial(_back_kernel, final=final, sample=True, nsub=1),
        grid=(steps,),
        in_specs=[wide, half, half, half, half, wide, wide, _layer_spec((taps, nb, D_FF), P['layer'])]
        + _back_weight_specs(P['layer']),
        out_specs=[wide, pl.BlockSpec((1, nb, D_FF), lambda t: (jnp.maximum(t - (steps - taps), 0), 0, 0))],
        out_shape=[jax.ShapeDtypeStruct((n, D_MODEL), F32), jax.ShapeDtypeStruct((taps, nb, D_FF), F32)],
        scratch_shapes=[pltpu.VMEM((taps, nb, D_FF), F32)],
        compiler_params=_params(("arbitrary",)),
        name="back_sample",
    )(x, y, bonus, g, o_a, gr, ga, conv_prev, *_back_weights(P, g_final))


def _rope_tables(pos):
    inv = ROPE_THETA ** (-jnp.arange(0, HEAD_DIM, 2, dtype=F32) / HEAD_DIM)
    ang = pos.astype(F32)[:, None] * inv[None, :]
    cos, sin = jnp.cos(ang), jnp.sin(ang)
    reps = LANES // HEAD_DIM
    return jnp.tile(jnp.concatenate([cos, cos], axis=1), (1, reps)), jnp.tile(jnp.concatenate([-sin, sin], axis=1),
                                                                             (1, reps))


def _stacked_params(norm_mix_g, w_in, rwkv_mu, rwkv_w0, rwkv_w2, rwkv_a0, rwkv_a2, rwkv_g2, rwkv_k_k, rwkv_k_a,
                    rwkv_r_k, rwkv_ln_g, rwkv_ln_b, attn_sinks, w_br_rwkv, w_br_attn, w_out, norm_ffn_g, ffn_w_in,
                    ffn_conv_w, ffn_conv_b, ffn_w_down, steps):
    d = RWKV_DIM
    depth = w_in.shape[0]
    head = jnp.arange(d) // HEAD_DIM
    row = lambda t: t.reshape(depth, 1, -1)
    sink_rows = jnp.broadcast_to(jnp.repeat(attn_sinks, steps, axis=1)[:, :, None],
                                 (depth, ATT_Q_HEADS * steps, LANES))
    return {
        'norm_mix_g': row(norm_mix_g), 'w_in': w_in.astype(BF16),
        'mu': row(rwkv_mu), 'w0': row(rwkv_w0),
        'w2p': jnp.pad(rwkv_w2, ((0, 0), (0, D_AAA_LORA), (0, 0))).astype(BF16),
        'a0': row(rwkv_a0),
        'a2p': jnp.pad(rwkv_a2, ((0, 0), (D_DECAY_LORA, 0), (0, 0))).astype(BF16),
        'g2': rwkv_g2.astype(BF16), 'k_k': row(rwkv_k_k), 'k_a': row(rwkv_k_a),
        'r_k': row(rwkv_r_k), 'ln_g': row(rwkv_ln_g), 'ln_b': row(rwkv_ln_b),
        'ones_bd': (head[:, None] == head[None, :]).astype(BF16),
        'sinks': row(attn_sinks), 'sink_rows': sink_rows,
        'w_br_rwkv': w_br_rwkv.astype(BF16), 'w_br_attn': w_br_attn.astype(BF16),
        'w_out': w_out.astype(BF16), 'norm_ffn_g': row(norm_ffn_g),
        'ffn_w_in': ffn_w_in.astype(BF16), 'ffn_conv_w': ffn_conv_w, 'ffn_conv_b': row(ffn_conv_b),
        'ffn_w_down': ffn_w_down.astype(BF16),
    }


def _prompt_layer(x, P, tables, g_final, final, batch, seq):
    tm = min(512, seq)
    q, k, v, gr, ga, z_tail, r, lw, kk, b, kf, vv, g, bonus = _front_prompt(x, P, tables[0], tables[1], batch,
                                                                            min(1024, seq))
    y, s_t = _wkv_prompt(r, lw, kk, b, kf, vv, batch, min(512, seq))
    o_a = _swa_prompt(q, k, v, P, batch, min(4, seq // WINDOW))
    x2, c_tail = _back_prompt(x, y, bonus, g, o_a, gr, ga, P, g_final, final, batch, tm)
    last = lambda t: t.reshape(batch, seq, -1)[:, -WINDOW:].reshape(batch, WINDOW, ATT_KV_HEADS, HEAD_DIM)
    state = (z_tail[:, SUBLANES - 1], s_t, last(k), last(v), c_tail[:, SUBLANES - (CONV_W - 1):])
    return x2, state


def _sample_layer(x, P, tables, g_final, final, shift_prev, wkv0, k_cache, v_cache, conv_prev, nb, steps):
    n = nb * steps
    layer = P['layer']
    q, k, v, gr, ga, z_tail, r, lw, kk, b, kf, vv, g, bonus = _front_sample(x, shift_prev, P, tables[0], tables[1])
    to_lanes = lambda t: t.reshape(steps, nb, RWKV_HEADS, HEAD_DIM).transpose(0, 2, 3, 1)
    y_l, s_l = _wkv_sample(*(to_lanes(t) for t in (r, lw, kk, b, kf)), to_lanes(vv).astype(F32), wkv0, layer, steps)
    y = y_l.transpose(0, 3, 1, 2).reshape(n, RWKV_DIM)

    rows = ATT_Q_HEADS * steps
    q5 = q.reshape(steps, nb, ATT_KV_HEADS, ATT_GROUP, HEAD_DIM).transpose(1, 2, 3, 0, 4)
    sel = jnp.eye(ATT_KV_HEADS, dtype=q.dtype)
    q_st = (q5[:, :, :, :, None, :] * sel[None, :, None, None, :, None]).reshape(nb, rows, ATT_KV_DIM)
    pad = ((0, 0), (0, SUBLANES - steps), (0, 0))
    k_new = k.reshape(steps, nb, ATT_KV_DIM).transpose(1, 0, 2)
    v_new = v.reshape(steps, nb, ATT_KV_DIM).transpose(1, 0, 2)
    o_st = _swa_sample(q_st, k_cache, v_cache, jnp.pad(k_new, pad), jnp.pad(v_new, pad), P['sink_rows'], layer,
                       steps, 16)
    o5 = o_st.reshape(nb, ATT_KV_HEADS, ATT_GROUP, steps, ATT_KV_HEADS, HEAD_DIM)
    o_a = jnp.stack([o5[:, j, :, :, j, :] for j in range(ATT_KV_HEADS)], axis=0)
    o_a = o_a.transpose(3, 1, 0, 2, 4).reshape(n, ATT_DIM)

    x2, c_tail = _back_sample(x, y, bonus, g, o_a, gr, ga, conv_prev, P, g_final, final)
    state = (z_tail, s_l, k_new, v_new, c_tail)
    return x2, state


def kernel(x_prompt, x_sample, state_rwkv_shift, state_rwkv_wkv, cache_swa_k, cache_swa_v, state_ffn_conv, norm_mix_g, w_in, rwkv_mu, rwkv_w0, rwkv_w2, rwkv_a0, rwkv_a2, rwkv_g2, rwkv_k_k, rwkv_k_a, rwkv_r_k, rwkv_ln_g, rwkv_ln_b, attn_sinks, w_br_rwkv, w_br_attn, w_out, norm_ffn_g, ffn_w_in, ffn_conv_w, ffn_conv_b, ffn_w_down, norm_final_g):
    bp, tp, _ = x_prompt.shape
    nb, steps, _ = x_sample.shape
    depth = w_in.shape[0]
    tab_p = _rope_tables(jnp.arange(tp, dtype=jnp.int32))
    tab_s = _rope_tables(PAST_LEN + jnp.repeat(jnp.arange(steps, dtype=jnp.int32), nb))
    g_final = norm_final_g[None]
    params = _stacked_params(norm_mix_g, w_in, rwkv_mu, rwkv_w0, rwkv_w2, rwkv_a0, rwkv_a2, rwkv_g2, rwkv_k_k,
                             rwkv_k_a, rwkv_r_k, rwkv_ln_g, rwkv_ln_b, attn_sinks, w_br_rwkv, w_br_attn, w_out,
                             norm_ffn_g, ffn_w_in, ffn_conv_w, ffn_conv_b, ffn_w_down, steps)
    wkv0 = state_rwkv_wkv.transpose(0, 2, 3, 4, 1)
    k_cache = cache_swa_k.reshape(depth, nb, WINDOW, ATT_KV_DIM)
    v_cache = cache_swa_v.reshape(depth, nb, WINDOW, ATT_KV_DIM)
    conv_prev = state_ffn_conv.transpose(0, 2, 1, 3)
    xp = x_prompt.reshape(bp * tp, D_MODEL)
    xs = x_sample.transpose(1, 0, 2).reshape(steps * nb, D_MODEL)
    outs_p, outs_s = [], []
    for l in range(depth):
        P = dict(params, layer=l)
        final = l == depth - 1
        xp, sp = _prompt_layer(xp, P, tab_p, g_final, final, bp, tp)
        xs, ss = _sample_layer(xs, P, tab_s, g_final, final, state_rwkv_shift, wkv0, k_cache, v_cache, conv_prev,
                               nb, steps)
        outs_p.append(sp)
        outs_s.append(ss)
    y_prompt = xp.reshape(bp, tp, D_MODEL)
    y_sample = xs.reshape(steps, nb, D_MODEL).transpose(1, 0, 2)
    stack = lambda outs, i: jnp.stack([o[i] for o in outs])
    kv_shape = (depth, nb, steps, ATT_KV_HEADS, HEAD_DIM)
    s_k = jnp.concatenate([cache_swa_k[:, :, steps:], stack(outs_s, 2).reshape(kv_shape)], axis=2)
    s_v = jnp.concatenate([cache_swa_v[:, :, steps:], stack(outs_s, 3).reshape(kv_shape)], axis=2)
    s_wkv = stack(outs_s, 1).transpose(0, 4, 1, 2, 3)
    s_conv = stack(outs_s, 4).transpose(0, 2, 1, 3)
    return (y_prompt, y_sample, *(stack(outs_p, i) for i in range(5)),
            stack(outs_s, 0), s_wkv, s_k, s_v, s_conv)
```

```python
import functools

import jax
import jax.numpy as jnp
from jax import lax
from jax.experimental import pallas as pl
from jax.experimental.pallas import tpu as pltpu

F32 = jnp.float32
BF16 = jnp.bfloat16

D_MODEL = 1024
HEAD_DIM = 64
RWKV_HEADS = 8
RWKV_DIM = RWKV_HEADS * HEAD_DIM
D_DECAY_LORA = 64
D_AAA_LORA = 64
D_GATE_LORA = 128
RWKV_PROJ = 3 * RWKV_DIM + D_DECAY_LORA + D_AAA_LORA + D_GATE_LORA
ATT_Q_HEADS = 8
ATT_KV_HEADS = 2
ATT_GROUP = ATT_Q_HEADS // ATT_KV_HEADS
ATT_DIM = ATT_Q_HEADS * HEAD_DIM
ATT_KV_DIM = ATT_KV_HEADS * HEAD_DIM
WINDOW = 128
IN_PROJ = RWKV_PROJ + ATT_DIM + 2 * ATT_KV_DIM + 2 * D_MODEL
D_FF = 2816
CONV_W = 3
ROPE_THETA = 10000.0
RMS_EPS = 1e-6
GN_EPS = 64e-5
PAST_LEN = 16384

LANES = 128
SUBLANES = 8
MXU_DIM = 256
CHUNK = 64
HEADS_PER_GROUP = MXU_DIM // HEAD_DIM
VMEM_LIMIT = 56 * 1024 * 1024


def _params(sem):
    return pltpu.CompilerParams(dimension_semantics=sem, vmem_limit_bytes=VMEM_LIMIT)


def _const_spec(shape):
    zeros = (0,) * len(shape)
    return pl.BlockSpec(shape, lambda *_: zeros, pipeline_mode=pl.Buffered(1))


def _layer_spec(shape, layer):
    index = (layer,) + (0,) * len(shape)
    return pl.BlockSpec((None,) + tuple(shape), lambda *_: index, pipeline_mode=pl.Buffered(1))


def _mm(a, b):
    return jnp.dot(a.astype(BF16), b.astype(BF16), preferred_element_type=F32)


def _mm_nt(a, b):
    return lax.dot_general(a.astype(BF16), b.astype(BF16), (((1,), (1,)), ((), ())),
                           preferred_element_type=F32)


def _mm_f32(a, b):
    return jnp.dot(a, b, precision=lax.Precision.HIGHEST, preferred_element_type=F32)


def _mm_tn(a, b):
    return lax.dot_general(a.astype(BF16), b.astype(BF16), (((0,), (0,)), ((), ())),
                           preferred_element_type=F32)


def _head_sum(x, ones_bd):
    return jnp.dot(x.astype(BF16), ones_bd, preferred_element_type=F32)


def _rmsnorm(x, g):
    return x * lax.rsqrt(jnp.mean(x * x, axis=-1, keepdims=True) + RMS_EPS) * g


def _rope(x, cos, sin_signed, first_half):
    partner = jnp.where(first_half, pltpu.roll(x, LANES - HEAD_DIM // 2, 1), pltpu.roll(x, HEAD_DIM // 2, 1))
    return x * cos + partner * sin_signed


def _prep_math(z, zp, mu, w0, w2p, a0, a2p, g2, k_k, k_a, r_k, ones_bd, outs, rs):
    r_ref, lw_ref, kk_ref, b_ref, kf_ref, v_ref, g_ref, bonus_ref = outs
    zs = z + mu * (zp - z)
    d = RWKV_DIM
    r, k, v = zs[:, 0:d], zs[:, d:2 * d], zs[:, 2 * d:3 * d]
    zwa = zs[:, 3 * d:3 * d + LANES]
    zg = zs[:, 3 * d + LANES:3 * d + 2 * LANES]
    w_pre = w0 + _mm(jnp.tanh(zwa), w2p)
    neg = -w_pre
    softplus = jnp.maximum(neg, 0.0) + jnp.log1p(jnp.exp(-jnp.abs(neg)))
    w_log = -softplus - 0.5
    lw_ref[rs, :] = -jnp.exp(w_log)
    a = jax.nn.sigmoid(a0 + _mm(zwa, a2p))
    g_ref[rs, :] = _mm(jax.nn.sigmoid(zg), g2).astype(g_ref.dtype)
    kk0 = k * k_k
    norm = jnp.sqrt(_head_sum(kk0 * kk0, ones_bd))
    kk = kk0 / jnp.maximum(norm, 1e-12)
    kf = k * (1.0 + (a - 1.0) * k_a)
    r_ref[rs, :] = r.astype(r_ref.dtype)
    kk_ref[rs, :] = kk.astype(kk_ref.dtype)
    b_ref[rs, :] = (kk * a).astype(b_ref.dtype)
    kf_ref[rs, :] = kf.astype(kf_ref.dtype)
    v_ref[rs, :] = v.astype(v_ref.dtype)
    bonus_ref[rs, :] = (_head_sum(r * kf * r_k, ones_bd) * v).astype(bonus_ref.dtype)


N_PREP_CONSTS = 10
N_FRONT_OUTS = 14


def _row_blocks(n, nsub):
    rb = n // nsub
    return [slice(i * rb, (i + 1) * rb) for i in range(nsub)], rb


def _staggered(stages, nsub):
    for step in range(nsub + len(stages) - 1):
        for i in range(nsub):
            if 0 <= step - i < len(stages):
                stages[step - i](i)


def _front_kernel(*refs, sample, nsub):
    if sample:
        x_ref, sp_ref, g_ref, w_ref, cos_ref, sin_ref = refs[:6]
        rest = refs[6:]
    else:
        x_ref, g_ref, w_ref, cos_ref, sin_ref = refs[:5]
        rest = refs[5:]
    consts = [c[...] for c in rest[:N_PREP_CONSTS]]
    outs = rest[N_PREP_CONSTS:N_PREP_CONSTS + N_FRONT_OUTS]
    q_ref, k_ref, v_ref, gr_ref, ga_ref, zt_ref = outs[:6]
    blocks, rb = _row_blocks(x_ref.shape[0], nsub)
    c0, c1, c2, c3, c4 = (RWKV_PROJ, RWKV_PROJ + ATT_DIM, RWKV_PROJ + ATT_DIM + ATT_KV_DIM,
                          RWKV_PROJ + ATT_DIM + 2 * ATT_KV_DIM, RWKV_PROJ + ATT_DIM + 2 * ATT_KV_DIM + D_MODEL)
    if not sample:
        carry = rest[N_PREP_CONSTS + N_FRONT_OUTS]

        @pl.when(pl.program_id(1) == 0)
        def _():
            carry[...] = jnp.zeros_like(carry)

    row = lax.broadcasted_iota(jnp.int32, (rb, 1), 0)
    lane = lax.broadcasted_iota(jnp.int32, (1, LANES), 1)
    first_half = (lane % HEAD_DIM) < (HEAD_DIM // 2)
    hs, zs = {}, {}

    def project(i):
        hs[i] = _rmsnorm(x_ref[blocks[i], :], g_ref[...]).astype(BF16)
        zs[i] = jnp.dot(hs[i], w_ref[:, 0:c0], preferred_element_type=F32)

    def rwkv_operands(i):
        z = zs[i]
        if sample:
            zp = sp_ref[...] if i == 0 else zs[i - 1]
            if i == nsub - 1:
                zt_ref[...] = z
        else:
            tail = carry[...] if i == 0 else zs[i - 1][rb - SUBLANES:, :]
            zp = jnp.where(row == 0, tail[SUBLANES - 1:SUBLANES, :], pltpu.roll(z, 1, 0))
            if i == nsub - 1:
                carry[...] = z[rb - SUBLANES:, :]
                zt_ref[0] = z[rb - SUBLANES:, :]
        _prep_math(z, zp, *consts, outs[6:], blocks[i])

    def attention_operands(i):
        rs, h = blocks[i], hs[i]
        cos = cos_ref[rs, :]
        sin = sin_ref[rs, :]
        q = jnp.dot(h, w_ref[:, c0:c1], preferred_element_type=F32)
        for grp in range(ATT_DIM // LANES):
            sl = slice(grp * LANES, (grp + 1) * LANES)
            q_ref[rs, sl] = _rope(q[:, sl], cos, sin, first_half).astype(q_ref.dtype)
        k = jnp.dot(h, w_ref[:, c1:c2], preferred_element_type=F32)
        k_ref[rs, :] = _rope(k, cos, sin, first_half)
        v_ref[rs, :] = jnp.dot(h, w_ref[:, c2:c3], preferred_element_type=F32)
        gr_ref[rs, :] = jax.nn.sigmoid(jnp.dot(h, w_ref[:, c3:c4], preferred_element_type=F32)).astype(gr_ref.dtype)
        ga_ref[rs, :] = jax.nn.sigmoid(jnp.dot(h, w_ref[:, c4:IN_PROJ], preferred_element_type=F32)).astype(ga_ref.dtype)

    _staggered([project, rwkv_operands, attention_operands], nsub)


def _prep_consts(P):
    return [P['mu'], P['w0'], P['w2p'], P['a0'], P['a2p'], P['g2'], P['k_k'], P['k_a'], P['r_k'], P['ones_bd']]


def _prep_const_specs(layer):
    d = RWKV_DIM
    shapes = [(1, RWKV_PROJ), (1, d), (LANES, d), (1, d), (LANES, d), (D_GATE_LORA, d), (1, d), (1, d), (1, d)]
    return [_layer_spec(s, layer) for s in shapes] + [_const_spec((d, d))]


def _front_out_types(n, tail_shape):
    d = RWKV_DIM
    widths = [(ATT_DIM, BF16), (ATT_KV_DIM, F32), (ATT_KV_DIM, F32), (D_MODEL, BF16), (D_MODEL, BF16)]
    prep = [(d, BF16), (d, F32), (d, BF16), (d, BF16), (d, BF16), (d, BF16), (d, BF16), (d, BF16)]
    shapes = ([jax.ShapeDtypeStruct((n, w), t) for w, t in widths] + [jax.ShapeDtypeStruct(tail_shape, F32)]
              + [jax.ShapeDtypeStruct((n, w), t) for w, t in prep])
    return shapes, [w for w, _ in widths], [w for w, _ in prep]


def _front_prompt(x2d, P, cos_t, sin_t, batch, tm):
    n = x2d.shape[0]
    layer = P['layer']
    nt = n // batch // tm
    row = lambda b, t: (b * nt + t, 0)
    pos = lambda b, t: (t, 0)
    shapes, widths, prep = _front_out_types(n, (batch, SUBLANES, RWKV_PROJ))
    return pl.pallas_call(
        functools.partial(_front_kernel, sample=False, nsub=tm // 128),
        grid=(batch, nt),
        in_specs=[pl.BlockSpec((tm, D_MODEL), row), _layer_spec((1, D_MODEL), layer),
                  _layer_spec((D_MODEL, IN_PROJ), layer),
                  pl.BlockSpec((tm, LANES), pos), pl.BlockSpec((tm, LANES), pos)] + _prep_const_specs(layer),
        out_specs=([pl.BlockSpec((tm, w), row) for w in widths]
                   + [pl.BlockSpec((1, SUBLANES, RWKV_PROJ), lambda b, t: (b, 0, 0))]
                   + [pl.BlockSpec((tm, w), row) for w in prep]),
        out_shape=shapes,
        scratch_shapes=[pltpu.VMEM((SUBLANES, RWKV_PROJ), F32)],
        compiler_params=_params(("arbitrary", "arbitrary")),
        name="front_prompt",
    )(x2d, P['norm_mix_g'], P['w_in'], cos_t, sin_t, *_prep_consts(P))


def _front_sample(x2d, shift_prev, P, cos_t, sin_t):
    n = x2d.shape[0]
    layer = P['layer']
    nb = shift_prev.shape[1]
    shapes, _, _ = _front_out_types(n, (nb, RWKV_PROJ))
    return pl.pallas_call(
        functools.partial(_front_kernel, sample=True, nsub=n // nb),
        grid=(1,),
        in_specs=[_const_spec((n, D_MODEL)), _layer_spec((nb, RWKV_PROJ), layer), _layer_spec((1, D_MODEL), layer),
                  _layer_spec((D_MODEL, IN_PROJ), layer), _const_spec((n, LANES)), _const_spec((n, LANES))]
        + _prep_const_specs(layer),
        out_specs=[_const_spec(sh.shape) for sh in shapes],
        out_shape=shapes,
        compiler_params=_params(("arbitrary",)),
        name="front_sample",
    )(x2d, shift_prev, P['norm_mix_g'], P['w_in'], cos_t, sin_t, *_prep_consts(P))


def _wkv_prompt_kernel(r_ref, lw_ref, kk_ref, b_ref, kf_ref, v_ref, y_ref, st_ref, s_scr, *, chunks, together):
    C, G, M = CHUNK, HEADS_PER_GROUP, MXU_DIM
    nbatch = r_ref.shape[0]
    ngrp = RWKV_HEADS // G

    @pl.when(pl.program_id(0) == 0)
    def _():
        s_scr[...] = jnp.zeros_like(s_scr)

    ti = lax.broadcasted_iota(jnp.int32, (C, M), 0)
    si = lax.broadcasted_iota(jnp.int32, (C, M), 1) % C
    strict = ti > si
    incl = ti >= si
    eye = (ti == si).astype(F32)
    tri = (lax.broadcasted_iota(jnp.int32, (C, C), 0) >= lax.broadcasted_iota(jnp.int32, (C, C), 1)).astype(F32)
    lane_head = lax.broadcasted_iota(jnp.int32, (1, M), 1) // HEAD_DIM
    same_head = (lax.broadcasted_iota(jnp.int32, (M, M), 0) // HEAD_DIM
                 == lax.broadcasted_iota(jnp.int32, (M, M), 1) // HEAD_DIM)

    def stack(x):
        xb = x.astype(BF16)
        return jnp.concatenate([jnp.where(lane_head == h, xb, jnp.zeros_like(xb)) for h in range(G)], axis=0)

    def rows2(top, bottom):
        return jnp.concatenate([top.astype(BF16), bottom.astype(BF16)], axis=0)

    def load(rows):
        probs = []
        for bi in range(nbatch):
            lw = lw_ref[bi, rows, :]
            cum = _mm_f32(tri, lw)
            w_end = jnp.exp(cum[C - 1:C, :])
            e_neg = jnp.exp(-cum)
            a_hat = -kk_ref[bi, rows, :] * jnp.exp(cum - lw)
            r_hat = r_ref[bi, rows, :] * jnp.exp(cum)
            b_til = b_ref[bi, rows, :] * e_neg
            k_til = kf_ref[bi, rows, :] * e_neg
            v = v_ref[bi, rows, :]
            for grp in range(ngrp):
                ls = slice(grp * M, (grp + 1) * M)
                we = w_end[:, ls]
                probs.append(dict(
                    bi=bi, grp=grp, ls=ls, we=we, rows=rows, ar=rows2(a_hat[:, ls], r_hat[:, ls]),
                    b=stack(b_til[:, ls]), k=stack(k_til[:, ls]), v=v[:, ls], vs=stack(v[:, ls]),
                    bk=rows2(b_til[:, ls] * we, k_til[:, ls] * we)))
        return probs

    def state_free(probs):
        l_ab, p_rb, lp_k = [], [], []
        for p in probs:
            x = _mm_nt(p['ar'], p['b'])
            l_ab.append(jnp.where(strict, x[0:C], 0.0))
            p_rb.append(jnp.where(incl, x[C:2 * C], 0.0).astype(BF16))
        for p in probs:
            x = _mm_nt(p['ar'], p['k'])
            lp_k.append(rows2(jnp.where(strict, x[0:C], 0.0), jnp.where(incl, x[C:2 * C], 0.0)))
        t_inv = [eye + x for x in l_ab]
        pw = [_mm(x, stack(x)) for x in l_ab]
        for _ in range(C.bit_length() - 3):
            for i, (t, x) in enumerate(zip(t_inv, pw)):
                both = _mm(rows2(t, x), stack(x))
                t_inv[i] = t + both[0:C]
                pw[i] = both[C:2 * C]
        t_inv = [t + _mm(t, stack(x)) for t, x in zip(t_inv, pw)]
        e0 = [_mm(x, p['vs']) for x, p in zip(lp_k, probs)]
        return t_inv, p_rb, e0

    def advance(probs, t_inv, p_rb, e0):
        s0 = [s_scr[p['bi'], p['grp']] for p in probs]
        d0 = [_mm_nt(p['ar'], x) for p, x in zip(probs, s0)]
        u = [_mm(t, stack(x[0:C] + y[0:C])) for t, x, y in zip(t_inv, d0, e0)]
        for p, x, y, w, q, st in zip(probs, d0, e0, u, p_rb, s0):
            y_ref[p['bi'], p['rows'], p['ls']] = x[C:2 * C] + y[C:2 * C] + _mm(q, stack(w))
            upd = _mm_tn(rows2(w, p['v']), p['bk'])
            s_scr[p['bi'], p['grp']] = st * p['we'] + jnp.where(same_head, upd, 0.0)

    def chunk_group(c, carry):
        chunk_probs = [load(pl.ds(pl.multiple_of((c * together + j) * C, C), C)) for j in range(together)]
        t_inv, p_rb, e0 = state_free([p for probs in chunk_probs for p in probs])
        n = len(chunk_probs[0])
        for j, probs in enumerate(chunk_probs):
            sl = slice(j * n, (j + 1) * n)
            advance(probs, t_inv[sl], p_rb[sl], e0[sl])
        return carry

    lax.fori_loop(0, chunks // together, chunk_group, 0)

    @pl.when(pl.program_id(0) == pl.num_programs(0) - 1)
    def _():
        st_ref[...] = s_scr[...]


def _wkv_prompt(r, lw, kk, b, kf, v, batch, tl):
    n = r.shape[0]
    seq = n // batch
    ngrp = RWKV_HEADS // HEADS_PER_GROUP
    blk = pl.BlockSpec((batch, tl, RWKV_DIM), lambda t: (0, t, 0))
    y, st = pl.pallas_call(
        functools.partial(_wkv_prompt_kernel, chunks=tl // CHUNK, together=2),
        grid=(seq // tl,),
        in_specs=[blk] * 6,
        out_specs=[blk, _const_spec((batch, ngrp, MXU_DIM, MXU_DIM))],
        out_shape=[jax.ShapeDtypeStruct((batch, seq, RWKV_DIM), F32),
                   jax.ShapeDtypeStruct((batch, ngrp, MXU_DIM, MXU_DIM), F32)],
        scratch_shapes=[pltpu.VMEM((batch, ngrp, MXU_DIM, MXU_DIM), F32)],
        compiler_params=_params(("arbitrary",)),
        name="wkv_prompt",
    )(*(t.reshape(batch, seq, RWKV_DIM) for t in (r, lw, kk, b, kf, v)))
    st = st.reshape(batch, ngrp, HEADS_PER_GROUP, HEAD_DIM, HEADS_PER_GROUP, HEAD_DIM)
    st = jnp.stack([st[:, :, h, :, h, :] for h in range(HEADS_PER_GROUP)], axis=2)
    st = st.reshape(batch, RWKV_HEADS, HEAD_DIM, HEAD_DIM)
    return y.reshape(n, RWKV_DIM), st


def _wkv_sample_kernel(r_ref, lw_ref, kk_ref, b_ref, kf_ref, v_ref, s_ref, y_ref, so_ref, *, steps):
    group = SUBLANES // 2

    def value_rows(gi, carry):
        vis = [gi * group + j for j in range(group)]
        ss = [s_ref[0, vi] for vi in vis]
        for t in range(steps):
            kk, w, b = kk_ref[t, 0].astype(F32), jnp.exp(lw_ref[t, 0]), b_ref[t, 0].astype(F32)
            kf, r = kf_ref[t, 0].astype(F32), r_ref[t, 0].astype(F32)
            sks = [jnp.sum(s * kk, axis=0, keepdims=True) for s in ss]
            ss = [s * w - sk * b + v_ref[t, 0, pl.ds(vi, 1), :] * kf for s, sk, vi in zip(ss, sks, vis)]
            for s, vi in zip(ss, vis):
                y_ref[t, 0, pl.ds(vi, 1), :] = jnp.sum(s * r, axis=0, keepdims=True)
        for s, vi in zip(ss, vis):
            so_ref[0, vi] = s
        return carry

    lax.fori_loop(0, HEAD_DIM // group, value_rows, 0)


def _wkv_sample(r, lw, kk, b, kf, v, s0, layer, steps):
    nb = r.shape[-1]
    vec = pl.BlockSpec((steps, 1, HEAD_DIM, nb), lambda h: (0, h, 0, 0))
    st = pl.BlockSpec((1, HEAD_DIM, HEAD_DIM, nb), lambda h: (h, 0, 0, 0))
    st_in = pl.BlockSpec((None, 1, HEAD_DIM, HEAD_DIM, nb), lambda h: (layer, h, 0, 0, 0))
    return pl.pallas_call(
        functools.partial(_wkv_sample_kernel, steps=steps),
        grid=(RWKV_HEADS,),
        in_specs=[vec] * 6 + [st_in],
        out_specs=[vec, st],
        out_shape=[jax.ShapeDtypeStruct(r.shape, F32), jax.ShapeDtypeStruct(s0.shape[1:], F32)],
        compiler_params=_params(("arbitrary",)),
        name="wkv_sample",
    )(r, lw, kk, b, kf, v, s0)


def _swa_prompt_kernel(q_ref, kc_ref, kp_ref, vc_ref, vp_ref, sink_ref, o_ref, *, sub):
    blk = WINDOW
    first = pl.program_id(1) == 0
    k_all = jnp.concatenate([kp_ref[...], kc_ref[...]], axis=0)
    v_all = jnp.concatenate([vp_ref[...], vc_ref[...]], axis=0).astype(BF16)
    v_ext = jnp.concatenate([v_all, jnp.ones_like(v_all)], axis=1)
    k_swapped = pltpu.roll(k_all, HEAD_DIM, 1)
    lane = lax.broadcasted_iota(jnp.int32, (1, LANES), 1)
    low = lane < HEAD_DIM
    k_var = {(0, 0): jnp.where(low, k_all, 0.0), (0, 1): jnp.where(low, 0.0, k_swapped),
             (1, 1): jnp.where(low, 0.0, k_all), (1, 0): jnp.where(low, k_swapped, 0.0)}
    k_var = {key: val.astype(BF16) for key, val in k_var.items()}
    order = [j * ATT_GROUP + g for j in range(ATT_KV_HEADS) for half in range(2)
             for g in range(ATT_GROUP) if g % 2 == half]
    n_rows = ATT_Q_HEADS * blk
    qi = lax.broadcasted_iota(jnp.int32, (n_rows, 2 * blk), 0) % blk
    kj = lax.broadcasted_iota(jnp.int32, (n_rows, 2 * blk), 1)
    diff = blk + qi - kj
    band = (diff >= 0) & (diff < WINDOW)
    row_head = lax.broadcasted_iota(jnp.int32, (n_rows, 1), 0) // blk
    sink = jnp.zeros((n_rows, 1), F32)
    for idx, head in enumerate(order):
        sink = jnp.where(row_head == idx, sink_ref[0:1, head:head + 1], sink)
    scale = HEAD_DIM ** -0.5
    scores, maxes, weights, sums = {}, {}, {}, {}

    def score(sb):
        qs = slice(sb * blk, (sb + 1) * blk)
        ks = slice(sb * blk, (sb + 2) * blk)
        parts = []
        for j in range(ATT_KV_HEADS):
            for half in range(2):
                heads = [j * ATT_GROUP + g for g in range(ATT_GROUP) if g % 2 == half]
                q_cat = jnp.concatenate([q_ref[qs, (h // 2) * LANES:(h // 2 + 1) * LANES] for h in heads], axis=0)
                parts.append(_mm_nt(q_cat * scale, k_var[(j, half)][ks]))
        s = jnp.concatenate(parts, axis=0)
        ok = band & jnp.logical_or(kj >= blk, jnp.logical_not(first)) if sb == 0 else band
        scores[sb] = jnp.where(ok, s, -jnp.inf)

    def exponentiate(sb):
        maxes[sb] = jnp.maximum(jnp.max(scores[sb], axis=-1, keepdims=True), sink)
        weights[sb] = jnp.exp(scores[sb] - maxes[sb]).astype(BF16)

    def weigh(sb):
        ks = slice(sb * blk, (sb + 2) * blk)
        sums[sb] = jnp.dot(weights[sb], v_ext[ks], preferred_element_type=F32)

    def emit(sb):
        qs = slice(sb * blk, (sb + 1) * blk)
        ev = sums[sb]
        o = ev[:, 0:LANES] / (ev[:, LANES:LANES + 1] + jnp.exp(sink - maxes[sb]))
        by_head = {}
        for idx, head in enumerate(order):
            o_h = o[idx * blk:(idx + 1) * blk]
            by_head[head] = pltpu.roll(o_h, HEAD_DIM, 1) if head // ATT_GROUP != head % 2 else o_h
        for grp in range(ATT_DIM // LANES):
            o_ref[qs, grp * LANES:(grp + 1) * LANES] = jnp.where(low, by_head[2 * grp],
                                                                 by_head[2 * grp + 1]).astype(o_ref.dtype)

    _staggered([score, exponentiate, weigh, emit], sub)


def _swa_prompt(q, k, v, P, batch, sub):
    n = q.shape[0]
    rows = sub * WINDOW
    nb = n // batch // rows
    cur = lambda b, i: (b * nb + i, 0)
    prev = lambda b, i: ((b * nb + i) * sub - jnp.minimum(i, 1), 0)
    return pl.pallas_call(
        functools.partial(_swa_prompt_kernel, sub=sub),
        grid=(batch, nb),
        in_specs=[pl.BlockSpec((rows, ATT_DIM), cur),
                  pl.BlockSpec((rows, ATT_KV_DIM), cur), pl.BlockSpec((WINDOW, ATT_KV_DIM), prev),
                  pl.BlockSpec((rows, ATT_KV_DIM), cur), pl.BlockSpec((WINDOW, ATT_KV_DIM), prev),
                  _layer_spec((1, ATT_Q_HEADS), P['layer'])],
        out_specs=pl.BlockSpec((rows, ATT_DIM), cur),
        out_shape=jax.ShapeDtypeStruct((n, ATT_DIM), BF16),
        compiler_params=_params(("arbitrary", "arbitrary")),
        name="swa_prompt",
    )(q, k, k, v, v, P['sinks'])


def _swa_sample_kernel(q_ref, kc_ref, vc_ref, kn_ref, vn_ref, sink_ref, o_ref, *, steps):
    q = q_ref[...].astype(BF16)
    rows = q.shape[1]
    nk = kc_ref.shape[1]
    nn = kn_ref.shape[1]
    bdot = lambda a, b, dims: lax.dot_general(a, b, (dims, ((0,), (0,))), preferred_element_type=F32)
    s_c = bdot(q, kc_ref[...].astype(BF16), ((2,), (2,))) * (HEAD_DIM ** -0.5)
    s_n = bdot(q, kn_ref[...].astype(BF16), ((2,), (2,))) * (HEAD_DIM ** -0.5)
    t_c = lax.broadcasted_iota(jnp.int32, (1, rows, nk), 1) % steps
    w_c = lax.broadcasted_iota(jnp.int32, (1, rows, nk), 2)
    s_c = jnp.where(w_c > t_c, s_c, -jnp.inf)
    t_n = lax.broadcasted_iota(jnp.int32, (1, rows, nn), 1) % steps
    w_n = lax.broadcasted_iota(jnp.int32, (1, rows, nn), 2)
    s_n = jnp.where(w_n <= t_n, s_n, -jnp.inf)
    sink = sink_ref[...][None, :, 0:1]
    m = jnp.maximum(jnp.maximum(jnp.max(s_c, axis=-1, keepdims=True), jnp.max(s_n, axis=-1, keepdims=True)), sink)
    e_c = jnp.exp(s_c - m)
    e_n = jnp.exp(s_n - m)
    den = jnp.sum(e_c, axis=-1, keepdims=True) + jnp.sum(e_n, axis=-1, keepdims=True) + jnp.exp(sink - m)
    o = (bdot((e_c / den).astype(BF16), vc_ref[...].astype(BF16), ((2,), (1,)))
         + bdot((e_n / den).astype(BF16), vn_ref[...].astype(BF16), ((2,), (1,))))
    o_ref[...] = o


def _swa_sample(q_st, k_cache, v_cache, k_new, v_new, sink_rows, layer, steps, bt):
    nb, rows, _ = q_st.shape
    nn = k_new.shape[1]
    blk = lambda r: pl.BlockSpec((bt, r, LANES), lambda i: (i, 0, 0))
    cache = pl.BlockSpec((None, bt, WINDOW, LANES), lambda i: (layer, i, 0, 0))
    return pl.pallas_call(
        functools.partial(_swa_sample_kernel, steps=steps),
        grid=(nb // bt,),
        in_specs=[blk(rows), cache, cache, blk(nn), blk(nn), _layer_spec((rows, LANES), layer)],
        out_specs=blk(rows),
        out_shape=jax.ShapeDtypeStruct((nb, rows, LANES), F32),
        compiler_params=_params(("arbitrary",)),
        name="swa_sample",
    )(q_st, k_cache, v_cache, k_new, v_new, sink_rows)


def _gelu(x):
    return 0.5 * x * (1.0 + lax.erf(x * (2.0 ** -0.5)))


def _back_kernel(*refs, final, sample, nsub):
    x_ref, y_ref, bonus_ref, g_ref, oa_ref, gr_ref, ga_ref = refs[:7]
    at = 7
    if sample:
        cp_ref = refs[at]
        at += 1
    (lng_ref, lnb_ref, ones_ref, wa_ref, wb_ref, wo_ref, gn_ref, wi_ref, cw_ref, cb_ref, wd_ref,
     gf_ref) = refs[at:at + 12]
    o_ref, ct_ref, carry = refs[at + 12:at + 15]
    blocks, rb = _row_blocks(x_ref.shape[0], nsub)
    ones_bd = ones_ref[...]

    if sample:
        @pl.when(pl.program_id(0) == 0)
        def _():
            carry[...] = cp_ref[...]
    else:
        @pl.when(pl.program_id(1) == 0)
        def _():
            carry[...] = jnp.zeros_like(carry)

    row = lax.broadcasted_iota(jnp.int32, (rb, 1), 0)
    xs, cs, ups, acts = {}, {}, {}, {}

    def mix(i):
        rs = blocks[i]
        y = y_ref[rs, :]
        mean = _head_sum(y, ones_bd) * (1.0 / HEAD_DIM)
        d = y - mean
        var = _head_sum(d * d, ones_bd) * (1.0 / HEAD_DIM)
        yn = d * lax.rsqrt(var + GN_EPS) * lng_ref[...] + lnb_ref[...]
        o_r = (yn + bonus_ref[rs, :]) * g_ref[rs, :]
        merged = gr_ref[rs, :] * _mm(o_r, wa_ref[...]) + ga_ref[rs, :] * _mm(oa_ref[rs, :], wb_ref[...])
        xs[i] = x_ref[rs, :] + _mm(merged, wo_ref[...])

    def up_project(i):
        h = _rmsnorm(xs[i], gn_ref[...]).astype(BF16)
        cs[i] = jnp.dot(h, wi_ref[:, 0:D_FF], preferred_element_type=F32)
        ups[i] = jnp.dot(h, wi_ref[:, D_FF:2 * D_FF], preferred_element_type=F32)

    def conv_gate(i):
        c = cs[i]
        if sample:
            c2, c1 = carry[0], carry[1]
            carry[0] = c1
            carry[1] = c
            ct_ref[0] = c
        else:
            tail = carry[...] if i == 0 else cs[i - 1][rb - SUBLANES:, :]
            last = tail[SUBLANES - 1:SUBLANES, :]
            c1 = jnp.where(row == 0, last, pltpu.roll(c, 1, 0))
            c2 = jnp.where(row == 0, tail[SUBLANES - 2:SUBLANES - 1, :],
                           jnp.where(row == 1, last, pltpu.roll(c, 2, 0)))
            if i == nsub - 1:
                carry[...] = c[rb - SUBLANES:, :]
                ct_ref[0] = c[rb - SUBLANES:, :]
        conv = cb_ref[...] + c2 * cw_ref[0:1, :] + c1 * cw_ref[1:2, :] + c * cw_ref[2:3, :]
        acts[i] = (_gelu(conv) * ups[i]).astype(BF16)

    def down_project(i):
        out = xs[i] + jnp.dot(acts[i], wd_ref[...], preferred_element_type=F32)
        o_ref[blocks[i], :] = _rmsnorm(out, gf_ref[...]) if final else out

    _staggered([mix, up_project, conv_gate, down_project], nsub)


def _back_weight_specs(layer):
    return [_layer_spec((1, RWKV_DIM), layer), _layer_spec((1, RWKV_DIM), layer), _const_spec((RWKV_DIM, RWKV_DIM)),
            _layer_spec((RWKV_DIM, D_MODEL), layer), _layer_spec((ATT_DIM, D_MODEL), layer),
            _layer_spec((D_MODEL, D_MODEL), layer),
            _layer_spec((1, D_MODEL), layer), _layer_spec((D_MODEL, 2 * D_FF), layer),
            _layer_spec((CONV_W, D_FF), layer), _layer_spec((1, D_FF), layer), _layer_spec((D_FF, D_MODEL), layer),
            _const_spec((1, D_MODEL))]


def _back_weights(P, g_final):
    return [P['ln_g'], P['ln_b'], P['ones_bd'], P['w_br_rwkv'], P['w_br_attn'], P['w_out'],
            P['norm_ffn_g'], P['ffn_w_in'], P['ffn_conv_w'], P['ffn_conv_b'], P['ffn_w_down'], g_final]


def _back_prompt(x, y, bonus, g, o_a, gr, ga, P, g_final, final, batch, tm):
    n = x.shape[0]
    nt = n // batch // tm
    row = lambda b, t: (b * nt + t, 0)
    wide = pl.BlockSpec((tm, D_MODEL), row)
    half = pl.BlockSpec((tm, RWKV_DIM), row)
    return pl.pallas_call(
        functools.partial(_back_kernel, final=final, sample=False, nsub=tm // 256),
        grid=(batch, nt),
        in_specs=[wide, half, half, half, half, wide, wide] + _back_weight_specs(P['layer']),
        out_specs=[wide, pl.BlockSpec((1, SUBLANES, D_FF), lambda b, t: (b, 0, 0))],
        out_shape=[jax.ShapeDtypeStruct((n, D_MODEL), F32), jax.ShapeDtypeStruct((batch, SUBLANES, D_FF), F32)],
        scratch_shapes=[pltpu.VMEM((SUBLANES, D_FF), F32)],
        compiler_params=_params(("arbitrary", "arbitrary")),
        name="back_prompt",
    )(x, y, bonus, g, o_a, gr, ga, *_back_weights(P, g_final))


def _back_sample(x, y, bonus, g, o_a, gr, ga, conv_prev, P, g_final, final):
    n = x.shape[0]
    nb = conv_prev.shape[2]
    steps = n // nb
    taps = CONV_W - 1
    row = lambda t: (t, 0)
    wide = pl.BlockSpec((nb, D_MODEL), row)
    half = pl.BlockSpec((nb, RWKV_DIM), row)
    return pl.pallas_call(
        functools.partial(_back_kernel, final=final, sample=True, nsub=1),
        grid=(steps,),
        in_specs=[wide, half, half, half, half, wide, wide, _layer_spec((taps, nb, D_FF), P['layer'])]
        + _back_weight_specs(P['layer']),
        out_specs=[wide, pl.BlockSpec((1, nb, D_FF), lambda t: (jnp.maximum(t - (steps - taps), 0), 0, 0))],
        out_shape=[jax.ShapeDtypeStruct((n, D_MODEL), F32), jax.ShapeDtypeStruct((taps, nb, D_FF), F32)],
        scratch_shapes=[pltpu.VMEM((taps, nb, D_FF), F32)],
        compiler_params=_params(("arbitrary",)),
        name="back_sample",
    )(x, y, bonus, g, o_a, gr, ga, conv_prev, *_back_weights(P, g_final))


def _rope_tables(pos):
    inv = ROPE_THETA ** (-jnp.arange(0, HEAD_DIM, 2, dtype=F32) / HEAD_DIM)
    ang = pos.astype(F32)[:, None] * inv[None, :]
    cos, sin = jnp.cos(ang), jnp.sin(ang)
    reps = LANES // HEAD_DIM
    return jnp.tile(jnp.concatenate([cos, cos], axis=1), (1, reps)), jnp.tile(jnp.concatenate([-sin, sin], axis=1),
                                                                             (1, reps))


def _stacked_params(norm_mix_g, w_in, rwkv_mu, rwkv_w0, rwkv_w2, rwkv_a0, rwkv_a2, rwkv_g2, rwkv_k_k, rwkv_k_a,
                    rwkv_r_k, rwkv_ln_g, rwkv_ln_b, attn_sinks, w_br_rwkv, w_br_attn, w_out, norm_ffn_g, ffn_w_in,
                    ffn_conv_w, ffn_conv_b, ffn_w_down, steps):
    d = RWKV_DIM
    depth = w_in.shape[0]
    head = jnp.arange(d) // HEAD_DIM
    row = lambda t: t.reshape(depth, 1, -1)
    sink_rows = jnp.broadcast_to(jnp.repeat(attn_sinks, steps, axis=1)[:, :, None],
                                 (depth, ATT_Q_HEADS * steps, LANES))
    return {
        'norm_mix_g': row(norm_mix_g), 'w_in': w_in.astype(BF16),
        'mu': row(rwkv_mu), 'w0': row(rwkv_w0),
        'w2p': jnp.pad(rwkv_w2, ((0, 0), (0, D_AAA_LORA), (0, 0))).astype(BF16),
        'a0': row(rwkv_a0),
        'a2p': jnp.pad(rwkv_a2, ((0, 0), (D_DECAY_LORA, 0), (0, 0))).astype(BF16),
        'g2': rwkv_g2.astype(BF16), 'k_k': row(rwkv_k_k), 'k_a': row(rwkv_k_a),
        'r_k': row(rwkv_r_k), 'ln_g': row(rwkv_ln_g), 'ln_b': row(rwkv_ln_b),
        'ones_bd': (head[:, None] == head[None, :]).astype(BF16),
        'sinks': row(attn_sinks), 'sink_rows': sink_rows,
        'w_br_rwkv': w_br_rwkv.astype(BF16), 'w_br_attn': w_br_attn.astype(BF16),
        'w_out': w_out.astype(BF16), 'norm_ffn_g': row(norm_ffn_g),
        'ffn_w_in': ffn_w_in.astype(BF16), 'ffn_conv_w': ffn_conv_w, 'ffn_conv_b': row(ffn_conv_b),
        'ffn_w_down': ffn_w_down.astype(BF16),
    }


def _prompt_layer(x, P, tables, g_final, final, batch, seq):
    tm = min(512, seq)
    q, k, v, gr, ga, z_tail, r, lw, kk, b, kf, vv, g, bonus = _front_prompt(x, P, tables[0], tables[1], batch,
                                                                            min(1024, seq))
    y, s_t = _wkv_prompt(r, lw, kk, b, kf, vv, batch, min(512, seq))
    o_a = _swa_prompt(q, k, v, P, batch, min(8, seq // WINDOW))
    x2, c_tail = _back_prompt(x, y, bonus, g, o_a, gr, ga, P, g_final, final, batch, tm)
    last = lambda t: t.reshape(batch, seq, -1)[:, -WINDOW:].reshape(batch, WINDOW, ATT_KV_HEADS, HEAD_DIM)
    state = (z_tail[:, SUBLANES - 1], s_t, last(k), last(v), c_tail[:, SUBLANES - (CONV_W - 1):])
    return x2, state


def _sample_layer(x, P, tables, g_final, final, shift_prev, wkv0, k_cache, v_cache, conv_prev, nb, steps):
    n = nb * steps
    layer = P['layer']
    q, k, v, gr, ga, z_tail, r, lw, kk, b, kf, vv, g, bonus = _front_sample(x, shift_prev, P, tables[0], tables[1])
    to_lanes = lambda t: t.reshape(steps, nb, RWKV_HEADS, HEAD_DIM).transpose(0, 2, 3, 1)
    y_l, s_l = _wkv_sample(*(to_lanes(t) for t in (r, lw, kk, b, kf)), to_lanes(vv).astype(F32), wkv0, layer, steps)
    y = y_l.transpose(0, 3, 1, 2).reshape(n, RWKV_DIM)

    rows = ATT_Q_HEADS * steps
    q5 = q.reshape(steps, nb, ATT_KV_HEADS, ATT_GROUP, HEAD_DIM).transpose(1, 2, 3, 0, 4)
    sel = jnp.eye(ATT_KV_HEADS, dtype=q.dtype)
    q_st = (q5[:, :, :, :, None, :] * sel[None, :, None, None, :, None]).reshape(nb, rows, ATT_KV_DIM)
    pad = ((0, 0), (0, SUBLANES - steps), (0, 0))
    k_new = k.reshape(steps, nb, ATT_KV_DIM).transpose(1, 0, 2)
    v_new = v.reshape(steps, nb, ATT_KV_DIM).transpose(1, 0, 2)
    o_st = _swa_sample(q_st, k_cache, v_cache, jnp.pad(k_new, pad), jnp.pad(v_new, pad), P['sink_rows'], layer,
                       steps, 16)
    o5 = o_st.reshape(nb, ATT_KV_HEADS, ATT_GROUP, steps, ATT_KV_HEADS, HEAD_DIM)
    o_a = jnp.stack([o5[:, j, :, :, j, :] for j in range(ATT_KV_HEADS)], axis=0)
    o_a = o_a.transpose(3, 1, 0, 2, 4).reshape(n, ATT_DIM)

    x2, c_tail = _back_sample(x, y, bonus, g, o_a, gr, ga, conv_prev, P, g_final, final)
    state = (z_tail, s_l, k_new, v_new, c_tail)
    return x2, state


def kernel(x_prompt, x_sample, state_rwkv_shift, state_rwkv_wkv, cache_swa_k, cache_swa_v, state_ffn_conv, norm_mix_g, w_in, rwkv_mu, rwkv_w0, rwkv_w2, rwkv_a0, rwkv_a2, rwkv_g2, rwkv_k_k, rwkv_k_a, rwkv_r_k, rwkv_ln_g, rwkv_ln_b, attn_sinks, w_br_rwkv, w_br_attn, w_out, norm_ffn_g, ffn_w_in, ffn_conv_w, ffn_conv_b, ffn_w_down, norm_final_g):
    bp, tp, _ = x_prompt.shape
    nb, steps, _ = x_sample.shape
    depth = w_in.shape[0]
    tab_p = _rope_tables(jnp.arange(tp, dtype=jnp.int32))
    tab_s = _rope_tables(PAST_LEN + jnp.repeat(jnp.arange(steps, dtype=jnp.int32), nb))
    g_final = norm_final_g[None]
    params = _stacked_params(norm_mix_g, w_in, rwkv_mu, rwkv_w0, rwkv_w2, rwkv_a0, rwkv_a2, rwkv_g2, rwkv_k_k,
                             rwkv_k_a, rwkv_r_k, rwkv_ln_g, rwkv_ln_b, attn_sinks, w_br_rwkv, w_br_attn, w_out,
                             norm_ffn_g, ffn_w_in, ffn_conv_w, ffn_conv_b, ffn_w_down, steps)
    wkv0 = state_rwkv_wkv.transpose(0, 2, 3, 4, 1)
    k_cache = cache_swa_k.reshape(depth, nb, WINDOW, ATT_KV_DIM)
    v_cache = cache_swa_v.reshape(depth, nb, WINDOW, ATT_KV_DIM)
    conv_prev = state_ffn_conv.transpose(0, 2, 1, 3)
    xp = x_prompt.reshape(bp * tp, D_MODEL)
    xs = x_sample.transpose(1, 0, 2).reshape(steps * nb, D_MODEL)
    outs_p, outs_s = [], []
    for l in range(depth):
        P = dict(params, layer=l)
        final = l == depth - 1
        xp, sp = _prompt_layer(xp, P, tab_p, g_final, final, bp, tp)
        xs, ss = _sample_layer(xs, P, tab_s, g_final, final, state_rwkv_shift, wkv0, k_cache, v_cache, conv_prev,
                               nb, steps)
        outs_p.append(sp)
        outs_s.append(ss)
    y_prompt = xp.reshape(bp, tp, D_MODEL)
    y_sample = xs.reshape(steps, nb, D_MODEL).transpose(1, 0, 2)
    stack = lambda outs, i: jnp.stack([o[i] for o in outs])
    kv_shape = (depth, nb, steps, ATT_KV_HEADS, HEAD_DIM)
    s_k = jnp.concatenate([cache_swa_k[:, :, steps:], stack(outs_s, 2).reshape(kv_shape)], axis=2)
    s_v = jnp.concatenate([cache_swa_v[:, :, steps:], stack(outs_s, 3).reshape(kv_shape)], axis=2)
    s_wkv = stack(outs_s, 1).transpose(0, 4, 1, 2, 3)
    s_conv = stack(outs_s, 4).transpose(0, 2, 1, 3)
    return (y_prompt, y_sample, *(stack(outs_p, i) for i in range(5)),
            stack(outs_s, 0), s_wkv, s_k, s_v, s_conv)
```

```python
import functools

import jax
import jax.numpy as jnp
from jax import lax
from jax.experimental import pallas as pl
from jax.experimental.pallas import tpu as pltpu

F32 = jnp.float32
BF16 = jnp.bfloat16

D_MODEL = 1024
HEAD_DIM = 64
RWKV_HEADS = 8
RWKV_DIM = RWKV_HEADS * HEAD_DIM
D_DECAY_LORA = 64
D_AAA_LORA = 64
D_GATE_LORA = 128
RWKV_PROJ = 3 * RWKV_DIM + D_DECAY_LORA + D_AAA_LORA + D_GATE_LORA
ATT_Q_HEADS = 8
ATT_KV_HEADS = 2
ATT_GROUP = ATT_Q_HEADS // ATT_KV_HEADS
ATT_DIM = ATT_Q_HEADS * HEAD_DIM
ATT_KV_DIM = ATT_KV_HEADS * HEAD_DIM
WINDOW = 128
IN_PROJ = RWKV_PROJ + ATT_DIM + 2 * ATT_KV_DIM + 2 * D_MODEL
D_FF = 2816
CONV_W = 3
ROPE_THETA = 10000.0
RMS_EPS = 1e-6
GN_EPS = 64e-5
PAST_LEN = 16384

LANES = 128
SUBLANES = 8
MXU_DIM = 256
CHUNK = 64
HEADS_PER_GROUP = MXU_DIM // HEAD_DIM
VMEM_LIMIT = 56 * 1024 * 1024


def _params(sem):
    return pltpu.CompilerParams(dimension_semantics=sem, vmem_limit_bytes=VMEM_LIMIT)


def _const_spec(shape):
    zeros = (0,) * len(shape)
    return pl.BlockSpec(shape, lambda *_: zeros, pipeline_mode=pl.Buffered(1))


def _layer_spec(shape, layer):
    index = (layer,) + (0,) * len(shape)
    return pl.BlockSpec((None,) + tuple(shape), lambda *_: index, pipeline_mode=pl.Buffered(1))


def _mm(a, b):
    return jnp.dot(a.astype(BF16), b.astype(BF16), preferred_element_type=F32)


def _mm_nt(a, b):
    return lax.dot_general(a.astype(BF16), b.astype(BF16), (((1,), (1,)), ((), ())),
                           preferred_element_type=F32)


def _mm_f32(a, b):
    return jnp.dot(a, b, precision=lax.Precision.HIGHEST, preferred_element_type=F32)


def _mm_tn(a, b):
    return lax.dot_general(a.astype(BF16), b.astype(BF16), (((0,), (0,)), ((), ())),
                           preferred_element_type=F32)


def _head_sum(x, ones_bd):
    return jnp.dot(x.astype(BF16), ones_bd, preferred_element_type=F32)


def _rmsnorm(x, g):
    return x * lax.rsqrt(jnp.mean(x * x, axis=-1, keepdims=True) + RMS_EPS) * g


def _rope(x, cos, sin_signed, first_half):
    partner = jnp.where(first_half, pltpu.roll(x, LANES - HEAD_DIM // 2, 1), pltpu.roll(x, HEAD_DIM // 2, 1))
    return x * cos + partner * sin_signed


def _prep_math(z, zp, mu, w0, w2p, a0, a2p, g2, k_k, k_a, r_k, ones_bd, outs, rs):
    r_ref, lw_ref, kk_ref, b_ref, kf_ref, v_ref, g_ref, bonus_ref = outs
    zs = z + mu * (zp - z)
    d = RWKV_DIM
    r, k, v = zs[:, 0:d], zs[:, d:2 * d], zs[:, 2 * d:3 * d]
    zwa = zs[:, 3 * d:3 * d + LANES]
    zg = zs[:, 3 * d + LANES:3 * d + 2 * LANES]
    w_pre = w0 + _mm(jnp.tanh(zwa), w2p)
    neg = -w_pre
    softplus = jnp.maximum(neg, 0.0) + jnp.log1p(jnp.exp(-jnp.abs(neg)))
    w_log = -softplus - 0.5
    lw_ref[rs, :] = -jnp.exp(w_log)
    a = jax.nn.sigmoid(a0 + _mm(zwa, a2p))
    g_ref[rs, :] = _mm(jax.nn.sigmoid(zg), g2).astype(g_ref.dtype)
    kk0 = k * k_k
    norm = jnp.sqrt(_head_sum(kk0 * kk0, ones_bd))
    kk = kk0 / jnp.maximum(norm, 1e-12)
    kf = k * (1.0 + (a - 1.0) * k_a)
    r_ref[rs, :] = r.astype(r_ref.dtype)
    kk_ref[rs, :] = kk.astype(kk_ref.dtype)
    b_ref[rs, :] = (kk * a).astype(b_ref.dtype)
    kf_ref[rs, :] = kf.astype(kf_ref.dtype)
    v_ref[rs, :] = v.astype(v_ref.dtype)
    bonus_ref[rs, :] = (_head_sum(r * kf * r_k, ones_bd) * v).astype(bonus_ref.dtype)


N_PREP_CONSTS = 10
N_FRONT_OUTS = 14


def _row_blocks(n, nsub):
    rb = n // nsub
    return [slice(i * rb, (i + 1) * rb) for i in range(nsub)], rb


def _staggered(stages, nsub):
    for step in range(nsub + len(stages) - 1):
        for i in range(nsub):
            if 0 <= step - i < len(stages):
                stages[step - i](i)


def _front_kernel(*refs, sample, nsub):
    if sample:
        x_ref, sp_ref, g_ref, w_ref, cos_ref, sin_ref = refs[:6]
        rest = refs[6:]
    else:
        x_ref, g_ref, w_ref, cos_ref, sin_ref = refs[:5]
        rest = refs[5:]
    consts = [c[...] for c in rest[:N_PREP_CONSTS]]
    outs = rest[N_PREP_CONSTS:N_PREP_CONSTS + N_FRONT_OUTS]
    q_ref, k_ref, v_ref, gr_ref, ga_ref, zt_ref = outs[:6]
    blocks, rb = _row_blocks(x_ref.shape[0], nsub)
    c0, c1, c2, c3, c4 = (RWKV_PROJ, RWKV_PROJ + ATT_DIM, RWKV_PROJ + ATT_DIM + ATT_KV_DIM,
                          RWKV_PROJ + ATT_DIM + 2 * ATT_KV_DIM, RWKV_PROJ + ATT_DIM + 2 * ATT_KV_DIM + D_MODEL)
    if not sample:
        carry = rest[N_PREP_CONSTS + N_FRONT_OUTS]

        @pl.when(pl.program_id(1) == 0)
        def _():
            carry[...] = jnp.zeros_like(carry)

    row = lax.broadcasted_iota(jnp.int32, (rb, 1), 0)
    lane = lax.broadcasted_iota(jnp.int32, (1, LANES), 1)
    first_half = (lane % HEAD_DIM) < (HEAD_DIM // 2)
    hs, zs = {}, {}

    def project(i):
        hs[i] = _rmsnorm(x_ref[blocks[i], :], g_ref[...]).astype(BF16)
        zs[i] = jnp.dot(hs[i], w_ref[:, 0:c0], preferred_element_type=F32)

    def rwkv_operands(i):
        z = zs[i]
        if sample:
            zp = sp_ref[...] if i == 0 else zs[i - 1]
            if i == nsub - 1:
                zt_ref[...] = z
        else:
            tail = carry[...] if i == 0 else zs[i - 1][rb - SUBLANES:, :]
            zp = jnp.where(row == 0, tail[SUBLANES - 1:SUBLANES, :], pltpu.roll(z, 1, 0))
            if i == nsub - 1:
                carry[...] = z[rb - SUBLANES:, :]
                zt_ref[0] = z[rb - SUBLANES:, :]
        _prep_math(z, zp, *consts, outs[6:], blocks[i])

    def attention_operands(i):
        rs, h = blocks[i], hs[i]
        cos = cos_ref[rs, :]
        sin = sin_ref[rs, :]
        q = jnp.dot(h, w_ref[:, c0:c1], preferred_element_type=F32)
        for grp in range(ATT_DIM // LANES):
            sl = slice(grp * LANES, (grp + 1) * LANES)
            q_ref[rs, sl] = _rope(q[:, sl], cos, sin, first_half).astype(q_ref.dtype)
        k = jnp.dot(h, w_ref[:, c1:c2], preferred_element_type=F32)
        k_ref[rs, :] = _rope(k, cos, sin, first_half)
        v_ref[rs, :] = jnp.dot(h, w_ref[:, c2:c3], preferred_element_type=F32)
        gr_ref[rs, :] = jax.nn.sigmoid(jnp.dot(h, w_ref[:, c3:c4], preferred_element_type=F32)).astype(gr_ref.dtype)
        ga_ref[rs, :] = jax.nn.sigmoid(jnp.dot(h, w_ref[:, c4:IN_PROJ], preferred_element_type=F32)).astype(ga_ref.dtype)

    _staggered([project, rwkv_operands, attention_operands], nsub)


def _prep_consts(P):
    return [P['mu'], P['w0'], P['w2p'], P['a0'], P['a2p'], P['g2'], P['k_k'], P['k_a'], P['r_k'], P['ones_bd']]


def _prep_const_specs(layer):
    d = RWKV_DIM
    shapes = [(1, RWKV_PROJ), (1, d), (LANES, d), (1, d), (LANES, d), (D_GATE_LORA, d), (1, d), (1, d), (1, d)]
    return [_layer_spec(s, layer) for s in shapes] + [_const_spec((d, d))]


def _front_out_types(n, tail_shape):
    d = RWKV_DIM
    widths = [(ATT_DIM, BF16), (ATT_KV_DIM, F32), (ATT_KV_DIM, F32), (D_MODEL, BF16), (D_MODEL, BF16)]
    prep = [(d, BF16), (d, F32), (d, BF16), (d, BF16), (d, BF16), (d, BF16), (d, BF16), (d, BF16)]
    shapes = ([jax.ShapeDtypeStruct((n, w), t) for w, t in widths] + [jax.ShapeDtypeStruct(tail_shape, F32)]
              + [jax.ShapeDtypeStruct((n, w), t) for w, t in prep])
    return shapes, [w for w, _ in widths], [w for w, _ in prep]


def _front_prompt(x2d, P, cos_t, sin_t, batch, tm):
    n = x2d.shape[0]
    layer = P['layer']
    nt = n // batch // tm
    row = lambda b, t: (b * nt + t, 0)
    pos = lambda b, t: (t, 0)
    shapes, widths, prep = _front_out_types(n, (batch, SUBLANES, RWKV_PROJ))
    return pl.pallas_call(
        functools.partial(_front_kernel, sample=False, nsub=tm // 128),
        grid=(batch, nt),
        in_specs=[pl.BlockSpec((tm, D_MODEL), row), _layer_spec((1, D_MODEL), layer),
                  _layer_spec((D_MODEL, IN_PROJ), layer),
                  pl.BlockSpec((tm, LANES), pos), pl.BlockSpec((tm, LANES), pos)] + _prep_const_specs(layer),
        out_specs=([pl.BlockSpec((tm, w), row) for w in widths]
                   + [pl.BlockSpec((1, SUBLANES, RWKV_PROJ), lambda b, t: (b, 0, 0))]
                   + [pl.BlockSpec((tm, w), row) for w in prep]),
        out_shape=shapes,
        scratch_shapes=[pltpu.VMEM((SUBLANES, RWKV_PROJ), F32)],
        compiler_params=_params(("arbitrary", "arbitrary")),
        name="front_prompt",
    )(x2d, P['norm_mix_g'], P['w_in'], cos_t, sin_t, *_prep_consts(P))


def _front_sample(x2d, shift_prev, P, cos_t, sin_t):
    n = x2d.shape[0]
    layer = P['layer']
    nb = shift_prev.shape[1]
    shapes, _, _ = _front_out_types(n, (nb, RWKV_PROJ))
    return pl.pallas_call(
        functools.partial(_front_kernel, sample=True, nsub=n // nb),
        grid=(1,),
        in_specs=[_const_spec((n, D_MODEL)), _layer_spec((nb, RWKV_PROJ), layer), _layer_spec((1, D_MODEL), layer),
                  _layer_spec((D_MODEL, IN_PROJ), layer), _const_spec((n, LANES)), _const_spec((n, LANES))]
        + _prep_const_specs(layer),
        out_specs=[_const_spec(sh.shape) for sh in shapes],
        out_shape=shapes,
        compiler_params=_params(("arbitrary",)),
        name="front_sample",
    )(x2d, shift_prev, P['norm_mix_g'], P['w_in'], cos_t, sin_t, *_prep_consts(P))


def _wkv_prompt_kernel(r_ref, lw_ref, kk_ref, b_ref, kf_ref, v_ref, y_ref, st_ref, s_scr, *, chunks, together):
    C, G, M = CHUNK, HEADS_PER_GROUP, MXU_DIM
    nbatch = r_ref.shape[0]
    ngrp = RWKV_HEADS // G

    @pl.when(pl.program_id(0) == 0)
    def _():
        s_scr[...] = jnp.zeros_like(s_scr)

    ti = lax.broadcasted_iota(jnp.int32, (C, M), 0)
    si = lax.broadcasted_iota(jnp.int32, (C, M), 1) % C
    strict = ti > si
    incl = ti >= si
    eye = (ti == si).astype(F32)
    tri = (lax.broadcasted_iota(jnp.int32, (C, C), 0) >= lax.broadcasted_iota(jnp.int32, (C, C), 1)).astype(F32)
    lane_head = lax.broadcasted_iota(jnp.int32, (1, M), 1) // HEAD_DIM
    same_head = (lax.broadcasted_iota(jnp.int32, (M, M), 0) // HEAD_DIM
                 == lax.broadcasted_iota(jnp.int32, (M, M), 1) // HEAD_DIM)

    def stack(x):
        xb = x.astype(BF16)
        return jnp.concatenate([jnp.where(lane_head == h, xb, jnp.zeros_like(xb)) for h in range(G)], axis=0)

    def rows2(top, bottom):
        return jnp.concatenate([top.astype(BF16), bottom.astype(BF16)], axis=0)

    def load(rows):
        probs = []
        for bi in range(nbatch):
            lw = lw_ref[bi, rows, :]
            cum = _mm_f32(tri, lw)
            w_end = jnp.exp(cum[C - 1:C, :])
            e_neg = jnp.exp(-cum)
            a_hat = -kk_ref[bi, rows, :] * jnp.exp(cum - lw)
            r_hat = r_ref[bi, rows, :] * jnp.exp(cum)
            b_til = b_ref[bi, rows, :] * e_neg
            k_til = kf_ref[bi, rows, :] * e_neg
            v = v_ref[bi, rows, :]
            for grp in range(ngrp):
                ls = slice(grp * M, (grp + 1) * M)
                we = w_end[:, ls]
                probs.append(dict(
                    bi=bi, grp=grp, ls=ls, we=we, rows=rows, ar=rows2(a_hat[:, ls], r_hat[:, ls]),
                    b=stack(b_til[:, ls]), k=stack(k_til[:, ls]), v=v[:, ls], vs=stack(v[:, ls]),
                    bk=rows2(b_til[:, ls] * we, k_til[:, ls] * we)))
        return probs

    def state_free(probs):
        l_ab, p_rb, lp_k = [], [], []
        for p in probs:
            x = _mm_nt(p['ar'], p['b'])
            l_ab.append(jnp.where(strict, x[0:C], 0.0))
            p_rb.append(jnp.where(incl, x[C:2 * C], 0.0).astype(BF16))
        for p in probs:
            x = _mm_nt(p['ar'], p['k'])
            lp_k.append(rows2(jnp.where(strict, x[0:C], 0.0), jnp.where(incl, x[C:2 * C], 0.0)))
        t_inv = [eye + x for x in l_ab]
        pw = [_mm(x, stack(x)) for x in l_ab]
        for _ in range(C.bit_length() - 3):
            for i, (t, x) in enumerate(zip(t_inv, pw)):
                both = _mm(rows2(t, x), stack(x))
                t_inv[i] = t + both[0:C]
                pw[i] = both[C:2 * C]
        t_inv = [t + _mm(t, stack(x)) for t, x in zip(t_inv, pw)]
        e0 = [_mm(x, p['vs']) for x, p in zip(lp_k, probs)]
        return t_inv, p_rb, e0

    def advance(probs, t_inv, p_rb, e0):
        s0 = [s_scr[p['bi'], p['grp']] for p in probs]
        d0 = [_mm_nt(p['ar'], x) for p, x in zip(probs, s0)]
        u = [_mm(t, stack(x[0:C] + y[0:C])) for t, x, y in zip(t_inv, d0, e0)]
        for p, x, y, w, q, st in zip(probs, d0, e0, u, p_rb, s0):
            y_ref[p['bi'], p['rows'], p['ls']] = x[C:2 * C] + y[C:2 * C] + _mm(q, stack(w))
            upd = _mm_tn(rows2(w, p['v']), p['bk'])
            s_scr[p['bi'], p['grp']] = st * p['we'] + jnp.where(same_head, upd, 0.0)

    def chunk_group(c, carry):
        chunk_probs = [load(pl.ds(pl.multiple_of((c * together + j) * C, C), C)) for j in range(together)]
        t_inv, p_rb, e0 = state_free([p for probs in chunk_probs for p in probs])
        n = len(chunk_probs[0])
        for j, probs in enumerate(chunk_probs):
            sl = slice(j * n, (j + 1) * n)
            advance(probs, t_inv[sl], p_rb[sl], e0[sl])
        return carry

    lax.fori_loop(0, chunks // together, chunk_group, 0)

    @pl.when(pl.program_id(0) == pl.num_programs(0) - 1)
    def _():
        st_ref[...] = s_scr[...]


def _wkv_prompt(r, lw, kk, b, kf, v, batch, tl):
    n = r.shape[0]
    seq = n // batch
    ngrp = RWKV_HEADS // HEADS_PER_GROUP
    blk = pl.BlockSpec((batch, tl, RWKV_DIM), lambda t: (0, t, 0))
    y, st = pl.pallas_call(
        functools.partial(_wkv_prompt_kernel, chunks=tl // CHUNK, together=2),
        grid=(seq // tl,),
        in_specs=[blk] * 6,
        out_specs=[blk, _const_spec((batch, ngrp, MXU_DIM, MXU_DIM))],
        out_shape=[jax.ShapeDtypeStruct((batch, seq, RWKV_DIM), F32),
                   jax.ShapeDtypeStruct((batch, ngrp, MXU_DIM, MXU_DIM), F32)],
        scratch_shapes=[pltpu.VMEM((batch, ngrp, MXU_DIM, MXU_DIM), F32)],
        compiler_params=_params(("arbitrary",)),
        name="wkv_prompt",
    )(*(t.reshape(batch, seq, RWKV_DIM) for t in (r, lw, kk, b, kf, v)))
    st = st.reshape(batch, ngrp, HEADS_PER_GROUP, HEAD_DIM, HEADS_PER_GROUP, HEAD_DIM)
    st = jnp.stack([st[:, :, h, :, h, :] for h in range(HEADS_PER_GROUP)], axis=2)
    st = st.reshape(batch, RWKV_HEADS, HEAD_DIM, HEAD_DIM)
    return y.reshape(n, RWKV_DIM), st


def _wkv_sample_kernel(r_ref, lw_ref, kk_ref, b_ref, kf_ref, v_ref, s_ref, y_ref, so_ref, *, steps):
    group = SUBLANES // 2

    def value_rows(gi, carry):
        vis = [gi * group + j for j in range(group)]
        ss = [s_ref[0, vi] for vi in vis]
        for t in range(steps):
            kk, w, b = kk_ref[t, 0].astype(F32), jnp.exp(lw_ref[t, 0]), b_ref[t, 0].astype(F32)
            kf, r = kf_ref[t, 0].astype(F32), r_ref[t, 0].astype(F32)
            sks = [jnp.sum(s * kk, axis=0, keepdims=True) for s in ss]
            ss = [s * w - sk * b + v_ref[t, 0, pl.ds(vi, 1), :] * kf for s, sk, vi in zip(ss, sks, vis)]
            for s, vi in zip(ss, vis):
                y_ref[t, 0, pl.ds(vi, 1), :] = jnp.sum(s * r, axis=0, keepdims=True)
        for s, vi in zip(ss, vis):
            so_ref[0, vi] = s
        return carry

    lax.fori_loop(0, HEAD_DIM // group, value_rows, 0)


def _wkv_sample(r, lw, kk, b, kf, v, s0, layer, steps):
    nb = r.shape[-1]
    vec = pl.BlockSpec((steps, 1, HEAD_DIM, nb), lambda h: (0, h, 0, 0))
    st = pl.BlockSpec((1, HEAD_DIM, HEAD_DIM, nb), lambda h: (h, 0, 0, 0))
    st_in = pl.BlockSpec((None, 1, HEAD_DIM, HEAD_DIM, nb), lambda h: (layer, h, 0, 0, 0))
    return pl.pallas_call(
        functools.partial(_wkv_sample_kernel, steps=steps),
        grid=(RWKV_HEADS,),
        in_specs=[vec] * 6 + [st_in],
        out_specs=[vec, st],
        out_shape=[jax.ShapeDtypeStruct(r.shape, F32), jax.ShapeDtypeStruct(s0.shape[1:], F32)],
        compiler_params=_params(("arbitrary",)),
        name="wkv_sample",
    )(r, lw, kk, b, kf, v, s0)


def _swa_prompt_kernel(q_ref, kc_ref, kp_ref, vc_ref, vp_ref, sink_ref, o_ref, *, sub):
    blk = WINDOW
    first = pl.program_id(1) == 0
    k_all = jnp.concatenate([kp_ref[...], kc_ref[...]], axis=0)
    v_all = jnp.concatenate([vp_ref[...], vc_ref[...]], axis=0).astype(BF16)
    v_ext = jnp.concatenate([v_all, jnp.ones_like(v_all)], axis=1)
    k_swapped = pltpu.roll(k_all, HEAD_DIM, 1)
    lane = lax.broadcasted_iota(jnp.int32, (1, LANES), 1)
    low = lane < HEAD_DIM
    k_var = {(0, 0): jnp.where(low, k_all, 0.0), (0, 1): jnp.where(low, 0.0, k_swapped),
             (1, 1): jnp.where(low, 0.0, k_all), (1, 0): jnp.where(low, k_swapped, 0.0)}
    k_var = {key: val.astype(BF16) for key, val in k_var.items()}
    order = [j * ATT_GROUP + g for j in range(ATT_KV_HEADS) for half in range(2)
             for g in range(ATT_GROUP) if g % 2 == half]
    n_rows = ATT_Q_HEADS * blk
    qi = lax.broadcasted_iota(jnp.int32, (n_rows, 2 * blk), 0) % blk
    kj = lax.broadcasted_iota(jnp.int32, (n_rows, 2 * blk), 1)
    diff = blk + qi - kj
    band = (diff >= 0) & (diff < WINDOW)
    row_head = lax.broadcasted_iota(jnp.int32, (n_rows, 1), 0) // blk
    sink = jnp.zeros((n_rows, LANES), F32)
    for idx, head in enumerate(order):
        sink = jnp.where(row_head == idx, sink_ref[0:1, head:head + 1], sink)
    scale = HEAD_DIM ** -0.5
    scores, maxes, weights, sums = {}, {}, {}, {}

    def score(sb):
        qs = slice(sb * blk, (sb + 1) * blk)
        ks = slice(sb * blk, (sb + 2) * blk)
        parts = []
        for j in range(ATT_KV_HEADS):
            for half in range(2):
                heads = [j * ATT_GROUP + g for g in range(ATT_GROUP) if g % 2 == half]
                q_cat = jnp.concatenate([q_ref[qs, (h // 2) * LANES:(h // 2 + 1) * LANES] for h in heads], axis=0)
                parts.append(_mm_nt(q_cat * scale, k_var[(j, half)][ks]))
        s = jnp.concatenate(parts, axis=0)
        ok = band & jnp.logical_or(kj >= blk, jnp.logical_not(first)) if sb == 0 else band
        scores[sb] = jnp.where(ok, s, -jnp.inf)

    def exponentiate(sb):
        s = scores[sb]
        row_max = jnp.max(jnp.maximum(s[:, 0:LANES], s[:, LANES:2 * LANES]), axis=-1, keepdims=True)
        m = jnp.maximum(row_max, sink)
        maxes[sb] = m
        weights[sb] = jnp.exp(s - jnp.concatenate([m, m], axis=1)).astype(BF16)

    def weigh(sb):
        ks = slice(sb * blk, (sb + 2) * blk)
        sums[sb] = jnp.dot(weights[sb], v_ext[ks], preferred_element_type=F32)

    def emit(sb):
        qs = slice(sb * blk, (sb + 1) * blk)
        ev = sums[sb]
        o = ev[:, 0:LANES] / (ev[:, LANES:2 * LANES] + jnp.exp(sink - maxes[sb]))
        by_head = {}
        for idx, head in enumerate(order):
            o_h = o[idx * blk:(idx + 1) * blk]
            by_head[head] = pltpu.roll(o_h, HEAD_DIM, 1) if head // ATT_GROUP != head % 2 else o_h
        for grp in range(ATT_DIM // LANES):
            o_ref[qs, grp * LANES:(grp + 1) * LANES] = jnp.where(low, by_head[2 * grp],
                                                                 by_head[2 * grp + 1]).astype(o_ref.dtype)

    _staggered([score, exponentiate, weigh, emit], sub)


def _swa_prompt(q, k, v, P, batch, sub):
    n = q.shape[0]
    rows = sub * WINDOW
    nb = n // batch // rows
    cur = lambda b, i: (b * nb + i, 0)
    prev = lambda b, i: ((b * nb + i) * sub - jnp.minimum(i, 1), 0)
    return pl.pallas_call(
        functools.partial(_swa_prompt_kernel, sub=sub),
        grid=(batch, nb),
        in_specs=[pl.BlockSpec((rows, ATT_DIM), cur),
                  pl.BlockSpec((rows, ATT_KV_DIM), cur), pl.BlockSpec((WINDOW, ATT_KV_DIM), prev),
                  pl.BlockSpec((rows, ATT_KV_DIM), cur), pl.BlockSpec((WINDOW, ATT_KV_DIM), prev),
                  _layer_spec((1, ATT_Q_HEADS), P['layer'])],
        out_specs=pl.BlockSpec((rows, ATT_DIM), cur),
        out_shape=jax.ShapeDtypeStruct((n, ATT_DIM), BF16),
        compiler_params=_params(("arbitrary", "arbitrary")),
        name="swa_prompt",
    )(q, k, k, v, v, P['sinks'])


def _swa_sample_kernel(q_ref, kc_ref, vc_ref, kn_ref, vn_ref, sink_ref, o_ref, *, steps):
    q = q_ref[...].astype(BF16)
    rows = q.shape[1]
    nk = kc_ref.shape[1]
    nn = kn_ref.shape[1]
    bdot = lambda a, b, dims: lax.dot_general(a, b, (dims, ((0,), (0,))), preferred_element_type=F32)
    s_c = bdot(q, kc_ref[...].astype(BF16), ((2,), (2,))) * (HEAD_DIM ** -0.5)
    s_n = bdot(q, kn_ref[...].astype(BF16), ((2,), (2,))) * (HEAD_DIM ** -0.5)
    t_c = lax.broadcasted_iota(jnp.int32, (1, rows, nk), 1) % steps
    w_c = lax.broadcasted_iota(jnp.int32, (1, rows, nk), 2)
    s_c = jnp.where(w_c > t_c, s_c, -jnp.inf)
    t_n = lax.broadcasted_iota(jnp.int32, (1, rows, nn), 1) % steps
    w_n = lax.broadcasted_iota(jnp.int32, (1, rows, nn), 2)
    s_n = jnp.where(w_n <= t_n, s_n, -jnp.inf)
    sink = sink_ref[...][None, :, 0:1]
    m = jnp.maximum(jnp.maximum(jnp.max(s_c, axis=-1, keepdims=True), jnp.max(s_n, axis=-1, keepdims=True)), sink)
    e_c = jnp.exp(s_c - m)
    e_n = jnp.exp(s_n - m)
    den = jnp.sum(e_c, axis=-1, keepdims=True) + jnp.sum(e_n, axis=-1, keepdims=True) + jnp.exp(sink - m)
    o = (bdot((e_c / den).astype(BF16), vc_ref[...].astype(BF16), ((2,), (1,)))
         + bdot((e_n / den).astype(BF16), vn_ref[...].astype(BF16), ((2,), (1,))))
    o_ref[...] = o


def _swa_sample(q_st, k_cache, v_cache, k_new, v_new, sink_rows, layer, steps, bt):
    nb, rows, _ = q_st.shape
    nn = k_new.shape[1]
    blk = lambda r: pl.BlockSpec((bt, r, LANES), lambda i: (i, 0, 0))
    cache = pl.BlockSpec((None, bt, WINDOW, LANES), lambda i: (layer, i, 0, 0))
    return pl.pallas_call(
        functools.partial(_swa_sample_kernel, steps=steps),
        grid=(nb // bt,),
        in_specs=[blk(rows), cache, cache, blk(nn), blk(nn), _layer_spec((rows, LANES), layer)],
        out_specs=blk(rows),
        out_shape=jax.ShapeDtypeStruct((nb, rows, LANES), F32),
        compiler_params=_params(("arbitrary",)),
        name="swa_sample",
    )(q_st, k_cache, v_cache, k_new, v_new, sink_rows)


def _gelu(x):
    return 0.5 * x * (1.0 + lax.erf(x * (2.0 ** -0.5)))


def _back_kernel(*refs, final, sample, nsub):
    x_ref, y_ref, bonus_ref, g_ref, oa_ref, gr_ref, ga_ref = refs[:7]
    at = 7
    if sample:
        cp_ref = refs[at]
        at += 1
    (lng_ref, lnb_ref, ones_ref, wa_ref, wb_ref, wo_ref, gn_ref, wi_ref, cw_ref, cb_ref, wd_ref,
     gf_ref) = refs[at:at + 12]
    o_ref, ct_ref, carry = refs[at + 12:at + 15]
    blocks, rb = _row_blocks(x_ref.shape[0], nsub)
    ones_bd = ones_ref[...]

    if sample:
        @pl.when(pl.program_id(0) == 0)
        def _():
            carry[...] = cp_ref[...]
    else:
        @pl.when(pl.program_id(1) == 0)
        def _():
            carry[...] = jnp.zeros_like(carry)

    row = lax.broadcasted_iota(jnp.int32, (rb, 1), 0)
    xs, cs, ups, acts = {}, {}, {}, {}

    def mix(i):
        rs = blocks[i]
        y = y_ref[rs, :]
        mean = _head_sum(y, ones_bd) * (1.0 / HEAD_DIM)
        d = y - mean
        var = _head_sum(d * d, ones_bd) * (1.0 / HEAD_DIM)
        yn = d * lax.rsqrt(var + GN_EPS) * lng_ref[...] + lnb_ref[...]
        o_r = (yn + bonus_ref[rs, :]) * g_ref[rs, :]
        merged = gr_ref[rs, :] * _mm(o_r, wa_ref[...]) + ga_ref[rs, :] * _mm(oa_ref[rs, :], wb_ref[...])
        xs[i] = x_ref[rs, :] + _mm(merged, wo_ref[...])

    def up_project(i):
        h = _rmsnorm(xs[i], gn_ref[...]).astype(BF16)
        cs[i] = jnp.dot(h, wi_ref[:, 0:D_FF], preferred_element_type=F32)
        ups[i] = jnp.dot(h, wi_ref[:, D_FF:2 * D_FF], preferred_element_type=F32)

    def conv_gate(i):
        c = cs[i]
        if sample:
            c2, c1 = carry[0], carry[1]
            carry[0] = c1
            carry[1] = c
            ct_ref[0] = c
        else:
            tail = carry[...] if i == 0 else cs[i - 1][rb - SUBLANES:, :]
            last = tail[SUBLANES - 1:SUBLANES, :]
            c1 = jnp.where(row == 0, last, pltpu.roll(c, 1, 0))
            c2 = jnp.where(row == 0, tail[SUBLANES - 2:SUBLANES - 1, :],
                           jnp.where(row == 1, last, pltpu.roll(c, 2, 0)))
            if i == nsub - 1:
                carry[...] = c[rb - SUBLANES:, :]
                ct_ref[0] = c[rb - SUBLANES:, :]
        conv = cb_ref[...] + c2 * cw_ref[0:1, :] + c1 * cw_ref[1:2, :] + c * cw_ref[2:3, :]
        acts[i] = (_gelu(conv) * ups[i]).astype(BF16)

    def down_project(i):
        out = xs[i] + jnp.dot(acts[i], wd_ref[...], preferred_element_type=F32)
        o_ref[blocks[i], :] = _rmsnorm(out, gf_ref[...]) if final else out

    _staggered([mix, up_project, conv_gate, down_project], nsub)


def _back_weight_specs(layer):
    return [_layer_spec((1, RWKV_DIM), layer), _layer_spec((1, RWKV_DIM), layer), _const_spec((RWKV_DIM, RWKV_DIM)),
            _layer_spec((RWKV_DIM, D_MODEL), layer), _layer_spec((ATT_DIM, D_MODEL), layer),
            _layer_spec((D_MODEL, D_MODEL), layer),
            _layer_spec((1, D_MODEL), layer), _layer_spec((D_MODEL, 2 * D_FF), layer),
            _layer_spec((CONV_W, D_FF), layer), _layer_spec((1, D_FF), layer), _layer_spec((D_FF, D_MODEL), layer),
            _const_spec((1, D_MODEL))]


def _back_weights(P, g_final):
    return [P['ln_g'], P['ln_b'], P['ones_bd'], P['w_br_rwkv'], P['w_br_attn'], P['w_out'],
            P['norm_ffn_g'], P['ffn_w_in'], P['ffn_conv_w'], P['ffn_conv_b'], P['ffn_w_down'], g_final]


def _back_prompt(x, y, bonus, g, o_a, gr, ga, P, g_final, final, batch, tm):
    n = x.shape[0]
    nt = n // batch // tm
    row = lambda b, t: (b * nt + t, 0)
    wide = pl.BlockSpec((tm, D_MODEL), row)
    half = pl.BlockSpec((tm, RWKV_DIM), row)
    return pl.pallas_call(
        functools.partial(_back_kernel, final=final, sample=False, nsub=tm // 256),
        grid=(batch, nt),
        in_specs=[wide, half, half, half, half, wide, wide] + _back_weight_specs(P['layer']),
        out_specs=[wide, pl.BlockSpec((1, SUBLANES, D_FF), lambda b, t: (b, 0, 0))],
        out_shape=[jax.ShapeDtypeStruct((n, D_MODEL), F32), jax.ShapeDtypeStruct((batch, SUBLANES, D_FF), F32)],
        scratch_shapes=[pltpu.VMEM((SUBLANES, D_FF), F32)],
        compiler_params=_params(("arbitrary", "arbitrary")),
        name="back_prompt",
    )(x, y, bonus, g, o_a, gr, ga, *_back_weights(P, g_final))


def _back_sample(x, y, bonus, g, o_a, gr, ga, conv_prev, P, g_final, final):
    n = x.shape[0]
    nb = conv_prev.shape[2]
    steps = n // nb
    taps = CONV_W - 1
    row = lambda t: (t, 0)
    wide = pl.BlockSpec((nb, D_MODEL), row)
    half = pl.BlockSpec((nb, RWKV_DIM), row)
    return pl.pallas_call(
        functools.partial(_back_kernel, final=final, sample=True, nsub=1),
        grid=(steps,),
        in_specs=[wide, half, half, half, half, wide, wide, _layer_spec((taps, nb, D_FF), P['layer'])]
        + _back_weight_specs(P['layer']),
        out_specs=[wide, pl.BlockSpec((1, nb, D_FF), lambda t: (jnp.maximum(t - (steps - taps), 0), 0, 0))],
        out_shape=[jax.ShapeDtypeStruct((n, D_MODEL), F32), jax.ShapeDtypeStruct((taps, nb, D_FF), F32)],
        scratch_shapes=[pltpu.VMEM((taps, nb, D_FF), F32)],
        compiler_params=_params(("arbitrary",)),
        name="back_sample",
    )(x, y, bonus, g, o_a, gr, ga, conv_prev, *_back_weights(P, g_final))


def _rope_tables(pos):
    inv = ROPE_THETA ** (-jnp.arange(0, HEAD_DIM, 2, dtype=F32) / HEAD_DIM)
    ang = pos.astype(F32)[:, None] * inv[None, :]
    cos, sin = jnp.cos(ang), jnp.sin(ang)
    reps = LANES // HEAD_DIM
    return jnp.tile(jnp.concatenate([cos, cos], axis=1), (1, reps)), jnp.tile(jnp.concatenate([-sin, sin], axis=1),
                                                                             (1, reps))


def _stacked_params(norm_mix_g, w_in, rwkv_mu, rwkv_w0, rwkv_w2, rwkv_a0, rwkv_a2, rwkv_g2, rwkv_k_k, rwkv_k_a,
                    rwkv_r_k, rwkv_ln_g, rwkv_ln_b, attn_sinks, w_br_rwkv, w_br_attn, w_out, norm_ffn_g, ffn_w_in,
                    ffn_conv_w, ffn_conv_b, ffn_w_down, steps):
    d = RWKV_DIM
    depth = w_in.shape[0]
    head = jnp.arange(d) // HEAD_DIM
    row = lambda t: t.reshape(depth, 1, -1)
    sink_rows = jnp.broadcast_to(jnp.repeat(attn_sinks, steps, axis=1)[:, :, None],
                                 (depth, ATT_Q_HEADS * steps, LANES))
    return {
        'norm_mix_g': row(norm_mix_g), 'w_in': w_in.astype(BF16),
        'mu': row(rwkv_mu), 'w0': row(rwkv_w0),
        'w2p': jnp.pad(rwkv_w2, ((0, 0), (0, D_AAA_LORA), (0, 0))).astype(BF16),
        'a0': row(rwkv_a0),
        'a2p': jnp.pad(rwkv_a2, ((0, 0), (D_DECAY_LORA, 0), (0, 0))).astype(BF16),
        'g2': rwkv_g2.astype(BF16), 'k_k': row(rwkv_k_k), 'k_a': row(rwkv_k_a),
        'r_k': row(rwkv_r_k), 'ln_g': row(rwkv_ln_g), 'ln_b': row(rwkv_ln_b),
        'ones_bd': (head[:, None] == head[None, :]).astype(BF16),
        'sinks': row(attn_sinks), 'sink_rows': sink_rows,
        'w_br_rwkv': w_br_rwkv.astype(BF16), 'w_br_attn': w_br_attn.astype(BF16),
        'w_out': w_out.astype(BF16), 'norm_ffn_g': row(norm_ffn_g),
        'ffn_w_in': ffn_w_in.astype(BF16), 'ffn_conv_w': ffn_conv_w, 'ffn_conv_b': row(ffn_conv_b),
        'ffn_w_down': ffn_w_down.astype(BF16),
    }


def _prompt_layer(x, P, tables, g_final, final, batch, seq):
    tm = min(512, seq)
    q, k, v, gr, ga, z_tail, r, lw, kk, b, kf, vv, g, bonus = _front_prompt(x, P, tables[0], tables[1], batch,
                                                                            min(1024, seq))
    y, s_t = _wkv_prompt(r, lw, kk, b, kf, vv, batch, min(512, seq))
    o_a = _swa_prompt(q, k, v, P, batch, min(8, seq // WINDOW))
    x2, c_tail = _back_prompt(x, y, bonus, g, o_a, gr, ga, P, g_final, final, batch, tm)
    last = lambda t: t.reshape(batch, seq, -1)[:, -WINDOW:].reshape(batch, WINDOW, ATT_KV_HEADS, HEAD_DIM)
    state = (z_tail[:, SUBLANES - 1], s_t, last(k), last(v), c_tail[:, SUBLANES - (CONV_W - 1):])
    return x2, state


def _sample_layer(x, P, tables, g_final, final, shift_prev, wkv0, k_cache, v_cache, conv_prev, nb, steps):
    n = nb * steps
    layer = P['layer']
    q, k, v, gr, ga, z_tail, r, lw, kk, b, kf, vv, g, bonus = _front_sample(x, shift_prev, P, tables[0], tables[1])
    to_lanes = lambda t: t.reshape(steps, nb, RWKV_HEADS, HEAD_DIM).transpose(0, 2, 3, 1)
    y_l, s_l = _wkv_sample(*(to_lanes(t) for t in (r, lw, kk, b, kf)), to_lanes(vv).astype(F32), wkv0, layer, steps)
    y = y_l.transpose(0, 3, 1, 2).reshape(n, RWKV_DIM)

    rows = ATT_Q_HEADS * steps
    q5 = q.reshape(steps, nb, ATT_KV_HEADS, ATT_GROUP, HEAD_DIM).transpose(1, 2, 3, 0, 4)
    sel = jnp.eye(ATT_KV_HEADS, dtype=q.dtype)
    q_st = (q5[:, :, :, :, None, :] * sel[None, :, None, None, :, None]).reshape(nb, rows, ATT_KV_DIM)
    pad = ((0, 0), (0, SUBLANES - steps), (0, 0))
    k_new = k.reshape(steps, nb, ATT_KV_DIM).transpose(1, 0, 2)
    v_new = v.reshape(steps, nb, ATT_KV_DIM).transpose(1, 0, 2)
    o_st = _swa_sample(q_st, k_cache, v_cache, jnp.pad(k_new, pad), jnp.pad(v_new, pad), P['sink_rows'], layer,
                       steps, 16)
    o5 = o_st.reshape(nb, ATT_KV_HEADS, ATT_GROUP, steps, ATT_KV_HEADS, HEAD_DIM)
    o_a = jnp.stack([o5[:, j, :, :, j, :] for j in range(ATT_KV_HEADS)], axis=0)
    o_a = o_a.transpose(3, 1, 0, 2, 4).reshape(n, ATT_DIM)

    x2, c_tail = _back_sample(x, y, bonus, g, o_a, gr, ga, conv_prev, P, g_final, final)
    state = (z_tail, s_l, k_new, v_new, c_tail)
    return x2, state


def kernel(x_prompt, x_sample, state_rwkv_shift, state_rwkv_wkv, cache_swa_k, cache_swa_v, state_ffn_conv, norm_mix_g, w_in, rwkv_mu, rwkv_w0, rwkv_w2, rwkv_a0, rwkv_a2, rwkv_g2, rwkv_k_k, rwkv_k_a, rwkv_r_k, rwkv_ln_g, rwkv_ln_b, attn_sinks, w_br_rwkv, w_br_attn, w_out, norm_ffn_g, ffn_w_in, ffn_conv_w, ffn_conv_b, ffn_w_down, norm_final_g):
    bp, tp, _ = x_prompt.shape
    nb, steps, _ = x_sample.shape
    depth = w_in.shape[0]
    tab_p = _rope_tables(jnp.arange(tp, dtype=jnp.int32))
    tab_s = _rope_tables(PAST_LEN + jnp.repeat(jnp.arange(steps, dtype=jnp.int32), nb))
    g_final = norm_final_g[None]
    params = _stacked_params(norm_mix_g, w_in, rwkv_mu, rwkv_w0, rwkv_w2, rwkv_a0, rwkv_a2, rwkv_g2, rwkv_k_k,
                             rwkv_k_a, rwkv_r_k, rwkv_ln_g, rwkv_ln_b, attn_sinks, w_br_rwkv, w_br_attn, w_out,
                             norm_ffn_g, ffn_w_in, ffn_conv_w, ffn_conv_b, ffn_w_down, steps)
    wkv0 = state_rwkv_wkv.transpose(0, 2, 3, 4, 1)
    k_cache = cache_swa_k.reshape(depth, nb, WINDOW, ATT_KV_DIM)
    v_cache = cache_swa_v.reshape(depth, nb, WINDOW, ATT_KV_DIM)
    conv_prev = state_ffn_conv.transpose(0, 2, 1, 3)
    xp = x_prompt.reshape(bp * tp, D_MODEL)
    xs = x_sample.transpose(1, 0, 2).reshape(steps * nb, D_MODEL)
    outs_p, outs_s = [], []
    for l in range(depth):
        P = dict(params, layer=l)
        final = l == depth - 1
        xp, sp = _prompt_layer(xp, P, tab_p, g_final, final, bp, tp)
        xs, ss = _sample_layer(xs, P, tab_s, g_final, final, state_rwkv_shift, wkv0, k_cache, v_cache, conv_prev,
                               nb, steps)
        outs_p.append(sp)
        outs_s.append(ss)
    y_prompt = xp.reshape(bp, tp, D_MODEL)
    y_sample = xs.reshape(steps, nb, D_MODEL).transpose(1, 0, 2)
    stack = lambda outs, i: jnp.stack([o[i] for o in outs])
    kv_shape = (depth, nb, steps, ATT_KV_HEADS, HEAD_DIM)
    s_k = jnp.concatenate([cache_swa_k[:, :, steps:], stack(outs_s, 2).reshape(kv_shape)], axis=2)
    s_v = jnp.concatenate([cache_swa_v[:, :, steps:], stack(outs_s, 3).reshape(kv_shape)], axis=2)
    s_wkv = stack(outs_s, 1).transpose(0, 4, 1, 2, 3)
    s_conv = stack(outs_s, 4).transpose(0, 2, 1, 3)
    return (y_prompt, y_sample, *(stack(outs_p, i) for i in range(5)),
            stack(outs_s, 0), s_wkv, s_k, s_v, s_conv)
```

```python
import functools

import jax
import jax.numpy as jnp
from jax import lax
from jax.experimental import pallas as pl
from jax.experimental.pallas import tpu as pltpu

F32 = jnp.float32
BF16 = jnp.bfloat16

D_MODEL = 1024
HEAD_DIM = 64
RWKV_HEADS = 8
RWKV_DIM = RWKV_HEADS * HEAD_DIM
D_DECAY_LORA = 64
D_AAA_LORA = 64
D_GATE_LORA = 128
RWKV_PROJ = 3 * RWKV_DIM + D_DECAY_LORA + D_AAA_LORA + D_GATE_LORA
ATT_Q_HEADS = 8
ATT_KV_HEADS = 2
ATT_GROUP = ATT_Q_HEADS // ATT_KV_HEADS
ATT_DIM = ATT_Q_HEADS * HEAD_DIM
ATT_KV_DIM = ATT_KV_HEADS * HEAD_DIM
WINDOW = 128
IN_PROJ = RWKV_PROJ + ATT_DIM + 2 * ATT_KV_DIM + 2 * D_MODEL
D_FF = 2816
CONV_W = 3
ROPE_THETA = 10000.0
RMS_EPS = 1e-6
GN_EPS = 64e-5
PAST_LEN = 16384

LANES = 128
SUBLANES = 8
MXU_DIM = 256
CHUNK = 64
HEADS_PER_GROUP = MXU_DIM // HEAD_DIM
VMEM_LIMIT = 56 * 1024 * 1024


def _params(sem):
    return pltpu.CompilerParams(dimension_semantics=sem, vmem_limit_bytes=VMEM_LIMIT)


def _const_spec(shape):
    zeros = (0,) * len(shape)
    return pl.BlockSpec(shape, lambda *_: zeros, pipeline_mode=pl.Buffered(1))


def _layer_spec(shape, layer):
    index = (layer,) + (0,) * len(shape)
    return pl.BlockSpec((None,) + tuple(shape), lambda *_: index, pipeline_mode=pl.Buffered(1))


def _mm(a, b):
    return jnp.dot(a.astype(BF16), b.astype(BF16), preferred_element_type=F32)


def _mm_nt(a, b):
    return lax.dot_general(a.astype(BF16), b.astype(BF16), (((1,), (1,)), ((), ())),
                           preferred_element_type=F32)


def _mm_f32(a, b):
    return jnp.dot(a, b, precision=lax.Precision.HIGHEST, preferred_element_type=F32)


def _mm_tn(a, b):
    return lax.dot_general(a.astype(BF16), b.astype(BF16), (((0,), (0,)), ((), ())),
                           preferred_element_type=F32)


def _head_sum(x, ones_bd):
    return jnp.dot(x.astype(BF16), ones_bd, preferred_element_type=F32)


def _rmsnorm(x, g):
    return x * lax.rsqrt(jnp.mean(x * x, axis=-1, keepdims=True) + RMS_EPS) * g


def _rope(x, cos, sin_signed, first_half):
    partner = jnp.where(first_half, pltpu.roll(x, LANES - HEAD_DIM // 2, 1), pltpu.roll(x, HEAD_DIM // 2, 1))
    return x * cos + partner * sin_signed


def _prep_math(z, zp, mu, w0, w2p, a0, a2p, g2, k_k, k_a, r_k, ones_bd, outs, rs):
    r_ref, lw_ref, kk_ref, b_ref, kf_ref, v_ref, g_ref, bonus_ref = outs
    zs = z + mu * (zp - z)
    d = RWKV_DIM
    r, k, v = zs[:, 0:d], zs[:, d:2 * d], zs[:, 2 * d:3 * d]
    zwa = zs[:, 3 * d:3 * d + LANES]
    zg = zs[:, 3 * d + LANES:3 * d + 2 * LANES]
    w_pre = w0 + _mm(jnp.tanh(zwa), w2p)
    neg = -w_pre
    softplus = jnp.maximum(neg, 0.0) + jnp.log1p(jnp.exp(-jnp.abs(neg)))
    w_log = -softplus - 0.5
    lw_ref[rs, :] = -jnp.exp(w_log)
    a = jax.nn.sigmoid(a0 + _mm(zwa, a2p))
    g_ref[rs, :] = _mm(jax.nn.sigmoid(zg), g2).astype(g_ref.dtype)
    kk0 = k * k_k
    norm = jnp.sqrt(_head_sum(kk0 * kk0, ones_bd))
    kk = kk0 / jnp.maximum(norm, 1e-12)
    kf = k * (1.0 + (a - 1.0) * k_a)
    r_ref[rs, :] = r.astype(r_ref.dtype)
    kk_ref[rs, :] = kk.astype(kk_ref.dtype)
    b_ref[rs, :] = (kk * a).astype(b_ref.dtype)
    kf_ref[rs, :] = kf.astype(kf_ref.dtype)
    v_ref[rs, :] = v.astype(v_ref.dtype)
    bonus_ref[rs, :] = (_head_sum(r * kf * r_k, ones_bd) * v).astype(bonus_ref.dtype)


N_PREP_CONSTS = 10
N_FRONT_OUTS = 14


def _row_blocks(n, nsub):
    rb = n // nsub
    return [slice(i * rb, (i + 1) * rb) for i in range(nsub)], rb


def _staggered(stages, nsub):
    for step in range(nsub + len(stages) - 1):
        for i in range(nsub):
            if 0 <= step - i < len(stages):
                stages[step - i](i)


def _front_kernel(*refs, sample, nsub):
    if sample:
        x_ref, sp_ref, g_ref, w_ref, cos_ref, sin_ref = refs[:6]
        rest = refs[6:]
    else:
        x_ref, g_ref, w_ref, cos_ref, sin_ref = refs[:5]
        rest = refs[5:]
    consts = [c[...] for c in rest[:N_PREP_CONSTS]]
    outs = rest[N_PREP_CONSTS:N_PREP_CONSTS + N_FRONT_OUTS]
    q_ref, k_ref, v_ref, gr_ref, ga_ref, zt_ref = outs[:6]
    blocks, rb = _row_blocks(x_ref.shape[0], nsub)
    c0, c1, c2, c3, c4 = (RWKV_PROJ, RWKV_PROJ + ATT_DIM, RWKV_PROJ + ATT_DIM + ATT_KV_DIM,
                          RWKV_PROJ + ATT_DIM + 2 * ATT_KV_DIM, RWKV_PROJ + ATT_DIM + 2 * ATT_KV_DIM + D_MODEL)
    if not sample:
        carry = rest[N_PREP_CONSTS + N_FRONT_OUTS]

        @pl.when(pl.program_id(1) == 0)
        def _():
            carry[...] = jnp.zeros_like(carry)

    row = lax.broadcasted_iota(jnp.int32, (rb, 1), 0)
    lane = lax.broadcasted_iota(jnp.int32, (1, LANES), 1)
    first_half = (lane % HEAD_DIM) < (HEAD_DIM // 2)
    hs, zs = {}, {}

    def project(i):
        hs[i] = _rmsnorm(x_ref[blocks[i], :], g_ref[...]).astype(BF16)
        zs[i] = jnp.dot(hs[i], w_ref[:, 0:c0], preferred_element_type=F32)

    def rwkv_operands(i):
        z = zs[i]
        if sample:
            zp = sp_ref[...] if i == 0 else zs[i - 1]
            if i == nsub - 1:
                zt_ref[...] = z
        else:
            tail = carry[...] if i == 0 else zs[i - 1][rb - SUBLANES:, :]
            zp = jnp.where(row == 0, tail[SUBLANES - 1:SUBLANES, :], pltpu.roll(z, 1, 0))
            if i == nsub - 1:
                carry[...] = z[rb - SUBLANES:, :]
                zt_ref[0] = z[rb - SUBLANES:, :]
        _prep_math(z, zp, *consts, outs[6:], blocks[i])

    def attention_operands(i):
        rs, h = blocks[i], hs[i]
        cos = cos_ref[rs, :]
        sin = sin_ref[rs, :]
        q = jnp.dot(h, w_ref[:, c0:c1], preferred_element_type=F32)
        for grp in range(ATT_DIM // LANES):
            sl = slice(grp * LANES, (grp + 1) * LANES)
            q_ref[rs, sl] = _rope(q[:, sl], cos, sin, first_half).astype(q_ref.dtype)
        k = jnp.dot(h, w_ref[:, c1:c2], preferred_element_type=F32)
        k_ref[rs, :] = _rope(k, cos, sin, first_half)
        v_ref[rs, :] = jnp.dot(h, w_ref[:, c2:c3], preferred_element_type=F32)
        gr_ref[rs, :] = jax.nn.sigmoid(jnp.dot(h, w_ref[:, c3:c4], preferred_element_type=F32)).astype(gr_ref.dtype)
        ga_ref[rs, :] = jax.nn.sigmoid(jnp.dot(h, w_ref[:, c4:IN_PROJ], preferred_element_type=F32)).astype(ga_ref.dtype)

    _staggered([project, rwkv_operands, attention_operands], nsub)


def _prep_consts(P):
    return [P['mu'], P['w0'], P['w2p'], P['a0'], P['a2p'], P['g2'], P['k_k'], P['k_a'], P['r_k'], P['ones_bd']]


def _prep_const_specs(layer):
    d = RWKV_DIM
    shapes = [(1, RWKV_PROJ), (1, d), (LANES, d), (1, d), (LANES, d), (D_GATE_LORA, d), (1, d), (1, d), (1, d)]
    return [_layer_spec(s, layer) for s in shapes] + [_const_spec((d, d))]


def _front_out_types(n, tail_shape):
    d = RWKV_DIM
    widths = [(ATT_DIM, BF16), (ATT_KV_DIM, F32), (ATT_KV_DIM, F32), (D_MODEL, BF16), (D_MODEL, BF16)]
    prep = [(d, BF16), (d, F32), (d, BF16), (d, BF16), (d, BF16), (d, BF16), (d, BF16), (d, BF16)]
    shapes = ([jax.ShapeDtypeStruct((n, w), t) for w, t in widths] + [jax.ShapeDtypeStruct(tail_shape, F32)]
              + [jax.ShapeDtypeStruct((n, w), t) for w, t in prep])
    return shapes, [w for w, _ in widths], [w for w, _ in prep]


def _front_prompt(x2d, P, cos_t, sin_t, batch, tm):
    n = x2d.shape[0]
    layer = P['layer']
    nt = n // batch // tm
    row = lambda b, t: (b * nt + t, 0)
    pos = lambda b, t: (t, 0)
    shapes, widths, prep = _front_out_types(n, (batch, SUBLANES, RWKV_PROJ))
    return pl.pallas_call(
        functools.partial(_front_kernel, sample=False, nsub=tm // 128),
        grid=(batch, nt),
        in_specs=[pl.BlockSpec((tm, D_MODEL), row), _layer_spec((1, D_MODEL), layer),
                  _layer_spec((D_MODEL, IN_PROJ), layer),
                  pl.BlockSpec((tm, LANES), pos), pl.BlockSpec((tm, LANES), pos)] + _prep_const_specs(layer),
        out_specs=([pl.BlockSpec((tm, w), row) for w in widths]
                   + [pl.BlockSpec((1, SUBLANES, RWKV_PROJ), lambda b, t: (b, 0, 0))]
                   + [pl.BlockSpec((tm, w), row) for w in prep]),
        out_shape=shapes,
        scratch_shapes=[pltpu.VMEM((SUBLANES, RWKV_PROJ), F32)],
        compiler_params=_params(("arbitrary", "arbitrary")),
        name="front_prompt",
    )(x2d, P['norm_mix_g'], P['w_in'], cos_t, sin_t, *_prep_consts(P))


def _front_sample(x2d, shift_prev, P, cos_t, sin_t):
    n = x2d.shape[0]
    layer = P['layer']
    nb = shift_prev.shape[1]
    shapes, _, _ = _front_out_types(n, (nb, RWKV_PROJ))
    return pl.pallas_call(
        functools.partial(_front_kernel, sample=True, nsub=n // nb),
        grid=(1,),
        in_specs=[_const_spec((n, D_MODEL)), _layer_spec((nb, RWKV_PROJ), layer), _layer_spec((1, D_MODEL), layer),
                  _layer_spec((D_MODEL, IN_PROJ), layer), _const_spec((n, LANES)), _const_spec((n, LANES))]
        + _prep_const_specs(layer),
        out_specs=[_const_spec(sh.shape) for sh in shapes],
        out_shape=shapes,
        compiler_params=_params(("arbitrary",)),
        name="front_sample",
    )(x2d, shift_prev, P['norm_mix_g'], P['w_in'], cos_t, sin_t, *_prep_consts(P))


def _wkv_prompt_kernel(r_ref, lw_ref, kk_ref, b_ref, kf_ref, v_ref, y_ref, st_ref, s_scr, *, chunks, together):
    C, G, M = CHUNK, HEADS_PER_GROUP, MXU_DIM
    nbatch = r_ref.shape[0]
    ngrp = RWKV_HEADS // G

    @pl.when(pl.program_id(0) == 0)
    def _():
        s_scr[...] = jnp.zeros_like(s_scr)

    ti = lax.broadcasted_iota(jnp.int32, (C, M), 0)
    si = lax.broadcasted_iota(jnp.int32, (C, M), 1) % C
    strict = ti > si
    incl = ti >= si
    eye = (ti == si).astype(F32)
    tri = (lax.broadcasted_iota(jnp.int32, (C, C), 0) >= lax.broadcasted_iota(jnp.int32, (C, C), 1)).astype(F32)
    lane_head = lax.broadcasted_iota(jnp.int32, (1, M), 1) // HEAD_DIM
    same_head = (lax.broadcasted_iota(jnp.int32, (M, M), 0) // HEAD_DIM
                 == lax.broadcasted_iota(jnp.int32, (M, M), 1) // HEAD_DIM)

    def stack(x):
        xb = x.astype(BF16)
        return jnp.concatenate([jnp.where(lane_head == h, xb, jnp.zeros_like(xb)) for h in range(G)], axis=0)

    def rows2(top, bottom):
        return jnp.concatenate([top.astype(BF16), bottom.astype(BF16)], axis=0)

    def load(rows):
        probs = []
        for bi in range(nbatch):
            lw = lw_ref[bi, rows, :]
            cum = _mm_f32(tri, lw)
            w_end = jnp.exp(cum[C - 1:C, :])
            e_neg = jnp.exp(-cum)
            a_hat = -kk_ref[bi, rows, :] * jnp.exp(cum - lw)
            r_hat = r_ref[bi, rows, :] * jnp.exp(cum)
            b_til = b_ref[bi, rows, :] * e_neg
            k_til = kf_ref[bi, rows, :] * e_neg
            v = v_ref[bi, rows, :]
            for grp in range(ngrp):
                ls = slice(grp * M, (grp + 1) * M)
                we = w_end[:, ls]
                probs.append(dict(
                    bi=bi, grp=grp, ls=ls, we=we, rows=rows, ar=rows2(a_hat[:, ls], r_hat[:, ls]),
                    b=stack(b_til[:, ls]), k=stack(k_til[:, ls]), v=v[:, ls], vs=stack(v[:, ls]),
                    bk=rows2(b_til[:, ls] * we, k_til[:, ls] * we)))
        return probs

    def state_free(probs):
        l_ab, p_rb, lp_k = [], [], []
        for p in probs:
            x = _mm_nt(p['ar'], p['b'])
            l_ab.append(jnp.where(strict, x[0:C], 0.0))
            p_rb.append(jnp.where(incl, x[C:2 * C], 0.0).astype(BF16))
        for p in probs:
            x = _mm_nt(p['ar'], p['k'])
            lp_k.append(rows2(jnp.where(strict, x[0:C], 0.0), jnp.where(incl, x[C:2 * C], 0.0)))
        t_inv = [eye + x for x in l_ab]
        pw = [_mm(x, stack(x)) for x in l_ab]
        for _ in range(C.bit_length() - 3):
            for i, (t, x) in enumerate(zip(t_inv, pw)):
                both = _mm(rows2(t, x), stack(x))
                t_inv[i] = t + both[0:C]
                pw[i] = both[C:2 * C]
        t_inv = [t + _mm(t, stack(x)) for t, x in zip(t_inv, pw)]
        e0 = [_mm(x, p['vs']) for x, p in zip(lp_k, probs)]
        return t_inv, p_rb, e0

    def advance(probs, t_inv, p_rb, e0):
        s0 = [s_scr[p['bi'], p['grp']] for p in probs]
        d0 = [_mm_nt(p['ar'], x) for p, x in zip(probs, s0)]
        u = [_mm(t, stack(x[0:C] + y[0:C])) for t, x, y in zip(t_inv, d0, e0)]
        for p, x, y, w, q, st in zip(probs, d0, e0, u, p_rb, s0):
            y_ref[p['bi'], p['rows'], p['ls']] = x[C:2 * C] + y[C:2 * C] + _mm(q, stack(w))
            upd = _mm_tn(rows2(w, p['v']), p['bk'])
            s_scr[p['bi'], p['grp']] = st * p['we'] + jnp.where(same_head, upd, 0.0)

    def chunk_group(c, carry):
        chunk_probs = [load(pl.ds(pl.multiple_of((c * together + j) * C, C), C)) for j in range(together)]
        t_inv, p_rb, e0 = state_free([p for probs in chunk_probs for p in probs])
        n = len(chunk_probs[0])
        for j, probs in enumerate(chunk_probs):
            sl = slice(j * n, (j + 1) * n)
            advance(probs, t_inv[sl], p_rb[sl], e0[sl])
        return carry

    lax.fori_loop(0, chunks // together, chunk_group, 0)

    @pl.when(pl.program_id(0) == pl.num_programs(0) - 1)
    def _():
        st_ref[...] = s_scr[...]


def _wkv_prompt(r, lw, kk, b, kf, v, batch, tl):
    n = r.shape[0]
    seq = n // batch
    ngrp = RWKV_HEADS // HEADS_PER_GROUP
    blk = pl.BlockSpec((batch, tl, RWKV_DIM), lambda t: (0, t, 0))
    y, st = pl.pallas_call(
        functools.partial(_wkv_prompt_kernel, chunks=tl // CHUNK, together=2),
        grid=(seq // tl,),
        in_specs=[blk] * 6,
        out_specs=[blk, _const_spec((batch, ngrp, MXU_DIM, MXU_DIM))],
        out_shape=[jax.ShapeDtypeStruct((batch, seq, RWKV_DIM), F32),
                   jax.ShapeDtypeStruct((batch, ngrp, MXU_DIM, MXU_DIM), F32)],
        scratch_shapes=[pltpu.VMEM((batch, ngrp, MXU_DIM, MXU_DIM), F32)],
        compiler_params=_params(("arbitrary",)),
        name="wkv_prompt",
    )(*(t.reshape(batch, seq, RWKV_DIM) for t in (r, lw, kk, b, kf, v)))
    st = st.reshape(batch, ngrp, HEADS_PER_GROUP, HEAD_DIM, HEADS_PER_GROUP, HEAD_DIM)
    st = jnp.stack([st[:, :, h, :, h, :] for h in range(HEADS_PER_GROUP)], axis=2)
    st = st.reshape(batch, RWKV_HEADS, HEAD_DIM, HEAD_DIM)
    return y.reshape(n, RWKV_DIM), st


def _wkv_sample_kernel(r_ref, lw_ref, kk_ref, b_ref, kf_ref, v_ref, s_ref, y_ref, so_ref, *, steps):
    group = SUBLANES // 2

    def value_rows(gi, carry):
        vis = [gi * group + j for j in range(group)]
        ss = [s_ref[0, vi] for vi in vis]
        for t in range(steps):
            kk, w, b = kk_ref[t, 0].astype(F32), jnp.exp(lw_ref[t, 0]), b_ref[t, 0].astype(F32)
            kf, r = kf_ref[t, 0].astype(F32), r_ref[t, 0].astype(F32)
            sks = [jnp.sum(s * kk, axis=0, keepdims=True) for s in ss]
            ss = [s * w - sk * b + v_ref[t, 0, pl.ds(vi, 1), :] * kf for s, sk, vi in zip(ss, sks, vis)]
            for s, vi in zip(ss, vis):
                y_ref[t, 0, pl.ds(vi, 1), :] = jnp.sum(s * r, axis=0, keepdims=True)
        for s, vi in zip(ss, vis):
            so_ref[0, vi] = s
        return carry

    lax.fori_loop(0, HEAD_DIM // group, value_rows, 0)


def _wkv_sample(ops, s0, layer, steps):
    nops, _, _, _, nb = ops.shape
    vec = pl.BlockSpec((steps, 1, HEAD_DIM, nb), lambda h: (0, h, 0, 0))
    vec_in = [pl.BlockSpec((None, steps, 1, HEAD_DIM, nb), functools.partial(lambda i, h: (i, 0, h, 0, 0), i))
              for i in range(nops)]
    st = pl.BlockSpec((1, HEAD_DIM, HEAD_DIM, nb), lambda h: (h, 0, 0, 0))
    st_in = pl.BlockSpec((None, 1, HEAD_DIM, HEAD_DIM, nb), lambda h: (layer, h, 0, 0, 0))
    return pl.pallas_call(
        functools.partial(_wkv_sample_kernel, steps=steps),
        grid=(RWKV_HEADS,),
        in_specs=vec_in + [st_in],
        out_specs=[vec, st],
        out_shape=[jax.ShapeDtypeStruct(ops.shape[1:], F32), jax.ShapeDtypeStruct(s0.shape[1:], F32)],
        compiler_params=_params(("arbitrary",)),
        name="wkv_sample",
    )(*([ops] * nops), s0)


def _swa_prompt_kernel(q_ref, kc_ref, kp_ref, vc_ref, vp_ref, sink_ref, o_ref, *, sub):
    blk = WINDOW
    first = pl.program_id(1) == 0
    k_all = jnp.concatenate([kp_ref[...], kc_ref[...]], axis=0)
    v_all = jnp.concatenate([vp_ref[...], vc_ref[...]], axis=0).astype(BF16)
    v_ext = jnp.concatenate([v_all, jnp.ones_like(v_all)], axis=1)
    k_swapped = pltpu.roll(k_all, HEAD_DIM, 1)
    lane = lax.broadcasted_iota(jnp.int32, (1, LANES), 1)
    low = lane < HEAD_DIM
    k_var = {(0, 0): jnp.where(low, k_all, 0.0), (0, 1): jnp.where(low, 0.0, k_swapped),
             (1, 1): jnp.where(low, 0.0, k_all), (1, 0): jnp.where(low, k_swapped, 0.0)}
    k_var = {key: val.astype(BF16) for key, val in k_var.items()}
    order = [j * ATT_GROUP + g for j in range(ATT_KV_HEADS) for half in range(2)
             for g in range(ATT_GROUP) if g % 2 == half]
    n_rows = ATT_Q_HEADS * blk
    qi = lax.broadcasted_iota(jnp.int32, (n_rows, 2 * blk), 0) % blk
    kj = lax.broadcasted_iota(jnp.int32, (n_rows, 2 * blk), 1)
    diff = blk + qi - kj
    band = (diff >= 0) & (diff < WINDOW)
    row_head = lax.broadcasted_iota(jnp.int32, (n_rows, 1), 0) // blk
    sink = jnp.zeros((n_rows, LANES), F32)
    for idx, head in enumerate(order):
        sink = jnp.where(row_head == idx, sink_ref[0:1, head:head + 1], sink)
    scale = HEAD_DIM ** -0.5
    scores, maxes, weights, sums = {}, {}, {}, {}

    def score(sb):
        qs = slice(sb * blk, (sb + 1) * blk)
        ks = slice(sb * blk, (sb + 2) * blk)
        parts = []
        for j in range(ATT_KV_HEADS):
            for half in range(2):
                heads = [j * ATT_GROUP + g for g in range(ATT_GROUP) if g % 2 == half]
                q_cat = jnp.concatenate([q_ref[qs, (h // 2) * LANES:(h // 2 + 1) * LANES] for h in heads], axis=0)
                parts.append(_mm_nt(q_cat * scale, k_var[(j, half)][ks]))
        s = jnp.concatenate(parts, axis=0)
        ok = band & jnp.logical_or(kj >= blk, jnp.logical_not(first)) if sb == 0 else band
        scores[sb] = jnp.where(ok, s, -jnp.inf)

    def exponentiate(sb):
        s = scores[sb]
        row_max = jnp.max(jnp.maximum(s[:, 0:LANES], s[:, LANES:2 * LANES]), axis=-1, keepdims=True)
        m = jnp.maximum(row_max, sink)
        maxes[sb] = m
        weights[sb] = jnp.exp(s - jnp.concatenate([m, m], axis=1)).astype(BF16)

    def weigh(sb):
        ks = slice(sb * blk, (sb + 2) * blk)
        sums[sb] = jnp.dot(weights[sb], v_ext[ks], preferred_element_type=F32)

    def emit(sb):
        qs = slice(sb * blk, (sb + 1) * blk)
        ev = sums[sb]
        o = ev[:, 0:LANES] / (ev[:, LANES:2 * LANES] + jnp.exp(sink - maxes[sb]))
        by_head = {}
        for idx, head in enumerate(order):
            o_h = o[idx * blk:(idx + 1) * blk]
            by_head[head] = pltpu.roll(o_h, HEAD_DIM, 1) if head // ATT_GROUP != head % 2 else o_h
        for grp in range(ATT_DIM // LANES):
            o_ref[qs, grp * LANES:(grp + 1) * LANES] = jnp.where(low, by_head[2 * grp],
                                                                 by_head[2 * grp + 1]).astype(o_ref.dtype)

    _staggered([score, exponentiate, weigh, emit], sub)


def _swa_prompt(q, k, v, P, batch, sub):
    n = q.shape[0]
    rows = sub * WINDOW
    nb = n // batch // rows
    cur = lambda b, i: (b * nb + i, 0)
    prev = lambda b, i: ((b * nb + i) * sub - jnp.minimum(i, 1), 0)
    return pl.pallas_call(
        functools.partial(_swa_prompt_kernel, sub=sub),
        grid=(batch, nb),
        in_specs=[pl.BlockSpec((rows, ATT_DIM), cur),
                  pl.BlockSpec((rows, ATT_KV_DIM), cur), pl.BlockSpec((WINDOW, ATT_KV_DIM), prev),
                  pl.BlockSpec((rows, ATT_KV_DIM), cur), pl.BlockSpec((WINDOW, ATT_KV_DIM), prev),
                  _layer_spec((1, ATT_Q_HEADS), P['layer'])],
        out_specs=pl.BlockSpec((rows, ATT_DIM), cur),
        out_shape=jax.ShapeDtypeStruct((n, ATT_DIM), BF16),
        compiler_params=_params(("arbitrary", "arbitrary")),
        name="swa_prompt",
    )(q, k, k, v, v, P['sinks'])


def _swa_sample_kernel(q_ref, kc_ref, vc_ref, kn_ref, vn_ref, sink_ref, o_ref, *, steps):
    q = q_ref[...].astype(BF16)
    rows = q.shape[1]
    nk = kc_ref.shape[1]
    nn = kn_ref.shape[1]
    bdot = lambda a, b, dims: lax.dot_general(a, b, (dims, ((0,), (0,))), preferred_element_type=F32)
    s_c = bdot(q, kc_ref[...].astype(BF16), ((2,), (2,))) * (HEAD_DIM ** -0.5)
    s_n = bdot(q, kn_ref[...].astype(BF16), ((2,), (2,))) * (HEAD_DIM ** -0.5)
    t_c = lax.broadcasted_iota(jnp.int32, (1, rows, nk), 1) % steps
    w_c = lax.broadcasted_iota(jnp.int32, (1, rows, nk), 2)
    s_c = jnp.where(w_c > t_c, s_c, -jnp.inf)
    t_n = lax.broadcasted_iota(jnp.int32, (1, rows, nn), 1) % steps
    w_n = lax.broadcasted_iota(jnp.int32, (1, rows, nn), 2)
    s_n = jnp.where(w_n <= t_n, s_n, -jnp.inf)
    sink = sink_ref[...][None, :, 0:1]
    m = jnp.maximum(jnp.maximum(jnp.max(s_c, axis=-1, keepdims=True), jnp.max(s_n, axis=-1, keepdims=True)), sink)
    e_c = jnp.exp(s_c - m)
    e_n = jnp.exp(s_n - m)
    den = jnp.sum(e_c, axis=-1, keepdims=True) + jnp.sum(e_n, axis=-1, keepdims=True) + jnp.exp(sink - m)
    o = (bdot((e_c / den).astype(BF16), vc_ref[...].astype(BF16), ((2,), (1,)))
         + bdot((e_n / den).astype(BF16), vn_ref[...].astype(BF16), ((2,), (1,))))
    o_ref[...] = o


def _swa_sample(q_st, k_cache, v_cache, k_new, v_new, sink_rows, layer, steps, bt):
    nb, rows, _ = q_st.shape
    nn = k_new.shape[1]
    blk = lambda r: pl.BlockSpec((bt, r, LANES), lambda i: (i, 0, 0))
    cache = pl.BlockSpec((None, bt, WINDOW, LANES), lambda i: (layer, i, 0, 0))
    return pl.pallas_call(
        functools.partial(_swa_sample_kernel, steps=steps),
        grid=(nb // bt,),
        in_specs=[blk(rows), cache, cache, blk(nn), blk(nn), _layer_spec((rows, LANES), layer)],
        out_specs=blk(rows),
        out_shape=jax.ShapeDtypeStruct((nb, rows, LANES), F32),
        compiler_params=_params(("arbitrary",)),
        name="swa_sample",
    )(q_st, k_cache, v_cache, k_new, v_new, sink_rows)


def _gelu(x):
    return 0.5 * x * (1.0 + lax.erf(x * (2.0 ** -0.5)))


def _back_kernel(*refs, final, sample, nsub):
    x_ref, y_ref, bonus_ref, g_ref, oa_ref, gr_ref, ga_ref = refs[:7]
    at = 7
    if sample:
        cp_ref = refs[at]
        at += 1
    (lng_ref, lnb_ref, ones_ref, wa_ref, wb_ref, wo_ref, gn_ref, wi_ref, cw_ref, cb_ref, wd_ref,
     gf_ref) = refs[at:at + 12]
    o_ref, ct_ref, carry = refs[at + 12:at + 15]
    blocks, rb = _row_blocks(x_ref.shape[0], nsub)
    ones_bd = ones_ref[...]

    if sample:
        @pl.when(pl.program_id(0) == 0)
        def _():
            carry[...] = cp_ref[...]
    else:
        @pl.when(pl.program_id(1) == 0)
        def _():
            carry[...] = jnp.zeros_like(carry)

    row = lax.broadcasted_iota(jnp.int32, (rb, 1), 0)
    xs, cs, ups, acts = {}, {}, {}, {}

    def mix(i):
        rs = blocks[i]
        y = y_ref[rs, :]
        mean = _head_sum(y, ones_bd) * (1.0 / HEAD_DIM)
        d = y - mean
        var = _head_sum(d * d, ones_bd) * (1.0 / HEAD_DIM)
        yn = d * lax.rsqrt(var + GN_EPS) * lng_ref[...] + lnb_ref[...]
        o_r = (yn + bonus_ref[rs, :]) * g_ref[rs, :]
        merged = gr_ref[rs, :] * _mm(o_r, wa_ref[...]) + ga_ref[rs, :] * _mm(oa_ref[rs, :], wb_ref[...])
        xs[i] = x_ref[rs, :] + _mm(merged, wo_ref[...])

    def up_project(i):
        h = _rmsnorm(xs[i], gn_ref[...]).astype(BF16)
        cs[i] = jnp.dot(h, wi_ref[:, 0:D_FF], preferred_element_type=F32)
        ups[i] = jnp.dot(h, wi_ref[:, D_FF:2 * D_FF], preferred_element_type=F32)

    def conv_gate(i):
        c = cs[i]
        if sample:
            c2, c1 = carry[0], carry[1]
            carry[0] = c1
            carry[1] = c
            ct_ref[0] = c
        else:
            tail = carry[...] if i == 0 else cs[i - 1][rb - SUBLANES:, :]
            last = tail[SUBLANES - 1:SUBLANES, :]
            c1 = jnp.where(row == 0, last, pltpu.roll(c, 1, 0))
            c2 = jnp.where(row == 0, tail[SUBLANES - 2:SUBLANES - 1, :],
                           jnp.where(row == 1, last, pltpu.roll(c, 2, 0)))
            if i == nsub - 1:
                carry[...] = c[rb - SUBLANES:, :]
                ct_ref[0] = c[rb - SUBLANES:, :]
        conv = cb_ref[...] + c2 * cw_ref[0:1, :] + c1 * cw_ref[1:2, :] + c * cw_ref[2:3, :]
        acts[i] = (_gelu(conv) * ups[i]).astype(BF16)

    def down_project(i):
        out = xs[i] + jnp.dot(acts[i], wd_ref[...], preferred_element_type=F32)
        o_ref[blocks[i], :] = _rmsnorm(out, gf_ref[...]) if final else out

    _staggered([mix, up_project, conv_gate, down_project], nsub)


def _back_weight_specs(layer):
    return [_layer_spec((1, RWKV_DIM), layer), _layer_spec((1, RWKV_DIM), layer), _const_spec((RWKV_DIM, RWKV_DIM)),
            _layer_spec((RWKV_DIM, D_MODEL), layer), _layer_spec((ATT_DIM, D_MODEL), layer),
            _layer_spec((D_MODEL, D_MODEL), layer),
            _layer_spec((1, D_MODEL), layer), _layer_spec((D_MODEL, 2 * D_FF), layer),
            _layer_spec((CONV_W, D_FF), layer), _layer_spec((1, D_FF), layer), _layer_spec((D_FF, D_MODEL), layer),
            _const_spec((1, D_MODEL))]


def _back_weights(P, g_final):
    return [P['ln_g'], P['ln_b'], P['ones_bd'], P['w_br_rwkv'], P['w_br_attn'], P['w_out'],
            P['norm_ffn_g'], P['ffn_w_in'], P['ffn_conv_w'], P['ffn_conv_b'], P['ffn_w_down'], g_final]


def _back_prompt(x, y, bonus, g, o_a, gr, ga, P, g_final, final, batch, tm):
    n = x.shape[0]
    nt = n // batch // tm
    row = lambda b, t: (b * nt + t, 0)
    wide = pl.BlockSpec((tm, D_MODEL), row)
    half = pl.BlockSpec((tm, RWKV_DIM), row)
    return pl.pallas_call(
        functools.partial(_back_kernel, final=final, sample=False, nsub=tm // 256),
        grid=(batch, nt),
        in_specs=[wide, half, half, half, half, wide, wide] + _back_weight_specs(P['layer']),
        out_specs=[wide, pl.BlockSpec((1, SUBLANES, D_FF), lambda b, t: (b, 0, 0))],
        out_shape=[jax.ShapeDtypeStruct((n, D_MODEL), F32), jax.ShapeDtypeStruct((batch, SUBLANES, D_FF), F32)],
        scratch_shapes=[pltpu.VMEM((SUBLANES, D_FF), F32)],
        compiler_params=_params(("arbitrary", "arbitrary")),
        name="back_prompt",
    )(x, y, bonus, g, o_a, gr, ga, *_back_weights(P, g_final))


def _back_sample(x, y, bonus, g, o_a, gr, ga, conv_prev, P, g_final, final):
    n = x.shape[0]
    nb = conv_prev.shape[2]
    steps = n // nb
    taps = CONV_W - 1
    row = lambda t: (t, 0)
    wide = pl.BlockSpec((nb, D_MODEL), row)
    half = pl.BlockSpec((nb, RWKV_DIM), row)
    return pl.pallas_call(
        functools.partial(_back_kernel, final=final, sample=True, nsub=1),
        grid=(steps,),
        in_specs=[wide, half, half, half, half, wide, wide, _layer_spec((taps, nb, D_FF), P['layer'])]
        + _back_weight_specs(P['layer']),
        out_specs=[wide, pl.BlockSpec((1, nb, D_FF), lambda t: (jnp.maximum(t - (steps - taps), 0), 0, 0))],
        out_shape=[jax.ShapeDtypeStruct((n, D_MODEL), F32), jax.ShapeDtypeStruct((taps, nb, D_FF), F32)],
        scratch_shapes=[pltpu.VMEM((taps, nb, D_FF), F32)],
        compiler_params=_params(("arbitrary",)),
        name="back_sample",
    )(x, y, bonus, g, o_a, gr, ga, conv_prev, *_back_weights(P, g_final))


def _rope_tables(pos):
    inv = ROPE_THETA ** (-jnp.arange(0, HEAD_DIM, 2, dtype=F32) / HEAD_DIM)
    ang = pos.astype(F32)[:, None] * inv[None, :]
    cos, sin = jnp.cos(ang), jnp.sin(ang)
    reps = LANES // HEAD_DIM
    return jnp.tile(jnp.concatenate([cos, cos], axis=1), (1, reps)), jnp.tile(jnp.concatenate([-sin, sin], axis=1),
                                                                             (1, reps))


def _stacked_params(norm_mix_g, w_in, rwkv_mu, rwkv_w0, rwkv_w2, rwkv_a0, rwkv_a2, rwkv_g2, rwkv_k_k, rwkv_k_a,
                    rwkv_r_k, rwkv_ln_g, rwkv_ln_b, attn_sinks, w_br_rwkv, w_br_attn, w_out, norm_ffn_g, ffn_w_in,
                    ffn_conv_w, ffn_conv_b, ffn_w_down, steps):
    d = RWKV_DIM
    depth = w_in.shape[0]
    head = jnp.arange(d) // HEAD_DIM
    row = lambda t: t.reshape(depth, 1, -1)
    sink_rows = jnp.broadcast_to(jnp.repeat(attn_sinks, steps, axis=1)[:, :, None],
                                 (depth, ATT_Q_HEADS * steps, LANES))
    return {
        'norm_mix_g': row(norm_mix_g), 'w_in': w_in.astype(BF16),
        'mu': row(rwkv_mu), 'w0': row(rwkv_w0),
        'w2p': jnp.pad(rwkv_w2, ((0, 0), (0, D_AAA_LORA), (0, 0))).astype(BF16),
        'a0': row(rwkv_a0),
        'a2p': jnp.pad(rwkv_a2, ((0, 0), (D_DECAY_LORA, 0), (0, 0))).astype(BF16),
        'g2': rwkv_g2.astype(BF16), 'k_k': row(rwkv_k_k), 'k_a': row(rwkv_k_a),
        'r_k': row(rwkv_r_k), 'ln_g': row(rwkv_ln_g), 'ln_b': row(rwkv_ln_b),
        'ones_bd': (head[:, None] == head[None, :]).astype(BF16),
        'sinks': row(attn_sinks), 'sink_rows': sink_rows,
        'w_br_rwkv': w_br_rwkv.astype(BF16), 'w_br_attn': w_br_attn.astype(BF16),
        'w_out': w_out.astype(BF16), 'norm_ffn_g': row(norm_ffn_g),
        'ffn_w_in': ffn_w_in.astype(BF16), 'ffn_conv_w': ffn_conv_w, 'ffn_conv_b': row(ffn_conv_b),
        'ffn_w_down': ffn_w_down.astype(BF16),
    }


def _prompt_layer(x, P, tables, g_final, final, batch, seq):
    tm = min(512, seq)
    q, k, v, gr, ga, z_tail, r, lw, kk, b, kf, vv, g, bonus = _front_prompt(x, P, tables[0], tables[1], batch,
                                                                            min(1024, seq))
    y, s_t = _wkv_prompt(r, lw, kk, b, kf, vv, batch, min(512, seq))
    o_a = _swa_prompt(q, k, v, P, batch, min(8, seq // WINDOW))
    x2, c_tail = _back_prompt(x, y, bonus, g, o_a, gr, ga, P, g_final, final, batch, tm)
    last = lambda t: t.reshape(batch, seq, -1)[:, -WINDOW:].reshape(batch, WINDOW, ATT_KV_HEADS, HEAD_DIM)
    state = (z_tail[:, SUBLANES - 1], s_t, last(k), last(v), c_tail[:, SUBLANES - (CONV_W - 1):])
    return x2, state


def _sample_layer(x, P, tables, g_final, final, shift_prev, wkv0, k_cache, v_cache, conv_prev, nb, steps):
    n = nb * steps
    layer = P['layer']
    q, k, v, gr, ga, z_tail, r, lw, kk, b, kf, vv, g, bonus = _front_sample(x, shift_prev, P, tables[0], tables[1])
    ops = jnp.stack([t.astype(F32) for t in (r, lw, kk, b, kf, vv)])
    ops = ops.reshape(6, steps, nb, RWKV_HEADS, HEAD_DIM).transpose(0, 1, 3, 4, 2)
    y_l, s_l = _wkv_sample(ops, wkv0, layer, steps)
    y = y_l.transpose(0, 3, 1, 2).reshape(n, RWKV_DIM)

    rows = ATT_Q_HEADS * steps
    q5 = q.reshape(steps, nb, ATT_KV_HEADS, ATT_GROUP, HEAD_DIM).transpose(1, 2, 3, 0, 4)
    sel = jnp.eye(ATT_KV_HEADS, dtype=q.dtype)
    q_st = (q5[:, :, :, :, None, :] * sel[None, :, None, None, :, None]).reshape(nb, rows, ATT_KV_DIM)
    pad = ((0, 0), (0, SUBLANES - steps), (0, 0))
    k_new = k.reshape(steps, nb, ATT_KV_DIM).transpose(1, 0, 2)
    v_new = v.reshape(steps, nb, ATT_KV_DIM).transpose(1, 0, 2)
    o_st = _swa_sample(q_st, k_cache, v_cache, jnp.pad(k_new, pad), jnp.pad(v_new, pad), P['sink_rows'], layer,
                       steps, 16)
    o5 = o_st.reshape(nb, ATT_KV_HEADS, ATT_GROUP, steps, ATT_KV_HEADS, HEAD_DIM)
    o_a = jnp.stack([o5[:, j, :, :, j, :] for j in range(ATT_KV_HEADS)], axis=0)
    o_a = o_a.transpose(3, 1, 0, 2, 4).reshape(n, ATT_DIM)

    x2, c_tail = _back_sample(x, y, bonus, g, o_a, gr, ga, conv_prev, P, g_final, final)
    state = (z_tail, s_l, k_new, v_new, c_tail)
    return x2, state


def kernel(x_prompt, x_sample, state_rwkv_shift, state_rwkv_wkv, cache_swa_k, cache_swa_v, state_ffn_conv, norm_mix_g, w_in, rwkv_mu, rwkv_w0, rwkv_w2, rwkv_a0, rwkv_a2, rwkv_g2, rwkv_k_k, rwkv_k_a, rwkv_r_k, rwkv_ln_g, rwkv_ln_b, attn_sinks, w_br_rwkv, w_br_attn, w_out, norm_ffn_g, ffn_w_in, ffn_conv_w, ffn_conv_b, ffn_w_down, norm_final_g):
    bp, tp, _ = x_prompt.shape
    nb, steps, _ = x_sample.shape
    depth = w_in.shape[0]
    tab_p = _rope_tables(jnp.arange(tp, dtype=jnp.int32))
    tab_s = _rope_tables(PAST_LEN + jnp.repeat(jnp.arange(steps, dtype=jnp.int32), nb))
    g_final = norm_final_g[None]
    params = _stacked_params(norm_mix_g, w_in, rwkv_mu, rwkv_w0, rwkv_w2, rwkv_a0, rwkv_a2, rwkv_g2, rwkv_k_k,
                             rwkv_k_a, rwkv_r_k, rwkv_ln_g, rwkv_ln_b, attn_sinks, w_br_rwkv, w_br_attn, w_out,
                             norm_ffn_g, ffn_w_in, ffn_conv_w, ffn_conv_b, ffn_w_down, steps)
    wkv0 = state_rwkv_wkv.transpose(0, 2, 3, 4, 1)
    k_cache = cache_swa_k.reshape(depth, nb, WINDOW, ATT_KV_DIM)
    v_cache = cache_swa_v.reshape(depth, nb, WINDOW, ATT_KV_DIM)
    conv_prev = state_ffn_conv.transpose(0, 2, 1, 3)
    xp = x_prompt.reshape(bp * tp, D_MODEL)
    xs = x_sample.transpose(1, 0, 2).reshape(steps * nb, D_MODEL)
    outs_p, outs_s = [], []
    for l in range(depth):
        P = dict(params, layer=l)
        final = l == depth - 1
        xp, sp = _prompt_layer(xp, P, tab_p, g_final, final, bp, tp)
        xs, ss = _sample_layer(xs, P, tab_s, g_final, final, state_rwkv_shift, wkv0, k_cache, v_cache, conv_prev,
                               nb, steps)
        outs_p.append(sp)
        outs_s.append(ss)
    y_prompt = xp.reshape(bp, tp, D_MODEL)
    y_sample = xs.reshape(steps, nb, D_MODEL).transpose(1, 0, 2)
    stack = lambda outs, i: jnp.stack([o[i] for o in outs])
    kv_shape = (depth, nb, steps, ATT_KV_HEADS, HEAD_DIM)
    s_k = jnp.concatenate([cache_swa_k[:, :, steps:], stack(outs_s, 2).reshape(kv_shape)], axis=2)
    s_v = jnp.concatenate([cache_swa_v[:, :, steps:], stack(outs_s, 3).reshape(kv_shape)], axis=2)
    s_wkv = stack(outs_s, 1).transpose(0, 4, 1, 2, 3)
    s_conv = stack(outs_s, 4).transpose(0, 2, 1, 3)
    return (y_prompt, y_sample, *(stack(outs_p, i) for i in range(5)),
            stack(outs_s, 0), s_wkv, s_k, s_v, s_conv)
```

```python
import functools

import jax
import jax.numpy as jnp
from jax import lax
from jax.experimental import pallas as pl
from jax.experimental.pallas import tpu as pltpu

F32 = jnp.float32
BF16 = jnp.bfloat16

D_MODEL = 1024
HEAD_DIM = 64
RWKV_HEADS = 8
RWKV_DIM = RWKV_HEADS * HEAD_DIM
D_DECAY_LORA = 64
D_AAA_LORA = 64
D_GATE_LORA = 128
RWKV_PROJ = 3 * RWKV_DIM + D_DECAY_LORA + D_AAA_LORA + D_GATE_LORA
ATT_Q_HEADS = 8
ATT_KV_HEADS = 2
ATT_GROUP = ATT_Q_HEADS // ATT_KV_HEADS
ATT_DIM = ATT_Q_HEADS * HEAD_DIM
ATT_KV_DIM = ATT_KV_HEADS * HEAD_DIM
WINDOW = 128
IN_PROJ = RWKV_PROJ + ATT_DIM + 2 * ATT_KV_DIM + 2 * D_MODEL
D_FF = 2816
CONV_W = 3
ROPE_THETA = 10000.0
RMS_EPS = 1e-6
GN_EPS = 64e-5
PAST_LEN = 16384

LANES = 128
SUBLANES = 8
MXU_DIM = 256
CHUNK = 64
HEADS_PER_GROUP = MXU_DIM // HEAD_DIM
VMEM_LIMIT = 56 * 1024 * 1024


def _params(sem):
    return pltpu.CompilerParams(dimension_semantics=sem, vmem_limit_bytes=VMEM_LIMIT)


def _const_spec(shape):
    zeros = (0,) * len(shape)
    return pl.BlockSpec(shape, lambda *_: zeros, pipeline_mode=pl.Buffered(1))


def _layer_spec(shape, layer):
    index = (layer,) + (0,) * len(shape)
    return pl.BlockSpec((None,) + tuple(shape), lambda *_: index, pipeline_mode=pl.Buffered(1))


def _mm(a, b):
    return jnp.dot(a.astype(BF16), b.astype(BF16), preferred_element_type=F32)


def _mm_nt(a, b):
    return lax.dot_general(a.astype(BF16), b.astype(BF16), (((1,), (1,)), ((), ())),
                           preferred_element_type=F32)


def _mm_f32(a, b):
    return jnp.dot(a, b, precision=lax.Precision.HIGHEST, preferred_element_type=F32)


def _mm_tn(a, b):
    return lax.dot_general(a.astype(BF16), b.astype(BF16), (((0,), (0,)), ((), ())),
                           preferred_element_type=F32)


def _head_sum(x, ones_bd):
    return jnp.dot(x.astype(BF16), ones_bd, preferred_element_type=F32)


def _rmsnorm(x, g):
    return x * lax.rsqrt(jnp.mean(x * x, axis=-1, keepdims=True) + RMS_EPS) * g


def _rope(x, cos, sin_signed, first_half):
    partner = jnp.where(first_half, pltpu.roll(x, LANES - HEAD_DIM // 2, 1), pltpu.roll(x, HEAD_DIM // 2, 1))
    return x * cos + partner * sin_signed


def _prep_math(z, zp, mu, w0, w2p, a0, a2p, g2, k_k, k_a, r_k, ones_bd, outs, rs):
    r_ref, lw_ref, kk_ref, b_ref, kf_ref, v_ref, g_ref, bonus_ref = outs
    zs = z + mu * (zp - z)
    d = RWKV_DIM
    r, k, v = zs[:, 0:d], zs[:, d:2 * d], zs[:, 2 * d:3 * d]
    zwa = zs[:, 3 * d:3 * d + LANES]
    zg = zs[:, 3 * d + LANES:3 * d + 2 * LANES]
    w_pre = w0 + _mm(jnp.tanh(zwa), w2p)
    neg = -w_pre
    softplus = jnp.maximum(neg, 0.0) + jnp.log1p(jnp.exp(-jnp.abs(neg)))
    w_log = -softplus - 0.5
    lw_ref[rs, :] = -jnp.exp(w_log)
    a = jax.nn.sigmoid(a0 + _mm(zwa, a2p))
    g_ref[rs, :] = _mm(jax.nn.sigmoid(zg), g2).astype(g_ref.dtype)
    kk0 = k * k_k
    norm = jnp.sqrt(_head_sum(kk0 * kk0, ones_bd))
    kk = kk0 / jnp.maximum(norm, 1e-12)
    kf = k * (1.0 + (a - 1.0) * k_a)
    r_ref[rs, :] = r.astype(r_ref.dtype)
    kk_ref[rs, :] = kk.astype(kk_ref.dtype)
    b_ref[rs, :] = (kk * a).astype(b_ref.dtype)
    kf_ref[rs, :] = kf.astype(kf_ref.dtype)
    v_ref[rs, :] = v.astype(v_ref.dtype)
    bonus_ref[rs, :] = (_head_sum(r * kf * r_k, ones_bd) * v).astype(bonus_ref.dtype)


N_PREP_CONSTS = 10
N_FRONT_OUTS = 14


def _row_blocks(n, nsub):
    rb = n // nsub
    return [slice(i * rb, (i + 1) * rb) for i in range(nsub)], rb


def _staggered(stages, nsub):
    for step in range(nsub + len(stages) - 1):
        for i in range(nsub):
            if 0 <= step - i < len(stages):
                stages[step - i](i)


def _front_kernel(*refs, sample, nsub):
    if sample:
        x_ref, sp_ref, g_ref, w_ref, cos_ref, sin_ref = refs[:6]
        rest = refs[6:]
    else:
        x_ref, g_ref, w_ref, cos_ref, sin_ref = refs[:5]
        rest = refs[5:]
    consts = [c[...] for c in rest[:N_PREP_CONSTS]]
    outs = rest[N_PREP_CONSTS:N_PREP_CONSTS + N_FRONT_OUTS]
    q_ref, k_ref, v_ref, gr_ref, ga_ref, zt_ref = outs[:6]
    blocks, rb = _row_blocks(x_ref.shape[0], nsub)
    c0, c1, c2, c3, c4 = (RWKV_PROJ, RWKV_PROJ + ATT_DIM, RWKV_PROJ + ATT_DIM + ATT_KV_DIM,
                          RWKV_PROJ + ATT_DIM + 2 * ATT_KV_DIM, RWKV_PROJ + ATT_DIM + 2 * ATT_KV_DIM + D_MODEL)
    if not sample:
        carry = rest[N_PREP_CONSTS + N_FRONT_OUTS]

        @pl.when(pl.program_id(1) == 0)
        def _():
            carry[...] = jnp.zeros_like(carry)

    row = lax.broadcasted_iota(jnp.int32, (rb, 1), 0)
    lane = lax.broadcasted_iota(jnp.int32, (1, LANES), 1)
    first_half = (lane % HEAD_DIM) < (HEAD_DIM // 2)
    hs, zs = {}, {}

    def project(i):
        hs[i] = _rmsnorm(x_ref[blocks[i], :], g_ref[...]).astype(BF16)
        zs[i] = jnp.dot(hs[i], w_ref[:, 0:c0], preferred_element_type=F32)

    def rwkv_operands(i):
        z = zs[i]
        if sample:
            zp = sp_ref[...] if i == 0 else zs[i - 1]
            if i == nsub - 1:
                zt_ref[...] = z
        else:
            tail = carry[...] if i == 0 else zs[i - 1][rb - SUBLANES:, :]
            zp = jnp.where(row == 0, tail[SUBLANES - 1:SUBLANES, :], pltpu.roll(z, 1, 0))
            if i == nsub - 1:
                carry[...] = z[rb - SUBLANES:, :]
                zt_ref[0] = z[rb - SUBLANES:, :]
        _prep_math(z, zp, *consts, outs[6:], blocks[i])

    def attention_operands(i):
        rs, h = blocks[i], hs[i]
        cos = cos_ref[rs, :]
        sin = sin_ref[rs, :]
        q = jnp.dot(h, w_ref[:, c0:c1], preferred_element_type=F32)
        for grp in range(ATT_DIM // LANES):
            sl = slice(grp * LANES, (grp + 1) * LANES)
            q_ref[rs, sl] = _rope(q[:, sl], cos, sin, first_half).astype(q_ref.dtype)
        k = jnp.dot(h, w_ref[:, c1:c2], preferred_element_type=F32)
        k_ref[rs, :] = _rope(k, cos, sin, first_half)
        v_ref[rs, :] = jnp.dot(h, w_ref[:, c2:c3], preferred_element_type=F32)
        gr_ref[rs, :] = jax.nn.sigmoid(jnp.dot(h, w_ref[:, c3:c4], preferred_element_type=F32)).astype(gr_ref.dtype)
        ga_ref[rs, :] = jax.nn.sigmoid(jnp.dot(h, w_ref[:, c4:IN_PROJ], preferred_element_type=F32)).astype(ga_ref.dtype)

    _staggered([project, rwkv_operands, attention_operands], nsub)


def _prep_consts(P):
    return [P['mu'], P['w0'], P['w2p'], P['a0'], P['a2p'], P['g2'], P['k_k'], P['k_a'], P['r_k'], P['ones_bd']]


def _prep_const_specs(layer):
    d = RWKV_DIM
    shapes = [(1, RWKV_PROJ), (1, d), (LANES, d), (1, d), (LANES, d), (D_GATE_LORA, d), (1, d), (1, d), (1, d)]
    return [_layer_spec(s, layer) for s in shapes] + [_const_spec((d, d))]


def _front_out_types(n, tail_shape):
    d = RWKV_DIM
    widths = [(ATT_DIM, BF16), (ATT_KV_DIM, F32), (ATT_KV_DIM, F32), (D_MODEL, BF16), (D_MODEL, BF16)]
    prep = [(d, BF16), (d, F32), (d, BF16), (d, BF16), (d, BF16), (d, BF16), (d, BF16), (d, BF16)]
    shapes = ([jax.ShapeDtypeStruct((n, w), t) for w, t in widths] + [jax.ShapeDtypeStruct(tail_shape, F32)]
              + [jax.ShapeDtypeStruct((n, w), t) for w, t in prep])
    return shapes, [w for w, _ in widths], [w for w, _ in prep]


def _front_prompt(x2d, P, cos_t, sin_t, batch, tm):
    n = x2d.shape[0]
    layer = P['layer']
    nt = n // batch // tm
    row = lambda b, t: (b * nt + t, 0)
    pos = lambda b, t: (t, 0)
    shapes, widths, prep = _front_out_types(n, (batch, SUBLANES, RWKV_PROJ))
    return pl.pallas_call(
        functools.partial(_front_kernel, sample=False, nsub=tm // 128),
        grid=(batch, nt),
        in_specs=[pl.BlockSpec((tm, D_MODEL), row), _layer_spec((1, D_MODEL), layer),
                  _layer_spec((D_MODEL, IN_PROJ), layer),
                  pl.BlockSpec((tm, LANES), pos), pl.BlockSpec((tm, LANES), pos)] + _prep_const_specs(layer),
        out_specs=([pl.BlockSpec((tm, w), row) for w in widths]
                   + [pl.BlockSpec((1, SUBLANES, RWKV_PROJ), lambda b, t: (b, 0, 0))]
                   + [pl.BlockSpec((tm, w), row) for w in prep]),
        out_shape=shapes,
        scratch_shapes=[pltpu.VMEM((SUBLANES, RWKV_PROJ), F32)],
        compiler_params=_params(("arbitrary", "arbitrary")),
        name="front_prompt",
    )(x2d, P['norm_mix_g'], P['w_in'], cos_t, sin_t, *_prep_consts(P))


def _front_sample(x2d, shift_prev, P, cos_t, sin_t):
    n = x2d.shape[0]
    layer = P['layer']
    nb = shift_prev.shape[1]
    shapes, _, _ = _front_out_types(n, (nb, RWKV_PROJ))
    return pl.pallas_call(
        functools.partial(_front_kernel, sample=True, nsub=n // nb),
        grid=(1,),
        in_specs=[_const_spec((n, D_MODEL)), _layer_spec((nb, RWKV_PROJ), layer), _layer_spec((1, D_MODEL), layer),
                  _layer_spec((D_MODEL, IN_PROJ), layer), _const_spec((n, LANES)), _const_spec((n, LANES))]
        + _prep_const_specs(layer),
        out_specs=[_const_spec(sh.shape) for sh in shapes],
        out_shape=shapes,
        compiler_params=_params(("arbitrary",)),
        name="front_sample",
    )(x2d, shift_prev, P['norm_mix_g'], P['w_in'], cos_t, sin_t, *_prep_consts(P))


def _wkv_prompt_kernel(r_ref, lw_ref, kk_ref, b_ref, kf_ref, v_ref, y_ref, st_ref, s_scr, *, chunks, together):
    C, G, M = CHUNK, HEADS_PER_GROUP, MXU_DIM
    nbatch = r_ref.shape[0]
    ngrp = RWKV_HEADS // G

    @pl.when(pl.program_id(0) == 0)
    def _():
        s_scr[...] = jnp.zeros_like(s_scr)

    ti = lax.broadcasted_iota(jnp.int32, (C, M), 0)
    si = lax.broadcasted_iota(jnp.int32, (C, M), 1) % C
    strict = ti > si
    incl = ti >= si
    eye = (ti == si).astype(F32)
    tri = (lax.broadcasted_iota(jnp.int32, (C, C), 0) >= lax.broadcasted_iota(jnp.int32, (C, C), 1)).astype(F32)
    lane_head = lax.broadcasted_iota(jnp.int32, (1, M), 1) // HEAD_DIM
    same_head = (lax.broadcasted_iota(jnp.int32, (M, M), 0) // HEAD_DIM
                 == lax.broadcasted_iota(jnp.int32, (M, M), 1) // HEAD_DIM)

    def stack(x):
        xb = x.astype(BF16)
        return jnp.concatenate([jnp.where(lane_head == h, xb, jnp.zeros_like(xb)) for h in range(G)], axis=0)

    def rows2(top, bottom):
        return jnp.concatenate([top.astype(BF16), bottom.astype(BF16)], axis=0)

    def load(rows):
        probs = []
        for bi in range(nbatch):
            lw = lw_ref[bi, rows, :]
            cum = _mm_f32(tri, lw)
            w_end = jnp.exp(cum[C - 1:C, :])
            e_neg = jnp.exp(-cum)
            a_hat = -kk_ref[bi, rows, :] * jnp.exp(cum - lw)
            r_hat = r_ref[bi, rows, :] * jnp.exp(cum)
            b_til = b_ref[bi, rows, :] * e_neg
            k_til = kf_ref[bi, rows, :] * e_neg
            v = v_ref[bi, rows, :]
            for grp in range(ngrp):
                ls = slice(grp * M, (grp + 1) * M)
                we = w_end[:, ls]
                probs.append(dict(
                    bi=bi, grp=grp, ls=ls, we=we, rows=rows, ar=rows2(a_hat[:, ls], r_hat[:, ls]),
                    b=stack(b_til[:, ls]), k=stack(k_til[:, ls]), v=v[:, ls], vs=stack(v[:, ls]),
                    bk=rows2(b_til[:, ls] * we, k_til[:, ls] * we)))
        return probs

    def state_free(probs):
        l_ab, p_rb, lp_k = [], [], []
        for p in probs:
            x = _mm_nt(p['ar'], p['b'])
            l_ab.append(jnp.where(strict, x[0:C], 0.0))
            p_rb.append(jnp.where(incl, x[C:2 * C], 0.0).astype(BF16))
        for p in probs:
            x = _mm_nt(p['ar'], p['k'])
            lp_k.append(rows2(jnp.where(strict, x[0:C], 0.0), jnp.where(incl, x[C:2 * C], 0.0)))
        t_inv = [eye + x for x in l_ab]
        pw = [_mm(x, stack(x)) for x in l_ab]
        for _ in range(C.bit_length() - 3):
            for i, (t, x) in enumerate(zip(t_inv, pw)):
                both = _mm(rows2(t, x), stack(x))
                t_inv[i] = t + both[0:C]
                pw[i] = both[C:2 * C]
        t_inv = [t + _mm(t, stack(x)) for t, x in zip(t_inv, pw)]
        e0 = [_mm(x, p['vs']) for x, p in zip(lp_k, probs)]
        return t_inv, p_rb, e0

    def advance(probs, t_inv, p_rb, e0):
        s0 = [s_scr[p['bi'], p['grp']] for p in probs]
        d0 = [_mm_nt(p['ar'], x) for p, x in zip(probs, s0)]
        u = [_mm(t, stack(x[0:C] + y[0:C])) for t, x, y in zip(t_inv, d0, e0)]
        for p, x, y, w, q, st in zip(probs, d0, e0, u, p_rb, s0):
            y_ref[p['bi'], p['rows'], p['ls']] = x[C:2 * C] + y[C:2 * C] + _mm(q, stack(w))
            upd = _mm_tn(rows2(w, p['v']), p['bk'])
            s_scr[p['bi'], p['grp']] = st * p['we'] + jnp.where(same_head, upd, 0.0)

    def chunk_group(c, carry):
        chunk_probs = [load(pl.ds(pl.multiple_of((c * together + j) * C, C), C)) for j in range(together)]
        t_inv, p_rb, e0 = state_free([p for probs in chunk_probs for p in probs])
        n = len(chunk_probs[0])
        for j, probs in enumerate(chunk_probs):
            sl = slice(j * n, (j + 1) * n)
            advance(probs, t_inv[sl], p_rb[sl], e0[sl])
        return carry

    lax.fori_loop(0, chunks // together, chunk_group, 0)

    @pl.when(pl.program_id(0) == pl.num_programs(0) - 1)
    def _():
        st_ref[...] = s_scr[...]


def _wkv_prompt(r, lw, kk, b, kf, v, batch, tl):
    n = r.shape[0]
    seq = n // batch
    ngrp = RWKV_HEADS // HEADS_PER_GROUP
    blk = pl.BlockSpec((batch, tl, RWKV_DIM), lambda t: (0, t, 0))
    y, st = pl.pallas_call(
        functools.partial(_wkv_prompt_kernel, chunks=tl // CHUNK, together=2),
        grid=(seq // tl,),
        in_specs=[blk] * 6,
        out_specs=[blk, _const_spec((batch, ngrp, MXU_DIM, MXU_DIM))],
        out_shape=[jax.ShapeDtypeStruct((batch, seq, RWKV_DIM), F32),
                   jax.ShapeDtypeStruct((batch, ngrp, MXU_DIM, MXU_DIM), F32)],
        scratch_shapes=[pltpu.VMEM((batch, ngrp, MXU_DIM, MXU_DIM), F32)],
        compiler_params=_params(("arbitrary",)),
        name="wkv_prompt",
    )(*(t.reshape(batch, seq, RWKV_DIM) for t in (r, lw, kk, b, kf, v)))
    st = st.reshape(batch, ngrp, HEADS_PER_GROUP, HEAD_DIM, HEADS_PER_GROUP, HEAD_DIM)
    st = jnp.stack([st[:, :, h, :, h, :] for h in range(HEADS_PER_GROUP)], axis=2)
    st = st.reshape(batch, RWKV_HEADS, HEAD_DIM, HEAD_DIM)
    return y.reshape(n, RWKV_DIM), st


def _wkv_sample_kernel(r_ref, lw_ref, kk_ref, b_ref, kf_ref, v_ref, s_ref, y_ref, so_ref, *, steps):
    group = SUBLANES // 2

    def value_rows(gi, carry):
        vis = [gi * group + j for j in range(group)]
        ss = [s_ref[0, vi] for vi in vis]
        for t in range(steps):
            kk, w, b = kk_ref[t, 0].astype(F32), jnp.exp(lw_ref[t, 0]), b_ref[t, 0].astype(F32)
            kf, r = kf_ref[t, 0].astype(F32), r_ref[t, 0].astype(F32)
            sks = [jnp.sum(s * kk, axis=0, keepdims=True) for s in ss]
            ss = [s * w - sk * b + v_ref[t, 0, pl.ds(vi, 1), :] * kf for s, sk, vi in zip(ss, sks, vis)]
            for s, vi in zip(ss, vis):
                y_ref[t, 0, pl.ds(vi, 1), :] = jnp.sum(s * r, axis=0, keepdims=True)
        for s, vi in zip(ss, vis):
            so_ref[0, vi] = s
        return carry

    lax.fori_loop(0, HEAD_DIM // group, value_rows, 0)


def _wkv_sample(r, lw, kk, b, kf, v, s0, layer, steps):
    nb = r.shape[-1]
    vec = pl.BlockSpec((steps, 1, HEAD_DIM, nb), lambda h: (0, h, 0, 0))
    st = pl.BlockSpec((1, HEAD_DIM, HEAD_DIM, nb), lambda h: (h, 0, 0, 0))
    st_in = pl.BlockSpec((None, 1, HEAD_DIM, HEAD_DIM, nb), lambda h: (layer, h, 0, 0, 0))
    return pl.pallas_call(
        functools.partial(_wkv_sample_kernel, steps=steps),
        grid=(RWKV_HEADS,),
        in_specs=[vec] * 6 + [st_in],
        out_specs=[vec, st],
        out_shape=[jax.ShapeDtypeStruct(r.shape, F32), jax.ShapeDtypeStruct(s0.shape[1:], F32)],
        compiler_params=_params(("arbitrary",)),
        name="wkv_sample",
    )(r, lw, kk, b, kf, v, s0)


def _swa_prompt_kernel(q_ref, kc_ref, kp_ref, vc_ref, vp_ref, sink_ref, o_ref, *, sub):
    blk = WINDOW
    first = pl.program_id(1) == 0
    k_all = jnp.concatenate([kp_ref[...], kc_ref[...]], axis=0)
    v_all = jnp.concatenate([vp_ref[...], vc_ref[...]], axis=0).astype(BF16)
    v_ext = jnp.concatenate([v_all, jnp.ones_like(v_all)], axis=1)
    k_swapped = pltpu.roll(k_all, HEAD_DIM, 1)
    lane = lax.broadcasted_iota(jnp.int32, (1, LANES), 1)
    low = lane < HEAD_DIM
    k_var = {(0, 0): jnp.where(low, k_all, 0.0), (0, 1): jnp.where(low, 0.0, k_swapped),
             (1, 1): jnp.where(low, 0.0, k_all), (1, 0): jnp.where(low, k_swapped, 0.0)}
    k_var = {key: val.astype(BF16) for key, val in k_var.items()}
    order = [j * ATT_GROUP + g for j in range(ATT_KV_HEADS) for half in range(2)
             for g in range(ATT_GROUP) if g % 2 == half]
    n_rows = ATT_Q_HEADS * blk
    qi = lax.broadcasted_iota(jnp.int32, (n_rows, 2 * blk), 0) % blk
    kj = lax.broadcasted_iota(jnp.int32, (n_rows, 2 * blk), 1)
    diff = blk + qi - kj
    band = (diff >= 0) & (diff < WINDOW)
    row_head = lax.broadcasted_iota(jnp.int32, (n_rows, 1), 0) // blk
    sink = jnp.zeros((n_rows, LANES), F32)
    for idx, head in enumerate(order):
        sink = jnp.where(row_head == idx, sink_ref[0:1, head:head + 1], sink)
    scale = HEAD_DIM ** -0.5
    scores, maxes, weights, sums = {}, {}, {}, {}

    def score(sb):
        qs = slice(sb * blk, (sb + 1) * blk)
        ks = slice(sb * blk, (sb + 2) * blk)
        parts = []
        for j in range(ATT_KV_HEADS):
            for half in range(2):
                heads = [j * ATT_GROUP + g for g in range(ATT_GROUP) if g % 2 == half]
                q_cat = jnp.concatenate([q_ref[qs, (h // 2) * LANES:(h // 2 + 1) * LANES] for h in heads], axis=0)
                parts.append(_mm_nt(q_cat * scale, k_var[(j, half)][ks]))
        s = jnp.concatenate(parts, axis=0)
        ok = band & jnp.logical_or(kj >= blk, jnp.logical_not(first)) if sb == 0 else band
        scores[sb] = jnp.where(ok, s, -jnp.inf)

    def exponentiate(sb):
        s = scores[sb]
        row_max = jnp.max(jnp.maximum(s[:, 0:LANES], s[:, LANES:2 * LANES]), axis=-1, keepdims=True)
        m = jnp.maximum(row_max, sink)
        maxes[sb] = m
        weights[sb] = jnp.exp(s - jnp.concatenate([m, m], axis=1)).astype(BF16)

    def weigh(sb):
        ks = slice(sb * blk, (sb + 2) * blk)
        sums[sb] = jnp.dot(weights[sb], v_ext[ks], preferred_element_type=F32)

    def emit(sb):
        qs = slice(sb * blk, (sb + 1) * blk)
        ev = sums[sb]
        o = ev[:, 0:LANES] / (ev[:, LANES:2 * LANES] + jnp.exp(sink - maxes[sb]))
        by_head = {}
        for idx, head in enumerate(order):
            o_h = o[idx * blk:(idx + 1) * blk]
            by_head[head] = pltpu.roll(o_h, HEAD_DIM, 1) if head // ATT_GROUP != head % 2 else o_h
        for grp in range(ATT_DIM // LANES):
            o_ref[qs, grp * LANES:(grp + 1) * LANES] = jnp.where(low, by_head[2 * grp],
                                                                 by_head[2 * grp + 1]).astype(o_ref.dtype)

    _staggered([score, exponentiate, weigh, emit], sub)


def _swa_prompt(q, k, v, P, batch, sub):
    n = q.shape[0]
    rows = sub * WINDOW
    nb = n // batch // rows
    cur = lambda b, i: (b * nb + i, 0)
    prev = lambda b, i: ((b * nb + i) * sub - jnp.minimum(i, 1), 0)
    return pl.pallas_call(
        functools.partial(_swa_prompt_kernel, sub=sub),
        grid=(batch, nb),
        in_specs=[pl.BlockSpec((rows, ATT_DIM), cur),
                  pl.BlockSpec((rows, ATT_KV_DIM), cur), pl.BlockSpec((WINDOW, ATT_KV_DIM), prev),
                  pl.BlockSpec((rows, ATT_KV_DIM), cur), pl.BlockSpec((WINDOW, ATT_KV_DIM), prev),
                  _layer_spec((1, ATT_Q_HEADS), P['layer'])],
        out_specs=pl.BlockSpec((rows, ATT_DIM), cur),
        out_shape=jax.ShapeDtypeStruct((n, ATT_DIM), BF16),
        compiler_params=_params(("arbitrary", "arbitrary")),
        name="swa_prompt",
    )(q, k, k, v, v, P['sinks'])


def _swa_sample_kernel(q_ref, kc_ref, vc_ref, kn_ref, vn_ref, sink_ref, o_ref, *, steps):
    q = q_ref[...].astype(BF16)
    rows = q.shape[1]
    nk = kc_ref.shape[1]
    nn = kn_ref.shape[1]
    bdot = lambda a, b, dims: lax.dot_general(a, b, (dims, ((0,), (0,))), preferred_element_type=F32)
    s_c = bdot(q, kc_ref[...].astype(BF16), ((2,), (2,))) * (HEAD_DIM ** -0.5)
    s_n = bdot(q, kn_ref[...].astype(BF16), ((2,), (2,))) * (HEAD_DIM ** -0.5)
    t_c = lax.broadcasted_iota(jnp.int32, (1, rows, nk), 1) % steps
    w_c = lax.broadcasted_iota(jnp.int32, (1, rows, nk), 2)
    s_c = jnp.where(w_c > t_c, s_c, -jnp.inf)
    t_n = lax.broadcasted_iota(jnp.int32, (1, rows, nn), 1) % steps
    w_n = lax.broadcasted_iota(jnp.int32, (1, rows, nn), 2)
    s_n = jnp.where(w_n <= t_n, s_n, -jnp.inf)
    sink = sink_ref[...][None, :, 0:1]
    m = jnp.maximum(jnp.maximum(jnp.max(s_c, axis=-1, keepdims=True), jnp.max(s_n, axis=-1, keepdims=True)), sink)
    e_c = jnp.exp(s_c - m)
    e_n = jnp.exp(s_n - m)
    den = jnp.sum(e_c, axis=-1, keepdims=True) + jnp.sum(e_n, axis=-1, keepdims=True) + jnp.exp(sink - m)
    o = (bdot((e_c / den).astype(BF16), vc_ref[...].astype(BF16), ((2,), (1,)))
         + bdot((e_n / den).astype(BF16), vn_ref[...].astype(BF16), ((2,), (1,))))
    o_ref[...] = o


def _swa_sample(q_st, k_cache, v_cache, k_new, v_new, sink_rows, layer, steps, bt):
    nb, rows, _ = q_st.shape
    nn = k_new.shape[1]
    blk = lambda r: pl.BlockSpec((bt, r, LANES), lambda i: (i, 0, 0))
    cache = pl.BlockSpec((None, bt, WINDOW, LANES), lambda i: (layer, i, 0, 0))
    return pl.pallas_call(
        functools.partial(_swa_sample_kernel, steps=steps),
        grid=(nb // bt,),
        in_specs=[blk(rows), cache, cache, blk(nn), blk(nn), _layer_spec((rows, LANES), layer)],
        out_specs=blk(rows),
        out_shape=jax.ShapeDtypeStruct((nb, rows, LANES), F32),
        compiler_params=_params(("arbitrary",)),
        name="swa_sample",
    )(q_st, k_cache, v_cache, k_new, v_new, sink_rows)


def _gelu(x):
    return 0.5 * x * (1.0 + lax.erf(x * (2.0 ** -0.5)))


def _back_kernel(*refs, final, sample, nsub):
    x_ref, y_ref, bonus_ref, g_ref, oa_ref, gr_ref, ga_ref = refs[:7]
    at = 7
    if sample:
        cp_ref = refs[at]
        at += 1
    (lng_ref, lnb_ref, ones_ref, wa_ref, wb_ref, wo_ref, gn_ref, wi_ref, cw_ref, cb_ref, wd_ref,
     gf_ref) = refs[at:at + 12]
    o_ref, ct_ref, carry = refs[at + 12:at + 15]
    blocks, rb = _row_blocks(x_ref.shape[0], nsub)
    ones_bd = ones_ref[...]

    if sample:
        @pl.when(pl.program_id(0) == 0)
        def _():
            carry[...] = cp_ref[...]
    else:
        @pl.when(pl.program_id(1) == 0)
        def _():
            carry[...] = jnp.zeros_like(carry)

    row = lax.broadcasted_iota(jnp.int32, (rb, 1), 0)
    xs, cs, ups, acts = {}, {}, {}, {}

    def mix(i):
        rs = blocks[i]
        y = y_ref[rs, :]
        mean = _head_sum(y, ones_bd) * (1.0 / HEAD_DIM)
        d = y - mean
        var = _head_sum(d * d, ones_bd) * (1.0 / HEAD_DIM)
        yn = d * lax.rsqrt(var + GN_EPS) * lng_ref[...] + lnb_ref[...]
        o_r = (yn + bonus_ref[rs, :]) * g_ref[rs, :]
        merged = gr_ref[rs, :] * _mm(o_r, wa_ref[...]) + ga_ref[rs, :] * _mm(oa_ref[rs, :], wb_ref[...])
        xs[i] = x_ref[rs, :] + _mm(merged, wo_ref[...])

    def up_project(i):
        h = _rmsnorm(xs[i], gn_ref[...]).astype(BF16)
        cs[i] = jnp.dot(h, wi_ref[:, 0:D_FF], preferred_element_type=F32)
        ups[i] = jnp.dot(h, wi_ref[:, D_FF:2 * D_FF], preferred_element_type=F32)

    def conv_gate(i):
        c = cs[i]
        if sample:
            c2, c1 = carry[0], carry[1]
            carry[0] = c1
            carry[1] = c
            ct_ref[0] = c
        else:
            tail = carry[...] if i == 0 else cs[i - 1][rb - SUBLANES:, :]
            last = tail[SUBLANES - 1:SUBLANES, :]
            c1 = jnp.where(row == 0, last, pltpu.roll(c, 1, 0))
            c2 = jnp.where(row == 0, tail[SUBLANES - 2:SUBLANES - 1, :],
                           jnp.where(row == 1, last, pltpu.roll(c, 2, 0)))
            if i == nsub - 1:
                carry[...] = c[rb - SUBLANES:, :]
                ct_ref[0] = c[rb - SUBLANES:, :]
        conv = cb_ref[...] + c2 * cw_ref[0:1, :] + c1 * cw_ref[1:2, :] + c * cw_ref[2:3, :]
        acts[i] = (_gelu(conv) * ups[i]).astype(BF16)

    def down_project(i):
        out = xs[i] + jnp.dot(acts[i], wd_ref[...], preferred_element_type=F32)
        o_ref[blocks[i], :] = _rmsnorm(out, gf_ref[...]) if final else out

    _staggered([mix, up_project, conv_gate, down_project], nsub)


def _back_weight_specs(layer):
    return [_layer_spec((1, RWKV_DIM), layer), _layer_spec((1, RWKV_DIM), layer), _const_spec((RWKV_DIM, RWKV_DIM)),
            _layer_spec((RWKV_DIM, D_MODEL), layer), _layer_spec((ATT_DIM, D_MODEL), layer),
            _layer_spec((D_MODEL, D_MODEL), layer),
            _layer_spec((1, D_MODEL), layer), _layer_spec((D_MODEL, 2 * D_FF), layer),
            _layer_spec((CONV_W, D_FF), layer), _layer_spec((1, D_FF), layer), _layer_spec((D_FF, D_MODEL), layer),
            _const_spec((1, D_MODEL))]


def _back_weights(P, g_final):
    return [P['ln_g'], P['ln_b'], P['ones_bd'], P['w_br_rwkv'], P['w_br_attn'], P['w_out'],
            P['norm_ffn_g'], P['ffn_w_in'], P['ffn_conv_w'], P['ffn_conv_b'], P['ffn_w_down'], g_final]


def _back_prompt(x, y, bonus, g, o_a, gr, ga, P, g_final, final, batch, tm):
    n = x.shape[0]
    nt = n // batch // tm
    row = lambda b, t: (b * nt + t, 0)
    wide = pl.BlockSpec((tm, D_MODEL), row)
    half = pl.BlockSpec((tm, RWKV_DIM), row)
    return pl.pallas_call(
        functools.partial(_back_kernel, final=final, sample=False, nsub=tm // 256),
        grid=(batch, nt),
        in_specs=[wide, half, half, half, half, wide, wide] + _back_weight_specs(P['layer']),
        out_specs=[wide, pl.BlockSpec((1, SUBLANES, D_FF), lambda b, t: (b, 0, 0))],
        out_shape=[jax.ShapeDtypeStruct((n, D_MODEL), F32), jax.ShapeDtypeStruct((batch, SUBLANES, D_FF), F32)],
        scratch_shapes=[pltpu.VMEM((SUBLANES, D_FF), F32)],
        compiler_params=_params(("arbitrary", "arbitrary")),
        name="back_prompt",
    )(x, y, bonus, g, o_a, gr, ga, *_back_weights(P, g_final))


def _back_sample(x, y, bonus, g, o_a, gr, ga, conv_prev, P, g_final, final):
    n = x.shape[0]
    nb = conv_prev.shape[2]
    steps = n // nb
    taps = CONV_W - 1
    row = lambda t: (t, 0)
    wide = pl.BlockSpec((nb, D_MODEL), row)
    half = pl.BlockSpec((nb, RWKV_DIM), row)
    return pl.pallas_call(
        functools.partial(_back_kernel, final=final, sample=True, nsub=1),
        grid=(steps,),
        in_specs=[wide, half, half, half, half, wide, wide, _layer_spec((taps, nb, D_FF), P['layer'])]
        + _back_weight_specs(P['layer']),
        out_specs=[wide, pl.BlockSpec((1, nb, D_FF), lambda t: (jnp.maximum(t - (steps - taps), 0), 0, 0))],
        out_shape=[jax.ShapeDtypeStruct((n, D_MODEL), F32), jax.ShapeDtypeStruct((taps, nb, D_FF), F32)],
        scratch_shapes=[pltpu.VMEM((taps, nb, D_FF), F32)],
        compiler_params=_params(("arbitrary",)),
        name="back_sample",
    )(x, y, bonus, g, o_a, gr, ga, conv_prev, *_back_weights(P, g_final))


def _rope_tables(pos):
    inv = ROPE_THETA ** (-jnp.arange(0, HEAD_DIM, 2, dtype=F32) / HEAD_DIM)
    ang = pos.astype(F32)[:, None] * inv[None, :]
    cos, sin = jnp.cos(ang), jnp.sin(ang)
    reps = LANES // HEAD_DIM
    return jnp.tile(jnp.concatenate([cos, cos], axis=1), (1, reps)), jnp.tile(jnp.concatenate([-sin, sin], axis=1),
                                                                             (1, reps))


def _stacked_params(norm_mix_g, w_in, rwkv_mu, rwkv_w0, rwkv_w2, rwkv_a0, rwkv_a2, rwkv_g2, rwkv_k_k, rwkv_k_a,
                    rwkv_r_k, rwkv_ln_g, rwkv_ln_b, attn_sinks, w_br_rwkv, w_br_attn, w_out, norm_ffn_g, ffn_w_in,
                    ffn_conv_w, ffn_conv_b, ffn_w_down, steps):
    d = RWKV_DIM
    depth = w_in.shape[0]
    head = jnp.arange(d) // HEAD_DIM
    row = lambda t: t.reshape(depth, 1, -1)
    sink_rows = jnp.broadcast_to(jnp.repeat(attn_sinks, steps, axis=1)[:, :, None],
                                 (depth, ATT_Q_HEADS * steps, LANES))
    return {
        'norm_mix_g': row(norm_mix_g), 'w_in': w_in.astype(BF16),
        'mu': row(rwkv_mu), 'w0': row(rwkv_w0),
        'w2p': jnp.pad(rwkv_w2, ((0, 0), (0, D_AAA_LORA), (0, 0))).astype(BF16),
        'a0': row(rwkv_a0),
        'a2p': jnp.pad(rwkv_a2, ((0, 0), (D_DECAY_LORA, 0), (0, 0))).astype(BF16),
        'g2': rwkv_g2.astype(BF16), 'k_k': row(rwkv_k_k), 'k_a': row(rwkv_k_a),
        'r_k': row(rwkv_r_k), 'ln_g': row(rwkv_ln_g), 'ln_b': row(rwkv_ln_b),
        'ones_bd': (head[:, None] == head[None, :]).astype(BF16),
        'sinks': row(attn_sinks), 'sink_rows': sink_rows,
        'w_br_rwkv': w_br_rwkv.astype(BF16), 'w_br_attn': w_br_attn.astype(BF16),
        'w_out': w_out.astype(BF16), 'norm_ffn_g': row(norm_ffn_g),
        'ffn_w_in': ffn_w_in.astype(BF16), 'ffn_conv_w': ffn_conv_w, 'ffn_conv_b': row(ffn_conv_b),
        'ffn_w_down': ffn_w_down.astype(BF16),
    }


def _prompt_layer(x, P, tables, g_final, final, batch, seq):
    tm = min(512, seq)
    q, k, v, gr, ga, z_tail, r, lw, kk, b, kf, vv, g, bonus = _front_prompt(x, P, tables[0], tables[1], batch,
                                                                            min(1024, seq))
    y, s_t = _wkv_prompt(r, lw, kk, b, kf, vv, batch, min(512, seq))
    o_a = _swa_prompt(q, k, v, P, batch, min(16, seq // WINDOW))
    x2, c_tail = _back_prompt(x, y, bonus, g, o_a, gr, ga, P, g_final, final, batch, tm)
    last = lambda t: t.reshape(batch, seq, -1)[:, -WINDOW:].reshape(batch, WINDOW, ATT_KV_HEADS, HEAD_DIM)
    state = (z_tail[:, SUBLANES - 1], s_t, last(k), last(v), c_tail[:, SUBLANES - (CONV_W - 1):])
    return x2, state


def _sample_layer(x, P, tables, g_final, final, shift_prev, wkv0, k_cache, v_cache, conv_prev, nb, steps):
    n = nb * steps
    layer = P['layer']
    q, k, v, gr, ga, z_tail, r, lw, kk, b, kf, vv, g, bonus = _front_sample(x, shift_prev, P, tables[0], tables[1])
    to_lanes = lambda t: t.reshape(steps, nb, RWKV_HEADS, HEAD_DIM).transpose(0, 2, 3, 1).astype(F32)
    y_l, s_l = _wkv_sample(*(to_lanes(t) for t in (r, lw, kk, b, kf, vv)), wkv0, layer, steps)
    y = y_l.transpose(0, 3, 1, 2).reshape(n, RWKV_DIM)

    rows = ATT_Q_HEADS * steps
    q5 = q.reshape(steps, nb, ATT_KV_HEADS, ATT_GROUP, HEAD_DIM).transpose(1, 2, 3, 0, 4)
    sel = jnp.eye(ATT_KV_HEADS, dtype=q.dtype)
    q_st = (q5[:, :, :, :, None, :] * sel[None, :, None, None, :, None]).reshape(nb, rows, ATT_KV_DIM)
    pad = ((0, 0), (0, SUBLANES - steps), (0, 0))
    k_new = k.reshape(steps, nb, ATT_KV_DIM).transpose(1, 0, 2)
    v_new = v.reshape(steps, nb, ATT_KV_DIM).transpose(1, 0, 2)
    o_st = _swa_sample(q_st, k_cache, v_cache, jnp.pad(k_new, pad), jnp.pad(v_new, pad), P['sink_rows'], layer,
                       steps, 16)
    o5 = o_st.reshape(nb, ATT_KV_HEADS, ATT_GROUP, steps, ATT_KV_HEADS, HEAD_DIM)
    o_a = jnp.stack([o5[:, j, :, :, j, :] for j in range(ATT_KV_HEADS)], axis=0)
    o_a = o_a.transpose(3, 1, 0, 2, 4).reshape(n, ATT_DIM)

    x2, c_tail = _back_sample(x, y, bonus, g, o_a, gr, ga, conv_prev, P, g_final, final)
    state = (z_tail, s_l, k_new, v_new, c_tail)
    return x2, state


def kernel(x_prompt, x_sample, state_rwkv_shift, state_rwkv_wkv, cache_swa_k, cache_swa_v, state_ffn_conv, norm_mix_g, w_in, rwkv_mu, rwkv_w0, rwkv_w2, rwkv_a0, rwkv_a2, rwkv_g2, rwkv_k_k, rwkv_k_a, rwkv_r_k, rwkv_ln_g, rwkv_ln_b, attn_sinks, w_br_rwkv, w_br_attn, w_out, norm_ffn_g, ffn_w_in, ffn_conv_w, ffn_conv_b, ffn_w_down, norm_final_g):
    bp, tp, _ = x_prompt.shape
    nb, steps, _ = x_sample.shape
    depth = w_in.shape[0]
    tab_p = _rope_tables(jnp.arange(tp, dtype=jnp.int32))
    tab_s = _rope_tables(PAST_LEN + jnp.repeat(jnp.arange(steps, dtype=jnp.int32), nb))
    g_final = norm_final_g[None]
    params = _stacked_params(norm_mix_g, w_in, rwkv_mu, rwkv_w0, rwkv_w2, rwkv_a0, rwkv_a2, rwkv_g2, rwkv_k_k,
                             rwkv_k_a, rwkv_r_k, rwkv_ln_g, rwkv_ln_b, attn_sinks, w_br_rwkv, w_br_attn, w_out,
                             norm_ffn_g, ffn_w_in, ffn_conv_w, ffn_conv_b, ffn_w_down, steps)
    wkv0 = state_rwkv_wkv.transpose(0, 2, 3, 4, 1)
    k_cache = cache_swa_k.reshape(depth, nb, WINDOW, ATT_KV_DIM)
    v_cache = cache_swa_v.reshape(depth, nb, WINDOW, ATT_KV_DIM)
    conv_prev = state_ffn_conv.transpose(0, 2, 1, 3)
    xp = x_prompt.reshape(bp * tp, D_MODEL)
    xs = x_sample.transpose(1, 0, 2).reshape(steps * nb, D_MODEL)
    outs_p, outs_s = [], []
    for l in range(depth):
        P = dict(params, layer=l)
        final = l == depth - 1
        xp, sp = _prompt_layer(xp, P, tab_p, g_final, final, bp, tp)
        xs, ss = _sample_layer(xs, P, tab_s, g_final, final, state_rwkv_shift, wkv0, k_cache, v_cache, conv_prev,
                               nb, steps)
        outs_p.append(sp)
        outs_s.append(ss)
    y_prompt = xp.reshape(bp, tp, D_MODEL)
    y_sample = xs.reshape(steps, nb, D_MODEL).transpose(1, 0, 2)
    stack = lambda outs, i: jnp.stack([o[i] for o in outs])
    kv_shape = (depth, nb, steps, ATT_KV_HEADS, HEAD_DIM)
    s_k = jnp.concatenate([cache_swa_k[:, :, steps:], stack(outs_s, 2).reshape(kv_shape)], axis=2)
    s_v = jnp.concatenate([cache_swa_v[:, :, steps:], stack(outs_s, 3).reshape(kv_shape)], axis=2)
    s_wkv = stack(outs_s, 1).transpose(0, 4, 1, 2, 3)
    s_conv = stack(outs_s, 4).transpose(0, 2, 1, 3)
    return (y_prompt, y_sample, *(stack(outs_p, i) for i in range(5)),
            stack(outs_s, 0), s_wkv, s_k, s_v, s_conv)
```
